```python
import jax, jax.numpy as jnp
from jax import lax
import numpy as np

D_MODEL = 1024
BATCH = 8
SEQ = 2048
DEPTH = 1
DEC_BATCH = 128
DEC_SEQ = 4
PAST_LEN = 16384
PAGE_SIZE = 128

N_META = 16
D_CONV = D_MODEL
CONV_W = 31
DN_HEADS = 8
DN_DK = 128
DN_DV = 128
DN_QK = DN_HEADS * DN_DK
DN_V = DN_HEADS * DN_DV
DN_QKV = 2 * DN_QK + DN_V
SHORT_W = 4
CHUNK = 64
O_GLU = 0
O_Q = O_GLU + 2 * D_CONV
O_K = O_Q + DN_QK
O_V = O_K + DN_QK
O_Z = O_V + DN_V
O_A = O_Z + DN_V
O_B = O_A + DN_HEADS
O_GATE = O_B + DN_HEADS
IN_COLS = O_GATE + 2 * D_MODEL
N_EXPERTS = 32
TOP_K = 4
D_FF = D_MODEL
SWIGLU_LIMIT = 7.0
SWIGLU_ALPHA = 1.702
MOE_BLOCK = 128
EPS = 1e-6

kernel_name = 'hybrid_conformer_gdn_moe_step'


def rmsnorm(x, w):
    xf = x.astype(jnp.float32)
    y = xf * lax.rsqrt(jnp.mean(xf * xf, axis=-1, keepdims=True) + EPS)
    return (y * w.astype(jnp.float32)).astype(x.dtype)


def layernorm(x, g, b):
    xf = x.astype(jnp.float32)
    mu = jnp.mean(xf, axis=-1, keepdims=True)
    var = jnp.mean(jnp.square(xf - mu), axis=-1, keepdims=True)
    y = (xf - mu) * lax.rsqrt(var + EPS) * g.astype(jnp.float32) + b.astype(jnp.float32)
    return y.astype(x.dtype)


def l2norm(x):
    xf = x.astype(jnp.float32)
    return xf * lax.rsqrt(jnp.sum(xf * xf, axis=-1, keepdims=True) + EPS)


def causal_dwconv(x_hist, w):
    C = x_hist.shape[-1]
    return lax.conv_general_dilated(
        x_hist, w.astype(x_hist.dtype)[:, None, :], window_strides=(1,), padding='VALID',
        dimension_numbers=('NWC', 'WIO', 'NWC'), feature_group_count=C)


def gated_delta_chunked(q, k, v, g, beta, S0):
    N, L, H, DK = q.shape
    DV = v.shape[-1]
    nc = L // CHUNK
    f32 = jnp.float32

    def to_chunks(t):
        t = t.astype(f32).reshape((N, nc, CHUNK, H) + t.shape[3:])
        return jnp.moveaxis(t, 3, 1)

    q, k, v, g, beta = [to_chunks(t) for t in (q, k, v, g, beta)]
    gc = jnp.cumsum(g, axis=-1)
    causal = jnp.tril(jnp.ones((CHUNK, CHUNK), dtype=bool))
    strict = jnp.tril(jnp.ones((CHUNK, CHUNK), dtype=bool), k=-1)
    diff = gc[..., :, None] - gc[..., None, :]
    decay = jnp.where(causal, jnp.exp(jnp.where(causal, diff, 0.0)), 0.0)
    kb = k * beta[..., None]
    m_low = jnp.where(strict, jnp.einsum('nhcid,nhcjd->nhcij', kb, k) * decay, 0.0)
    a_mat = m_low + jnp.eye(CHUNK, dtype=f32)
    rhs = jnp.concatenate([v * beta[..., None], kb * jnp.exp(gc)[..., None]], axis=-1)
    sol = lax.linalg.triangular_solve(a_mat, rhs, left_side=True, lower=True)
    u = sol[..., :DV]
    w = sol[..., DV:]
    qk = jnp.where(causal, jnp.einsum('nhcid,nhcjd->nhcij', q, k) * decay, 0.0)
    q_dec = q * jnp.exp(gc)[..., None]
    g_last = gc[..., -1]
    k_dec = k * jnp.exp(g_last[..., None] - gc)[..., None]

    def step(S, xs):
        u_c, w_c, qk_c, qd_c, kd_c, gl_c = xs
        v_new = u_c - jnp.einsum('nhid,nhde->nhie', w_c, S)
        o = jnp.einsum('nhid,nhde->nhie', qd_c, S) + jnp.einsum('nhij,nhje->nhie', qk_c, v_new)
        S = S * jnp.exp(gl_c)[..., None, None] + jnp.einsum('nhid,nhie->nhde', kd_c, v_new)
        return S, o

    xs = tuple(jnp.moveaxis(t, 2, 0) for t in (u, w, qk, q_dec, k_dec, g_last))
    S, o = lax.scan(step, S0, xs)
    o = jnp.transpose(o, (1, 0, 3, 2, 4)).reshape(N, L, H, DV)
    return o, S


def moe(xn, w_router, b_router, w_up, b_up, w_down, b_down):
    N, L, D = xn.shape
    T = N * L
    f32 = jnp.float32
    xf = xn.reshape(T, D)
    logits = (xf @ w_router + b_router).astype(f32)
    top_v, top_i = lax.top_k(logits, TOP_K)
    gates = jax.nn.softmax(top_v, axis=-1)
    A = T * TOP_K
    n_blocks = -(-A // MOE_BLOCK) + N_EXPERTS
    R = n_blocks * MOE_BLOCK
    flat_e = top_i.reshape(-1).astype(jnp.int32)
    order = jnp.argsort(flat_e)
    se = flat_e[order]
    stok = (order // TOP_K).astype(jnp.int32)
    sgate = gates.reshape(-1)[order]
    counts = jnp.bincount(flat_e, length=N_EXPERTS)
    padded = (counts + MOE_BLOCK - 1) // MOE_BLOCK * MOE_BLOCK
    pend = jnp.cumsum(padded)
    pstart = pend - padded
    start = jnp.cumsum(counts) - counts
    dest = pstart[se] + jnp.arange(A) - start[se]
    tok_pad = jnp.full((R,), T, dtype=jnp.int32).at[dest].set(stok)
    gate_pad = jnp.zeros((R,), f32).at[dest].set(sgate)
    block_e = jnp.minimum(jnp.searchsorted(pend, jnp.arange(n_blocks) * MOE_BLOCK, side='right'),
                          N_EXPERTS - 1)
    x_ext = jnp.concatenate([xf, jnp.zeros((1, D), xf.dtype)], axis=0)
    xb = x_ext[tok_pad].reshape(n_blocks, MOE_BLOCK, D)

    def expert_block(args):
        xblk, e = args
        hmid = xblk @ w_up[e] + b_up[e]
        hg = jnp.minimum(hmid[:, :D_FF], SWIGLU_LIMIT)
        hl = jnp.clip(hmid[:, D_FF:], -SWIGLU_LIMIT, SWIGLU_LIMIT)
        act = hg * jax.nn.sigmoid(SWIGLU_ALPHA * hg) * (hl + 1.0)
        return act @ w_down[e] + b_down[e]

    yb = lax.map(expert_block, (xb, block_e))
    contrib = yb.reshape(R, D).astype(f32) * gate_pad[:, None]
    y = jax.ops.segment_sum(contrib, tok_pad, num_segments=T + 1)[:T]
    return y.astype(xn.dtype).reshape(N, L, D)


def _layer(h, conf_hist, dn_hist, S0, front_pad, lp):
    (norm_mix, w_in, w_conf_dw, b_conf_dw, ln_conf_g, ln_conf_b, w_conf_out, b_conf_out,
     w_dn_conv, dn_a_log, dn_dt_bias, dn_norm_w, w_dn_out, w_out,
     norm_ffn, w_router, b_router, w_up, b_up, w_down, b_down) = lp
    N, L, _ = h.shape
    f32 = jnp.float32
    xn = rmsnorm(h, norm_mix)
    p = xn @ w_in
    u = p[..., O_GLU:O_GLU + D_CONV] * jax.nn.sigmoid(p[..., O_GLU + D_CONV:O_Q])
    u_hist = jnp.concatenate([conf_hist.astype(u.dtype), u], axis=1)
    c = causal_dwconv(u_hist, w_conf_dw) + b_conf_dw
    c = jax.nn.silu(layernorm(c, ln_conf_g, ln_conf_b))
    y_a = c @ w_conf_out + b_conf_out
    qkv_hist = jnp.concatenate([dn_hist.astype(p.dtype), p[..., O_Q:O_Z]], axis=1)
    qkv = jax.nn.silu(causal_dwconv(qkv_hist, w_dn_conv))
    q = l2norm(qkv[..., :DN_QK].reshape(N, L, DN_HEADS, DN_DK)) * (DN_DK ** -0.5)
    k = l2norm(qkv[..., DN_QK:2 * DN_QK].reshape(N, L, DN_HEADS, DN_DK))
    v = qkv[..., 2 * DN_QK:].reshape(N, L, DN_HEADS, DN_DV).astype(f32)
    beta = jax.nn.sigmoid(p[..., O_B:O_GATE].astype(f32))
    g = -jnp.exp(dn_a_log.astype(f32)) * jax.nn.softplus(p[..., O_A:O_B].astype(f32) + dn_dt_bias.astype(f32))
    back = (-(front_pad + L)) % CHUNK

    def pad(t):
        return jnp.pad(t, [(0, 0), (front_pad, back)] + [(0, 0)] * (t.ndim - 2))

    o, S = gated_delta_chunked(pad(q), pad(k), pad(v), pad(g), pad(beta), S0.astype(f32))
    o = o[:, front_pad:front_pad + L]
    o = o * lax.rsqrt(jnp.mean(o * o, axis=-1, keepdims=True) + EPS) * dn_norm_w.astype(f32)
    o = o * jax.nn.silu(p[..., O_Z:O_A].reshape(N, L, DN_HEADS, DN_DV).astype(f32))
    y_b = o.reshape(N, L, DN_V).astype(h.dtype) @ w_dn_out
    gate = jax.nn.sigmoid(p[..., O_GATE:])
    mixed = gate[..., :D_MODEL] * y_a + gate[..., D_MODEL:] * y_b
    h = h + mixed @ w_out
    h = h + moe(rmsnorm(h, norm_ffn), w_router, b_router, w_up, b_up, w_down, b_down)
    return h, u_hist[:, -(CONV_W - 1):], qkv_hist[:, -(SHORT_W - 1):], S


def setup_inputs(seed: int = 0) -> dict:
    key = jax.random.key(seed)
    ks = jax.random.split(key, 32)
    f32 = jnp.float32
    nrm = lambda k, shape, s: jax.random.normal(k, shape, f32) * s
    dt = jnp.exp(jax.random.uniform(ks[12], (DEPTH, DN_HEADS), f32, np.log(1e-3), np.log(1e-1)))
    return {
        'x_prompt': nrm(ks[0], (BATCH, SEQ, D_MODEL), 1.0),
        'x_sample': nrm(ks[1], (DEC_BATCH, DEC_SEQ, D_MODEL), 1.0),
        'state_conf_conv': nrm(ks[2], (DEPTH, DEC_BATCH, CONV_W - 1, D_CONV), 0.5),
        'state_dn_conv': nrm(ks[3], (DEPTH, DEC_BATCH, SHORT_W - 1, DN_QKV), 1.0),
        'state_dn_S': nrm(ks[4], (DEPTH, DEC_BATCH, DN_HEADS, DN_DK, DN_DV), 0.5),
        'meta_tokens': nrm(ks[5], (N_META, D_MODEL), 1.0),
        'norm_mix': 1.0 + nrm(ks[6], (DEPTH, D_MODEL), 0.02),
        'w_in': nrm(ks[7], (DEPTH, D_MODEL, IN_COLS), D_MODEL ** -0.5),
        'w_conf_dw': nrm(ks[8], (DEPTH, CONV_W, D_CONV), CONV_W ** -0.5),
        'b_conf_dw': nrm(ks[9], (DEPTH, D_CONV), 0.02),
        'ln_conf_g': 1.0 + nrm(ks[10], (DEPTH, D_CONV), 0.02),
        'ln_conf_b': nrm(ks[11], (DEPTH, D_CONV), 0.02),
        'w_conf_out': nrm(ks[13], (DEPTH, D_CONV, D_MODEL), D_CONV ** -0.5),
        'b_conf_out': nrm(ks[14], (DEPTH, D_MODEL), 0.02),
        'w_dn_conv': nrm(ks[15], (DEPTH, SHORT_W, DN_QKV), SHORT_W ** -0.5),
        'dn_a_log': jnp.log(jax.random.uniform(ks[16], (DEPTH, DN_HEADS), f32, 1.0, 16.0)),
        'dn_dt_bias': dt + jnp.log(-jnp.expm1(-dt)),
        'dn_norm_w': 1.0 + nrm(ks[17], (DEPTH, DN_DV), 0.02),
        'w_dn_out': nrm(ks[18], (DEPTH, DN_V, D_MODEL), DN_V ** -0.5),
        'w_out': nrm(ks[19], (DEPTH, D_MODEL, D_MODEL), D_MODEL ** -0.5),
        'norm_ffn': 1.0 + nrm(ks[20], (DEPTH, D_MODEL), 0.02),
        'w_router': nrm(ks[21], (DEPTH, D_MODEL, N_EXPERTS), D_MODEL ** -0.5),
        'b_router': nrm(ks[22], (DEPTH, N_EXPERTS), 0.01),
        'w_up': nrm(ks[23], (DEPTH, N_EXPERTS, D_MODEL, 2 * D_FF), D_MODEL ** -0.5),
        'b_up': nrm(ks[24], (DEPTH, N_EXPERTS, 2 * D_FF), 0.02),
        'w_down': nrm(ks[25], (DEPTH, N_EXPERTS, D_FF, D_MODEL), D_FF ** -0.5),
        'b_down': nrm(ks[26], (DEPTH, N_EXPERTS, D_MODEL), 0.02),
        'norm_final': 1.0 + nrm(ks[27], (D_MODEL,), 0.02),
    }


def reference(x_prompt, x_sample, state_conf_conv, state_dn_conv, state_dn_S, meta_tokens,
              norm_mix, w_in, w_conf_dw, b_conf_dw, ln_conf_g, ln_conf_b, w_conf_out, b_conf_out,
              w_dn_conv, dn_a_log, dn_dt_bias, dn_norm_w, w_dn_out, w_out,
              norm_ffn, w_router, b_router, w_up, b_up, w_down, b_down, norm_final):
    B = x_prompt.shape[0]
    meta = jnp.broadcast_to(meta_tokens[None].astype(x_prompt.dtype), (B, N_META, D_MODEL))
    hp = jnp.concatenate([meta, x_prompt], axis=1)
    hs = x_sample
    prompt_front = (-N_META) % CHUNK
    sample_front = (PAST_LEN - N_META) % CHUNK
    cp, dp, sp, cs, ds, ss = [], [], [], [], [], []
    for l in range(DEPTH):
        lp = (norm_mix[l], w_in[l], w_conf_dw[l], b_conf_dw[l], ln_conf_g[l], ln_conf_b[l],
              w_conf_out[l], b_conf_out[l], w_dn_conv[l], dn_a_log[l], dn_dt_bias[l], dn_norm_w[l],
              w_dn_out[l], w_out[l], norm_ffn[l], w_router[l], b_router[l], w_up[l], b_up[l],
              w_down[l], b_down[l])
        hp, c1, c2, c3 = _layer(
            hp, jnp.zeros((B, CONV_W - 1, D_CONV), hp.dtype), jnp.zeros((B, SHORT_W - 1, DN_QKV), hp.dtype),
            jnp.zeros((B, DN_HEADS, DN_DK, DN_DV), jnp.float32), prompt_front, lp)
        hs, d1, d2, d3 = _layer(hs, state_conf_conv[l], state_dn_conv[l], state_dn_S[l], sample_front, lp)
        cp.append(c1); dp.append(c2); sp.append(c3)
        cs.append(d1); ds.append(d2); ss.append(d3)
    y_prompt = rmsnorm(hp, norm_final)[:, N_META:]
    y_sample = rmsnorm(hs, norm_final)
    return (y_prompt, y_sample, jnp.stack(cp), jnp.stack(dp), jnp.stack(sp),
            jnp.stack(cs), jnp.stack(ds), jnp.stack(ss))
```

```python
import functools

import jax
import jax.numpy as jnp
from jax import lax
from jax.experimental import pallas as pl
from jax.experimental.pallas import tpu as pltpu

D_MODEL = 1024
N_META = 16
CONV_W = 31
SHORT_W = 4
DN_HEADS = 8
DN_DK = 128
DN_DV = 128
CHUNK = 64
N_EXPERTS = 32
TOP_K = 4
D_FF = 1024
SWIGLU_LIMIT = 7.0
SWIGLU_ALPHA = 1.702
EPS = 1e-6

FRONT = (-N_META) % CHUNK
SAMPLE_CHUNK = 16
LANES = 128
HALO = 32
MOE_ROWS = 256
VMEM_LIMIT = 48 * 1024 * 1024

F32 = jnp.float32
BF16 = jnp.bfloat16
HIGHEST = lax.Precision.HIGHEST


def _row_tile(n, pref):
    best = 16
    for t in range(16, min(n, pref) + 1, 16):
        if n % t == 0:
            best = t
    assert n % best == 0
    return best


def _sigmoid(x):
    return 1.0 / (1.0 + jnp.exp(-x))


def _dot(a, b):
    return jnp.dot(a, b, preferred_element_type=F32)


def _dot_nt(a, b):
    return lax.dot_general(a, b, (((1,), (1,)), ((), ())), preferred_element_type=F32)


def _dot_tn(a, b):
    return lax.dot_general(a, b, (((0,), (0,)), ((), ())), preferred_element_type=F32)


def _dot_hi(a, b):
    return jnp.dot(a, b, preferred_element_type=F32, precision=HIGHEST)


def _params(sem):
    return pltpu.CompilerParams(dimension_semantics=sem, vmem_limit_bytes=VMEM_LIMIT)


def _rms_ab_kernel(h_ref, nw_ref, wab_ref, xn_ref, ab_ref):
    x = h_ref[...]
    y = x * lax.rsqrt(jnp.mean(x * x, axis=-1, keepdims=True) + EPS) * nw_ref[...]
    yb = y.astype(BF16)
    xn_ref[...] = yb
    ab_ref[...] = _dot(yb, wab_ref[...])


def _rms_ab(h, norm_w, w_ab):
    T = h.shape[0]
    tm = _row_tile(T, 1024)
    return pl.pallas_call(
        _rms_ab_kernel,
        out_shape=(jax.ShapeDtypeStruct((T, D_MODEL), BF16), jax.ShapeDtypeStruct((T, LANES), F32)),
        grid=(T // tm,),
        in_specs=[pl.BlockSpec((tm, D_MODEL), lambda i: (i, 0)),
                  pl.BlockSpec((1, D_MODEL), lambda i: (0, 0)),
                  pl.BlockSpec((D_MODEL, LANES), lambda i: (0, 0))],
        out_specs=(pl.BlockSpec((tm, D_MODEL), lambda i: (i, 0)),
                   pl.BlockSpec((tm, LANES), lambda i: (i, 0))),
        compiler_params=_params(("arbitrary",)),
        name="rms_ab",
    )(h, norm_w, w_ab)


def _mm_in_kernel(x_ref, w_ref, o_ref, wb_ref):
    @pl.when(pl.program_id(1) == 0)
    def _():
        wb_ref[...] = w_ref[...].astype(BF16)

    o_ref[...] = _dot(x_ref[...], wb_ref[...])


def _mm_in(xn, w):
    T, K = xn.shape
    N = w.shape[1]
    tm = _row_tile(T, 1024)
    tn = 1024
    assert N % tn == 0
    return pl.pallas_call(
        _mm_in_kernel,
        out_shape=jax.ShapeDtypeStruct((T, N), F32),
        grid=(N // tn, T // tm),
        in_specs=[pl.BlockSpec((tm, K), lambda j, i: (i, 0)),
                  pl.BlockSpec((K, tn), lambda j, i: (0, j))],
        out_specs=pl.BlockSpec((tm, tn), lambda j, i: (i, j)),
        scratch_shapes=[pltpu.VMEM((K, tn), BF16)],
        compiler_params=_params(("arbitrary", "arbitrary")),
        name="in_proj",
    )(xn, w)


def _ln_silu(x, g, b):
    mu = jnp.mean(x, axis=-1, keepdims=True)
    xc = x - mu
    var = jnp.mean(xc * xc, axis=-1, keepdims=True)
    y = xc * lax.rsqrt(var + EPS) * g + b
    return y * _sigmoid(y)


def _conf_prompt_kernel(pa_ref, pb_ref, ha_ref, hb_ref, wdw_ref, bdw_ref, lng_ref, lnb_ref,
                        c_ref, ust_ref, ubuf, cbuf, *, tl, rt, ct):
    t = pl.program_id(1)
    u = pa_ref[...] * _sigmoid(pb_ref[...])
    uh = ha_ref[...] * _sigmoid(hb_ref[...])
    ubuf[0:HALO, :] = jnp.where(t > 0, uh, 0.0)
    ubuf[HALO:, :] = u
    first = HALO - (CONV_W - 1)
    for r0 in range(0, tl, rt):
        for c0 in range(0, D_MODEL, ct):
            acc = jnp.zeros((rt, ct), F32)
            for w in range(CONV_W):
                acc = acc + ubuf[r0 + first + w:r0 + first + w + rt, c0:c0 + ct] * wdw_ref[w:w + 1, c0:c0 + ct]
            cbuf[r0:r0 + rt, c0:c0 + ct] = acc + bdw_ref[:, c0:c0 + ct]
    c_ref[...] = _ln_silu(cbuf[...], lng_ref[...], lnb_ref[...]).astype(BF16)

    @pl.when(t == pl.num_programs(1) - 1)
    def _():
        ust_ref[0] = ubuf[tl:tl + HALO, :]


def _conf_prompt(p, B, LP, w_dw, b_dw, ln_g, ln_b):
    tl = 192 if LP % 192 == 0 else CHUNK
    nt = LP // tl
    hb = tl // HALO
    kern = functools.partial(_conf_prompt_kernel, tl=tl, rt=32, ct=256)
    halo_idx = lambda b, t: (jnp.maximum((b * nt + t) * hb - 1, 0), 0)
    halo_idx1 = lambda b, t: (jnp.maximum((b * nt + t) * hb - 1, 0), 1)
    vec = lambda: pl.BlockSpec((1, D_MODEL), lambda b, t: (0, 0))
    return pl.pallas_call(
        kern,
        out_shape=(jax.ShapeDtypeStruct((B * LP, D_MODEL), BF16),
                   jax.ShapeDtypeStruct((B, HALO, D_MODEL), F32)),
        grid=(B, nt),
        in_specs=[pl.BlockSpec((tl, D_MODEL), lambda b, t: (b * nt + t, 0)),
                  pl.BlockSpec((tl, D_MODEL), lambda b, t: (b * nt + t, 1)),
                  pl.BlockSpec((HALO, D_MODEL), halo_idx),
                  pl.BlockSpec((HALO, D_MODEL), halo_idx1),
                  pl.BlockSpec((CONV_W, D_MODEL), lambda b, t: (0, 0)),
                  vec(), vec(), vec()],
        out_specs=(pl.BlockSpec((tl, D_MODEL), lambda b, t: (b * nt + t, 0)),
                   pl.BlockSpec((1, HALO, D_MODEL), lambda b, t: (b, 0, 0))),
        scratch_shapes=[pltpu.VMEM((HALO + tl, D_MODEL), F32), pltpu.VMEM((tl, D_MODEL), F32)],
        compiler_params=_params(("arbitrary", "arbitrary")),
        name="conf_prompt",
    )(p, p, p, p, w_dw, b_dw, ln_g, ln_b)


def _conf_sample_kernel(st_ref, pa_ref, pb_ref, wdw_ref, bdw_ref, lng_ref, lnb_ref,
                        c_ref, nst_ref, xh, *, sb, ls):
    hist = CONV_W - 1
    for s in range(sb):
        u = pa_ref[s] * _sigmoid(pb_ref[s])
        xh[0:hist, :] = st_ref[s]
        xh[hist:hist + ls, :] = u
        acc = jnp.zeros((ls, D_MODEL), F32)
        for w in range(CONV_W):
            acc = acc + xh[w:w + ls, :] * wdw_ref[w:w + 1, :]
        c_ref[s] = _ln_silu(acc + bdw_ref[...], lng_ref[...], lnb_ref[...])
        nst_ref[s] = xh[ls:ls + hist, :]


def _conf_sample(p_s3, state, w_dw, b_dw, ln_g, ln_b):
    NB, ls, _ = p_s3.shape
    hist = CONV_W - 1
    sb = 8 if NB % 8 == 0 else 1
    kern = functools.partial(_conf_sample_kernel, sb=sb, ls=ls)
    vec = lambda: pl.BlockSpec((1, D_MODEL), lambda i: (0, 0))
    return pl.pallas_call(
        kern,
        out_shape=(jax.ShapeDtypeStruct((NB, ls, D_MODEL), F32),
                   jax.ShapeDtypeStruct((NB, hist, D_MODEL), F32)),
        grid=(NB // sb,),
        in_specs=[pl.BlockSpec((sb, hist, D_MODEL), lambda i: (i, 0, 0)),
                  pl.BlockSpec((sb, ls, D_MODEL), lambda i: (i, 0, 0)),
                  pl.BlockSpec((sb, ls, D_MODEL), lambda i: (i, 0, 1)),
                  pl.BlockSpec((CONV_W, D_MODEL), lambda i: (0, 0)),
                  vec(), vec(), vec()],
        out_specs=(pl.BlockSpec((sb, ls, D_MODEL), lambda i: (i, 0, 0)),
                   pl.BlockSpec((sb, hist, D_MODEL), lambda i: (i, 0, 0))),
        scratch_shapes=[pltpu.VMEM((hist + ls + 8, D_MODEL), F32)],
        compiler_params=_params(("arbitrary",)),
        name="conf_sample",
    )(state, p_s3, p_s3, w_dw, b_dw, ln_g, ln_b)


def _gdn_chunk(xq, xk, xv, z, ab, valid, s_ref, alog_ref, dtb_ref, nw_ref, levels):
    C = xq.shape[0]
    row = lax.broadcasted_iota(jnp.int32, (C, C), 0)
    col = lax.broadcasted_iota(jnp.int32, (C, C), 1)
    causal = row >= col
    strict = row > col
    eye = (row == col).astype(F32)
    tril = causal.astype(F32)

    xa = ab + dtb_ref[...]
    softplus = jnp.maximum(xa, 0.0) + jnp.log(1.0 + jnp.exp(-jnp.abs(xa)))
    g_all = jnp.where(valid, -jnp.exp(alog_ref[...]) * softplus, 0.0)
    beta_all = jnp.where(valid, _sigmoid(ab), 0.0)
    gc_all = _dot_hi(tril, g_all)
    gc_pad = jnp.concatenate([gc_all, jnp.zeros((LANES - C, LANES), F32)], axis=0)
    gc_t = gc_pad.T

    outs = []
    for h in range(DN_HEADS):
        sl = slice(h * DN_DK, (h + 1) * DN_DK)
        gc = gc_all[:, h:h + 1]
        gc_row = gc_t[h:h + 1, 0:C]
        beta = beta_all[:, DN_HEADS + h:DN_HEADS + h + 1]
        decay = jnp.where(causal, jnp.exp(jnp.where(causal, gc - gc_row, 0.0)), 0.0)
        q = xq[:, sl]
        k = xk[:, sl]
        q = jnp.where(valid, q * lax.rsqrt(jnp.sum(q * q, axis=-1, keepdims=True) + EPS) * (DN_DK ** -0.5), 0.0)
        k = jnp.where(valid, k * lax.rsqrt(jnp.sum(k * k, axis=-1, keepdims=True) + EPS), 0.0)
        v = jnp.where(valid, xv[:, sl], 0.0)
        kb = k * beta
        kbf = k.astype(BF16)
        m = jnp.where(strict, _dot_nt(kb.astype(BF16), kbf) * decay, 0.0)
        inv = eye - m
        mp = m
        for _ in range(levels - 1):
            mp = _dot_hi(mp, mp)
            inv = inv + _dot_hi(mp, inv)
        egc = jnp.exp(gc)
        sol = _dot_hi(inv, jnp.concatenate([v * beta, kb * egc], axis=1))
        u = sol[:, :DN_DV]
        w = sol[:, DN_DV:]
        qk = jnp.where(causal, _dot_nt(q.astype(BF16), kbf) * decay, 0.0)
        g_last = gc[C - 1:C, :]
        k_dec = k * jnp.exp(g_last - gc)
        s_old = s_ref[h]
        s_bf = s_old.astype(BF16)
        v_new = u - _dot(w.astype(BF16), s_bf)
        v_new_bf = v_new.astype(BF16)
        o = _dot((q * egc).astype(BF16), s_bf) + _dot(qk.astype(BF16), v_new_bf)
        s_ref[h] = s_old * jnp.exp(g_last) + _dot_tn(k_dec.astype(BF16), v_new_bf)
        o = o * lax.rsqrt(jnp.mean(o * o, axis=-1, keepdims=True) + EPS) * nw_ref[...]
        zh = z[:, sl]
        outs.append(o * (zh * _sigmoid(zh)))
    return outs


def _short_conv_silu(xbuf, wc_ref, rows):
    first = 8 - (SHORT_W - 1)
    acc = xbuf[first:first + rows, :] * wc_ref[0:1, :]
    for w in range(1, SHORT_W):
        acc = acc + xbuf[first + w:first + w + rows, :] * wc_ref[w:w + 1, :]
    return acc * _sigmoid(acc)


def _gdn_prompt_kernel(q_ref, k_ref, v_ref, z_ref, ab_ref, wc_ref, alog_ref, dtb_ref, nw_ref,
                       o_ref, s_ref, xbuf):
    c = pl.program_id(1)
    n_qk = DN_HEADS * DN_DK

    @pl.when(c == 0)
    def _():
        s_ref[...] = jnp.zeros_like(s_ref)
        xbuf[0:8, :] = jnp.zeros((8, xbuf.shape[1]), F32)

    xbuf[8:8 + CHUNK, 0:n_qk] = q_ref[...]
    xbuf[8:8 + CHUNK, n_qk:2 * n_qk] = k_ref[...]
    xbuf[8:8 + CHUNK, 2 * n_qk:] = v_ref[...]
    x = _short_conv_silu(xbuf, wc_ref, CHUNK)
    xbuf[0:8, :] = xbuf[CHUNK:CHUNK + 8, :]
    rows = lax.broadcasted_iota(jnp.int32, (CHUNK, 1), 0)
    valid = jnp.logical_or(rows >= FRONT, c > 0)
    outs = _gdn_chunk(x[:, 0:n_qk], x[:, n_qk:2 * n_qk], x[:, 2 * n_qk:], z_ref[...], ab_ref[...], valid,
                      s_ref.at[0], alog_ref, dtb_ref, nw_ref, levels=6)
    for h in range(DN_HEADS):
        o_ref[:, h * DN_DV:(h + 1) * DN_DV] = outs[h].astype(BF16)


def _gdn_prompt(p, ab, B, LP, w_conv, alog, dtb, nw):
    nc = LP // CHUNK
    n_qk = DN_HEADS * DN_DK
    blk = lambda col: pl.BlockSpec((CHUNK, n_qk), lambda b, c: (b * nc + c, col))
    vec = lambda n: pl.BlockSpec((1, n), lambda b, c: (0, 0))
    return pl.pallas_call(
        _gdn_prompt_kernel,
        out_shape=(jax.ShapeDtypeStruct((B * LP, n_qk), BF16),
                   jax.ShapeDtypeStruct((B, DN_HEADS, DN_DK, DN_DV), F32)),
        grid=(B, nc),
        in_specs=[blk(2), blk(3), blk(4), blk(5),
                  pl.BlockSpec((CHUNK, LANES), lambda b, c: (b * nc + c, 0)),
                  pl.BlockSpec((SHORT_W, 3 * n_qk), lambda b, c: (0, 0)),
                  vec(LANES), vec(LANES), vec(DN_DV)],
        out_specs=(pl.BlockSpec((CHUNK, n_qk), lambda b, c: (b * nc + c, 0)),
                   pl.BlockSpec((1, DN_HEADS, DN_DK, DN_DV), lambda b, c: (b, 0, 0, 0))),
        scratch_shapes=[pltpu.VMEM((CHUNK + 8, 3 * n_qk), F32)],
        compiler_params=_params(("arbitrary", "arbitrary")),
        name="gdn_prompt",
    )(p, p, p, p, ab, w_conv, alog, dtb, nw)


def _gdn_sample_kernel(st_ref, q_ref, k_ref, v_ref, z_ref, ab_ref, s0_ref, wc_ref, alog_ref, dtb_ref, nw_ref,
                       o_ref, s_ref, xbuf, zbuf, abbuf, *, ls, levels):
    n_qk = DN_HEADS * DN_DK
    C = SAMPLE_CHUNK
    hist = SHORT_W - 1
    xbuf[...] = jnp.zeros_like(xbuf)
    zbuf[...] = jnp.zeros_like(zbuf)
    abbuf[...] = jnp.zeros_like(abbuf)
    xbuf[8 - hist:8, :] = st_ref[0]
    xbuf[8:8 + ls, 0:n_qk] = q_ref[0]
    xbuf[8:8 + ls, n_qk:2 * n_qk] = k_ref[0]
    xbuf[8:8 + ls, 2 * n_qk:] = v_ref[0]
    zbuf[0:ls, :] = z_ref[0]
    abbuf[0:ls, :] = ab_ref[0]
    s_ref[...] = s0_ref[...]
    x = _short_conv_silu(xbuf, wc_ref, C)
    valid = lax.broadcasted_iota(jnp.int32, (C, 1), 0) < ls
    outs = _gdn_chunk(x[:, 0:n_qk], x[:, n_qk:2 * n_qk], x[:, 2 * n_qk:], zbuf[...], abbuf[...], valid,
                      s_ref.at[0], alog_ref, dtb_ref, nw_ref, levels=levels)
    for h in range(DN_HEADS):
        o_ref[0, :, h * DN_DV:(h + 1) * DN_DV] = outs[h][0:ls, :]


def _gdn_sample(p_s3, ab_s3, st_conv, s0, w_conv, alog, dtb, nw):
    NB, ls, _ = p_s3.shape
    n_qk = DN_HEADS * DN_DK
    hist = SHORT_W - 1
    assert ls <= SAMPLE_CHUNK
    levels = max(1, (ls - 1).bit_length())
    kern = functools.partial(_gdn_sample_kernel, ls=ls, levels=levels)
    blk = lambda col: pl.BlockSpec((1, ls, n_qk), lambda i: (i, 0, col))
    vec = lambda n: pl.BlockSpec((1, n), lambda i: (0, 0))
    sspec = lambda: pl.BlockSpec((1, DN_HEADS, DN_DK, DN_DV), lambda i: (i, 0, 0, 0))
    return pl.pallas_call(
        kern,
        out_shape=(jax.ShapeDtypeStruct((NB, ls, n_qk), F32),
                   jax.ShapeDtypeStruct((NB, DN_HEADS, DN_DK, DN_DV), F32)),
        grid=(NB,),
        in_specs=[pl.BlockSpec((1, hist, 3 * n_qk), lambda i: (i, 0, 0)),
                  blk(2), blk(3), blk(4), blk(5),
                  pl.BlockSpec((1, ls, LANES), lambda i: (i, 0, 0)),
                  sspec(),
                  pl.BlockSpec((SHORT_W, 3 * n_qk), lambda i: (0, 0)),
                  vec(LANES), vec(LANES), vec(DN_DV)],
        out_specs=(pl.BlockSpec((1, ls, n_qk), lambda i: (i, 0, 0)), sspec()),
        scratch_shapes=[pltpu.VMEM((SAMPLE_CHUNK + 8, 3 * n_qk), F32),
                        pltpu.VMEM((SAMPLE_CHUNK, n_qk), F32),
                        pltpu.VMEM((SAMPLE_CHUNK, LANES), F32)],
        compiler_params=_params(("arbitrary",)),
        name="gdn_sample",
    )(st_conv, p_s3, p_s3, p_s3, p_s3, ab_s3, s0, w_conv, alog, dtb, nw)


def _merge_kernel(c_ref, og_ref, ga_ref, gb_ref, h_ref, wco_ref, bco_ref, wdo_ref, wo_ref, nf_ref, wr_ref, br_ref,
                  h1_ref, xn_ref, lg_ref):
    ya = _dot(c_ref[...], wco_ref[...]) + bco_ref[...]
    yb = _dot(og_ref[...], wdo_ref[...])
    mixed = _sigmoid(ga_ref[...]) * ya + _sigmoid(gb_ref[...]) * yb
    h1 = h_ref[...] + _dot(mixed.astype(BF16), wo_ref[...])
    h1_ref[...] = h1
    xn = (h1 * lax.rsqrt(jnp.mean(h1 * h1, axis=-1, keepdims=True) + EPS) * nf_ref[...]).astype(BF16)
    xn_ref[...] = xn
    lg_ref[...] = _dot(xn, wr_ref[...]) + br_ref[...]


def _merge(c, og, p, h, wco, bco, wdo, wo, nf, wr, br):
    T = h.shape[0]
    tm = _row_tile(T, 512)
    row = lambda col: pl.BlockSpec((tm, D_MODEL), lambda i: (i, col))
    full = lambda a, b: pl.BlockSpec((a, b), lambda i: (0, 0))
    return pl.pallas_call(
        _merge_kernel,
        out_shape=(jax.ShapeDtypeStruct((T, D_MODEL), F32),
                   jax.ShapeDtypeStruct((T, D_MODEL), BF16),
                   jax.ShapeDtypeStruct((T, LANES), F32)),
        grid=(T // tm,),
        in_specs=[row(0), row(0), row(6), row(7), row(0),
                  full(D_MODEL, D_MODEL), full(1, D_MODEL), full(D_MODEL, D_MODEL), full(D_MODEL, D_MODEL),
                  full(1, D_MODEL), full(D_MODEL, LANES), full(1, LANES)],
        out_specs=(row(0), row(0), pl.BlockSpec((tm, LANES), lambda i: (i, 0))),
        compiler_params=_params(("arbitrary",)),
        name="merge",
    )(c, og, p, p, h, wco, bco, wdo, wo, nf, wr, br)


def _moe_kernel(be_ref, x_ref, g_ref, wu_ref, bu_ref, wd_ref, bd_ref, o_ref):
    del be_ref
    hmid = _dot(x_ref[...], wu_ref[0]) + bu_ref[0]
    hg = jnp.minimum(hmid[:, :D_FF], SWIGLU_LIMIT)
    hl = jnp.clip(hmid[:, D_FF:], -SWIGLU_LIMIT, SWIGLU_LIMIT)
    act = hg * _sigmoid(SWIGLU_ALPHA * hg) * (hl + 1.0)
    y = _dot(act.astype(BF16), wd_ref[0]) + bd_ref[0]
    o_ref[...] = y * g_ref[...]


def _moe(block_e, xb, gate_pad, w_up, b_up, w_down, b_down):
    R = xb.shape[0]
    nb = R // MOE_ROWS
    grid_spec = pltpu.PrefetchScalarGridSpec(
        num_scalar_prefetch=1,
        grid=(nb,),
        in_specs=[pl.BlockSpec((MOE_ROWS, D_MODEL), lambda i, be: (i, 0)),
                  pl.BlockSpec((MOE_ROWS, 1), lambda i, be: (i, 0)),
                  pl.BlockSpec((1, D_MODEL, 2 * D_FF), lambda i, be: (be[i], 0, 0)),
                  pl.BlockSpec((1, 1, 2 * D_FF), lambda i, be: (be[i], 0, 0)),
                  pl.BlockSpec((1, D_FF, D_MODEL), lambda i, be: (be[i], 0, 0)),
                  pl.BlockSpec((1, 1, D_MODEL), lambda i, be: (be[i], 0, 0))],
        out_specs=pl.BlockSpec((MOE_ROWS, D_MODEL), lambda i, be: (i, 0)),
    )
    return pl.pallas_call(
        _moe_kernel,
        out_shape=jax.ShapeDtypeStruct((R, D_MODEL), F32),
        grid_spec=grid_spec,
        compiler_params=_params(("arbitrary",)),
        name="moe_experts",
    )(block_e, xb, gate_pad, w_up, b_up, w_down, b_down)


def _route(logits, T):
    top_v, top_i = lax.top_k(logits, TOP_K)
    gates = jax.nn.softmax(top_v, axis=-1)
    A = T * TOP_K
    n_blocks = -(-A // MOE_ROWS) + N_EXPERTS
    R = n_blocks * MOE_ROWS
    flat_e = top_i.reshape(-1).astype(jnp.int32)
    order = jnp.argsort(flat_e)
    se = flat_e[order]
    stok = (order // TOP_K).astype(jnp.int32)
    sgate = gates.reshape(-1)[order]
    counts = jnp.bincount(flat_e, length=N_EXPERTS)
    padded = (counts + MOE_ROWS - 1) // MOE_ROWS * MOE_ROWS
    pend = jnp.cumsum(padded)
    pstart = pend - padded
    start = jnp.cumsum(counts) - counts
    dest = pstart[se] + jnp.arange(A) - start[se]
    tok_pad = jnp.full((R,), T, dtype=jnp.int32).at[dest].set(stok)
    gate_pad = jnp.zeros((R,), F32).at[dest].set(sgate)
    block_e = jnp.minimum(jnp.searchsorted(pend, jnp.arange(n_blocks) * MOE_ROWS, side='right'),
                          N_EXPERTS - 1).astype(jnp.int32)
    return tok_pad, gate_pad, block_e


def _final_kernel(h_ref, y_ref, nw_ref, o_ref):
    x = h_ref[...] + y_ref[...]
    o_ref[...] = x * lax.rsqrt(jnp.mean(x * x, axis=-1, keepdims=True) + EPS) * nw_ref[...]


def _final(h1, y, nw):
    T = h1.shape[0]
    tm = _row_tile(T, 1024)
    row = lambda: pl.BlockSpec((tm, D_MODEL), lambda i: (i, 0))
    return pl.pallas_call(
        _final_kernel,
        out_shape=jax.ShapeDtypeStruct((T, D_MODEL), F32),
        grid=(T // tm,),
        in_specs=[row(), row(), pl.BlockSpec((1, D_MODEL), lambda i: (0, 0))],
        out_specs=row(),
        compiler_params=_params(("arbitrary",)),
        name="final_norm",
    )(h1, y, nw)


def _pad_lanes(v, fill=0.0):
    v = v.reshape(1, -1).astype(F32)
    return jnp.pad(v, ((0, 0), (0, LANES - v.shape[1])), constant_values=fill)


def kernel(x_prompt, x_sample, state_conf_conv, state_dn_conv, state_dn_S, meta_tokens, norm_mix, w_in, w_conf_dw, b_conf_dw, ln_conf_g, ln_conf_b, w_conf_out, b_conf_out, w_dn_conv, dn_a_log, dn_dt_bias, dn_norm_w, w_dn_out, w_out, norm_ffn, w_router, b_router, w_up, b_up, w_down, b_down, norm_final):
    B, SEQ, D = x_prompt.shape
    NB, LS, _ = x_sample.shape
    depth = w_in.shape[0]
    assert D == D_MODEL and depth == 1 and SEQ % CHUNK == 0 and LS >= SHORT_W - 1
    LP = FRONT + N_META + SEQ
    TP = B * LP
    T = TP + NB * LS
    n_qk = DN_HEADS * DN_DK
    o_q = 2 * D_MODEL
    o_a = o_q + 4 * n_qk
    o_gate = o_a + 2 * DN_HEADS

    meta = jnp.broadcast_to(meta_tokens[None].astype(F32), (B, N_META, D))
    hp = jnp.concatenate([jnp.zeros((B, FRONT, D), F32), meta, x_prompt], axis=1).reshape(TP, D)
    h0 = jnp.concatenate([hp, x_sample.reshape(NB * LS, D)], axis=0)

    w_in0 = w_in[0]
    w_main = jnp.concatenate([w_in0[:, :o_a], w_in0[:, o_gate:]], axis=1)
    w_ab = jnp.pad(w_in0[:, o_a:o_gate], ((0, 0), (0, LANES - 2 * DN_HEADS))).astype(BF16)
    wco = w_conf_out[0].astype(BF16)
    wdo = w_dn_out[0].astype(BF16)
    wo = w_out[0].astype(BF16)
    wr = jnp.pad(w_router[0], ((0, 0), (0, LANES - N_EXPERTS))).astype(BF16)
    br = _pad_lanes(b_router[0], fill=-1e30)
    alog = _pad_lanes(dn_a_log[0])
    dtb = _pad_lanes(dn_dt_bias[0])
    row = lambda v: v.reshape(1, -1).astype(F32)

    xn, ab = _rms_ab(h0, row(norm_mix[0]), w_ab)
    p = _mm_in(xn, w_main)
    p_s3 = p[TP:].reshape(NB, LS, 8 * D_MODEL)
    ab_s3 = ab[TP:].reshape(NB, LS, LANES)

    c_p, ust_p = _conf_prompt(p, B, LP, w_conf_dw[0], row(b_conf_dw[0]), row(ln_conf_g[0]), row(ln_conf_b[0]))
    c_s, conf_state_s = _conf_sample(p_s3, state_conf_conv[0], w_conf_dw[0], row(b_conf_dw[0]),
                                     row(ln_conf_g[0]), row(ln_conf_b[0]))
    c_all = jnp.concatenate([c_p, c_s.reshape(NB * LS, D).astype(BF16)], axis=0)

    og_p, s_p = _gdn_prompt(p, ab, B, LP, w_dn_conv[0], alog, dtb, row(dn_norm_w[0]))
    og_s, s_s = _gdn_sample(p_s3, ab_s3, state_dn_conv[0], state_dn_S[0], w_dn_conv[0], alog, dtb,
                            row(dn_norm_w[0]))
    og_all = jnp.concatenate([og_p, og_s.reshape(NB * LS, n_qk).astype(BF16)], axis=0)

    h1, xn2, logits = _merge(c_all, og_all, p, h0, wco, row(b_conf_out[0]), wdo, wo, row(norm_ffn[0]), wr, br)

    tok_pad, gate_pad, block_e = _route(logits[:, :N_EXPERTS], T)
    x_ext = jnp.concatenate([xn2, jnp.zeros((1, D), BF16)], axis=0)
    xb = x_ext[tok_pad]
    yb = _moe(block_e, xb, gate_pad.reshape(-1, 1), w_up[0].astype(BF16), b_up[0].reshape(N_EXPERTS, 1, -1),
              w_down[0].astype(BF16), b_down[0].reshape(N_EXPERTS, 1, -1))
    y_moe = jax.ops.segment_sum(yb, tok_pad, num_segments=T + 1)[:T]

    y = _final(h1, y_moe, row(norm_final))

    y_prompt = y[:TP].reshape(B, LP, D)[:, FRONT + N_META:]
    y_sample = y[TP:].reshape(NB, LS, D)
    hist = CONV_W - 1
    conf_conv_prompt = ust_p[:, HALO - hist:][None]
    p_p3 = p[:TP].reshape(B, LP, 8 * D_MODEL)
    dn_conv_prompt = p_p3[:, LP - (SHORT_W - 1):, o_q:o_q + 3 * n_qk][None]
    dn_conv_sample = p_s3[:, LS - (SHORT_W - 1):, o_q:o_q + 3 * n_qk][None]
    return (y_prompt, y_sample, conf_conv_prompt, dn_conv_prompt, s_p[None],
            conf_state_s[None], dn_conv_sample, s_s[None])
```

```python
import functools

import jax
import jax.numpy as jnp
from jax import lax
from jax.experimental import pallas as pl
from jax.experimental.pallas import tpu as pltpu

D_MODEL = 1024
N_META = 16
CONV_W = 31
SHORT_W = 4
DN_HEADS = 8
DN_DK = 128
DN_DV = 128
CHUNK = 64
N_EXPERTS = 32
TOP_K = 4
D_FF = 1024
SWIGLU_LIMIT = 7.0
SWIGLU_ALPHA = 1.702
EPS = 1e-6

FRONT = (-N_META) % CHUNK
SAMPLE_CHUNK = 16
STACK = 128
LANES = 128
HALO = 32
MOE_ROWS = 256
VMEM_LIMIT = 48 * 1024 * 1024

F32 = jnp.float32
BF16 = jnp.bfloat16


def _row_tile(n, pref):
    best = 16
    for t in range(16, min(n, pref) + 1, 16):
        if n % t == 0:
            best = t
    assert n % best == 0
    return best


def _sigmoid(x):
    return 1.0 / (1.0 + jnp.exp(-x))


def _dot(a, b):
    return jnp.dot(a, b, preferred_element_type=F32)


def _dot_nt(a, b):
    return lax.dot_general(a, b, (((1,), (1,)), ((), ())), preferred_element_type=F32)


def _dot_tn(a, b):
    return lax.dot_general(a, b, (((0,), (0,)), ((), ())), preferred_element_type=F32)


def _params(sem):
    return pltpu.CompilerParams(dimension_semantics=sem, vmem_limit_bytes=VMEM_LIMIT)


def _rms_ab_kernel(h_ref, nw_ref, wab_ref, xn_ref, ab_ref):
    x = h_ref[...]
    y = x * lax.rsqrt(jnp.mean(x * x, axis=-1, keepdims=True) + EPS) * nw_ref[...]
    yb = y.astype(BF16)
    xn_ref[...] = yb
    ab_ref[...] = _dot(yb, wab_ref[...])


def _rms_ab(h, norm_w, w_ab):
    T = h.shape[0]
    tm = _row_tile(T, 1024)
    return pl.pallas_call(
        _rms_ab_kernel,
        out_shape=(jax.ShapeDtypeStruct((T, D_MODEL), BF16), jax.ShapeDtypeStruct((T, LANES), F32)),
        grid=(T // tm,),
        in_specs=[pl.BlockSpec((tm, D_MODEL), lambda i: (i, 0)),
                  pl.BlockSpec((1, D_MODEL), lambda i: (0, 0)),
                  pl.BlockSpec((D_MODEL, LANES), lambda i: (0, 0))],
        out_specs=(pl.BlockSpec((tm, D_MODEL), lambda i: (i, 0)),
                   pl.BlockSpec((tm, LANES), lambda i: (i, 0))),
        compiler_params=_params(("arbitrary",)),
        name="rms_ab",
    )(h, norm_w, w_ab)


def _mm_in_kernel(x_ref, w_ref, o_ref, wb_ref):
    @pl.when(pl.program_id(1) == 0)
    def _():
        wb_ref[...] = w_ref[...].astype(BF16)

    o_ref[...] = _dot(x_ref[...], wb_ref[...])


def _mm_in(xn, w):
    T, K = xn.shape
    N = w.shape[1]
    tm = _row_tile(T, 1024)
    tn = 1024
    assert N % tn == 0
    return pl.pallas_call(
        _mm_in_kernel,
        out_shape=jax.ShapeDtypeStruct((T, N), F32),
        grid=(N // tn, T // tm),
        in_specs=[pl.BlockSpec((tm, K), lambda j, i: (i, 0)),
                  pl.BlockSpec((K, tn), lambda j, i: (0, j))],
        out_specs=pl.BlockSpec((tm, tn), lambda j, i: (i, j)),
        scratch_shapes=[pltpu.VMEM((K, tn), BF16)],
        compiler_params=_params(("arbitrary", "arbitrary")),
        name="in_proj",
    )(xn, w)


def _ln_silu(x, g, b):
    mu = jnp.mean(x, axis=-1, keepdims=True)
    xc = x - mu
    var = jnp.mean(xc * xc, axis=-1, keepdims=True)
    y = xc * lax.rsqrt(var + EPS) * g + b
    return y * _sigmoid(y)


def _conf_prompt_kernel(pa_ref, pb_ref, ha_ref, hb_ref, wdw_ref, bdw_ref, lng_ref, lnb_ref,
                        c_ref, ust_ref, ubuf, cbuf, *, tl, rt, ct):
    t = pl.program_id(1)
    u = pa_ref[...] * _sigmoid(pb_ref[...])
    uh = ha_ref[...] * _sigmoid(hb_ref[...])
    ubuf[0:HALO, :] = jnp.where(t > 0, uh, 0.0)
    ubuf[HALO:, :] = u
    first = HALO - (CONV_W - 1)
    for r0 in range(0, tl, rt):
        for c0 in range(0, D_MODEL, ct):
            acc = jnp.zeros((rt, ct), F32)
            for w in range(CONV_W):
                acc = acc + ubuf[r0 + first + w:r0 + first + w + rt, c0:c0 + ct] * wdw_ref[w:w + 1, c0:c0 + ct]
            cbuf[r0:r0 + rt, c0:c0 + ct] = acc + bdw_ref[:, c0:c0 + ct]
    c_ref[...] = _ln_silu(cbuf[...], lng_ref[...], lnb_ref[...]).astype(BF16)

    @pl.when(t == pl.num_programs(1) - 1)
    def _():
        ust_ref[0] = ubuf[tl:tl + HALO, :]


def _conf_prompt(p, B, LP, w_dw, b_dw, ln_g, ln_b):
    tl = 192 if LP % 192 == 0 else CHUNK
    nt = LP // tl
    hb = tl // HALO
    kern = functools.partial(_conf_prompt_kernel, tl=tl, rt=32, ct=256)
    halo_idx = lambda b, t: (jnp.maximum((b * nt + t) * hb - 1, 0), 0)
    halo_idx1 = lambda b, t: (jnp.maximum((b * nt + t) * hb - 1, 0), 1)
    vec = lambda: pl.BlockSpec((1, D_MODEL), lambda b, t: (0, 0))
    return pl.pallas_call(
        kern,
        out_shape=(jax.ShapeDtypeStruct((B * LP, D_MODEL), BF16),
                   jax.ShapeDtypeStruct((B, HALO, D_MODEL), F32)),
        grid=(B, nt),
        in_specs=[pl.BlockSpec((tl, D_MODEL), lambda b, t: (b * nt + t, 0)),
                  pl.BlockSpec((tl, D_MODEL), lambda b, t: (b * nt + t, 1)),
                  pl.BlockSpec((HALO, D_MODEL), halo_idx),
                  pl.BlockSpec((HALO, D_MODEL), halo_idx1),
                  pl.BlockSpec((CONV_W, D_MODEL), lambda b, t: (0, 0)),
                  vec(), vec(), vec()],
        out_specs=(pl.BlockSpec((tl, D_MODEL), lambda b, t: (b * nt + t, 0)),
                   pl.BlockSpec((1, HALO, D_MODEL), lambda b, t: (b, 0, 0))),
        scratch_shapes=[pltpu.VMEM((HALO + tl, D_MODEL), F32), pltpu.VMEM((tl, D_MODEL), F32)],
        compiler_params=_params(("arbitrary", "arbitrary")),
        name="conf_prompt",
    )(p, p, p, p, w_dw, b_dw, ln_g, ln_b)


def _conf_sample_kernel(st_ref, pa_ref, pb_ref, wdw_ref, bdw_ref, lng_ref, lnb_ref,
                        c_ref, nst_ref, xh, *, sb, ls):
    hist = CONV_W - 1
    for s in range(sb):
        u = pa_ref[s] * _sigmoid(pb_ref[s])
        xh[0:hist, :] = st_ref[s]
        xh[hist:hist + ls, :] = u
        acc = jnp.zeros((ls, D_MODEL), F32)
        for w in range(CONV_W):
            acc = acc + xh[w:w + ls, :] * wdw_ref[w:w + 1, :]
        c_ref[s] = _ln_silu(acc + bdw_ref[...], lng_ref[...], lnb_ref[...])
        nst_ref[s] = xh[ls:ls + hist, :]


def _conf_sample(p_s3, state, w_dw, b_dw, ln_g, ln_b):
    NB, ls, _ = p_s3.shape
    hist = CONV_W - 1
    sb = 8 if NB % 8 == 0 else 1
    kern = functools.partial(_conf_sample_kernel, sb=sb, ls=ls)
    vec = lambda: pl.BlockSpec((1, D_MODEL), lambda i: (0, 0))
    return pl.pallas_call(
        kern,
        out_shape=(jax.ShapeDtypeStruct((NB, ls, D_MODEL), F32),
                   jax.ShapeDtypeStruct((NB, hist, D_MODEL), F32)),
        grid=(NB // sb,),
        in_specs=[pl.BlockSpec((sb, hist, D_MODEL), lambda i: (i, 0, 0)),
                  pl.BlockSpec((sb, ls, D_MODEL), lambda i: (i, 0, 0)),
                  pl.BlockSpec((sb, ls, D_MODEL), lambda i: (i, 0, 1)),
                  pl.BlockSpec((CONV_W, D_MODEL), lambda i: (0, 0)),
                  vec(), vec(), vec()],
        out_specs=(pl.BlockSpec((sb, ls, D_MODEL), lambda i: (i, 0, 0)),
                   pl.BlockSpec((sb, hist, D_MODEL), lambda i: (i, 0, 0))),
        scratch_shapes=[pltpu.VMEM((hist + ls + 8, D_MODEL), F32)],
        compiler_params=_params(("arbitrary",)),
        name="conf_sample",
    )(state, p_s3, p_s3, w_dw, b_dw, ln_g, ln_b)


def _split(a):
    hi = a.astype(BF16)
    return hi, (a - hi.astype(F32)).astype(BF16)


def _mm3(a, b):
    ah, al = a
    bh, bl = b
    return _dot(jnp.concatenate([ah, al, ah], axis=1), jnp.concatenate([bh, bh, bl], axis=0))


def _tri_inverse(ms, i, j, C, nil):
    same = lambda n: (i >> (n.bit_length() - 1)) == (j >> (n.bit_length() - 1))
    base = min(16, C)
    eye = (i == j).astype(F32)
    dps = [jnp.where(same(base), m, 0.0) for m in ms]
    xs = [eye - d for d in dps]
    for _ in range(max(0, (min(base, nil) - 1).bit_length() - 1)):
        sp = [_split(d) for d in dps]
        dps = [_mm3(s, s) for s in sp]
        xs = [x + _mm3(_split(d), _split(x)) for d, x in zip(dps, xs)]
    blk = base
    while blk < C:
        sel = jnp.logical_and(same(2 * blk), jnp.logical_not(same(blk)))
        xsp = [_split(x) for x in xs]
        ys = [_mm3(_split(jnp.where(sel, m, 0.0)), x) for m, x in zip(ms, xsp)]
        xs = [x - _mm3(xp, _split(y)) for x, xp, y in zip(xs, xsp, ys)]
        blk *= 2
    return xs


def _gdn_chunk(xq, xk, xv, z, ab, valid, s_ref, alog_ref, dtb_ref, nw_ref, nil):
    C = xq.shape[0]
    G = STACK // C
    ok = valid > 0.5
    ri = lax.broadcasted_iota(jnp.int32, (C, C), 0)
    ci = lax.broadcasted_iota(jnp.int32, (C, C), 1)
    tril = (ri >= ci).astype(BF16)
    i = lax.broadcasted_iota(jnp.int32, (STACK, STACK), 0)
    j = lax.broadcasted_iota(jnp.int32, (STACK, STACK), 1)
    shift = C.bit_length() - 1
    same = (i >> shift) == (j >> shift)
    causal = jnp.logical_and(same, i >= j)
    strict = jnp.logical_and(same, i > j)

    xa = ab + dtb_ref[...]
    softplus = jnp.maximum(xa, 0.0) + jnp.log(1.0 + jnp.exp(-jnp.abs(xa)))
    g_all = jnp.where(ok, -jnp.exp(alog_ref[...]) * softplus, 0.0)
    beta_all = jnp.where(ok, _sigmoid(ab), 0.0)
    g1 = g_all.astype(BF16)
    r1 = g_all - g1.astype(F32)
    g2 = r1.astype(BF16)
    g3 = (r1 - g2.astype(F32)).astype(BF16)
    gc_all = _dot(tril, g1) + _dot(tril, g2) + _dot(tril, g3)

    ok_st = jnp.concatenate([valid] * G, axis=0) > 0.5
    stacks = [list(range(h0, h0 + G)) for h0 in range(0, DN_HEADS, G)]
    pre = []
    for heads in stacks:
        stack = lambda x: jnp.concatenate([x[:, h * DN_DK:(h + 1) * DN_DK] for h in heads], axis=0)
        col = lambda a, off: jnp.concatenate([a[:, off + h:off + h + 1] for h in heads], axis=0)
        q = stack(xq)
        k = stack(xk)
        q = jnp.where(ok_st, q * lax.rsqrt(jnp.sum(q * q, axis=-1, keepdims=True) + EPS) * (DN_DK ** -0.5), 0.0)
        k = jnp.where(ok_st, k * lax.rsqrt(jnp.sum(k * k, axis=-1, keepdims=True) + EPS), 0.0)
        v = jnp.where(ok_st, stack(xv), 0.0)
        gc = col(gc_all, 0)
        beta = col(beta_all, DN_HEADS)
        g_last = jnp.concatenate([jnp.broadcast_to(gc_all[C - 1:C, h:h + 1], (C, 1)) for h in heads], axis=0)
        gb = jnp.broadcast_to(gc, (STACK, STACK))
        decay = jnp.where(causal, jnp.exp(jnp.where(causal, gb - gb.T, 0.0)), 0.0)
        egc = jnp.exp(gc)
        kb = k * beta
        pre.append(dict(q=q, k=k, kb=kb, kbf=k.astype(BF16), decay=decay, egc=egc,
                        rhs=jnp.concatenate([v * beta, kb * egc], axis=1),
                        k_dec=(k * jnp.exp(g_last - gc)).astype(BF16), zs=stack(z)))
    ms = [jnp.where(strict, _dot_nt(p["kb"].astype(BF16), p["kbf"]) * p["decay"], 0.0) for p in pre]
    qks = [jnp.where(causal, _dot_nt(p["q"].astype(BF16), p["kbf"]) * p["decay"], 0.0).astype(BF16) for p in pre]
    invs = _tri_inverse(ms, i, j, C, nil)
    sols = [_mm3(_split(inv), _split(p["rhs"])) for inv, p in zip(invs, pre)]
    wss = []
    for heads, p, sol in zip(stacks, pre, sols):
        w = sol[:, DN_DV:].astype(BF16)
        q_dec = (p["q"] * p["egc"]).astype(BF16)
        wss.append([_dot(jnp.concatenate([w[g * C:(g + 1) * C], q_dec[g * C:(g + 1) * C]], axis=0),
                         s_ref[h].astype(BF16)) for g, h in enumerate(heads)])
    outs = [None] * DN_HEADS
    for heads, p, sol, ws, qk in zip(stacks, pre, sols, wss, qks):
        v_new = [(sol[g * C:(g + 1) * C, :DN_DV] - ws[g][:C]).astype(BF16) for g in range(G)]
        for g, h in enumerate(heads):
            s_ref[h] = (s_ref[h] * jnp.exp(gc_all[C - 1:C, h:h + 1])
                        + _dot_tn(p["k_dec"][g * C:(g + 1) * C], v_new[g]))
        o = jnp.concatenate([w[C:] for w in ws], axis=0) + _dot(qk, jnp.concatenate(v_new, axis=0))
        o = o * lax.rsqrt(jnp.mean(o * o, axis=-1, keepdims=True) + EPS) * nw_ref[...]
        og = o * (p["zs"] * _sigmoid(p["zs"]))
        for g, h in enumerate(heads):
            outs[h] = og[g * C:(g + 1) * C]
    return outs


def _short_conv_silu(xbuf, wc_ref, rows):
    first = 8 - (SHORT_W - 1)
    acc = xbuf[first:first + rows, :] * wc_ref[0:1, :]
    for w in range(1, SHORT_W):
        acc = acc + xbuf[first + w:first + w + rows, :] * wc_ref[w:w + 1, :]
    return acc * _sigmoid(acc)


def _gdn_prompt_kernel(q_ref, k_ref, v_ref, z_ref, ab_ref, wc_ref, alog_ref, dtb_ref, nw_ref,
                       o_ref, s_ref, xbuf):
    c = pl.program_id(1)
    n_qk = DN_HEADS * DN_DK

    @pl.when(c == 0)
    def _():
        s_ref[...] = jnp.zeros_like(s_ref)
        xbuf[0:8, :] = jnp.zeros((8, xbuf.shape[1]), F32)

    xbuf[8:8 + CHUNK, 0:n_qk] = q_ref[...]
    xbuf[8:8 + CHUNK, n_qk:2 * n_qk] = k_ref[...]
    xbuf[8:8 + CHUNK, 2 * n_qk:] = v_ref[...]
    x = _short_conv_silu(xbuf, wc_ref, CHUNK)
    xbuf[0:8, :] = xbuf[CHUNK:CHUNK + 8, :]
    rows = lax.broadcasted_iota(jnp.int32, (CHUNK, 1), 0)
    valid = jnp.logical_or(rows >= FRONT, c > 0).astype(F32)
    outs = _gdn_chunk(x[:, 0:n_qk], x[:, n_qk:2 * n_qk], x[:, 2 * n_qk:], z_ref[...], ab_ref[...], valid,
                      s_ref.at[0], alog_ref, dtb_ref, nw_ref, nil=CHUNK)
    for h in range(DN_HEADS):
        o_ref[:, h * DN_DV:(h + 1) * DN_DV] = outs[h].astype(BF16)


def _gdn_prompt(p, ab, B, LP, w_conv, alog, dtb, nw):
    nc = LP // CHUNK
    n_qk = DN_HEADS * DN_DK
    blk = lambda col: pl.BlockSpec((CHUNK, n_qk), lambda b, c: (b * nc + c, col))
    vec = lambda n: pl.BlockSpec((1, n), lambda b, c: (0, 0))
    return pl.pallas_call(
        _gdn_prompt_kernel,
        out_shape=(jax.ShapeDtypeStruct((B * LP, n_qk), BF16),
                   jax.ShapeDtypeStruct((B, DN_HEADS, DN_DK, DN_DV), F32)),
        grid=(B, nc),
        in_specs=[blk(2), blk(3), blk(4), blk(5),
                  pl.BlockSpec((CHUNK, LANES), lambda b, c: (b * nc + c, 0)),
                  pl.BlockSpec((SHORT_W, 3 * n_qk), lambda b, c: (0, 0)),
                  vec(LANES), vec(LANES), vec(DN_DV)],
        out_specs=(pl.BlockSpec((CHUNK, n_qk), lambda b, c: (b * nc + c, 0)),
                   pl.BlockSpec((1, DN_HEADS, DN_DK, DN_DV), lambda b, c: (b, 0, 0, 0))),
        scratch_shapes=[pltpu.VMEM((CHUNK + 8, 3 * n_qk), F32)],
        compiler_params=_params(("arbitrary", "arbitrary")),
        name="gdn_prompt",
    )(p, p, p, p, ab, w_conv, alog, dtb, nw)


def _gdn_sample_kernel(st_ref, q_ref, k_ref, v_ref, z_ref, ab_ref, s0_ref, wc_ref, alog_ref, dtb_ref, nw_ref,
                       o_ref, s_ref, xbuf, zbuf, abbuf, *, sb, ls):
    n_qk = DN_HEADS * DN_DK
    C = SAMPLE_CHUNK
    hist = SHORT_W - 1
    xbuf[...] = jnp.zeros_like(xbuf)
    zbuf[...] = jnp.zeros_like(zbuf)
    abbuf[...] = jnp.zeros_like(abbuf)
    s_ref[...] = s0_ref[...]
    valid = (lax.broadcasted_iota(jnp.int32, (C, 1), 0) < ls).astype(F32)
    for s in range(sb):
        xb = xbuf.at[s]
        xb[8 - hist:8, :] = st_ref[s]
        xb[8:8 + ls, 0:n_qk] = q_ref[s]
        xb[8:8 + ls, n_qk:2 * n_qk] = k_ref[s]
        xb[8:8 + ls, 2 * n_qk:] = v_ref[s]
        zbuf[s, 0:ls, :] = z_ref[s]
        abbuf[s, 0:ls, :] = ab_ref[s]
        x = _short_conv_silu(xb, wc_ref, C)
        outs = _gdn_chunk(x[:, 0:n_qk], x[:, n_qk:2 * n_qk], x[:, 2 * n_qk:], zbuf[s], abbuf[s], valid,
                          s_ref.at[s], alog_ref, dtb_ref, nw_ref, nil=ls)
        for h in range(DN_HEADS):
            o_ref[s, :, h * DN_DV:(h + 1) * DN_DV] = outs[h][0:ls, :]


def _gdn_sample(p_s3, ab_s3, st_conv, s0, w_conv, alog, dtb, nw):
    NB, ls, _ = p_s3.shape
    n_qk = DN_HEADS * DN_DK
    hist = SHORT_W - 1
    assert ls <= SAMPLE_CHUNK
    sb = 4 if NB % 4 == 0 else 1
    kern = functools.partial(_gdn_sample_kernel, sb=sb, ls=ls)
    blk = lambda col: pl.BlockSpec((sb, ls, n_qk), lambda i: (i, 0, col))
    vec = lambda n: pl.BlockSpec((1, n), lambda i: (0, 0))
    sspec = lambda: pl.BlockSpec((sb, DN_HEADS, DN_DK, DN_DV), lambda i: (i, 0, 0, 0))
    return pl.pallas_call(
        kern,
        out_shape=(jax.ShapeDtypeStruct((NB, ls, n_qk), F32),
                   jax.ShapeDtypeStruct((NB, DN_HEADS, DN_DK, DN_DV), F32)),
        grid=(NB // sb,),
        in_specs=[pl.BlockSpec((sb, hist, 3 * n_qk), lambda i: (i, 0, 0)),
                  blk(2), blk(3), blk(4), blk(5),
                  pl.BlockSpec((sb, ls, LANES), lambda i: (i, 0, 0)),
                  sspec(),
                  pl.BlockSpec((SHORT_W, 3 * n_qk), lambda i: (0, 0)),
                  vec(LANES), vec(LANES), vec(DN_DV)],
        out_specs=(pl.BlockSpec((sb, ls, n_qk), lambda i: (i, 0, 0)), sspec()),
        scratch_shapes=[pltpu.VMEM((sb, SAMPLE_CHUNK + 8, 3 * n_qk), F32),
                        pltpu.VMEM((sb, SAMPLE_CHUNK, n_qk), F32),
                        pltpu.VMEM((sb, SAMPLE_CHUNK, LANES), F32)],
        compiler_params=_params(("arbitrary",)),
        name="gdn_sample",
    )(st_conv, p_s3, p_s3, p_s3, p_s3, ab_s3, s0, w_conv, alog, dtb, nw)


def _merge_kernel(c_ref, og_ref, ga_ref, gb_ref, h_ref, wco_ref, bco_ref, wdo_ref, wo_ref, nf_ref, wr_ref, br_ref,
                  h1_ref, xn_ref, lg_ref):
    ya = _dot(c_ref[...], wco_ref[...]) + bco_ref[...]
    yb = _dot(og_ref[...], wdo_ref[...])
    mixed = _sigmoid(ga_ref[...]) * ya + _sigmoid(gb_ref[...]) * yb
    h1 = h_ref[...] + _dot(mixed.astype(BF16), wo_ref[...])
    h1_ref[...] = h1
    xn = (h1 * lax.rsqrt(jnp.mean(h1 * h1, axis=-1, keepdims=True) + EPS) * nf_ref[...]).astype(BF16)
    xn_ref[...] = xn
    lg_ref[...] = _dot(xn, wr_ref[...]) + br_ref[...]


def _merge(c, og, p, h, wco, bco, wdo, wo, nf, wr, br):
    T = h.shape[0]
    tm = _row_tile(T, 512)
    row = lambda col: pl.BlockSpec((tm, D_MODEL), lambda i: (i, col))
    full = lambda a, b: pl.BlockSpec((a, b), lambda i: (0, 0))
    return pl.pallas_call(
        _merge_kernel,
        out_shape=(jax.ShapeDtypeStruct((T, D_MODEL), F32),
                   jax.ShapeDtypeStruct((T, D_MODEL), BF16),
                   jax.ShapeDtypeStruct((T, LANES), F32)),
        grid=(T // tm,),
        in_specs=[row(0), row(0), row(6), row(7), row(0),
                  full(D_MODEL, D_MODEL), full(1, D_MODEL), full(D_MODEL, D_MODEL), full(D_MODEL, D_MODEL),
                  full(1, D_MODEL), full(D_MODEL, LANES), full(1, LANES)],
        out_specs=(row(0), row(0), pl.BlockSpec((tm, LANES), lambda i: (i, 0))),
        compiler_params=_params(("arbitrary",)),
        name="merge",
    )(c, og, p, p, h, wco, bco, wdo, wo, nf, wr, br)


def _moe_kernel(be_ref, x_ref, g_ref, wu_ref, bu_ref, wd_ref, bd_ref, o_ref):
    del be_ref
    hmid = _dot(x_ref[...], wu_ref[0]) + bu_ref[0]
    hg = jnp.minimum(hmid[:, :D_FF], SWIGLU_LIMIT)
    hl = jnp.clip(hmid[:, D_FF:], -SWIGLU_LIMIT, SWIGLU_LIMIT)
    act = hg * _sigmoid(SWIGLU_ALPHA * hg) * (hl + 1.0)
    y = _dot(act.astype(BF16), wd_ref[0]) + bd_ref[0]
    o_ref[...] = y * g_ref[...]


def _moe(block_e, xb, gate_pad, w_up, b_up, w_down, b_down):
    R = xb.shape[0]
    nb = R // MOE_ROWS
    grid_spec = pltpu.PrefetchScalarGridSpec(
        num_scalar_prefetch=1,
        grid=(nb,),
        in_specs=[pl.BlockSpec((MOE_ROWS, D_MODEL), lambda i, be: (i, 0)),
                  pl.BlockSpec((MOE_ROWS, 1), lambda i, be: (i, 0)),
                  pl.BlockSpec((1, D_MODEL, 2 * D_FF), lambda i, be: (be[i], 0, 0)),
                  pl.BlockSpec((1, 1, 2 * D_FF), lambda i, be: (be[i], 0, 0)),
                  pl.BlockSpec((1, D_FF, D_MODEL), lambda i, be: (be[i], 0, 0)),
                  pl.BlockSpec((1, 1, D_MODEL), lambda i, be: (be[i], 0, 0))],
        out_specs=pl.BlockSpec((MOE_ROWS, D_MODEL), lambda i, be: (i, 0)),
    )
    return pl.pallas_call(
        _moe_kernel,
        out_shape=jax.ShapeDtypeStruct((R, D_MODEL), F32),
        grid_spec=grid_spec,
        compiler_params=_params(("arbitrary",)),
        name="moe_experts",
    )(block_e, xb, gate_pad, w_up, b_up, w_down, b_down)


def _route(logits, T):
    top_v, top_i = lax.top_k(logits, TOP_K)
    gates = jax.nn.softmax(top_v, axis=-1)
    A = T * TOP_K
    n_blocks = -(-A // MOE_ROWS) + N_EXPERTS
    R = n_blocks * MOE_ROWS
    flat_e = top_i.reshape(-1).astype(jnp.int32)
    order = jnp.argsort(flat_e)
    se = flat_e[order]
    stok = (order // TOP_K).astype(jnp.int32)
    sgate = gates.reshape(-1)[order]
    counts = jnp.bincount(flat_e, length=N_EXPERTS)
    padded = (counts + MOE_ROWS - 1) // MOE_ROWS * MOE_ROWS
    pend = jnp.cumsum(padded)
    pstart = pend - padded
    start = jnp.cumsum(counts) - counts
    dest = pstart[se] + jnp.arange(A) - start[se]
    tok_pad = jnp.full((R,), T, dtype=jnp.int32).at[dest].set(stok)
    gate_pad = jnp.zeros((R,), F32).at[dest].set(sgate)
    block_e = jnp.minimum(jnp.searchsorted(pend, jnp.arange(n_blocks) * MOE_ROWS, side='right'),
                          N_EXPERTS - 1).astype(jnp.int32)
    return tok_pad, gate_pad, block_e


def _final_kernel(h_ref, y_ref, nw_ref, o_ref):
    x = h_ref[...] + y_ref[...]
    o_ref[...] = x * lax.rsqrt(jnp.mean(x * x, axis=-1, keepdims=True) + EPS) * nw_ref[...]


def _final(h1, y, nw):
    T = h1.shape[0]
    tm = _row_tile(T, 1024)
    row = lambda: pl.BlockSpec((tm, D_MODEL), lambda i: (i, 0))
    return pl.pallas_call(
        _final_kernel,
        out_shape=jax.ShapeDtypeStruct((T, D_MODEL), F32),
        grid=(T // tm,),
        in_specs=[row(), row(), pl.BlockSpec((1, D_MODEL), lambda i: (0, 0))],
        out_specs=row(),
        compiler_params=_params(("arbitrary",)),
        name="final_norm",
    )(h1, y, nw)


def _pad_lanes(v, fill=0.0):
    v = v.reshape(1, -1).astype(F32)
    return jnp.pad(v, ((0, 0), (0, LANES - v.shape[1])), constant_values=fill)


def kernel(x_prompt, x_sample, state_conf_conv, state_dn_conv, state_dn_S, meta_tokens, norm_mix, w_in, w_conf_dw, b_conf_dw, ln_conf_g, ln_conf_b, w_conf_out, b_conf_out, w_dn_conv, dn_a_log, dn_dt_bias, dn_norm_w, w_dn_out, w_out, norm_ffn, w_router, b_router, w_up, b_up, w_down, b_down, norm_final):
    B, SEQ, D = x_prompt.shape
    NB, LS, _ = x_sample.shape
    depth = w_in.shape[0]
    assert D == D_MODEL and depth == 1 and SEQ % CHUNK == 0 and LS >= SHORT_W - 1
    LP = FRONT + N_META + SEQ
    TP = B * LP
    T = TP + NB * LS
    n_qk = DN_HEADS * DN_DK
    o_q = 2 * D_MODEL
    o_a = o_q + 4 * n_qk
    o_gate = o_a + 2 * DN_HEADS

    meta = jnp.broadcast_to(meta_tokens[None].astype(F32), (B, N_META, D))
    hp = jnp.concatenate([jnp.zeros((B, FRONT, D), F32), meta, x_prompt], axis=1).reshape(TP, D)
    h0 = jnp.concatenate([hp, x_sample.reshape(NB * LS, D)], axis=0)

    w_in0 = w_in[0]
    w_main = jnp.concatenate([w_in0[:, :o_a], w_in0[:, o_gate:]], axis=1)
    w_ab = jnp.pad(w_in0[:, o_a:o_gate], ((0, 0), (0, LANES - 2 * DN_HEADS))).astype(BF16)
    wco = w_conf_out[0].astype(BF16)
    wdo = w_dn_out[0].astype(BF16)
    wo = w_out[0].astype(BF16)
    wr = jnp.pad(w_router[0], ((0, 0), (0, LANES - N_EXPERTS))).astype(BF16)
    br = _pad_lanes(b_router[0], fill=-1e30)
    alog = _pad_lanes(dn_a_log[0])
    dtb = _pad_lanes(dn_dt_bias[0])
    row = lambda v: v.reshape(1, -1).astype(F32)

    xn, ab = _rms_ab(h0, row(norm_mix[0]), w_ab)
    p = _mm_in(xn, w_main)
    p_s3 = p[TP:].reshape(NB, LS, 8 * D_MODEL)
    ab_s3 = ab[TP:].reshape(NB, LS, LANES)

    c_p, ust_p = _conf_prompt(p, B, LP, w_conf_dw[0], row(b_conf_dw[0]), row(ln_conf_g[0]), row(ln_conf_b[0]))
    c_s, conf_state_s = _conf_sample(p_s3, state_conf_conv[0], w_conf_dw[0], row(b_conf_dw[0]),
                                     row(ln_conf_g[0]), row(ln_conf_b[0]))
    c_all = jnp.concatenate([c_p, c_s.reshape(NB * LS, D).astype(BF16)], axis=0)

    og_p, s_p = _gdn_prompt(p, ab, B, LP, w_dn_conv[0], alog, dtb, row(dn_norm_w[0]))
    og_s, s_s = _gdn_sample(p_s3, ab_s3, state_dn_conv[0], state_dn_S[0], w_dn_conv[0], alog, dtb,
                            row(dn_norm_w[0]))
    og_all = jnp.concatenate([og_p, og_s.reshape(NB * LS, n_qk).astype(BF16)], axis=0)

    h1, xn2, logits = _merge(c_all, og_all, p, h0, wco, row(b_conf_out[0]), wdo, wo, row(norm_ffn[0]), wr, br)

    tok_pad, gate_pad, block_e = _route(logits[:, :N_EXPERTS], T)
    x_ext = jnp.concatenate([xn2, jnp.zeros((1, D), BF16)], axis=0)
    xb = x_ext[tok_pad]
    yb = _moe(block_e, xb, gate_pad.reshape(-1, 1), w_up[0].astype(BF16), b_up[0].reshape(N_EXPERTS, 1, -1),
              w_down[0].astype(BF16), b_down[0].reshape(N_EXPERTS, 1, -1))
    y_moe = jax.ops.segment_sum(yb, tok_pad, num_segments=T + 1)[:T]

    y = _final(h1, y_moe, row(norm_final))

    y_prompt = y[:TP].reshape(B, LP, D)[:, FRONT + N_META:]
    y_sample = y[TP:].reshape(NB, LS, D)
    hist = CONV_W - 1
    conf_conv_prompt = ust_p[:, HALO - hist:][None]
    p_p3 = p[:TP].reshape(B, LP, 8 * D_MODEL)
    dn_conv_prompt = p_p3[:, LP - (SHORT_W - 1):, o_q:o_q + 3 * n_qk][None]
    dn_conv_sample = p_s3[:, LS - (SHORT_W - 1):, o_q:o_q + 3 * n_qk][None]
    return (y_prompt, y_sample, conf_conv_prompt, dn_conv_prompt, s_p[None],
            conf_state_s[None], dn_conv_sample, s_s[None])
```

```python
import functools

import jax
import jax.numpy as jnp
from jax import lax
from jax.experimental import pallas as pl
from jax.experimental.pallas import tpu as pltpu

D_MODEL = 1024
N_META = 16
CONV_W = 31
SHORT_W = 4
DN_HEADS = 8
DN_DK = 128
DN_DV = 128
CHUNK = 64
N_EXPERTS = 32
TOP_K = 4
D_FF = 1024
SWIGLU_LIMIT = 7.0
SWIGLU_ALPHA = 1.702
EPS = 1e-6

FRONT = (-N_META) % CHUNK
SAMPLE_CHUNK = 16
STACK = 128
LANES = 128
HALO = 32
MOE_ROWS = 256
VMEM_LIMIT = 48 * 1024 * 1024

F32 = jnp.float32
BF16 = jnp.bfloat16


def _row_tile(n, pref):
    best = 16
    for t in range(16, min(n, pref) + 1, 16):
        if n % t == 0:
            best = t
    assert n % best == 0
    return best


def _sigmoid(x):
    return 1.0 / (1.0 + jnp.exp(-x))


def _dot(a, b):
    return jnp.dot(a, b, preferred_element_type=F32)


def _dot_nt(a, b):
    return lax.dot_general(a, b, (((1,), (1,)), ((), ())), preferred_element_type=F32)


def _dot_tn(a, b):
    return lax.dot_general(a, b, (((0,), (0,)), ((), ())), preferred_element_type=F32)


def _params(sem):
    return pltpu.CompilerParams(dimension_semantics=sem, vmem_limit_bytes=VMEM_LIMIT)


def _rms_ab_kernel(h_ref, nw_ref, wab_ref, xn_ref, ab_ref):
    x = h_ref[...]
    y = x * lax.rsqrt(jnp.mean(x * x, axis=-1, keepdims=True) + EPS) * nw_ref[...]
    yb = y.astype(BF16)
    xn_ref[...] = yb
    ab_ref[...] = _dot(yb, wab_ref[...])


def _rms_ab(h, norm_w, w_ab):
    T = h.shape[0]
    tm = _row_tile(T, 1024)
    return pl.pallas_call(
        _rms_ab_kernel,
        out_shape=(jax.ShapeDtypeStruct((T, D_MODEL), BF16), jax.ShapeDtypeStruct((T, LANES), F32)),
        grid=(T // tm,),
        in_specs=[pl.BlockSpec((tm, D_MODEL), lambda i: (i, 0)),
                  pl.BlockSpec((1, D_MODEL), lambda i: (0, 0)),
                  pl.BlockSpec((D_MODEL, LANES), lambda i: (0, 0))],
        out_specs=(pl.BlockSpec((tm, D_MODEL), lambda i: (i, 0)),
                   pl.BlockSpec((tm, LANES), lambda i: (i, 0))),
        compiler_params=_params(("arbitrary",)),
        name="rms_ab",
    )(h, norm_w, w_ab)


def _mm_in_kernel(x_ref, w_ref, o_ref, wb_ref):
    @pl.when(pl.program_id(1) == 0)
    def _():
        wb_ref[...] = w_ref[...].astype(BF16)

    o_ref[...] = _dot(x_ref[...], wb_ref[...])


def _mm_in(xn, w):
    T, K = xn.shape
    N = w.shape[1]
    tm = _row_tile(T, 1024)
    tn = 1024
    assert N % tn == 0
    return pl.pallas_call(
        _mm_in_kernel,
        out_shape=jax.ShapeDtypeStruct((T, N), F32),
        grid=(N // tn, T // tm),
        in_specs=[pl.BlockSpec((tm, K), lambda j, i: (i, 0)),
                  pl.BlockSpec((K, tn), lambda j, i: (0, j))],
        out_specs=pl.BlockSpec((tm, tn), lambda j, i: (i, j)),
        scratch_shapes=[pltpu.VMEM((K, tn), BF16)],
        compiler_params=_params(("arbitrary", "arbitrary")),
        name="in_proj",
    )(xn, w)


def _ln_silu(x, g, b):
    mu = jnp.mean(x, axis=-1, keepdims=True)
    xc = x - mu
    var = jnp.mean(xc * xc, axis=-1, keepdims=True)
    y = xc * lax.rsqrt(var + EPS) * g + b
    return y * _sigmoid(y)


def _conf_prompt_kernel(pa_ref, pb_ref, ha_ref, hb_ref, wdw_ref, bdw_ref, lng_ref, lnb_ref,
                        c_ref, ust_ref, ubuf, cbuf, *, tl, rt, ct):
    t = pl.program_id(1)
    u = pa_ref[...] * _sigmoid(pb_ref[...])
    uh = ha_ref[...] * _sigmoid(hb_ref[...])
    ubuf[0:HALO, :] = jnp.where(t > 0, uh, 0.0)
    ubuf[HALO:, :] = u
    first = HALO - (CONV_W - 1)
    for r0 in range(0, tl, rt):
        for c0 in range(0, D_MODEL, ct):
            acc = jnp.zeros((rt, ct), F32)
            for w in range(CONV_W):
                acc = acc + ubuf[r0 + first + w:r0 + first + w + rt, c0:c0 + ct] * wdw_ref[w:w + 1, c0:c0 + ct]
            cbuf[r0:r0 + rt, c0:c0 + ct] = acc + bdw_ref[:, c0:c0 + ct]
    c_ref[...] = _ln_silu(cbuf[...], lng_ref[...], lnb_ref[...]).astype(BF16)

    @pl.when(t == pl.num_programs(1) - 1)
    def _():
        ust_ref[0] = ubuf[tl:tl + HALO, :]


def _conf_prompt(p, B, LP, w_dw, b_dw, ln_g, ln_b):
    tl = 192 if LP % 192 == 0 else CHUNK
    nt = LP // tl
    hb = tl // HALO
    kern = functools.partial(_conf_prompt_kernel, tl=tl, rt=32, ct=256)
    halo_idx = lambda b, t: (jnp.maximum((b * nt + t) * hb - 1, 0), 0)
    halo_idx1 = lambda b, t: (jnp.maximum((b * nt + t) * hb - 1, 0), 1)
    vec = lambda: pl.BlockSpec((1, D_MODEL), lambda b, t: (0, 0))
    return pl.pallas_call(
        kern,
        out_shape=(jax.ShapeDtypeStruct((B * LP, D_MODEL), BF16),
                   jax.ShapeDtypeStruct((B, HALO, D_MODEL), F32)),
        grid=(B, nt),
        in_specs=[pl.BlockSpec((tl, D_MODEL), lambda b, t: (b * nt + t, 0)),
                  pl.BlockSpec((tl, D_MODEL), lambda b, t: (b * nt + t, 1)),
                  pl.BlockSpec((HALO, D_MODEL), halo_idx),
                  pl.BlockSpec((HALO, D_MODEL), halo_idx1),
                  pl.BlockSpec((CONV_W, D_MODEL), lambda b, t: (0, 0)),
                  vec(), vec(), vec()],
        out_specs=(pl.BlockSpec((tl, D_MODEL), lambda b, t: (b * nt + t, 0)),
                   pl.BlockSpec((1, HALO, D_MODEL), lambda b, t: (b, 0, 0))),
        scratch_shapes=[pltpu.VMEM((HALO + tl, D_MODEL), F32), pltpu.VMEM((tl, D_MODEL), F32)],
        compiler_params=_params(("arbitrary", "arbitrary")),
        name="conf_prompt",
    )(p, p, p, p, w_dw, b_dw, ln_g, ln_b)


def _conf_sample_kernel(st_ref, pa_ref, pb_ref, wdw_ref, bdw_ref, lng_ref, lnb_ref,
                        c_ref, nst_ref, xh, *, sb, ls):
    hist = CONV_W - 1
    for s in range(sb):
        u = pa_ref[s] * _sigmoid(pb_ref[s])
        xh[0:hist, :] = st_ref[s]
        xh[hist:hist + ls, :] = u
        acc = jnp.zeros((ls, D_MODEL), F32)
        for w in range(CONV_W):
            acc = acc + xh[w:w + ls, :] * wdw_ref[w:w + 1, :]
        c_ref[s] = _ln_silu(acc + bdw_ref[...], lng_ref[...], lnb_ref[...])
        nst_ref[s] = xh[ls:ls + hist, :]


def _conf_sample(p_s3, state, w_dw, b_dw, ln_g, ln_b):
    NB, ls, _ = p_s3.shape
    hist = CONV_W - 1
    sb = 8 if NB % 8 == 0 else 1
    kern = functools.partial(_conf_sample_kernel, sb=sb, ls=ls)
    vec = lambda: pl.BlockSpec((1, D_MODEL), lambda i: (0, 0))
    return pl.pallas_call(
        kern,
        out_shape=(jax.ShapeDtypeStruct((NB, ls, D_MODEL), F32),
                   jax.ShapeDtypeStruct((NB, hist, D_MODEL), F32)),
        grid=(NB // sb,),
        in_specs=[pl.BlockSpec((sb, hist, D_MODEL), lambda i: (i, 0, 0)),
                  pl.BlockSpec((sb, ls, D_MODEL), lambda i: (i, 0, 0)),
                  pl.BlockSpec((sb, ls, D_MODEL), lambda i: (i, 0, 1)),
                  pl.BlockSpec((CONV_W, D_MODEL), lambda i: (0, 0)),
                  vec(), vec(), vec()],
        out_specs=(pl.BlockSpec((sb, ls, D_MODEL), lambda i: (i, 0, 0)),
                   pl.BlockSpec((sb, hist, D_MODEL), lambda i: (i, 0, 0))),
        scratch_shapes=[pltpu.VMEM((hist + ls + 8, D_MODEL), F32)],
        compiler_params=_params(("arbitrary",)),
        name="conf_sample",
    )(state, p_s3, p_s3, w_dw, b_dw, ln_g, ln_b)


def _split(a):
    hi = a.astype(BF16)
    return hi, (a - hi.astype(F32)).astype(BF16)


def _mm3(a, b):
    ah, al = a
    bh, bl = b
    return _dot(jnp.concatenate([ah, al, ah], axis=1), jnp.concatenate([bh, bh, bl], axis=0))


def _tri_inverse(ms, i, j, C, nil):
    same = lambda n: (i >> (n.bit_length() - 1)) == (j >> (n.bit_length() - 1))
    base = min(16, C)
    eye = (i == j).astype(F32)
    dps = [jnp.where(same(base), m, 0.0) for m in ms]
    xs = [eye - d for d in dps]
    for _ in range(max(0, (min(base, nil) - 1).bit_length() - 1)):
        sp = [_split(d) for d in dps]
        dps = [_mm3(s, s) for s in sp]
        xs = [x + _mm3(_split(d), _split(x)) for d, x in zip(dps, xs)]
    blk = base
    while blk < C:
        sel = jnp.logical_and(same(2 * blk), jnp.logical_not(same(blk)))
        xsp = [_split(x) for x in xs]
        ys = [_mm3(_split(jnp.where(sel, m, 0.0)), x) for m, x in zip(ms, xsp)]
        xs = [x - _mm3(xp, _split(y)) for x, xp, y in zip(xs, xsp, ys)]
        blk *= 2
    return xs


def _gdn_chunk(xq, xk, xv, z, ab, valid, s_ref, alog_ref, dtb_ref, nw_ref, nil):
    C = xq.shape[0]
    G = STACK // C
    ok = valid > 0.5
    ri = lax.broadcasted_iota(jnp.int32, (C, C), 0)
    ci = lax.broadcasted_iota(jnp.int32, (C, C), 1)
    tril = (ri >= ci).astype(BF16)
    i = lax.broadcasted_iota(jnp.int32, (STACK, STACK), 0)
    j = lax.broadcasted_iota(jnp.int32, (STACK, STACK), 1)
    shift = C.bit_length() - 1
    same = (i >> shift) == (j >> shift)
    causal = jnp.logical_and(same, i >= j)
    strict = jnp.logical_and(same, i > j)

    xa = ab + dtb_ref[...]
    softplus = jnp.maximum(xa, 0.0) + jnp.log(1.0 + jnp.exp(-jnp.abs(xa)))
    g_all = jnp.where(ok, -jnp.exp(alog_ref[...]) * softplus, 0.0)
    beta_all = jnp.where(ok, _sigmoid(ab), 0.0)
    g1 = g_all.astype(BF16)
    r1 = g_all - g1.astype(F32)
    g2 = r1.astype(BF16)
    g3 = (r1 - g2.astype(F32)).astype(BF16)
    gc_all = _dot(tril, g1) + _dot(tril, g2) + _dot(tril, g3)

    ok_st = jnp.concatenate([valid] * G, axis=0) > 0.5
    stacks = [list(range(h0, h0 + G)) for h0 in range(0, DN_HEADS, G)]
    pre = []
    for heads in stacks:
        stack = lambda x: jnp.concatenate([x[:, h * DN_DK:(h + 1) * DN_DK] for h in heads], axis=0)
        col = lambda a, off: jnp.concatenate([a[:, off + h:off + h + 1] for h in heads], axis=0)
        q = stack(xq)
        k = stack(xk)
        q = jnp.where(ok_st, q * lax.rsqrt(jnp.sum(q * q, axis=-1, keepdims=True) + EPS) * (DN_DK ** -0.5), 0.0)
        k = jnp.where(ok_st, k * lax.rsqrt(jnp.sum(k * k, axis=-1, keepdims=True) + EPS), 0.0)
        v = jnp.where(ok_st, stack(xv), 0.0)
        gc = col(gc_all, 0)
        beta = col(beta_all, DN_HEADS)
        g_last = jnp.concatenate([jnp.broadcast_to(gc_all[C - 1:C, h:h + 1], (C, 1)) for h in heads], axis=0)
        gb = jnp.broadcast_to(gc, (STACK, STACK))
        decay = jnp.where(causal, jnp.exp(jnp.where(causal, gb - gb.T, 0.0)), 0.0)
        egc = jnp.exp(gc)
        kb = k * beta
        pre.append(dict(q=q, k=k, kb=kb, kbf=k.astype(BF16), decay=decay, egc=egc,
                        rhs=jnp.concatenate([v * beta, kb * egc], axis=1),
                        k_dec=(k * jnp.exp(g_last - gc)).astype(BF16), zs=stack(z)))
    ms = [jnp.where(strict, _dot_nt(p["kb"].astype(BF16), p["kbf"]) * p["decay"], 0.0) for p in pre]
    qks = [jnp.where(causal, _dot_nt(p["q"].astype(BF16), p["kbf"]) * p["decay"], 0.0).astype(BF16) for p in pre]
    invs = _tri_inverse(ms, i, j, C, nil)
    sols = [_mm3(_split(inv), _split(p["rhs"])) for inv, p in zip(invs, pre)]
    wss = []
    for heads, p, sol in zip(stacks, pre, sols):
        w = sol[:, DN_DV:].astype(BF16)
        q_dec = (p["q"] * p["egc"]).astype(BF16)
        wss.append([_dot(jnp.concatenate([w[g * C:(g + 1) * C], q_dec[g * C:(g + 1) * C]], axis=0),
                         s_ref[h].astype(BF16)) for g, h in enumerate(heads)])
    outs = [None] * DN_HEADS
    for heads, p, sol, ws, qk in zip(stacks, pre, sols, wss, qks):
        v_new = [(sol[g * C:(g + 1) * C, :DN_DV] - ws[g][:C]).astype(BF16) for g in range(G)]
        for g, h in enumerate(heads):
            s_ref[h] = (s_ref[h] * jnp.exp(gc_all[C - 1:C, h:h + 1])
                        + _dot_tn(p["k_dec"][g * C:(g + 1) * C], v_new[g]))
        o = jnp.concatenate([w[C:] for w in ws], axis=0) + _dot(qk, jnp.concatenate(v_new, axis=0))
        o = o * lax.rsqrt(jnp.mean(o * o, axis=-1, keepdims=True) + EPS) * nw_ref[...]
        og = o * (p["zs"] * _sigmoid(p["zs"]))
        for g, h in enumerate(heads):
            outs[h] = og[g * C:(g + 1) * C]
    return outs


def _short_conv_silu(xbuf, wc_ref, rows):
    first = 8 - (SHORT_W - 1)
    acc = xbuf[first:first + rows, :] * wc_ref[0:1, :]
    for w in range(1, SHORT_W):
        acc = acc + xbuf[first + w:first + w + rows, :] * wc_ref[w:w + 1, :]
    return acc * _sigmoid(acc)


def _gdn_prompt_kernel(q_ref, k_ref, v_ref, z_ref, ab_ref, wc_ref, alog_ref, dtb_ref, nw_ref,
                       o_ref, s_ref, xbuf):
    c = pl.program_id(1)
    n_qk = DN_HEADS * DN_DK

    @pl.when(c == 0)
    def _():
        s_ref[...] = jnp.zeros_like(s_ref)
        xbuf[0:8, :] = jnp.zeros((8, xbuf.shape[1]), F32)

    xbuf[8:8 + CHUNK, 0:n_qk] = q_ref[...]
    xbuf[8:8 + CHUNK, n_qk:2 * n_qk] = k_ref[...]
    xbuf[8:8 + CHUNK, 2 * n_qk:] = v_ref[...]
    x = _short_conv_silu(xbuf, wc_ref, CHUNK)
    xbuf[0:8, :] = xbuf[CHUNK:CHUNK + 8, :]
    rows = lax.broadcasted_iota(jnp.int32, (CHUNK, 1), 0)
    valid = jnp.logical_or(rows >= FRONT, c > 0).astype(F32)
    outs = _gdn_chunk(x[:, 0:n_qk], x[:, n_qk:2 * n_qk], x[:, 2 * n_qk:], z_ref[...], ab_ref[...], valid,
                      s_ref.at[0], alog_ref, dtb_ref, nw_ref, nil=CHUNK)
    for h in range(DN_HEADS):
        o_ref[:, h * DN_DV:(h + 1) * DN_DV] = outs[h].astype(BF16)


def _gdn_prompt(p, ab, B, LP, w_conv, alog, dtb, nw):
    nc = LP // CHUNK
    n_qk = DN_HEADS * DN_DK
    blk = lambda col: pl.BlockSpec((CHUNK, n_qk), lambda b, c: (b * nc + c, col))
    vec = lambda n: pl.BlockSpec((1, n), lambda b, c: (0, 0))
    return pl.pallas_call(
        _gdn_prompt_kernel,
        out_shape=(jax.ShapeDtypeStruct((B * LP, n_qk), BF16),
                   jax.ShapeDtypeStruct((B, DN_HEADS, DN_DK, DN_DV), F32)),
        grid=(B, nc),
        in_specs=[blk(2), blk(3), blk(4), blk(5),
                  pl.BlockSpec((CHUNK, LANES), lambda b, c: (b * nc + c, 0)),
                  pl.BlockSpec((SHORT_W, 3 * n_qk), lambda b, c: (0, 0)),
                  vec(LANES), vec(LANES), vec(DN_DV)],
        out_specs=(pl.BlockSpec((CHUNK, n_qk), lambda b, c: (b * nc + c, 0)),
                   pl.BlockSpec((1, DN_HEADS, DN_DK, DN_DV), lambda b, c: (b, 0, 0, 0))),
        scratch_shapes=[pltpu.VMEM((CHUNK + 8, 3 * n_qk), F32)],
        compiler_params=_params(("arbitrary", "arbitrary")),
        name="gdn_prompt",
    )(p, p, p, p, ab, w_conv, alog, dtb, nw)


def _gdn_sample_kernel(st_ref, q_ref, k_ref, v_ref, z_ref, ab_ref, s0_ref, wc_ref, alog_ref, dtb_ref, nw_ref,
                       o_ref, s_ref, xbuf, zbuf, abbuf, *, sb, ls):
    n_qk = DN_HEADS * DN_DK
    C = SAMPLE_CHUNK
    hist = SHORT_W - 1
    xbuf[...] = jnp.zeros_like(xbuf)
    zbuf[...] = jnp.zeros_like(zbuf)
    abbuf[...] = jnp.zeros_like(abbuf)
    s_ref[...] = s0_ref[...]
    valid = (lax.broadcasted_iota(jnp.int32, (C, 1), 0) < ls).astype(F32)
    for s in range(sb):
        xb = xbuf.at[s]
        xb[8 - hist:8, :] = st_ref[s]
        xb[8:8 + ls, 0:n_qk] = q_ref[s]
        xb[8:8 + ls, n_qk:2 * n_qk] = k_ref[s]
        xb[8:8 + ls, 2 * n_qk:] = v_ref[s]
        zbuf[s, 0:ls, :] = z_ref[s]
        abbuf[s, 0:ls, :] = ab_ref[s]
        x = _short_conv_silu(xb, wc_ref, C)
        outs = _gdn_chunk(x[:, 0:n_qk], x[:, n_qk:2 * n_qk], x[:, 2 * n_qk:], zbuf[s], abbuf[s], valid,
                          s_ref.at[s], alog_ref, dtb_ref, nw_ref, nil=ls)
        for h in range(DN_HEADS):
            o_ref[s, :, h * DN_DV:(h + 1) * DN_DV] = outs[h][0:ls, :]


def _gdn_sample(p_s3, ab_s3, st_conv, s0, w_conv, alog, dtb, nw):
    NB, ls, _ = p_s3.shape
    n_qk = DN_HEADS * DN_DK
    hist = SHORT_W - 1
    assert ls <= SAMPLE_CHUNK
    sb = 4 if NB % 4 == 0 else 1
    kern = functools.partial(_gdn_sample_kernel, sb=sb, ls=ls)
    blk = lambda col: pl.BlockSpec((sb, ls, n_qk), lambda i: (i, 0, col))
    vec = lambda n: pl.BlockSpec((1, n), lambda i: (0, 0))
    sspec = lambda: pl.BlockSpec((sb, DN_HEADS, DN_DK, DN_DV), lambda i: (i, 0, 0, 0))
    return pl.pallas_call(
        kern,
        out_shape=(jax.ShapeDtypeStruct((NB, ls, n_qk), F32),
                   jax.ShapeDtypeStruct((NB, DN_HEADS, DN_DK, DN_DV), F32)),
        grid=(NB // sb,),
        in_specs=[pl.BlockSpec((sb, hist, 3 * n_qk), lambda i: (i, 0, 0)),
                  blk(2), blk(3), blk(4), blk(5),
                  pl.BlockSpec((sb, ls, LANES), lambda i: (i, 0, 0)),
                  sspec(),
                  pl.BlockSpec((SHORT_W, 3 * n_qk), lambda i: (0, 0)),
                  vec(LANES), vec(LANES), vec(DN_DV)],
        out_specs=(pl.BlockSpec((sb, ls, n_qk), lambda i: (i, 0, 0)), sspec()),
        scratch_shapes=[pltpu.VMEM((sb, SAMPLE_CHUNK + 8, 3 * n_qk), F32),
                        pltpu.VMEM((sb, SAMPLE_CHUNK, n_qk), F32),
                        pltpu.VMEM((sb, SAMPLE_CHUNK, LANES), F32)],
        compiler_params=_params(("arbitrary",)),
        name="gdn_sample",
    )(st_conv, p_s3, p_s3, p_s3, p_s3, ab_s3, s0, w_conv, alog, dtb, nw)


def _pack_bf16_pairs(x):
    half = x.shape[1] // 2
    lo = lax.bitcast_convert_type(x[:, :half].astype(BF16).astype(F32), jnp.uint32)
    hi = lax.bitcast_convert_type(x[:, half:].astype(BF16).astype(F32), jnp.uint32)
    return jnp.bitwise_or(jnp.bitwise_and(hi, jnp.uint32(0xFFFF0000)), lax.shift_right_logical(lo, jnp.uint32(16)))


def _unpack_bf16_pairs(xp):
    lo = lax.bitcast_convert_type(lax.shift_left(xp, jnp.uint32(16)), F32)
    hi = lax.bitcast_convert_type(jnp.bitwise_and(xp, jnp.uint32(0xFFFF0000)), F32)
    return jnp.concatenate([lo, hi], axis=1).astype(BF16)


def _merge_kernel(c_ref, og_ref, ga_ref, gb_ref, h_ref, wco_ref, bco_ref, wdo_ref, wo_ref, nf_ref, wr_ref, br_ref,
                  h1_ref, xp_ref, ei_ref, gt_ref, rk_ref, cnt_ref, carry):
    step = pl.program_id(0)

    @pl.when(step == 0)
    def _():
        carry[...] = jnp.zeros_like(carry)

    ya = _dot(c_ref[...], wco_ref[...]) + bco_ref[...]
    yb = _dot(og_ref[...], wdo_ref[...])
    mixed = _sigmoid(ga_ref[...]) * ya + _sigmoid(gb_ref[...]) * yb
    h1 = h_ref[...] + _dot(mixed.astype(BF16), wo_ref[...])
    h1_ref[...] = h1
    xn = h1 * lax.rsqrt(jnp.mean(h1 * h1, axis=-1, keepdims=True) + EPS) * nf_ref[...]
    xp_ref[...] = _pack_bf16_pairs(xn)
    logits = _dot(xn.astype(BF16), wr_ref[...]) + br_ref[...]

    tm = logits.shape[0]
    lane = lax.broadcasted_iota(jnp.int32, (tm, LANES), 1)
    work = logits
    sels, vals = [], []
    for _ in range(TOP_K):
        m = jnp.max(work, axis=-1, keepdims=True)
        idx = jnp.min(jnp.where(work == m, lane, N_EXPERTS - 1), axis=-1, keepdims=True)
        sel = lane == idx
        sels.append(sel)
        vals.append(m)
        work = jnp.where(sel, -jnp.inf, work)
    exps = [jnp.exp(v - vals[0]) for v in vals]
    denom = exps[0]
    for e in exps[1:]:
        denom = denom + e
    onehot = jnp.zeros((tm, LANES), F32)
    for sel in sels:
        onehot = onehot + sel.astype(F32)
    ri = lax.broadcasted_iota(jnp.int32, (tm, tm), 0)
    ci = lax.broadcasted_iota(jnp.int32, (tm, tm), 1)
    before = _dot((ri > ci).astype(BF16), onehot.astype(BF16)) + carry[...]
    ei = jnp.zeros((tm, LANES), jnp.int32)
    gt = jnp.zeros((tm, LANES), F32)
    rk = jnp.zeros((tm, LANES), jnp.int32)
    for k in range(TOP_K):
        at_k = lane == k
        e_k = jnp.max(jnp.where(sels[k], lane, 0), axis=-1, keepdims=True)
        r_k = jnp.sum(jnp.where(sels[k], before, 0.0), axis=-1, keepdims=True).astype(jnp.int32)
        ei = jnp.where(at_k, e_k, ei)
        gt = jnp.where(at_k, exps[k] / denom, gt)
        rk = jnp.where(at_k, r_k, rk)
    ei_ref[...] = ei
    gt_ref[...] = gt
    rk_ref[...] = rk
    carry[...] = carry[...] + jnp.sum(onehot, axis=0, keepdims=True)
    cnt_ref[...] = carry[...]


def _merge(c, og, p, h, wco, bco, wdo, wo, nf, wr, br):
    T = h.shape[0]
    tm = _row_tile(T, 512)
    row = lambda col: pl.BlockSpec((tm, D_MODEL), lambda i: (i, col))
    full = lambda a, b: pl.BlockSpec((a, b), lambda i: (0, 0))
    lanes = lambda: pl.BlockSpec((tm, LANES), lambda i: (i, 0))
    return pl.pallas_call(
        _merge_kernel,
        out_shape=(jax.ShapeDtypeStruct((T, D_MODEL), F32),
                   jax.ShapeDtypeStruct((T, D_MODEL // 2), jnp.uint32),
                   jax.ShapeDtypeStruct((T, LANES), jnp.int32),
                   jax.ShapeDtypeStruct((T, LANES), F32),
                   jax.ShapeDtypeStruct((T, LANES), jnp.int32),
                   jax.ShapeDtypeStruct((1, LANES), F32)),
        grid=(T // tm,),
        in_specs=[row(0), row(0), row(6), row(7), row(0),
                  full(D_MODEL, D_MODEL), full(1, D_MODEL), full(D_MODEL, D_MODEL), full(D_MODEL, D_MODEL),
                  full(1, D_MODEL), full(D_MODEL, LANES), full(1, LANES)],
        out_specs=(row(0), pl.BlockSpec((tm, D_MODEL // 2), lambda i: (i, 0)), lanes(), lanes(), lanes(),
                   full(1, LANES)),
        scratch_shapes=[pltpu.VMEM((1, LANES), F32)],
        compiler_params=_params(("arbitrary",)),
        name="merge",
    )(c, og, p, p, h, wco, bco, wdo, wo, nf, wr, br)


def _row_copy(src, src_row, dst, dst_row, sem):
    return pltpu.make_async_copy(src.at[pl.ds(src_row, 1), :], dst.at[pl.ds(dst_row, 1), :], sem)


def _dispatch_kernel(dest_ref, x_ref, xb_in, xb_out, sem, *, tm):
    del xb_in

    def issue(t, carry):
        for k in range(TOP_K):
            _row_copy(x_ref, t, xb_out, dest_ref[t * TOP_K + k], sem).start()
        return carry

    def drain(t, carry):
        for k in range(TOP_K):
            _row_copy(x_ref, 0, xb_out, 0, sem).wait()
        return carry

    lax.fori_loop(0, tm, issue, 0)
    lax.fori_loop(0, tm, drain, 0)


def _dispatch(dest, xp, xb_init):
    T, W = xp.shape
    tm = _row_tile(T, 256)
    return pl.pallas_call(
        functools.partial(_dispatch_kernel, tm=tm),
        out_shape=jax.ShapeDtypeStruct(xb_init.shape, xb_init.dtype),
        grid=(T // tm,),
        in_specs=[pl.BlockSpec((tm * TOP_K,), lambda i: (i,), memory_space=pltpu.SMEM),
                  pl.BlockSpec((tm, W), lambda i: (i, 0)),
                  pl.BlockSpec(memory_space=pl.ANY)],
        out_specs=pl.BlockSpec(memory_space=pl.ANY),
        scratch_shapes=[pltpu.SemaphoreType.DMA],
        input_output_aliases={2: 0},
        compiler_params=_params(("arbitrary",)),
        name="moe_dispatch",
    )(dest, xp, xb_init)


def _moe_kernel(be_ref, x_ref, wu_ref, bu_ref, wd_ref, bd_ref, o_ref):
    del be_ref
    hmid = _dot(_unpack_bf16_pairs(x_ref[...]), wu_ref[0]) + bu_ref[0]
    hg = jnp.minimum(hmid[:, :D_FF], SWIGLU_LIMIT)
    hl = jnp.clip(hmid[:, D_FF:], -SWIGLU_LIMIT, SWIGLU_LIMIT)
    act = hg * _sigmoid(SWIGLU_ALPHA * hg) * (hl + 1.0)
    o_ref[...] = _dot(act.astype(BF16), wd_ref[0]) + bd_ref[0]


def _moe(block_e, xb, w_up, b_up, w_down, b_down):
    R = xb.shape[0]
    nb = R // MOE_ROWS
    grid_spec = pltpu.PrefetchScalarGridSpec(
        num_scalar_prefetch=1,
        grid=(nb,),
        in_specs=[pl.BlockSpec((MOE_ROWS, D_MODEL // 2), lambda i, be: (i, 0)),
                  pl.BlockSpec((1, D_MODEL, 2 * D_FF), lambda i, be: (be[i], 0, 0)),
                  pl.BlockSpec((1, 1, 2 * D_FF), lambda i, be: (be[i], 0, 0)),
                  pl.BlockSpec((1, D_FF, D_MODEL), lambda i, be: (be[i], 0, 0)),
                  pl.BlockSpec((1, 1, D_MODEL), lambda i, be: (be[i], 0, 0))],
        out_specs=pl.BlockSpec((MOE_ROWS, D_MODEL), lambda i, be: (i, 0)),
    )
    return pl.pallas_call(
        _moe_kernel,
        out_shape=jax.ShapeDtypeStruct((R, D_MODEL), F32),
        grid_spec=grid_spec,
        compiler_params=_params(("arbitrary",)),
        name="moe_experts",
    )(block_e, xb, w_up, b_up, w_down, b_down)


def _dispatch_plan(counts, ei, rk, T):
    A = T * TOP_K
    n_blocks = -(-A // MOE_ROWS) + N_EXPERTS
    counts = counts[0, :N_EXPERTS].astype(jnp.int32)
    padded = (counts + MOE_ROWS - 1) // MOE_ROWS * MOE_ROWS
    pend = jnp.cumsum(padded)
    pstart = pend - padded
    dest = (pstart[ei[:, :TOP_K]] + rk[:, :TOP_K]).reshape(-1).astype(jnp.int32)
    block_e = jnp.minimum(jnp.searchsorted(pend, jnp.arange(n_blocks) * MOE_ROWS, side='right'),
                          N_EXPERTS - 1).astype(jnp.int32)
    return dest, block_e, n_blocks * MOE_ROWS


def _combine_kernel(dest_ref, gt_ref, h_ref, nw_ref, yb_ref, o_ref, buf, sem, *, tm):
    def issue(t, carry):
        for k in range(TOP_K):
            _row_copy(yb_ref, dest_ref[t * TOP_K + k], buf.at[k], t, sem).start()
        return carry

    def drain(t, carry):
        for k in range(TOP_K):
            _row_copy(yb_ref, 0, buf.at[k], 0, sem).wait()
        return carry

    lax.fori_loop(0, tm, issue, 0)
    lax.fori_loop(0, tm, drain, 0)
    gt = gt_ref[...]
    x = h_ref[...]
    for k in range(TOP_K):
        x = x + gt[:, k:k + 1] * buf[k]
    o_ref[...] = x * lax.rsqrt(jnp.mean(x * x, axis=-1, keepdims=True) + EPS) * nw_ref[...]


def _combine(dest, gt, h1, nw, yb):
    T = h1.shape[0]
    tm = _row_tile(T, 256)
    row = lambda: pl.BlockSpec((tm, D_MODEL), lambda i: (i, 0))
    return pl.pallas_call(
        functools.partial(_combine_kernel, tm=tm),
        out_shape=jax.ShapeDtypeStruct((T, D_MODEL), F32),
        grid=(T // tm,),
        in_specs=[pl.BlockSpec((tm * TOP_K,), lambda i: (i,), memory_space=pltpu.SMEM),
                  pl.BlockSpec((tm, LANES), lambda i: (i, 0)),
                  row(),
                  pl.BlockSpec((1, D_MODEL), lambda i: (0, 0)),
                  pl.BlockSpec(memory_space=pl.ANY)],
        out_specs=row(),
        scratch_shapes=[pltpu.VMEM((TOP_K, tm, D_MODEL), F32), pltpu.SemaphoreType.DMA],
        compiler_params=_params(("arbitrary",)),
        name="moe_combine_final",
    )(dest, gt, h1, nw, yb)


def _pad_lanes(v, fill=0.0):
    v = v.reshape(1, -1).astype(F32)
    return jnp.pad(v, ((0, 0), (0, LANES - v.shape[1])), constant_values=fill)


def kernel(x_prompt, x_sample, state_conf_conv, state_dn_conv, state_dn_S, meta_tokens, norm_mix, w_in, w_conf_dw, b_conf_dw, ln_conf_g, ln_conf_b, w_conf_out, b_conf_out, w_dn_conv, dn_a_log, dn_dt_bias, dn_norm_w, w_dn_out, w_out, norm_ffn, w_router, b_router, w_up, b_up, w_down, b_down, norm_final):
    B, SEQ, D = x_prompt.shape
    NB, LS, _ = x_sample.shape
    depth = w_in.shape[0]
    assert D == D_MODEL and depth == 1 and SEQ % CHUNK == 0 and LS >= SHORT_W - 1
    LP = FRONT + N_META + SEQ
    TP = B * LP
    T = TP + NB * LS
    n_qk = DN_HEADS * DN_DK
    o_q = 2 * D_MODEL
    o_a = o_q + 4 * n_qk
    o_gate = o_a + 2 * DN_HEADS

    meta = jnp.broadcast_to(meta_tokens[None].astype(F32), (B, N_META, D))
    hp = jnp.concatenate([jnp.zeros((B, FRONT, D), F32), meta, x_prompt], axis=1).reshape(TP, D)
    h0 = jnp.concatenate([hp, x_sample.reshape(NB * LS, D)], axis=0)

    w_in0 = w_in[0]
    w_main = jnp.concatenate([w_in0[:, :o_a], w_in0[:, o_gate:]], axis=1)
    w_ab = jnp.pad(w_in0[:, o_a:o_gate], ((0, 0), (0, LANES - 2 * DN_HEADS))).astype(BF16)
    wco = w_conf_out[0].astype(BF16)
    wdo = w_dn_out[0].astype(BF16)
    wo = w_out[0].astype(BF16)
    wr = jnp.pad(w_router[0], ((0, 0), (0, LANES - N_EXPERTS))).astype(BF16)
    br = _pad_lanes(b_router[0], fill=-1e30)
    alog = _pad_lanes(dn_a_log[0])
    dtb = _pad_lanes(dn_dt_bias[0])
    row = lambda v: v.reshape(1, -1).astype(F32)

    xn, ab = _rms_ab(h0, row(norm_mix[0]), w_ab)
    p = _mm_in(xn, w_main)
    p_s3 = p[TP:].reshape(NB, LS, 8 * D_MODEL)
    ab_s3 = ab[TP:].reshape(NB, LS, LANES)

    c_p, ust_p = _conf_prompt(p, B, LP, w_conf_dw[0], row(b_conf_dw[0]), row(ln_conf_g[0]), row(ln_conf_b[0]))
    c_s, conf_state_s = _conf_sample(p_s3, state_conf_conv[0], w_conf_dw[0], row(b_conf_dw[0]),
                                     row(ln_conf_g[0]), row(ln_conf_b[0]))
    c_all = jnp.concatenate([c_p, c_s.reshape(NB * LS, D).astype(BF16)], axis=0)

    og_p, s_p = _gdn_prompt(p, ab, B, LP, w_dn_conv[0], alog, dtb, row(dn_norm_w[0]))
    og_s, s_s = _gdn_sample(p_s3, ab_s3, state_dn_conv[0], state_dn_S[0], w_dn_conv[0], alog, dtb,
                            row(dn_norm_w[0]))
    og_all = jnp.concatenate([og_p, og_s.reshape(NB * LS, n_qk).astype(BF16)], axis=0)

    h1, xp2, ei, gt, rk, counts = _merge(c_all, og_all, p, h0, wco, row(b_conf_out[0]), wdo, wo,
                                         row(norm_ffn[0]), wr, br)

    dest, block_e, R = _dispatch_plan(counts, ei, rk, T)
    xb = _dispatch(dest, xp2, jnp.zeros((R, D // 2), jnp.uint32))
    yb = _moe(block_e, xb, w_up[0].astype(BF16), b_up[0].reshape(N_EXPERTS, 1, -1),
              w_down[0].astype(BF16), b_down[0].reshape(N_EXPERTS, 1, -1))
    y = _combine(dest, gt, h1, row(norm_final), yb)

    y_prompt = y[:TP].reshape(B, LP, D)[:, FRONT + N_META:]
    y_sample = y[TP:].reshape(NB, LS, D)
    hist = CONV_W - 1
    conf_conv_prompt = ust_p[:, HALO - hist:][None]
    p_p3 = p[:TP].reshape(B, LP, 8 * D_MODEL)
    dn_conv_prompt = p_p3[:, LP - (SHORT_W - 1):, o_q:o_q + 3 * n_qk][None]
    dn_conv_sample = p_s3[:, LS - (SHORT_W - 1):, o_q:o_q + 3 * n_qk][None]
    return (y_prompt, y_sample, conf_conv_prompt, dn_conv_prompt, s_p[None],
            conf_state_s[None], dn_conv_sample, s_s[None])
```

```python
import functools

import jax
import jax.numpy as jnp
from jax import lax
from jax.experimental import pallas as pl
from jax.experimental.pallas import tpu as pltpu

D_MODEL = 1024
N_META = 16
CONV_W = 31
SHORT_W = 4
DN_HEADS = 8
DN_DK = 128
DN_DV = 128
CHUNK = 64
N_EXPERTS = 32
TOP_K = 4
D_FF = 1024
SWIGLU_LIMIT = 7.0
SWIGLU_ALPHA = 1.702
EPS = 1e-6

FRONT = (-N_META) % CHUNK
SAMPLE_CHUNK = 16
STACK = 128
LANES = 128
HALO = 32
MOE_ROWS = 512
VMEM_LIMIT = 48 * 1024 * 1024
MOE_VMEM_LIMIT = 58 * 1024 * 1024

F32 = jnp.float32
BF16 = jnp.bfloat16


def _row_tile(n, pref):
    best = 16
    for t in range(16, min(n, pref) + 1, 16):
        if n % t == 0:
            best = t
    assert n % best == 0
    return best


def _sigmoid(x):
    return 1.0 / (1.0 + jnp.exp(-x))


def _dot(a, b):
    return jnp.dot(a, b, preferred_element_type=F32)


def _dot_nt(a, b):
    return lax.dot_general(a, b, (((1,), (1,)), ((), ())), preferred_element_type=F32)


def _dot_tn(a, b):
    return lax.dot_general(a, b, (((0,), (0,)), ((), ())), preferred_element_type=F32)


def _params(sem):
    return pltpu.CompilerParams(dimension_semantics=sem, vmem_limit_bytes=VMEM_LIMIT)


def _rms_ab_kernel(h_ref, nw_ref, wab_ref, xn_ref, ab_ref):
    x = h_ref[...]
    y = x * lax.rsqrt(jnp.mean(x * x, axis=-1, keepdims=True) + EPS) * nw_ref[...]
    yb = y.astype(BF16)
    xn_ref[...] = yb
    ab_ref[...] = _dot(yb, wab_ref[...])


def _rms_ab(h, norm_w, w_ab):
    T = h.shape[0]
    tm = _row_tile(T, 1024)
    return pl.pallas_call(
        _rms_ab_kernel,
        out_shape=(jax.ShapeDtypeStruct((T, D_MODEL), BF16), jax.ShapeDtypeStruct((T, LANES), F32)),
        grid=(T // tm,),
        in_specs=[pl.BlockSpec((tm, D_MODEL), lambda i: (i, 0)),
                  pl.BlockSpec((1, D_MODEL), lambda i: (0, 0)),
                  pl.BlockSpec((D_MODEL, LANES), lambda i: (0, 0))],
        out_specs=(pl.BlockSpec((tm, D_MODEL), lambda i: (i, 0)),
                   pl.BlockSpec((tm, LANES), lambda i: (i, 0))),
        compiler_params=_params(("arbitrary",)),
        name="rms_ab",
    )(h, norm_w, w_ab)


def _mm_in_kernel(x_ref, w_ref, o_ref, wb_ref):
    @pl.when(pl.program_id(1) == 0)
    def _():
        wb_ref[...] = w_ref[...].astype(BF16)

    o_ref[...] = _dot(x_ref[...], wb_ref[...])


def _mm_in(xn, w):
    T, K = xn.shape
    N = w.shape[1]
    tm = _row_tile(T, 1024)
    tn = 1024
    assert N % tn == 0
    return pl.pallas_call(
        _mm_in_kernel,
        out_shape=jax.ShapeDtypeStruct((T, N), F32),
        grid=(N // tn, T // tm),
        in_specs=[pl.BlockSpec((tm, K), lambda j, i: (i, 0)),
                  pl.BlockSpec((K, tn), lambda j, i: (0, j))],
        out_specs=pl.BlockSpec((tm, tn), lambda j, i: (i, j)),
        scratch_shapes=[pltpu.VMEM((K, tn), BF16)],
        compiler_params=_params(("arbitrary", "arbitrary")),
        name="in_proj",
    )(xn, w)


def _ln_silu(x, g, b):
    mu = jnp.mean(x, axis=-1, keepdims=True)
    xc = x - mu
    var = jnp.mean(xc * xc, axis=-1, keepdims=True)
    y = xc * lax.rsqrt(var + EPS) * g + b
    return y * _sigmoid(y)


def _conf_prompt_kernel(pa_ref, pb_ref, ha_ref, hb_ref, wdw_ref, bdw_ref, lng_ref, lnb_ref,
                        c_ref, ust_ref, ubuf, cbuf, *, tl, rt, ct):
    t = pl.program_id(1)
    u = pa_ref[...] * _sigmoid(pb_ref[...])
    uh = ha_ref[...] * _sigmoid(hb_ref[...])
    ubuf[0:HALO, :] = jnp.where(t > 0, uh, 0.0)
    ubuf[HALO:, :] = u
    first = HALO - (CONV_W - 1)
    for r0 in range(0, tl, rt):
        for c0 in range(0, D_MODEL, ct):
            acc = jnp.zeros((rt, ct), F32)
            for w in range(CONV_W):
                acc = acc + ubuf[r0 + first + w:r0 + first + w + rt, c0:c0 + ct] * wdw_ref[w:w + 1, c0:c0 + ct]
            cbuf[r0:r0 + rt, c0:c0 + ct] = acc + bdw_ref[:, c0:c0 + ct]
    c_ref[...] = _ln_silu(cbuf[...], lng_ref[...], lnb_ref[...]).astype(BF16)

    @pl.when(t == pl.num_programs(1) - 1)
    def _():
        ust_ref[0] = ubuf[tl:tl + HALO, :]


def _conf_prompt(p, B, LP, w_dw, b_dw, ln_g, ln_b):
    tl = 192 if LP % 192 == 0 else CHUNK
    nt = LP // tl
    hb = tl // HALO
    kern = functools.partial(_conf_prompt_kernel, tl=tl, rt=32, ct=256)
    halo_idx = lambda b, t: (jnp.maximum((b * nt + t) * hb - 1, 0), 0)
    halo_idx1 = lambda b, t: (jnp.maximum((b * nt + t) * hb - 1, 0), 1)
    vec = lambda: pl.BlockSpec((1, D_MODEL), lambda b, t: (0, 0))
    return pl.pallas_call(
        kern,
        out_shape=(jax.ShapeDtypeStruct((B * LP, D_MODEL), BF16),
                   jax.ShapeDtypeStruct((B, HALO, D_MODEL), F32)),
        grid=(B, nt),
        in_specs=[pl.BlockSpec((tl, D_MODEL), lambda b, t: (b * nt + t, 0)),
                  pl.BlockSpec((tl, D_MODEL), lambda b, t: (b * nt + t, 1)),
                  pl.BlockSpec((HALO, D_MODEL), halo_idx),
                  pl.BlockSpec((HALO, D_MODEL), halo_idx1),
                  pl.BlockSpec((CONV_W, D_MODEL), lambda b, t: (0, 0)),
                  vec(), vec(), vec()],
        out_specs=(pl.BlockSpec((tl, D_MODEL), lambda b, t: (b * nt + t, 0)),
                   pl.BlockSpec((1, HALO, D_MODEL), lambda b, t: (b, 0, 0))),
        scratch_shapes=[pltpu.VMEM((HALO + tl, D_MODEL), F32), pltpu.VMEM((tl, D_MODEL), F32)],
        compiler_params=_params(("arbitrary", "arbitrary")),
        name="conf_prompt",
    )(p, p, p, p, w_dw, b_dw, ln_g, ln_b)


def _conf_sample_kernel(st_ref, pa_ref, pb_ref, wdw_ref, bdw_ref, lng_ref, lnb_ref,
                        c_ref, nst_ref, xh, *, sb, ls):
    hist = CONV_W - 1
    for s in range(sb):
        u = pa_ref[s] * _sigmoid(pb_ref[s])
        xh[0:hist, :] = st_ref[s]
        xh[hist:hist + ls, :] = u
        acc = jnp.zeros((ls, D_MODEL), F32)
        for w in range(CONV_W):
            acc = acc + xh[w:w + ls, :] * wdw_ref[w:w + 1, :]
        c_ref[s] = _ln_silu(acc + bdw_ref[...], lng_ref[...], lnb_ref[...])
        nst_ref[s] = xh[ls:ls + hist, :]


def _conf_sample(p_s3, state, w_dw, b_dw, ln_g, ln_b):
    NB, ls, _ = p_s3.shape
    hist = CONV_W - 1
    sb = 8 if NB % 8 == 0 else 1
    kern = functools.partial(_conf_sample_kernel, sb=sb, ls=ls)
    vec = lambda: pl.BlockSpec((1, D_MODEL), lambda i: (0, 0))
    return pl.pallas_call(
        kern,
        out_shape=(jax.ShapeDtypeStruct((NB, ls, D_MODEL), F32),
                   jax.ShapeDtypeStruct((NB, hist, D_MODEL), F32)),
        grid=(NB // sb,),
        in_specs=[pl.BlockSpec((sb, hist, D_MODEL), lambda i: (i, 0, 0)),
                  pl.BlockSpec((sb, ls, D_MODEL), lambda i: (i, 0, 0)),
                  pl.BlockSpec((sb, ls, D_MODEL), lambda i: (i, 0, 1)),
                  pl.BlockSpec((CONV_W, D_MODEL), lambda i: (0, 0)),
                  vec(), vec(), vec()],
        out_specs=(pl.BlockSpec((sb, ls, D_MODEL), lambda i: (i, 0, 0)),
                   pl.BlockSpec((sb, hist, D_MODEL), lambda i: (i, 0, 0))),
        scratch_shapes=[pltpu.VMEM((hist + ls + 8, D_MODEL), F32)],
        compiler_params=_params(("arbitrary",)),
        name="conf_sample",
    )(state, p_s3, p_s3, w_dw, b_dw, ln_g, ln_b)


def _split(a):
    hi = a.astype(BF16)
    return hi, (a - hi.astype(F32)).astype(BF16)


def _mm3(a, b):
    ah, al = a
    bh, bl = b
    return _dot(jnp.concatenate([ah, al, ah], axis=1), jnp.concatenate([bh, bh, bl], axis=0))


def _tri_inverse(ms, i, j, C, nil):
    same = lambda n: (i >> (n.bit_length() - 1)) == (j >> (n.bit_length() - 1))
    base = min(16, C)
    eye = (i == j).astype(F32)
    dps = [jnp.where(same(base), m, 0.0) for m in ms]
    xs = [eye - d for d in dps]
    for _ in range(max(0, (min(base, nil) - 1).bit_length() - 1)):
        sp = [_split(d) for d in dps]
        dps = [_mm3(s, s) for s in sp]
        xs = [x + _mm3(_split(d), _split(x)) for d, x in zip(dps, xs)]
    blk = base
    while blk < C:
        sel = jnp.logical_and(same(2 * blk), jnp.logical_not(same(blk)))
        xsp = [_split(x) for x in xs]
        ys = [_mm3(_split(jnp.where(sel, m, 0.0)), x) for m, x in zip(ms, xsp)]
        xs = [x - _mm3(xp, _split(y)) for x, xp, y in zip(xs, xsp, ys)]
        blk *= 2
    return xs


def _gdn_chunk(xq, xk, xv, z, ab, valid, s_ref, alog_ref, dtb_ref, nw_ref, nil):
    C = xq.shape[0]
    G = STACK // C
    ok = valid > 0.5
    ri = lax.broadcasted_iota(jnp.int32, (C, C), 0)
    ci = lax.broadcasted_iota(jnp.int32, (C, C), 1)
    tril = (ri >= ci).astype(BF16)
    i = lax.broadcasted_iota(jnp.int32, (STACK, STACK), 0)
    j = lax.broadcasted_iota(jnp.int32, (STACK, STACK), 1)
    shift = C.bit_length() - 1
    same = (i >> shift) == (j >> shift)
    causal = jnp.logical_and(same, i >= j)
    strict = jnp.logical_and(same, i > j)

    xa = ab + dtb_ref[...]
    softplus = jnp.maximum(xa, 0.0) + jnp.log(1.0 + jnp.exp(-jnp.abs(xa)))
    g_all = jnp.where(ok, -jnp.exp(alog_ref[...]) * softplus, 0.0)
    beta_all = jnp.where(ok, _sigmoid(ab), 0.0)
    g1 = g_all.astype(BF16)
    r1 = g_all - g1.astype(F32)
    g2 = r1.astype(BF16)
    g3 = (r1 - g2.astype(F32)).astype(BF16)
    gc_all = _dot(tril, g1) + _dot(tril, g2) + _dot(tril, g3)

    ok_st = jnp.concatenate([valid] * G, axis=0) > 0.5
    stacks = [list(range(h0, h0 + G)) for h0 in range(0, DN_HEADS, G)]
    pre = []
    for heads in stacks:
        stack = lambda x: jnp.concatenate([x[:, h * DN_DK:(h + 1) * DN_DK] for h in heads], axis=0)
        col = lambda a, off: jnp.concatenate([a[:, off + h:off + h + 1] for h in heads], axis=0)
        q = stack(xq)
        k = stack(xk)
        q = jnp.where(ok_st, q * lax.rsqrt(jnp.sum(q * q, axis=-1, keepdims=True) + EPS) * (DN_DK ** -0.5), 0.0)
        k = jnp.where(ok_st, k * lax.rsqrt(jnp.sum(k * k, axis=-1, keepdims=True) + EPS), 0.0)
        v = jnp.where(ok_st, stack(xv), 0.0)
        gc = col(gc_all, 0)
        beta = col(beta_all, DN_HEADS)
        g_last = jnp.concatenate([jnp.broadcast_to(gc_all[C - 1:C, h:h + 1], (C, 1)) for h in heads], axis=0)
        gb = jnp.broadcast_to(gc, (STACK, STACK))
        decay = jnp.where(causal, jnp.exp(jnp.where(causal, gb - gb.T, 0.0)), 0.0)
        egc = jnp.exp(gc)
        kb = k * beta
        pre.append(dict(q=q, k=k, kb=kb, kbf=k.astype(BF16), decay=decay, egc=egc,
                        rhs=jnp.concatenate([v * beta, kb * egc], axis=1),
                        k_dec=(k * jnp.exp(g_last - gc)).astype(BF16), zs=stack(z)))
    ms = [jnp.where(strict, _dot_nt(p["kb"].astype(BF16), p["kbf"]) * p["decay"], 0.0) for p in pre]
    qks = [jnp.where(causal, _dot_nt(p["q"].astype(BF16), p["kbf"]) * p["decay"], 0.0).astype(BF16) for p in pre]
    invs = _tri_inverse(ms, i, j, C, nil)
    sols = [_mm3(_split(inv), _split(p["rhs"])) for inv, p in zip(invs, pre)]
    wss = []
    for heads, p, sol in zip(stacks, pre, sols):
        w = sol[:, DN_DV:].astype(BF16)
        q_dec = (p["q"] * p["egc"]).astype(BF16)
        wss.append([_dot(jnp.concatenate([w[g * C:(g + 1) * C], q_dec[g * C:(g + 1) * C]], axis=0),
                         s_ref[h].astype(BF16)) for g, h in enumerate(heads)])
    outs = [None] * DN_HEADS
    for heads, p, sol, ws, qk in zip(stacks, pre, sols, wss, qks):
        v_new = [(sol[g * C:(g + 1) * C, :DN_DV] - ws[g][:C]).astype(BF16) for g in range(G)]
        for g, h in enumerate(heads):
            s_ref[h] = (s_ref[h] * jnp.exp(gc_all[C - 1:C, h:h + 1])
                        + _dot_tn(p["k_dec"][g * C:(g + 1) * C], v_new[g]))
        o = jnp.concatenate([w[C:] for w in ws], axis=0) + _dot(qk, jnp.concatenate(v_new, axis=0))
        o = o * lax.rsqrt(jnp.mean(o * o, axis=-1, keepdims=True) + EPS) * nw_ref[...]
        og = o * (p["zs"] * _sigmoid(p["zs"]))
        for g, h in enumerate(heads):
            outs[h] = og[g * C:(g + 1) * C]
    return outs


def _short_conv_silu(xbuf, wc_ref, rows):
    first = 8 - (SHORT_W - 1)
    acc = xbuf[first:first + rows, :] * wc_ref[0:1, :]
    for w in range(1, SHORT_W):
        acc = acc + xbuf[first + w:first + w + rows, :] * wc_ref[w:w + 1, :]
    return acc * _sigmoid(acc)


def _gdn_prompt_kernel(q_ref, k_ref, v_ref, z_ref, ab_ref, wc_ref, alog_ref, dtb_ref, nw_ref,
                       o_ref, s_ref, xbuf):
    c = pl.program_id(1)
    n_qk = DN_HEADS * DN_DK

    @pl.when(c == 0)
    def _():
        s_ref[...] = jnp.zeros_like(s_ref)
        xbuf[0:8, :] = jnp.zeros((8, xbuf.shape[1]), F32)

    xbuf[8:8 + CHUNK, 0:n_qk] = q_ref[...]
    xbuf[8:8 + CHUNK, n_qk:2 * n_qk] = k_ref[...]
    xbuf[8:8 + CHUNK, 2 * n_qk:] = v_ref[...]
    x = _short_conv_silu(xbuf, wc_ref, CHUNK)
    xbuf[0:8, :] = xbuf[CHUNK:CHUNK + 8, :]
    rows = lax.broadcasted_iota(jnp.int32, (CHUNK, 1), 0)
    valid = jnp.logical_or(rows >= FRONT, c > 0).astype(F32)
    outs = _gdn_chunk(x[:, 0:n_qk], x[:, n_qk:2 * n_qk], x[:, 2 * n_qk:], z_ref[...], ab_ref[...], valid,
                      s_ref.at[0], alog_ref, dtb_ref, nw_ref, nil=CHUNK)
    for h in range(DN_HEADS):
        o_ref[:, h * DN_DV:(h + 1) * DN_DV] = outs[h].astype(BF16)


def _gdn_prompt(p, ab, B, LP, w_conv, alog, dtb, nw):
    nc = LP // CHUNK
    n_qk = DN_HEADS * DN_DK
    blk = lambda col: pl.BlockSpec((CHUNK, n_qk), lambda b, c: (b * nc + c, col))
    vec = lambda n: pl.BlockSpec((1, n), lambda b, c: (0, 0))
    return pl.pallas_call(
        _gdn_prompt_kernel,
        out_shape=(jax.ShapeDtypeStruct((B * LP, n_qk), BF16),
                   jax.ShapeDtypeStruct((B, DN_HEADS, DN_DK, DN_DV), F32)),
        grid=(B, nc),
        in_specs=[blk(2), blk(3), blk(4), blk(5),
                  pl.BlockSpec((CHUNK, LANES), lambda b, c: (b * nc + c, 0)),
                  pl.BlockSpec((SHORT_W, 3 * n_qk), lambda b, c: (0, 0)),
                  vec(LANES), vec(LANES), vec(DN_DV)],
        out_specs=(pl.BlockSpec((CHUNK, n_qk), lambda b, c: (b * nc + c, 0)),
                   pl.BlockSpec((1, DN_HEADS, DN_DK, DN_DV), lambda b, c: (b, 0, 0, 0))),
        scratch_shapes=[pltpu.VMEM((CHUNK + 8, 3 * n_qk), F32)],
        compiler_params=_params(("arbitrary", "arbitrary")),
        name="gdn_prompt",
    )(p, p, p, p, ab, w_conv, alog, dtb, nw)


def _gdn_sample_kernel(st_ref, q_ref, k_ref, v_ref, z_ref, ab_ref, s0_ref, wc_ref, alog_ref, dtb_ref, nw_ref,
                       o_ref, s_ref, xbuf, zbuf, abbuf, *, sb, ls):
    n_qk = DN_HEADS * DN_DK
    C = SAMPLE_CHUNK
    hist = SHORT_W - 1
    xbuf[...] = jnp.zeros_like(xbuf)
    zbuf[...] = jnp.zeros_like(zbuf)
    abbuf[...] = jnp.zeros_like(abbuf)
    s_ref[...] = s0_ref[...]
    valid = (lax.broadcasted_iota(jnp.int32, (C, 1), 0) < ls).astype(F32)
    for s in range(sb):
        xb = xbuf.at[s]
        xb[8 - hist:8, :] = st_ref[s]
        xb[8:8 + ls, 0:n_qk] = q_ref[s]
        xb[8:8 + ls, n_qk:2 * n_qk] = k_ref[s]
        xb[8:8 + ls, 2 * n_qk:] = v_ref[s]
        zbuf[s, 0:ls, :] = z_ref[s]
        abbuf[s, 0:ls, :] = ab_ref[s]
        x = _short_conv_silu(xb, wc_ref, C)
        outs = _gdn_chunk(x[:, 0:n_qk], x[:, n_qk:2 * n_qk], x[:, 2 * n_qk:], zbuf[s], abbuf[s], valid,
                          s_ref.at[s], alog_ref, dtb_ref, nw_ref, nil=ls)
        for h in range(DN_HEADS):
            o_ref[s, :, h * DN_DV:(h + 1) * DN_DV] = outs[h][0:ls, :]


def _gdn_sample(p_s3, ab_s3, st_conv, s0, w_conv, alog, dtb, nw):
    NB, ls, _ = p_s3.shape
    n_qk = DN_HEADS * DN_DK
    hist = SHORT_W - 1
    assert ls <= SAMPLE_CHUNK
    sb = 4 if NB % 4 == 0 else 1
    kern = functools.partial(_gdn_sample_kernel, sb=sb, ls=ls)
    blk = lambda col: pl.BlockSpec((sb, ls, n_qk), lambda i: (i, 0, col))
    vec = lambda n: pl.BlockSpec((1, n), lambda i: (0, 0))
    sspec = lambda: pl.BlockSpec((sb, DN_HEADS, DN_DK, DN_DV), lambda i: (i, 0, 0, 0))
    return pl.pallas_call(
        kern,
        out_shape=(jax.ShapeDtypeStruct((NB, ls, n_qk), F32),
                   jax.ShapeDtypeStruct((NB, DN_HEADS, DN_DK, DN_DV), F32)),
        grid=(NB // sb,),
        in_specs=[pl.BlockSpec((sb, hist, 3 * n_qk), lambda i: (i, 0, 0)),
                  blk(2), blk(3), blk(4), blk(5),
                  pl.BlockSpec((sb, ls, LANES), lambda i: (i, 0, 0)),
                  sspec(),
                  pl.BlockSpec((SHORT_W, 3 * n_qk), lambda i: (0, 0)),
                  vec(LANES), vec(LANES), vec(DN_DV)],
        out_specs=(pl.BlockSpec((sb, ls, n_qk), lambda i: (i, 0, 0)), sspec()),
        scratch_shapes=[pltpu.VMEM((sb, SAMPLE_CHUNK + 8, 3 * n_qk), F32),
                        pltpu.VMEM((sb, SAMPLE_CHUNK, n_qk), F32),
                        pltpu.VMEM((sb, SAMPLE_CHUNK, LANES), F32)],
        compiler_params=_params(("arbitrary",)),
        name="gdn_sample",
    )(st_conv, p_s3, p_s3, p_s3, p_s3, ab_s3, s0, w_conv, alog, dtb, nw)


def _pack_bf16_pairs(x):
    half = x.shape[1] // 2
    lo = lax.bitcast_convert_type(x[:, :half].astype(BF16).astype(F32), jnp.uint32)
    hi = lax.bitcast_convert_type(x[:, half:].astype(BF16).astype(F32), jnp.uint32)
    return jnp.bitwise_or(jnp.bitwise_and(hi, jnp.uint32(0xFFFF0000)), lax.shift_right_logical(lo, jnp.uint32(16)))


def _unpack_bf16_pairs(xp):
    lo = lax.bitcast_convert_type(lax.shift_left(xp, jnp.uint32(16)), F32)
    hi = lax.bitcast_convert_type(jnp.bitwise_and(xp, jnp.uint32(0xFFFF0000)), F32)
    return jnp.concatenate([lo, hi], axis=1).astype(BF16)


def _merge_kernel(c_ref, og_ref, ga_ref, gb_ref, h_ref, wco_ref, bco_ref, wdo_ref, wo_ref, nf_ref, wr_ref, br_ref,
                  h1_ref, xp_ref, ei_ref, gt_ref, rk_ref, cnt_ref, carry):
    step = pl.program_id(0)

    @pl.when(step == 0)
    def _():
        carry[...] = jnp.zeros_like(carry)

    ya = _dot(c_ref[...], wco_ref[...]) + bco_ref[...]
    yb = _dot(og_ref[...], wdo_ref[...])
    mixed = _sigmoid(ga_ref[...]) * ya + _sigmoid(gb_ref[...]) * yb
    h1 = h_ref[...] + _dot(mixed.astype(BF16), wo_ref[...])
    h1_ref[...] = h1
    xn = h1 * lax.rsqrt(jnp.mean(h1 * h1, axis=-1, keepdims=True) + EPS) * nf_ref[...]
    xp_ref[...] = _pack_bf16_pairs(xn)
    logits = _dot(xn.astype(BF16), wr_ref[...]) + br_ref[...]

    tm = logits.shape[0]
    lane = lax.broadcasted_iota(jnp.int32, (tm, LANES), 1)
    work = logits
    sels, vals = [], []
    for _ in range(TOP_K):
        m = jnp.max(work, axis=-1, keepdims=True)
        idx = jnp.min(jnp.where(work == m, lane, N_EXPERTS - 1), axis=-1, keepdims=True)
        sel = lane == idx
        sels.append(sel)
        vals.append(m)
        work = jnp.where(sel, -jnp.inf, work)
    exps = [jnp.exp(v - vals[0]) for v in vals]
    denom = exps[0]
    for e in exps[1:]:
        denom = denom + e
    onehot = jnp.zeros((tm, LANES), F32)
    for sel in sels:
        onehot = onehot + sel.astype(F32)
    ri = lax.broadcasted_iota(jnp.int32, (tm, tm), 0)
    ci = lax.broadcasted_iota(jnp.int32, (tm, tm), 1)
    before = _dot((ri > ci).astype(BF16), onehot.astype(BF16)) + carry[...]
    ei = jnp.zeros((tm, LANES), jnp.int32)
    gt = jnp.zeros((tm, LANES), F32)
    rk = jnp.zeros((tm, LANES), jnp.int32)
    for k in range(TOP_K):
        at_k = lane == k
        e_k = jnp.max(jnp.where(sels[k], lane, 0), axis=-1, keepdims=True)
        r_k = jnp.sum(jnp.where(sels[k], before, 0.0), axis=-1, keepdims=True).astype(jnp.int32)
        ei = jnp.where(at_k, e_k, ei)
        gt = jnp.where(at_k, exps[k] / denom, gt)
        rk = jnp.where(at_k, r_k, rk)
    ei_ref[...] = ei
    gt_ref[...] = gt
    rk_ref[...] = rk
    carry[...] = carry[...] + jnp.sum(onehot, axis=0, keepdims=True)
    cnt_ref[...] = carry[...]


def _merge(c, og, p, h, wco, bco, wdo, wo, nf, wr, br):
    T = h.shape[0]
    tm = _row_tile(T, 512)
    row = lambda col: pl.BlockSpec((tm, D_MODEL), lambda i: (i, col))
    full = lambda a, b: pl.BlockSpec((a, b), lambda i: (0, 0))
    lanes = lambda: pl.BlockSpec((tm, LANES), lambda i: (i, 0))
    return pl.pallas_call(
        _merge_kernel,
        out_shape=(jax.ShapeDtypeStruct((T, D_MODEL), F32),
                   jax.ShapeDtypeStruct((T, D_MODEL // 2), jnp.uint32),
                   jax.ShapeDtypeStruct((T, LANES), jnp.int32),
                   jax.ShapeDtypeStruct((T, LANES), F32),
                   jax.ShapeDtypeStruct((T, LANES), jnp.int32),
                   jax.ShapeDtypeStruct((1, LANES), F32)),
        grid=(T // tm,),
        in_specs=[row(0), row(0), row(6), row(7), row(0),
                  full(D_MODEL, D_MODEL), full(1, D_MODEL), full(D_MODEL, D_MODEL), full(D_MODEL, D_MODEL),
                  full(1, D_MODEL), full(D_MODEL, LANES), full(1, LANES)],
        out_specs=(row(0), pl.BlockSpec((tm, D_MODEL // 2), lambda i: (i, 0)), lanes(), lanes(), lanes(),
                   full(1, LANES)),
        scratch_shapes=[pltpu.VMEM((1, LANES), F32)],
        compiler_params=_params(("arbitrary",)),
        name="merge",
    )(c, og, p, p, h, wco, bco, wdo, wo, nf, wr, br)


def _row_copy(src, src_row, dst, dst_row, sem):
    return pltpu.make_async_copy(src.at[pl.ds(src_row, 1), :], dst.at[pl.ds(dst_row, 1), :], sem)


def _dispatch_kernel(dest_ref, x_ref, xb_in, xb_out, sem, *, tm):
    del xb_in

    def issue(t, carry):
        for k in range(TOP_K):
            _row_copy(x_ref, t, xb_out, dest_ref[t * TOP_K + k], sem).start(priority=k % 2)
        return carry

    def drain(t, carry):
        for k in range(TOP_K):
            _row_copy(x_ref, 0, xb_out, 0, sem).wait()
        return carry

    lax.fori_loop(0, tm, issue, 0)
    lax.fori_loop(0, tm, drain, 0)


def _dispatch(dest, xp, xb_init):
    T, W = xp.shape
    tm = _row_tile(T, 256)
    return pl.pallas_call(
        functools.partial(_dispatch_kernel, tm=tm),
        out_shape=jax.ShapeDtypeStruct(xb_init.shape, xb_init.dtype),
        grid=(T // tm,),
        in_specs=[pl.BlockSpec((tm * TOP_K,), lambda i: (i,), memory_space=pltpu.SMEM),
                  pl.BlockSpec((tm, W), lambda i: (i, 0)),
                  pl.BlockSpec(memory_space=pl.ANY)],
        out_specs=pl.BlockSpec(memory_space=pl.ANY),
        scratch_shapes=[pltpu.SemaphoreType.DMA],
        input_output_aliases={2: 0},
        compiler_params=_params(("arbitrary",)),
        name="moe_dispatch",
    )(dest, xp, xb_init)


def _moe_kernel(be_ref, nu_ref, x_ref, wu_ref, bu_ref, wd_ref, bd_ref, o_ref, wub, wdb):
    i = pl.program_id(0)

    @pl.when(jnp.logical_or(i == 0, be_ref[i] != be_ref[jnp.maximum(i - 1, 0)]))
    def _():
        wub[...] = wu_ref[0].astype(BF16)
        wdb[...] = wd_ref[0].astype(BF16)

    @pl.when(i < nu_ref[0])
    def _():
        hmid = _dot(_unpack_bf16_pairs(x_ref[...]), wub[...]) + bu_ref[0]
        hg = jnp.minimum(hmid[:, :D_FF], SWIGLU_LIMIT)
        hl = jnp.clip(hmid[:, D_FF:], -SWIGLU_LIMIT, SWIGLU_LIMIT)
        act = hg * _sigmoid(SWIGLU_ALPHA * hg) * (hl + 1.0)
        o_ref[...] = _dot(act.astype(BF16), wdb[...]) + bd_ref[0]

    @pl.when(i >= nu_ref[0])
    def _():
        o_ref[...] = jnp.zeros_like(o_ref)


def _moe(block_e, n_used, xb, w_up, b_up, w_down, b_down):
    R = xb.shape[0]
    nb = R // MOE_ROWS
    grid_spec = pltpu.PrefetchScalarGridSpec(
        num_scalar_prefetch=2,
        grid=(nb,),
        in_specs=[pl.BlockSpec((MOE_ROWS, D_MODEL // 2), lambda i, be, nu: (i, 0)),
                  pl.BlockSpec((1, D_MODEL, 2 * D_FF), lambda i, be, nu: (be[i], 0, 0)),
                  pl.BlockSpec((1, 1, 2 * D_FF), lambda i, be, nu: (be[i], 0, 0)),
                  pl.BlockSpec((1, D_FF, D_MODEL), lambda i, be, nu: (be[i], 0, 0)),
                  pl.BlockSpec((1, 1, D_MODEL), lambda i, be, nu: (be[i], 0, 0))],
        out_specs=pl.BlockSpec((MOE_ROWS, D_MODEL), lambda i, be, nu: (i, 0)),
        scratch_shapes=[pltpu.VMEM((D_MODEL, 2 * D_FF), BF16), pltpu.VMEM((D_FF, D_MODEL), BF16)],
    )
    return pl.pallas_call(
        _moe_kernel,
        out_shape=jax.ShapeDtypeStruct((R, D_MODEL), F32),
        grid_spec=grid_spec,
        compiler_params=pltpu.CompilerParams(dimension_semantics=("arbitrary",), vmem_limit_bytes=MOE_VMEM_LIMIT),
        name="moe_experts",
    )(block_e, n_used, xb, w_up, b_up, w_down, b_down)


def _dispatch_plan(counts, ei, rk, T):
    A = T * TOP_K
    n_blocks = -(-A // MOE_ROWS) + N_EXPERTS
    counts = counts[0, :N_EXPERTS].astype(jnp.int32)
    padded = (counts + MOE_ROWS - 1) // MOE_ROWS * MOE_ROWS
    pend = jnp.cumsum(padded)
    pstart = pend - padded
    dest = (pstart[ei[:, :TOP_K]] + rk[:, :TOP_K]).reshape(-1).astype(jnp.int32)
    starts = jnp.arange(n_blocks, dtype=jnp.int32) * MOE_ROWS
    block_e = jnp.minimum(jnp.sum((pend[None, :] <= starts[:, None]).astype(jnp.int32), axis=1), N_EXPERTS - 1)
    n_used = (pend[N_EXPERTS - 1:] // MOE_ROWS).astype(jnp.int32)
    return dest, block_e, n_used, n_blocks * MOE_ROWS


def _combine_kernel(dest_ref, gt_ref, h_ref, nw_ref, yb_ref, o_ref, buf, sem, *, tm):
    def issue(t, carry):
        for k in range(TOP_K):
            _row_copy(yb_ref, dest_ref[t * TOP_K + k], buf.at[k], t, sem).start(priority=k % 2)
        return carry

    def drain(t, carry):
        for k in range(TOP_K):
            _row_copy(yb_ref, 0, buf.at[k], 0, sem).wait()
        return carry

    lax.fori_loop(0, tm, issue, 0)
    lax.fori_loop(0, tm, drain, 0)
    gt = gt_ref[...]
    x = h_ref[...]
    for k in range(TOP_K):
        x = x + gt[:, k:k + 1] * buf[k]
    o_ref[...] = x * lax.rsqrt(jnp.mean(x * x, axis=-1, keepdims=True) + EPS) * nw_ref[...]


def _combine(dest, gt, h1, nw, yb):
    T = h1.shape[0]
    tm = _row_tile(T, 256)
    row = lambda: pl.BlockSpec((tm, D_MODEL), lambda i: (i, 0))
    return pl.pallas_call(
        functools.partial(_combine_kernel, tm=tm),
        out_shape=jax.ShapeDtypeStruct((T, D_MODEL), F32),
        grid=(T // tm,),
        in_specs=[pl.BlockSpec((tm * TOP_K,), lambda i: (i,), memory_space=pltpu.SMEM),
                  pl.BlockSpec((tm, LANES), lambda i: (i, 0)),
                  row(),
                  pl.BlockSpec((1, D_MODEL), lambda i: (0, 0)),
                  pl.BlockSpec(memory_space=pl.ANY)],
        out_specs=row(),
        scratch_shapes=[pltpu.VMEM((TOP_K, tm, D_MODEL), F32), pltpu.SemaphoreType.DMA],
        compiler_params=_params(("arbitrary",)),
        name="moe_combine_final",
    )(dest, gt, h1, nw, yb)


def _pad_lanes(v, fill=0.0):
    v = v.reshape(1, -1).astype(F32)
    return jnp.pad(v, ((0, 0), (0, LANES - v.shape[1])), constant_values=fill)


def kernel(x_prompt, x_sample, state_conf_conv, state_dn_conv, state_dn_S, meta_tokens, norm_mix, w_in, w_conf_dw, b_conf_dw, ln_conf_g, ln_conf_b, w_conf_out, b_conf_out, w_dn_conv, dn_a_log, dn_dt_bias, dn_norm_w, w_dn_out, w_out, norm_ffn, w_router, b_router, w_up, b_up, w_down, b_down, norm_final):
    B, SEQ, D = x_prompt.shape
    NB, LS, _ = x_sample.shape
    depth = w_in.shape[0]
    assert D == D_MODEL and depth == 1 and SEQ % CHUNK == 0 and LS >= SHORT_W - 1
    LP = FRONT + N_META + SEQ
    TP = B * LP
    T = TP + NB * LS
    n_qk = DN_HEADS * DN_DK
    o_q = 2 * D_MODEL
    o_a = o_q + 4 * n_qk
    o_gate = o_a + 2 * DN_HEADS

    meta = jnp.broadcast_to(meta_tokens[None].astype(F32), (B, N_META, D))
    hp = jnp.concatenate([jnp.zeros((B, FRONT, D), F32), meta, x_prompt], axis=1).reshape(TP, D)
    h0 = jnp.concatenate([hp, x_sample.reshape(NB * LS, D)], axis=0)

    w_in0 = w_in[0]
    w_main = jnp.concatenate([w_in0[:, :o_a], w_in0[:, o_gate:]], axis=1)
    w_ab = jnp.pad(w_in0[:, o_a:o_gate], ((0, 0), (0, LANES - 2 * DN_HEADS))).astype(BF16)
    wco = w_conf_out[0].astype(BF16)
    wdo = w_dn_out[0].astype(BF16)
    wo = w_out[0].astype(BF16)
    wr = jnp.pad(w_router[0], ((0, 0), (0, LANES - N_EXPERTS))).astype(BF16)
    br = _pad_lanes(b_router[0], fill=-1e30)
    alog = _pad_lanes(dn_a_log[0])
    dtb = _pad_lanes(dn_dt_bias[0])
    row = lambda v: v.reshape(1, -1).astype(F32)

    xn, ab = _rms_ab(h0, row(norm_mix[0]), w_ab)
    p = _mm_in(xn, w_main)
    p_s3 = p[TP:].reshape(NB, LS, 8 * D_MODEL)
    ab_s3 = ab[TP:].reshape(NB, LS, LANES)

    c_p, ust_p = _conf_prompt(p, B, LP, w_conf_dw[0], row(b_conf_dw[0]), row(ln_conf_g[0]), row(ln_conf_b[0]))
    c_s, conf_state_s = _conf_sample(p_s3, state_conf_conv[0], w_conf_dw[0], row(b_conf_dw[0]),
                                     row(ln_conf_g[0]), row(ln_conf_b[0]))
    c_all = jnp.concatenate([c_p, c_s.reshape(NB * LS, D).astype(BF16)], axis=0)

    og_p, s_p = _gdn_prompt(p, ab, B, LP, w_dn_conv[0], alog, dtb, row(dn_norm_w[0]))
    og_s, s_s = _gdn_sample(p_s3, ab_s3, state_dn_conv[0], state_dn_S[0], w_dn_conv[0], alog, dtb,
                            row(dn_norm_w[0]))
    og_all = jnp.concatenate([og_p, og_s.reshape(NB * LS, n_qk).astype(BF16)], axis=0)

    h1, xp2, ei, gt, rk, counts = _merge(c_all, og_all, p, h0, wco, row(b_conf_out[0]), wdo, wo,
                                         row(norm_ffn[0]), wr, br)

    dest, block_e, n_used, R = _dispatch_plan(counts, ei, rk, T)
    xb = _dispatch(dest, xp2, jnp.zeros((R, D // 2), jnp.uint32))
    yb = _moe(block_e, n_used, xb, w_up[0], b_up[0].reshape(N_EXPERTS, 1, -1),
              w_down[0], b_down[0].reshape(N_EXPERTS, 1, -1))
    y = _combine(dest, gt, h1, row(norm_final), yb)

    y_prompt = y[:TP].reshape(B, LP, D)[:, FRONT + N_META:]
    y_sample = y[TP:].reshape(NB, LS, D)
    hist = CONV_W - 1
    conf_conv_prompt = ust_p[:, HALO - hist:][None]
    dn_conv_prompt = jnp.stack([p[(b + 1) * LP - (SHORT_W - 1):(b + 1) * LP, o_q:o_q + 3 * n_qk]
                                for b in range(B)])[None]
    dn_conv_sample = p_s3[:, LS - (SHORT_W - 1):, o_q:o_q + 3 * n_qk][None]
    return (y_prompt, y_sample, conf_conv_prompt, dn_conv_prompt, s_p[None],
            conf_state_s[None], dn_conv_sample, s_s[None])
```

```python
import functools

import jax
import jax.numpy as jnp
from jax import lax
from jax.experimental import pallas as pl
from jax.experimental.pallas import tpu as pltpu

D_MODEL = 1024
N_META = 16
CONV_W = 31
SHORT_W = 4
DN_HEADS = 8
DN_DK = 128
DN_DV = 128
CHUNK = 64
N_EXPERTS = 32
TOP_K = 4
D_FF = 1024
SWIGLU_LIMIT = 7.0
SWIGLU_ALPHA = 1.702
EPS = 1e-6

FRONT = (-N_META) % CHUNK
SAMPLE_CHUNK = 16
STACK = 128
LANES = 128
HALO = 32
MOE_ROWS = 512
VMEM_LIMIT = 48 * 1024 * 1024
MOE_VMEM_LIMIT = 58 * 1024 * 1024

F32 = jnp.float32
BF16 = jnp.bfloat16


def _row_tile(n, pref):
    best = 16
    for t in range(16, min(n, pref) + 1, 16):
        if n % t == 0:
            best = t
    assert n % best == 0
    return best


def _sigmoid(x):
    return 1.0 / (1.0 + jnp.exp(-x))


def _dot(a, b):
    return jnp.dot(a, b, preferred_element_type=F32)


def _dot_nt(a, b):
    return lax.dot_general(a, b, (((1,), (1,)), ((), ())), preferred_element_type=F32)


def _dot_tn(a, b):
    return lax.dot_general(a, b, (((0,), (0,)), ((), ())), preferred_element_type=F32)


def _params(sem):
    return pltpu.CompilerParams(dimension_semantics=sem, vmem_limit_bytes=VMEM_LIMIT)


def _rms_ab_kernel(h_ref, nw_ref, wab_ref, xn_ref, ab_ref):
    x = h_ref[...]
    y = x * lax.rsqrt(jnp.mean(x * x, axis=-1, keepdims=True) + EPS) * nw_ref[...]
    yb = y.astype(BF16)
    xn_ref[...] = yb
    ab_ref[...] = _dot(yb, wab_ref[...])


def _rms_ab(h, norm_w, w_ab):
    T = h.shape[0]
    tm = _row_tile(T, 1024)
    return pl.pallas_call(
        _rms_ab_kernel,
        out_shape=(jax.ShapeDtypeStruct((T, D_MODEL), BF16), jax.ShapeDtypeStruct((T, LANES), F32)),
        grid=(T // tm,),
        in_specs=[pl.BlockSpec((tm, D_MODEL), lambda i: (i, 0)),
                  pl.BlockSpec((1, D_MODEL), lambda i: (0, 0)),
                  pl.BlockSpec((D_MODEL, LANES), lambda i: (0, 0))],
        out_specs=(pl.BlockSpec((tm, D_MODEL), lambda i: (i, 0)),
                   pl.BlockSpec((tm, LANES), lambda i: (i, 0))),
        compiler_params=_params(("arbitrary",)),
        name="rms_ab",
    )(h, norm_w, w_ab)


def _mm_in_kernel(x_ref, w_ref, o_ref, wb_ref):
    @pl.when(pl.program_id(1) == 0)
    def _():
        wb_ref[...] = w_ref[...].astype(BF16)

    o_ref[...] = _dot(x_ref[...], wb_ref[...])


def _mm_in(xn, w):
    T, K = xn.shape
    N = w.shape[1]
    tm = _row_tile(T, 1024)
    tn = 1024
    assert N % tn == 0
    return pl.pallas_call(
        _mm_in_kernel,
        out_shape=jax.ShapeDtypeStruct((T, N), F32),
        grid=(N // tn, T // tm),
        in_specs=[pl.BlockSpec((tm, K), lambda j, i: (i, 0)),
                  pl.BlockSpec((K, tn), lambda j, i: (0, j))],
        out_specs=pl.BlockSpec((tm, tn), lambda j, i: (i, j)),
        scratch_shapes=[pltpu.VMEM((K, tn), BF16)],
        compiler_params=_params(("arbitrary", "arbitrary")),
        name="in_proj",
    )(xn, w)


def _ln_silu(x, g, b):
    mu = jnp.mean(x, axis=-1, keepdims=True)
    xc = x - mu
    var = jnp.mean(xc * xc, axis=-1, keepdims=True)
    y = xc * lax.rsqrt(var + EPS) * g + b
    return y * _sigmoid(y)


def _conf_prompt_kernel(pa_ref, pb_ref, ha_ref, hb_ref, wdw_ref, bdw_ref, lng_ref, lnb_ref,
                        c_ref, ust_ref, ubuf, cbuf, *, tl, rt, ct):
    t = pl.program_id(1)
    u = pa_ref[...] * _sigmoid(pb_ref[...])
    uh = ha_ref[...] * _sigmoid(hb_ref[...])
    ubuf[0:HALO, :] = jnp.where(t > 0, uh, 0.0)
    ubuf[HALO:, :] = u
    first = HALO - (CONV_W - 1)
    for r0 in range(0, tl, rt):
        for c0 in range(0, D_MODEL, ct):
            acc = jnp.zeros((rt, ct), F32)
            for w in range(CONV_W):
                acc = acc + ubuf[r0 + first + w:r0 + first + w + rt, c0:c0 + ct] * wdw_ref[w:w + 1, c0:c0 + ct]
            cbuf[r0:r0 + rt, c0:c0 + ct] = acc + bdw_ref[:, c0:c0 + ct]
    c_ref[...] = _ln_silu(cbuf[...], lng_ref[...], lnb_ref[...]).astype(BF16)

    @pl.when(t == pl.num_programs(1) - 1)
    def _():
        ust_ref[0] = ubuf[tl:tl + HALO, :]


def _conf_prompt(p, B, LP, w_dw, b_dw, ln_g, ln_b):
    tl = 192 if LP % 192 == 0 else CHUNK
    nt = LP // tl
    hb = tl // HALO
    kern = functools.partial(_conf_prompt_kernel, tl=tl, rt=32, ct=256)
    halo_idx = lambda b, t: (jnp.maximum((b * nt + t) * hb - 1, 0), 0)
    halo_idx1 = lambda b, t: (jnp.maximum((b * nt + t) * hb - 1, 0), 1)
    vec = lambda: pl.BlockSpec((1, D_MODEL), lambda b, t: (0, 0))
    return pl.pallas_call(
        kern,
        out_shape=(jax.ShapeDtypeStruct((B * LP, D_MODEL), BF16),
                   jax.ShapeDtypeStruct((B, HALO, D_MODEL), F32)),
        grid=(B, nt),
        in_specs=[pl.BlockSpec((tl, D_MODEL), lambda b, t: (b * nt + t, 0)),
                  pl.BlockSpec((tl, D_MODEL), lambda b, t: (b * nt + t, 1)),
                  pl.BlockSpec((HALO, D_MODEL), halo_idx),
                  pl.BlockSpec((HALO, D_MODEL), halo_idx1),
                  pl.BlockSpec((CONV_W, D_MODEL), lambda b, t: (0, 0)),
                  vec(), vec(), vec()],
        out_specs=(pl.BlockSpec((tl, D_MODEL), lambda b, t: (b * nt + t, 0)),
                   pl.BlockSpec((1, HALO, D_MODEL), lambda b, t: (b, 0, 0))),
        scratch_shapes=[pltpu.VMEM((HALO + tl, D_MODEL), F32), pltpu.VMEM((tl, D_MODEL), F32)],
        compiler_params=_params(("arbitrary", "arbitrary")),
        name="conf_prompt",
    )(p, p, p, p, w_dw, b_dw, ln_g, ln_b)


def _conf_sample_kernel(st_ref, pa_ref, pb_ref, wdw_ref, bdw_ref, lng_ref, lnb_ref,
                        c_ref, nst_ref, xh, *, sb, ls):
    hist = CONV_W - 1
    for s in range(sb):
        u = pa_ref[s] * _sigmoid(pb_ref[s])
        xh[0:hist, :] = st_ref[s]
        xh[hist:hist + ls, :] = u
        acc = jnp.zeros((ls, D_MODEL), F32)
        for w in range(CONV_W):
            acc = acc + xh[w:w + ls, :] * wdw_ref[w:w + 1, :]
        c_ref[s] = _ln_silu(acc + bdw_ref[...], lng_ref[...], lnb_ref[...])
        nst_ref[s] = xh[ls:ls + hist, :]


def _conf_sample(p_s3, state, w_dw, b_dw, ln_g, ln_b):
    NB, ls, _ = p_s3.shape
    hist = CONV_W - 1
    sb = 8 if NB % 8 == 0 else 1
    kern = functools.partial(_conf_sample_kernel, sb=sb, ls=ls)
    vec = lambda: pl.BlockSpec((1, D_MODEL), lambda i: (0, 0))
    return pl.pallas_call(
        kern,
        out_shape=(jax.ShapeDtypeStruct((NB, ls, D_MODEL), F32),
                   jax.ShapeDtypeStruct((NB, hist, D_MODEL), F32)),
        grid=(NB // sb,),
        in_specs=[pl.BlockSpec((sb, hist, D_MODEL), lambda i: (i, 0, 0)),
                  pl.BlockSpec((sb, ls, D_MODEL), lambda i: (i, 0, 0)),
                  pl.BlockSpec((sb, ls, D_MODEL), lambda i: (i, 0, 1)),
                  pl.BlockSpec((CONV_W, D_MODEL), lambda i: (0, 0)),
                  vec(), vec(), vec()],
        out_specs=(pl.BlockSpec((sb, ls, D_MODEL), lambda i: (i, 0, 0)),
                   pl.BlockSpec((sb, hist, D_MODEL), lambda i: (i, 0, 0))),
        scratch_shapes=[pltpu.VMEM((hist + ls + 8, D_MODEL), F32)],
        compiler_params=_params(("arbitrary",)),
        name="conf_sample",
    )(state, p_s3, p_s3, w_dw, b_dw, ln_g, ln_b)


def _split(a):
    hi = a.astype(BF16)
    return hi, (a - hi.astype(F32)).astype(BF16)


def _mm3(a, b):
    ah, al = a
    bh, bl = b
    return _dot(jnp.concatenate([ah, al, ah], axis=1), jnp.concatenate([bh, bh, bl], axis=0))


def _tri_inverse(ms, i, j, C, nil):
    same = lambda n: (i >> (n.bit_length() - 1)) == (j >> (n.bit_length() - 1))
    base = min(16, C)
    eye = (i == j).astype(F32)
    dps = [jnp.where(same(base), m, 0.0) for m in ms]
    xs = [eye - d for d in dps]
    for _ in range(max(0, (min(base, nil) - 1).bit_length() - 1)):
        sp = [_split(d) for d in dps]
        dps = [_mm3(s, s) for s in sp]
        xs = [x + _mm3(_split(d), _split(x)) for d, x in zip(dps, xs)]
    blk = base
    while blk < C:
        sel = jnp.logical_and(same(2 * blk), jnp.logical_not(same(blk)))
        xsp = [_split(x) for x in xs]
        ys = [_mm3(_split(jnp.where(sel, m, 0.0)), x) for m, x in zip(ms, xsp)]
        xs = [x - _mm3(xp, _split(y)) for x, xp, y in zip(xs, xsp, ys)]
        blk *= 2
    return xs


def _gdn_chunk(xq, xk, xv, z, ab, valid, s_ref, alog_ref, dtb_ref, nw_ref, nil):
    C = xq.shape[0]
    G = STACK // C
    ok = valid > 0.5
    ri = lax.broadcasted_iota(jnp.int32, (C, C), 0)
    ci = lax.broadcasted_iota(jnp.int32, (C, C), 1)
    tril = (ri >= ci).astype(BF16)
    i = lax.broadcasted_iota(jnp.int32, (STACK, STACK), 0)
    j = lax.broadcasted_iota(jnp.int32, (STACK, STACK), 1)
    shift = C.bit_length() - 1
    same = (i >> shift) == (j >> shift)
    causal = jnp.logical_and(same, i >= j)
    strict = jnp.logical_and(same, i > j)

    xa = ab + dtb_ref[...]
    softplus = jnp.maximum(xa, 0.0) + jnp.log(1.0 + jnp.exp(-jnp.abs(xa)))
    g_all = jnp.where(ok, -jnp.exp(alog_ref[...]) * softplus, 0.0)
    beta_all = jnp.where(ok, _sigmoid(ab), 0.0)
    g1 = g_all.astype(BF16)
    r1 = g_all - g1.astype(F32)
    g2 = r1.astype(BF16)
    g3 = (r1 - g2.astype(F32)).astype(BF16)
    gc_all = _dot(tril, g1) + _dot(tril, g2) + _dot(tril, g3)

    ok_st = jnp.concatenate([valid] * G, axis=0) > 0.5
    stacks = [list(range(h0, h0 + G)) for h0 in range(0, DN_HEADS, G)]
    pre = []
    for heads in stacks:
        stack = lambda x: jnp.concatenate([x[:, h * DN_DK:(h + 1) * DN_DK] for h in heads], axis=0)
        col = lambda a, off: jnp.concatenate([a[:, off + h:off + h + 1] for h in heads], axis=0)
        q = stack(xq)
        k = stack(xk)
        q = jnp.where(ok_st, q * lax.rsqrt(jnp.sum(q * q, axis=-1, keepdims=True) + EPS) * (DN_DK ** -0.5), 0.0)
        k = jnp.where(ok_st, k * lax.rsqrt(jnp.sum(k * k, axis=-1, keepdims=True) + EPS), 0.0)
        v = jnp.where(ok_st, stack(xv), 0.0)
        gc = col(gc_all, 0)
        beta = col(beta_all, DN_HEADS)
        g_last = jnp.concatenate([jnp.broadcast_to(gc_all[C - 1:C, h:h + 1], (C, 1)) for h in heads], axis=0)
        gb = jnp.broadcast_to(gc, (STACK, STACK))
        decay = jnp.where(causal, jnp.exp(jnp.where(causal, gb - gb.T, 0.0)), 0.0)
        egc = jnp.exp(gc)
        kb = k * beta
        pre.append(dict(q=q, k=k, kb=kb, kbf=k.astype(BF16), decay=decay, egc=egc,
                        rhs=jnp.concatenate([v * beta, kb * egc], axis=1),
                        k_dec=(k * jnp.exp(g_last - gc)).astype(BF16), zs=stack(z)))
    ms = [jnp.where(strict, _dot_nt(p["kb"].astype(BF16), p["kbf"]) * p["decay"], 0.0) for p in pre]
    qks = [jnp.where(causal, _dot_nt(p["q"].astype(BF16), p["kbf"]) * p["decay"], 0.0).astype(BF16) for p in pre]
    invs = _tri_inverse(ms, i, j, C, nil)
    sols = [_mm3(_split(inv), _split(p["rhs"])) for inv, p in zip(invs, pre)]
    wss = []
    for heads, p, sol in zip(stacks, pre, sols):
        w = sol[:, DN_DV:].astype(BF16)
        q_dec = (p["q"] * p["egc"]).astype(BF16)
        wss.append([_dot(jnp.concatenate([w[g * C:(g + 1) * C], q_dec[g * C:(g + 1) * C]], axis=0),
                         s_ref[h].astype(BF16)) for g, h in enumerate(heads)])
    outs = [None] * DN_HEADS
    for heads, p, sol, ws, qk in zip(stacks, pre, sols, wss, qks):
        v_new = [(sol[g * C:(g + 1) * C, :DN_DV] - ws[g][:C]).astype(BF16) for g in range(G)]
        for g, h in enumerate(heads):
            s_ref[h] = (s_ref[h] * jnp.exp(gc_all[C - 1:C, h:h + 1])
                        + _dot_tn(p["k_dec"][g * C:(g + 1) * C], v_new[g]))
        o = jnp.concatenate([w[C:] for w in ws], axis=0) + _dot(qk, jnp.concatenate(v_new, axis=0))
        o = o * lax.rsqrt(jnp.mean(o * o, axis=-1, keepdims=True) + EPS) * nw_ref[...]
        og = o * (p["zs"] * _sigmoid(p["zs"]))
        for g, h in enumerate(heads):
            outs[h] = og[g * C:(g + 1) * C]
    return outs


def _short_conv_silu(xbuf, wc_ref, rows):
    first = 8 - (SHORT_W - 1)
    acc = xbuf[first:first + rows, :] * wc_ref[0:1, :]
    for w in range(1, SHORT_W):
        acc = acc + xbuf[first + w:first + w + rows, :] * wc_ref[w:w + 1, :]
    return acc * _sigmoid(acc)


def _gdn_prompt_kernel(q_ref, k_ref, v_ref, z_ref, ab_ref, wc_ref, alog_ref, dtb_ref, nw_ref,
                       o_ref, s_ref, xbuf):
    c = pl.program_id(1)
    n_qk = DN_HEADS * DN_DK

    @pl.when(c == 0)
    def _():
        s_ref[...] = jnp.zeros_like(s_ref)
        xbuf[0:8, :] = jnp.zeros((8, xbuf.shape[1]), F32)

    xbuf[8:8 + CHUNK, 0:n_qk] = q_ref[...]
    xbuf[8:8 + CHUNK, n_qk:2 * n_qk] = k_ref[...]
    xbuf[8:8 + CHUNK, 2 * n_qk:] = v_ref[...]
    x = _short_conv_silu(xbuf, wc_ref, CHUNK)
    xbuf[0:8, :] = xbuf[CHUNK:CHUNK + 8, :]
    rows = lax.broadcasted_iota(jnp.int32, (CHUNK, 1), 0)
    valid = jnp.logical_or(rows >= FRONT, c > 0).astype(F32)
    outs = _gdn_chunk(x[:, 0:n_qk], x[:, n_qk:2 * n_qk], x[:, 2 * n_qk:], z_ref[...], ab_ref[...], valid,
                      s_ref.at[0], alog_ref, dtb_ref, nw_ref, nil=CHUNK)
    for h in range(DN_HEADS):
        o_ref[:, h * DN_DV:(h + 1) * DN_DV] = outs[h].astype(BF16)


def _gdn_prompt(p, ab, B, LP, w_conv, alog, dtb, nw):
    nc = LP // CHUNK
    n_qk = DN_HEADS * DN_DK
    blk = lambda col: pl.BlockSpec((CHUNK, n_qk), lambda b, c: (b * nc + c, col))
    vec = lambda n: pl.BlockSpec((1, n), lambda b, c: (0, 0))
    return pl.pallas_call(
        _gdn_prompt_kernel,
        out_shape=(jax.ShapeDtypeStruct((B * LP, n_qk), BF16),
                   jax.ShapeDtypeStruct((B, DN_HEADS, DN_DK, DN_DV), F32)),
        grid=(B, nc),
        in_specs=[blk(2), blk(3), blk(4), blk(5),
                  pl.BlockSpec((CHUNK, LANES), lambda b, c: (b * nc + c, 0)),
                  pl.BlockSpec((SHORT_W, 3 * n_qk), lambda b, c: (0, 0)),
                  vec(LANES), vec(LANES), vec(DN_DV)],
        out_specs=(pl.BlockSpec((CHUNK, n_qk), lambda b, c: (b * nc + c, 0)),
                   pl.BlockSpec((1, DN_HEADS, DN_DK, DN_DV), lambda b, c: (b, 0, 0, 0))),
        scratch_shapes=[pltpu.VMEM((CHUNK + 8, 3 * n_qk), F32)],
        compiler_params=_params(("arbitrary", "arbitrary")),
        name="gdn_prompt",
    )(p, p, p, p, ab, w_conv, alog, dtb, nw)


def _gdn_sample_kernel(st_ref, q_ref, k_ref, v_ref, z_ref, ab_ref, s0_ref, wc_ref, alog_ref, dtb_ref, nw_ref,
                       o_ref, s_ref, xbuf, zbuf, abbuf, *, sb, ls):
    n_qk = DN_HEADS * DN_DK
    C = SAMPLE_CHUNK
    hist = SHORT_W - 1
    xbuf[...] = jnp.zeros_like(xbuf)
    zbuf[...] = jnp.zeros_like(zbuf)
    abbuf[...] = jnp.zeros_like(abbuf)
    s_ref[...] = s0_ref[...]
    valid = (lax.broadcasted_iota(jnp.int32, (C, 1), 0) < ls).astype(F32)
    for s in range(sb):
        xb = xbuf.at[s]
        xb[8 - hist:8, :] = st_ref[s]
        xb[8:8 + ls, 0:n_qk] = q_ref[s]
        xb[8:8 + ls, n_qk:2 * n_qk] = k_ref[s]
        xb[8:8 + ls, 2 * n_qk:] = v_ref[s]
        zbuf[s, 0:ls, :] = z_ref[s]
        abbuf[s, 0:ls, :] = ab_ref[s]
        x = _short_conv_silu(xb, wc_ref, C)
        outs = _gdn_chunk(x[:, 0:n_qk], x[:, n_qk:2 * n_qk], x[:, 2 * n_qk:], zbuf[s], abbuf[s], valid,
                          s_ref.at[s], alog_ref, dtb_ref, nw_ref, nil=ls)
        for h in range(DN_HEADS):
            o_ref[s, :, h * DN_DV:(h + 1) * DN_DV] = outs[h][0:ls, :]


def _gdn_sample(p_s3, ab_s3, st_conv, s0, w_conv, alog, dtb, nw):
    NB, ls, _ = p_s3.shape
    n_qk = DN_HEADS * DN_DK
    hist = SHORT_W - 1
    assert ls <= SAMPLE_CHUNK
    sb = 4 if NB % 4 == 0 else 1
    kern = functools.partial(_gdn_sample_kernel, sb=sb, ls=ls)
    blk = lambda col: pl.BlockSpec((sb, ls, n_qk), lambda i: (i, 0, col))
    vec = lambda n: pl.BlockSpec((1, n), lambda i: (0, 0))
    sspec = lambda: pl.BlockSpec((sb, DN_HEADS, DN_DK, DN_DV), lambda i: (i, 0, 0, 0))
    return pl.pallas_call(
        kern,
        out_shape=(jax.ShapeDtypeStruct((NB, ls, n_qk), F32),
                   jax.ShapeDtypeStruct((NB, DN_HEADS, DN_DK, DN_DV), F32)),
        grid=(NB // sb,),
        in_specs=[pl.BlockSpec((sb, hist, 3 * n_qk), lambda i: (i, 0, 0)),
                  blk(2), blk(3), blk(4), blk(5),
                  pl.BlockSpec((sb, ls, LANES), lambda i: (i, 0, 0)),
                  sspec(),
                  pl.BlockSpec((SHORT_W, 3 * n_qk), lambda i: (0, 0)),
                  vec(LANES), vec(LANES), vec(DN_DV)],
        out_specs=(pl.BlockSpec((sb, ls, n_qk), lambda i: (i, 0, 0)), sspec()),
        scratch_shapes=[pltpu.VMEM((sb, SAMPLE_CHUNK + 8, 3 * n_qk), F32),
                        pltpu.VMEM((sb, SAMPLE_CHUNK, n_qk), F32),
                        pltpu.VMEM((sb, SAMPLE_CHUNK, LANES), F32)],
        compiler_params=_params(("arbitrary",)),
        name="gdn_sample",
    )(st_conv, p_s3, p_s3, p_s3, p_s3, ab_s3, s0, w_conv, alog, dtb, nw)


def _pack_bf16_pairs(x):
    half = x.shape[1] // 2
    lo = lax.bitcast_convert_type(x[:, :half].astype(BF16).astype(F32), jnp.uint32)
    hi = lax.bitcast_convert_type(x[:, half:].astype(BF16).astype(F32), jnp.uint32)
    return jnp.bitwise_or(jnp.bitwise_and(hi, jnp.uint32(0xFFFF0000)), lax.shift_right_logical(lo, jnp.uint32(16)))


def _unpack_bf16_pairs(xp):
    lo = lax.bitcast_convert_type(lax.shift_left(xp, jnp.uint32(16)), F32)
    hi = lax.bitcast_convert_type(jnp.bitwise_and(xp, jnp.uint32(0xFFFF0000)), F32)
    return jnp.concatenate([lo, hi], axis=1).astype(BF16)


def _merge_kernel(c_ref, og_ref, ga_ref, gb_ref, h_ref, wco_ref, bco_ref, wdo_ref, wo_ref, nf_ref, wr_ref, br_ref,
                  h1_ref, xp_ref, ei_ref, gt_ref, rk_ref, cnt_ref, carry):
    step = pl.program_id(0)

    @pl.when(step == 0)
    def _():
        carry[...] = jnp.zeros_like(carry)

    ya = _dot(c_ref[...], wco_ref[...]) + bco_ref[...]
    yb = _dot(og_ref[...], wdo_ref[...])
    mixed = _sigmoid(ga_ref[...]) * ya + _sigmoid(gb_ref[...]) * yb
    h1 = h_ref[...] + _dot(mixed.astype(BF16), wo_ref[...])
    h1_ref[...] = h1
    xn = h1 * lax.rsqrt(jnp.mean(h1 * h1, axis=-1, keepdims=True) + EPS) * nf_ref[...]
    xp_ref[...] = _pack_bf16_pairs(xn)
    logits = _dot(xn.astype(BF16), wr_ref[...]) + br_ref[...]

    tm = logits.shape[0]
    lane = lax.broadcasted_iota(jnp.int32, (tm, LANES), 1)
    work = logits
    sels, vals = [], []
    for _ in range(TOP_K):
        m = jnp.max(work, axis=-1, keepdims=True)
        idx = jnp.min(jnp.where(work == m, lane, N_EXPERTS - 1), axis=-1, keepdims=True)
        sel = lane == idx
        sels.append(sel)
        vals.append(m)
        work = jnp.where(sel, -jnp.inf, work)
    exps = [jnp.exp(v - vals[0]) for v in vals]
    denom = exps[0]
    for e in exps[1:]:
        denom = denom + e
    onehot = jnp.zeros((tm, LANES), F32)
    for sel in sels:
        onehot = onehot + sel.astype(F32)
    ri = lax.broadcasted_iota(jnp.int32, (tm, tm), 0)
    ci = lax.broadcasted_iota(jnp.int32, (tm, tm), 1)
    before = _dot((ri > ci).astype(BF16), onehot.astype(BF16)) + carry[...]
    ei = jnp.zeros((tm, LANES), jnp.int32)
    gt = jnp.zeros((tm, LANES), F32)
    rk = jnp.zeros((tm, LANES), jnp.int32)
    for k in range(TOP_K):
        at_k = lane == k
        e_k = jnp.max(jnp.where(sels[k], lane, 0), axis=-1, keepdims=True)
        r_k = jnp.sum(jnp.where(sels[k], before, 0.0), axis=-1, keepdims=True).astype(jnp.int32)
        ei = jnp.where(at_k, e_k, ei)
        gt = jnp.where(at_k, exps[k] / denom, gt)
        rk = jnp.where(at_k, r_k, rk)
    ei_ref[...] = ei
    gt_ref[...] = gt
    rk_ref[...] = rk
    carry[...] = carry[...] + jnp.sum(onehot, axis=0, keepdims=True)
    cnt_ref[...] = carry[...]


def _merge(c, og, p, h, wco, bco, wdo, wo, nf, wr, br):
    T = h.shape[0]
    tm = _row_tile(T, 512)
    row = lambda col: pl.BlockSpec((tm, D_MODEL), lambda i: (i, col))
    full = lambda a, b: pl.BlockSpec((a, b), lambda i: (0, 0))
    lanes = lambda: pl.BlockSpec((tm, LANES), lambda i: (i, 0))
    return pl.pallas_call(
        _merge_kernel,
        out_shape=(jax.ShapeDtypeStruct((T, D_MODEL), F32),
                   jax.ShapeDtypeStruct((T, D_MODEL // 2), jnp.uint32),
                   jax.ShapeDtypeStruct((T, LANES), jnp.int32),
                   jax.ShapeDtypeStruct((T, LANES), F32),
                   jax.ShapeDtypeStruct((T, LANES), jnp.int32),
                   jax.ShapeDtypeStruct((1, LANES), F32)),
        grid=(T // tm,),
        in_specs=[row(0), row(0), row(6), row(7), row(0),
                  full(D_MODEL, D_MODEL), full(1, D_MODEL), full(D_MODEL, D_MODEL), full(D_MODEL, D_MODEL),
                  full(1, D_MODEL), full(D_MODEL, LANES), full(1, LANES)],
        out_specs=(row(0), pl.BlockSpec((tm, D_MODEL // 2), lambda i: (i, 0)), lanes(), lanes(), lanes(),
                   full(1, LANES)),
        scratch_shapes=[pltpu.VMEM((1, LANES), F32)],
        compiler_params=_params(("arbitrary",)),
        name="merge",
    )(c, og, p, p, h, wco, bco, wdo, wo, nf, wr, br)


ISSUE_UNROLL = 8


def _row_copy(src, src_row, dst, dst_row, sem):
    return pltpu.make_async_copy(src.at[pl.ds(src_row, 1), :], dst.at[pl.ds(dst_row, 1), :], sem)


def _dispatch_kernel(dest_ref, x_ref, xb_in, xb_out, sem, *, tm):
    del xb_in

    def issue(t, carry):
        for k in range(TOP_K):
            _row_copy(x_ref, t, xb_out, dest_ref[t * TOP_K + k], sem).start()
        return carry

    lax.fori_loop(0, tm, issue, 0, unroll=ISSUE_UNROLL)
    for k in range(TOP_K):
        pltpu.make_async_copy(x_ref, xb_out.at[pl.ds(0, tm), :], sem).wait()


def _dispatch(dest, xp, xb_init):
    T, W = xp.shape
    tm = _row_tile(T, 256)
    return pl.pallas_call(
        functools.partial(_dispatch_kernel, tm=tm),
        out_shape=jax.ShapeDtypeStruct(xb_init.shape, xb_init.dtype),
        grid=(T // tm,),
        in_specs=[pl.BlockSpec((tm * TOP_K,), lambda i: (i,), memory_space=pltpu.SMEM),
                  pl.BlockSpec((tm, W), lambda i: (i, 0)),
                  pl.BlockSpec(memory_space=pl.ANY)],
        out_specs=pl.BlockSpec(memory_space=pl.ANY),
        scratch_shapes=[pltpu.SemaphoreType.DMA],
        input_output_aliases={2: 0},
        compiler_params=_params(("arbitrary",)),
        name="moe_dispatch",
    )(dest, xp, xb_init)


def _moe_kernel(be_ref, nu_ref, x_ref, wu_ref, bu_ref, wd_ref, bd_ref, o_ref, wub, wdb):
    i = pl.program_id(0)

    @pl.when(jnp.logical_or(i == 0, be_ref[i] != be_ref[jnp.maximum(i - 1, 0)]))
    def _():
        wub[...] = wu_ref[0].astype(BF16)
        wdb[...] = wd_ref[0].astype(BF16)

    @pl.when(i < nu_ref[0])
    def _():
        hmid = _dot(_unpack_bf16_pairs(x_ref[...]), wub[...]) + bu_ref[0]
        hg = jnp.minimum(hmid[:, :D_FF], SWIGLU_LIMIT)
        hl = jnp.clip(hmid[:, D_FF:], -SWIGLU_LIMIT, SWIGLU_LIMIT)
        act = hg * _sigmoid(SWIGLU_ALPHA * hg) * (hl + 1.0)
        o_ref[...] = _dot(act.astype(BF16), wdb[...]) + bd_ref[0]

    @pl.when(i >= nu_ref[0])
    def _():
        o_ref[...] = jnp.zeros_like(o_ref)


def _moe(block_e, n_used, xb, w_up, b_up, w_down, b_down):
    R = xb.shape[0]
    nb = R // MOE_ROWS
    grid_spec = pltpu.PrefetchScalarGridSpec(
        num_scalar_prefetch=2,
        grid=(nb,),
        in_specs=[pl.BlockSpec((MOE_ROWS, D_MODEL // 2), lambda i, be, nu: (i, 0)),
                  pl.BlockSpec((1, D_MODEL, 2 * D_FF), lambda i, be, nu: (be[i], 0, 0)),
                  pl.BlockSpec((1, 1, 2 * D_FF), lambda i, be, nu: (be[i], 0, 0)),
                  pl.BlockSpec((1, D_FF, D_MODEL), lambda i, be, nu: (be[i], 0, 0)),
                  pl.BlockSpec((1, 1, D_MODEL), lambda i, be, nu: (be[i], 0, 0))],
        out_specs=pl.BlockSpec((MOE_ROWS, D_MODEL), lambda i, be, nu: (i, 0)),
        scratch_shapes=[pltpu.VMEM((D_MODEL, 2 * D_FF), BF16), pltpu.VMEM((D_FF, D_MODEL), BF16)],
    )
    return pl.pallas_call(
        _moe_kernel,
        out_shape=jax.ShapeDtypeStruct((R, D_MODEL), F32),
        grid_spec=grid_spec,
        compiler_params=pltpu.CompilerParams(dimension_semantics=("arbitrary",), vmem_limit_bytes=MOE_VMEM_LIMIT),
        name="moe_experts",
    )(block_e, n_used, xb, w_up, b_up, w_down, b_down)


def _dispatch_plan(counts, ei, rk, T):
    A = T * TOP_K
    n_blocks = -(-A // MOE_ROWS) + N_EXPERTS
    counts = counts[0, :N_EXPERTS].astype(jnp.int32)
    padded = (counts + MOE_ROWS - 1) // MOE_ROWS * MOE_ROWS
    pend = jnp.cumsum(padded)
    pstart = pend - padded
    dest = (pstart[ei[:, :TOP_K]] + rk[:, :TOP_K]).reshape(-1).astype(jnp.int32)
    starts = jnp.arange(n_blocks, dtype=jnp.int32) * MOE_ROWS
    block_e = jnp.minimum(jnp.sum((pend[None, :] <= starts[:, None]).astype(jnp.int32), axis=1), N_EXPERTS - 1)
    n_used = (pend[N_EXPERTS - 1:] // MOE_ROWS).astype(jnp.int32)
    return dest, block_e, n_used, n_blocks * MOE_ROWS


def _combine_kernel(dest_ref, gt_ref, h_ref, nw_ref, yb_ref, o_ref, buf, sem, *, tm):
    def issue(t, carry):
        for k in range(TOP_K):
            _row_copy(yb_ref, dest_ref[t * TOP_K + k], buf.at[k], t, sem).start()
        return carry

    lax.fori_loop(0, tm, issue, 0, unroll=ISSUE_UNROLL)
    for k in range(TOP_K):
        pltpu.make_async_copy(yb_ref.at[pl.ds(0, tm), :], buf.at[k], sem).wait()
    gt = gt_ref[...]
    x = h_ref[...]
    for k in range(TOP_K):
        x = x + gt[:, k:k + 1] * buf[k]
    o_ref[...] = x * lax.rsqrt(jnp.mean(x * x, axis=-1, keepdims=True) + EPS) * nw_ref[...]


def _combine(dest, gt, h1, nw, yb):
    T = h1.shape[0]
    tm = _row_tile(T, 256)
    row = lambda: pl.BlockSpec((tm, D_MODEL), lambda i: (i, 0))
    return pl.pallas_call(
        functools.partial(_combine_kernel, tm=tm),
        out_shape=jax.ShapeDtypeStruct((T, D_MODEL), F32),
        grid=(T // tm,),
        in_specs=[pl.BlockSpec((tm * TOP_K,), lambda i: (i,), memory_space=pltpu.SMEM),
                  pl.BlockSpec((tm, LANES), lambda i: (i, 0)),
                  row(),
                  pl.BlockSpec((1, D_MODEL), lambda i: (0, 0)),
                  pl.BlockSpec(memory_space=pl.ANY)],
        out_specs=row(),
        scratch_shapes=[pltpu.VMEM((TOP_K, tm, D_MODEL), F32), pltpu.SemaphoreType.DMA],
        compiler_params=_params(("arbitrary",)),
        name="moe_combine_final",
    )(dest, gt, h1, nw, yb)


def _pad_lanes(v, fill=0.0):
    v = v.reshape(1, -1).astype(F32)
    return jnp.pad(v, ((0, 0), (0, LANES - v.shape[1])), constant_values=fill)


def kernel(x_prompt, x_sample, state_conf_conv, state_dn_conv, state_dn_S, meta_tokens, norm_mix, w_in, w_conf_dw, b_conf_dw, ln_conf_g, ln_conf_b, w_conf_out, b_conf_out, w_dn_conv, dn_a_log, dn_dt_bias, dn_norm_w, w_dn_out, w_out, norm_ffn, w_router, b_router, w_up, b_up, w_down, b_down, norm_final):
    B, SEQ, D = x_prompt.shape
    NB, LS, _ = x_sample.shape
    depth = w_in.shape[0]
    assert D == D_MODEL and depth == 1 and SEQ % CHUNK == 0 and LS >= SHORT_W - 1
    LP = FRONT + N_META + SEQ
    TP = B * LP
    T = TP + NB * LS
    n_qk = DN_HEADS * DN_DK
    o_q = 2 * D_MODEL
    o_a = o_q + 4 * n_qk
    o_gate = o_a + 2 * DN_HEADS

    meta = jnp.broadcast_to(meta_tokens[None].astype(F32), (B, N_META, D))
    hp = jnp.concatenate([jnp.zeros((B, FRONT, D), F32), meta, x_prompt], axis=1).reshape(TP, D)
    h0 = jnp.concatenate([hp, x_sample.reshape(NB * LS, D)], axis=0)

    w_in0 = w_in[0]
    w_main = jnp.concatenate([w_in0[:, :o_a], w_in0[:, o_gate:]], axis=1)
    w_ab = jnp.pad(w_in0[:, o_a:o_gate], ((0, 0), (0, LANES - 2 * DN_HEADS))).astype(BF16)
    wco = w_conf_out[0].astype(BF16)
    wdo = w_dn_out[0].astype(BF16)
    wo = w_out[0].astype(BF16)
    wr = jnp.pad(w_router[0], ((0, 0), (0, LANES - N_EXPERTS))).astype(BF16)
    br = _pad_lanes(b_router[0], fill=-1e30)
    alog = _pad_lanes(dn_a_log[0])
    dtb = _pad_lanes(dn_dt_bias[0])
    row = lambda v: v.reshape(1, -1).astype(F32)

    xn, ab = _rms_ab(h0, row(norm_mix[0]), w_ab)
    p = _mm_in(xn, w_main)
    p_s3 = p[TP:].reshape(NB, LS, 8 * D_MODEL)
    ab_s3 = ab[TP:].reshape(NB, LS, LANES)

    c_p, ust_p = _conf_prompt(p, B, LP, w_conf_dw[0], row(b_conf_dw[0]), row(ln_conf_g[0]), row(ln_conf_b[0]))
    c_s, conf_state_s = _conf_sample(p_s3, state_conf_conv[0], w_conf_dw[0], row(b_conf_dw[0]),
                                     row(ln_conf_g[0]), row(ln_conf_b[0]))
    c_all = jnp.concatenate([c_p, c_s.reshape(NB * LS, D).astype(BF16)], axis=0)

    og_p, s_p = _gdn_prompt(p, ab, B, LP, w_dn_conv[0], alog, dtb, row(dn_norm_w[0]))
    og_s, s_s = _gdn_sample(p_s3, ab_s3, state_dn_conv[0], state_dn_S[0], w_dn_conv[0], alog, dtb,
                            row(dn_norm_w[0]))
    og_all = jnp.concatenate([og_p, og_s.reshape(NB * LS, n_qk).astype(BF16)], axis=0)

    h1, xp2, ei, gt, rk, counts = _merge(c_all, og_all, p, h0, wco, row(b_conf_out[0]), wdo, wo,
                                         row(norm_ffn[0]), wr, br)

    dest, block_e, n_used, R = _dispatch_plan(counts, ei, rk, T)
    xb = _dispatch(dest, xp2, jnp.zeros((R, D // 2), jnp.uint32))
    yb = _moe(block_e, n_used, xb, w_up[0], b_up[0].reshape(N_EXPERTS, 1, -1),
              w_down[0], b_down[0].reshape(N_EXPERTS, 1, -1))
    y = _combine(dest, gt, h1, row(norm_final), yb)

    y_prompt = y[:TP].reshape(B, LP, D)[:, FRONT + N_META:]
    y_sample = y[TP:].reshape(NB, LS, D)
    hist = CONV_W - 1
    conf_conv_prompt = ust_p[:, HALO - hist:][None]
    dn_conv_prompt = jnp.stack([p[(b + 1) * LP - (SHORT_W - 1):(b + 1) * LP, o_q:o_q + 3 * n_qk]
                                for b in range(B)])[None]
    dn_conv_sample = p_s3[:, LS - (SHORT_W - 1):, o_q:o_q + 3 * n_qk][None]
    return (y_prompt, y_sample, conf_conv_prompt, dn_conv_prompt, s_p[None],
            conf_state_s[None], dn_conv_sample, s_s[None])
```

```python
import functools

import jax
import jax.numpy as jnp
from jax import lax
from jax.experimental import pallas as pl
from jax.experimental.pallas import tpu as pltpu

D_MODEL = 1024
N_META = 16
CONV_W = 31
SHORT_W = 4
DN_HEADS = 8
DN_DK = 128
DN_DV = 128
CHUNK = 64
N_EXPERTS = 32
TOP_K = 4
D_FF = 1024
SWIGLU_LIMIT = 7.0
SWIGLU_ALPHA = 1.702
EPS = 1e-6

FRONT = (-N_META) % CHUNK
SAMPLE_CHUNK = 16
STACK = 128
LANES = 128
HALO = 32
MOE_ROWS = 512
VMEM_LIMIT = 48 * 1024 * 1024
MOE_VMEM_LIMIT = 58 * 1024 * 1024

F32 = jnp.float32
BF16 = jnp.bfloat16


def _row_tile(n, pref):
    best = 16
    for t in range(16, min(n, pref) + 1, 16):
        if n % t == 0:
            best = t
    assert n % best == 0
    return best


def _sigmoid(x):
    return 1.0 / (1.0 + jnp.exp(-x))


def _dot(a, b):
    return jnp.dot(a, b, preferred_element_type=F32)


def _dot_nt(a, b):
    return lax.dot_general(a, b, (((1,), (1,)), ((), ())), preferred_element_type=F32)


def _dot_tn(a, b):
    return lax.dot_general(a, b, (((0,), (0,)), ((), ())), preferred_element_type=F32)


def _params(sem):
    return pltpu.CompilerParams(dimension_semantics=sem, vmem_limit_bytes=VMEM_LIMIT)


def _tokens_kernel(xp_ref, xs_ref, meta_ref, h_ref, head, sems, *, lp, seq, n_prompt_rows):
    b = pl.program_id(0)
    n_head = FRONT + N_META
    head[0:FRONT, :] = jnp.zeros((FRONT, D_MODEL), F32)
    head[FRONT:, :] = meta_ref[...]
    row0 = pl.multiple_of(b * lp, 8)
    body = pltpu.make_async_copy(xp_ref.at[b], h_ref.at[pl.ds(row0 + n_head, seq), :], sems.at[0])
    front = pltpu.make_async_copy(head, h_ref.at[pl.ds(row0, n_head), :], sems.at[1])
    body.start()
    front.start()

    @pl.when(b == 0)
    def _():
        tail = pltpu.make_async_copy(xs_ref, h_ref.at[pl.ds(n_prompt_rows, xs_ref.shape[0]), :], sems.at[2])
        tail.start()
        tail.wait()

    body.wait()
    front.wait()


def _tokens(x_prompt, x_sample2, meta):
    B, SEQ, D = x_prompt.shape
    lp = FRONT + N_META + SEQ
    T = B * lp + x_sample2.shape[0]
    return pl.pallas_call(
        functools.partial(_tokens_kernel, lp=lp, seq=SEQ, n_prompt_rows=B * lp),
        out_shape=jax.ShapeDtypeStruct((T, D), F32),
        grid=(B,),
        in_specs=[pl.BlockSpec(memory_space=pl.ANY), pl.BlockSpec(memory_space=pl.ANY),
                  pl.BlockSpec((N_META, D), lambda b: (0, 0))],
        out_specs=pl.BlockSpec(memory_space=pl.ANY),
        scratch_shapes=[pltpu.VMEM((FRONT + N_META, D), F32), pltpu.SemaphoreType.DMA((3,))],
        compiler_params=_params(("arbitrary",)),
        name="token_layout",
    )(x_prompt, x_sample2, meta)


def _rms_ab_kernel(h_ref, nw_ref, wab_ref, xn_ref, ab_ref):
    x = h_ref[...]
    y = x * lax.rsqrt(jnp.mean(x * x, axis=-1, keepdims=True) + EPS) * nw_ref[...]
    yb = y.astype(BF16)
    xn_ref[...] = yb
    ab_ref[...] = _dot(yb, wab_ref[...])


def _rms_ab(h, norm_w, w_ab):
    T = h.shape[0]
    tm = _row_tile(T, 1024)
    return pl.pallas_call(
        _rms_ab_kernel,
        out_shape=(jax.ShapeDtypeStruct((T, D_MODEL), BF16), jax.ShapeDtypeStruct((T, LANES), F32)),
        grid=(T // tm,),
        in_specs=[pl.BlockSpec((tm, D_MODEL), lambda i: (i, 0)),
                  pl.BlockSpec((1, D_MODEL), lambda i: (0, 0)),
                  pl.BlockSpec((D_MODEL, LANES), lambda i: (0, 0))],
        out_specs=(pl.BlockSpec((tm, D_MODEL), lambda i: (i, 0)),
                   pl.BlockSpec((tm, LANES), lambda i: (i, 0))),
        compiler_params=_params(("arbitrary",)),
        name="rms_ab",
    )(h, norm_w, w_ab)


def _mm_in_kernel(x_ref, w_ref, o_ref, wb_ref):
    @pl.when(pl.program_id(1) == 0)
    def _():
        wb_ref[...] = w_ref[...].astype(BF16)

    o_ref[...] = _dot(x_ref[...], wb_ref[...])


def _mm_in(xn, w):
    T, K = xn.shape
    N = w.shape[1]
    tm = _row_tile(T, 1024)
    tn = 1024
    assert N % tn == 0
    return pl.pallas_call(
        _mm_in_kernel,
        out_shape=jax.ShapeDtypeStruct((T, N), F32),
        grid=(N // tn, T // tm),
        in_specs=[pl.BlockSpec((tm, K), lambda j, i: (i, 0)),
                  pl.BlockSpec((K, tn), lambda j, i: (0, j))],
        out_specs=pl.BlockSpec((tm, tn), lambda j, i: (i, j)),
        scratch_shapes=[pltpu.VMEM((K, tn), BF16)],
        compiler_params=_params(("arbitrary", "arbitrary")),
        name="in_proj",
    )(xn, w)


def _ln_silu(x, g, b):
    mu = jnp.mean(x, axis=-1, keepdims=True)
    xc = x - mu
    var = jnp.mean(xc * xc, axis=-1, keepdims=True)
    y = xc * lax.rsqrt(var + EPS) * g + b
    return y * _sigmoid(y)


def _conf_prompt_kernel(pa_ref, pb_ref, ha_ref, hb_ref, wdw_ref, bdw_ref, lng_ref, lnb_ref,
                        c_ref, ust_ref, ubuf, cbuf, *, tl, rt, ct):
    t = pl.program_id(1)
    u = pa_ref[...] * _sigmoid(pb_ref[...])
    uh = ha_ref[...] * _sigmoid(hb_ref[...])
    ubuf[0:HALO, :] = jnp.where(t > 0, uh, 0.0)
    ubuf[HALO:, :] = u
    first = HALO - (CONV_W - 1)
    for r0 in range(0, tl, rt):
        for c0 in range(0, D_MODEL, ct):
            acc = jnp.zeros((rt, ct), F32)
            for s in range(8):
                part = None
                for w in range(CONV_W):
                    if (first + w) % 8 != s:
                        continue
                    base = r0 + (first + w) // 8 * 8
                    term = ubuf[base:base + rt + (8 if s else 0), c0:c0 + ct] * wdw_ref[w:w + 1, c0:c0 + ct]
                    part = term if part is None else part + term
                if part is not None:
                    acc = acc + part[s:s + rt, :]
            cbuf[r0:r0 + rt, c0:c0 + ct] = acc + bdw_ref[:, c0:c0 + ct]
    c_ref[...] = _ln_silu(cbuf[...], lng_ref[...], lnb_ref[...]).astype(BF16)

    @pl.when(t == pl.num_programs(1) - 1)
    def _():
        ust_ref[0] = ubuf[tl:tl + HALO, :]


def _conf_prompt(p, B, LP, w_dw, b_dw, ln_g, ln_b):
    tl = 192 if LP % 192 == 0 else CHUNK
    nt = LP // tl
    hb = tl // HALO
    kern = functools.partial(_conf_prompt_kernel, tl=tl, rt=64, ct=128)
    halo_idx = lambda b, t: (jnp.maximum((b * nt + t) * hb - 1, 0), 0)
    halo_idx1 = lambda b, t: (jnp.maximum((b * nt + t) * hb - 1, 0), 1)
    vec = lambda: pl.BlockSpec((1, D_MODEL), lambda b, t: (0, 0))
    return pl.pallas_call(
        kern,
        out_shape=(jax.ShapeDtypeStruct((B * LP, D_MODEL), BF16),
                   jax.ShapeDtypeStruct((B, HALO, D_MODEL), F32)),
        grid=(B, nt),
        in_specs=[pl.BlockSpec((tl, D_MODEL), lambda b, t: (b * nt + t, 0)),
                  pl.BlockSpec((tl, D_MODEL), lambda b, t: (b * nt + t, 1)),
                  pl.BlockSpec((HALO, D_MODEL), halo_idx),
                  pl.BlockSpec((HALO, D_MODEL), halo_idx1),
                  pl.BlockSpec((CONV_W, D_MODEL), lambda b, t: (0, 0)),
                  vec(), vec(), vec()],
        out_specs=(pl.BlockSpec((tl, D_MODEL), lambda b, t: (b * nt + t, 0)),
                   pl.BlockSpec((1, HALO, D_MODEL), lambda b, t: (b, 0, 0))),
        scratch_shapes=[pltpu.VMEM((HALO + tl, D_MODEL), F32), pltpu.VMEM((tl, D_MODEL), F32)],
        compiler_params=_params(("arbitrary", "arbitrary")),
        name="conf_prompt",
    )(p, p, p, p, w_dw, b_dw, ln_g, ln_b)


def _conf_sample_kernel(st_ref, pa_ref, pb_ref, wdw_ref, bdw_ref, lng_ref, lnb_ref,
                        c_ref, nst_ref, xh, *, sb, ls):
    hist = CONV_W - 1
    for s in range(sb):
        u = pa_ref[s] * _sigmoid(pb_ref[s])
        xh[0:hist, :] = st_ref[s]
        xh[hist:hist + ls, :] = u
        acc = jnp.zeros((ls, D_MODEL), F32)
        for w in range(CONV_W):
            acc = acc + xh[w:w + ls, :] * wdw_ref[w:w + 1, :]
        c_ref[s] = _ln_silu(acc + bdw_ref[...], lng_ref[...], lnb_ref[...])
        nst_ref[s] = xh[ls:ls + hist, :]


def _conf_sample(p_s3, state, w_dw, b_dw, ln_g, ln_b):
    NB, ls, _ = p_s3.shape
    hist = CONV_W - 1
    sb = 8 if NB % 8 == 0 else 1
    kern = functools.partial(_conf_sample_kernel, sb=sb, ls=ls)
    vec = lambda: pl.BlockSpec((1, D_MODEL), lambda i: (0, 0))
    return pl.pallas_call(
        kern,
        out_shape=(jax.ShapeDtypeStruct((NB, ls, D_MODEL), F32),
                   jax.ShapeDtypeStruct((NB, hist, D_MODEL), F32)),
        grid=(NB // sb,),
        in_specs=[pl.BlockSpec((sb, hist, D_MODEL), lambda i: (i, 0, 0)),
                  pl.BlockSpec((sb, ls, D_MODEL), lambda i: (i, 0, 0)),
                  pl.BlockSpec((sb, ls, D_MODEL), lambda i: (i, 0, 1)),
                  pl.BlockSpec((CONV_W, D_MODEL), lambda i: (0, 0)),
                  vec(), vec(), vec()],
        out_specs=(pl.BlockSpec((sb, ls, D_MODEL), lambda i: (i, 0, 0)),
                   pl.BlockSpec((sb, hist, D_MODEL), lambda i: (i, 0, 0))),
        scratch_shapes=[pltpu.VMEM((hist + ls + 8, D_MODEL), F32)],
        compiler_params=_params(("arbitrary",)),
        name="conf_sample",
    )(state, p_s3, p_s3, w_dw, b_dw, ln_g, ln_b)


def _split(a):
    hi = a.astype(BF16)
    return hi, (a - hi.astype(F32)).astype(BF16)


def _mm3(a, b):
    ah, al = a
    bh, bl = b
    return _dot(jnp.concatenate([ah, al, ah], axis=1), jnp.concatenate([bh, bh, bl], axis=0))


def _tri_inverse(ms, i, j, C, nil):
    same = lambda n: (i >> (n.bit_length() - 1)) == (j >> (n.bit_length() - 1))
    base = min(16, C)
    eye = (i == j).astype(F32)
    dps = [jnp.where(same(base), m, 0.0) for m in ms]
    xs = [eye - d for d in dps]
    for _ in range(max(0, (min(base, nil) - 1).bit_length() - 1)):
        sp = [_split(d) for d in dps]
        dps = [_mm3(s, s) for s in sp]
        xs = [x + _mm3(_split(d), _split(x)) for d, x in zip(dps, xs)]
    blk = base
    while blk < C:
        sel = jnp.logical_and(same(2 * blk), jnp.logical_not(same(blk)))
        xsp = [_split(x) for x in xs]
        ys = [_mm3(_split(jnp.where(sel, m, 0.0)), x) for m, x in zip(ms, xsp)]
        xs = [x - _mm3(xp, _split(y)) for x, xp, y in zip(xs, xsp, ys)]
        blk *= 2
    return xs


def _gdn_chunk(xq, xk, xv, z, ab, valid, s_ref, alog_ref, dtb_ref, nw_ref, nil):
    C = xq.shape[0]
    G = STACK // C
    ok = valid > 0.5
    ri = lax.broadcasted_iota(jnp.int32, (C, C), 0)
    ci = lax.broadcasted_iota(jnp.int32, (C, C), 1)
    tril = (ri >= ci).astype(BF16)
    i = lax.broadcasted_iota(jnp.int32, (STACK, STACK), 0)
    j = lax.broadcasted_iota(jnp.int32, (STACK, STACK), 1)
    shift = C.bit_length() - 1
    same = (i >> shift) == (j >> shift)
    causal = jnp.logical_and(same, i >= j)
    strict = jnp.logical_and(same, i > j)

    xa = ab + dtb_ref[...]
    softplus = jnp.maximum(xa, 0.0) + jnp.log(1.0 + jnp.exp(-jnp.abs(xa)))
    g_all = jnp.where(ok, -jnp.exp(alog_ref[...]) * softplus, 0.0)
    beta_all = jnp.where(ok, _sigmoid(ab), 0.0)
    g1 = g_all.astype(BF16)
    r1 = g_all - g1.astype(F32)
    g2 = r1.astype(BF16)
    g3 = (r1 - g2.astype(F32)).astype(BF16)
    gc_all = _dot(tril, g1) + _dot(tril, g2) + _dot(tril, g3)

    ok_st = jnp.concatenate([valid] * G, axis=0) > 0.5
    stacks = [list(range(h0, h0 + G)) for h0 in range(0, DN_HEADS, G)]
    pre = []
    for heads in stacks:
        stack = lambda x: jnp.concatenate([x[:, h * DN_DK:(h + 1) * DN_DK] for h in heads], axis=0)
        col = lambda a, off: jnp.concatenate([a[:, off + h:off + h + 1] for h in heads], axis=0)
        q = stack(xq)
        k = stack(xk)
        q = jnp.where(ok_st, q * lax.rsqrt(jnp.sum(q * q, axis=-1, keepdims=True) + EPS) * (DN_DK ** -0.5), 0.0)
        k = jnp.where(ok_st, k * lax.rsqrt(jnp.sum(k * k, axis=-1, keepdims=True) + EPS), 0.0)
        v = jnp.where(ok_st, stack(xv), 0.0)
        gc = col(gc_all, 0)
        beta = col(beta_all, DN_HEADS)
        g_last = jnp.concatenate([jnp.broadcast_to(gc_all[C - 1:C, h:h + 1], (C, 1)) for h in heads], axis=0)
        gb = jnp.broadcast_to(gc, (STACK, STACK))
        decay = jnp.where(causal, jnp.exp(jnp.where(causal, gb - gb.T, 0.0)), 0.0)
        egc = jnp.exp(gc)
        kb = k * beta
        pre.append(dict(q=q, k=k, kb=kb, kbf=k.astype(BF16), decay=decay, egc=egc,
                        rhs=jnp.concatenate([v * beta, kb * egc], axis=1),
                        k_dec=(k * jnp.exp(g_last - gc)).astype(BF16), zs=stack(z)))
    ms = [jnp.where(strict, _dot_nt(p["kb"].astype(BF16), p["kbf"]) * p["decay"], 0.0) for p in pre]
    qks = [jnp.where(causal, _dot_nt(p["q"].astype(BF16), p["kbf"]) * p["decay"], 0.0).astype(BF16) for p in pre]
    invs = _tri_inverse(ms, i, j, C, nil)
    sols = [_mm3(_split(inv), _split(p["rhs"])) for inv, p in zip(invs, pre)]
    wss = []
    for heads, p, sol in zip(stacks, pre, sols):
        w = sol[:, DN_DV:].astype(BF16)
        q_dec = (p["q"] * p["egc"]).astype(BF16)
        wss.append([_dot(jnp.concatenate([w[g * C:(g + 1) * C], q_dec[g * C:(g + 1) * C]], axis=0),
                         s_ref[h].astype(BF16)) for g, h in enumerate(heads)])
    outs = [None] * DN_HEADS
    for heads, p, sol, ws, qk in zip(stacks, pre, sols, wss, qks):
        v_new = [(sol[g * C:(g + 1) * C, :DN_DV] - ws[g][:C]).astype(BF16) for g in range(G)]
        for g, h in enumerate(heads):
            s_ref[h] = (s_ref[h] * jnp.exp(gc_all[C - 1:C, h:h + 1])
                        + _dot_tn(p["k_dec"][g * C:(g + 1) * C], v_new[g]))
        o = jnp.concatenate([w[C:] for w in ws], axis=0) + _dot(qk, jnp.concatenate(v_new, axis=0))
        o = o * lax.rsqrt(jnp.mean(o * o, axis=-1, keepdims=True) + EPS) * nw_ref[...]
        og = o * (p["zs"] * _sigmoid(p["zs"]))
        for g, h in enumerate(heads):
            outs[h] = og[g * C:(g + 1) * C]
    return outs


def _short_conv_silu(xbuf, wc_ref, rows):
    first = 8 - (SHORT_W - 1)
    acc = xbuf[first:first + rows, :] * wc_ref[0:1, :]
    for w in range(1, SHORT_W):
        acc = acc + xbuf[first + w:first + w + rows, :] * wc_ref[w:w + 1, :]
    return acc * _sigmoid(acc)


def _gdn_prompt_kernel(q_ref, k_ref, v_ref, z_ref, ab_ref, wc_ref, alog_ref, dtb_ref, nw_ref,
                       o_ref, s_ref, xbuf):
    c = pl.program_id(1)
    n_qk = DN_HEADS * DN_DK

    @pl.when(c == 0)
    def _():
        s_ref[...] = jnp.zeros_like(s_ref)
        xbuf[0:8, :] = jnp.zeros((8, xbuf.shape[1]), F32)

    xbuf[8:8 + CHUNK, 0:n_qk] = q_ref[...]
    xbuf[8:8 + CHUNK, n_qk:2 * n_qk] = k_ref[...]
    xbuf[8:8 + CHUNK, 2 * n_qk:] = v_ref[...]
    x = _short_conv_silu(xbuf, wc_ref, CHUNK)
    xbuf[0:8, :] = xbuf[CHUNK:CHUNK + 8, :]
    rows = lax.broadcasted_iota(jnp.int32, (CHUNK, 1), 0)
    valid = jnp.logical_or(rows >= FRONT, c > 0).astype(F32)
    outs = _gdn_chunk(x[:, 0:n_qk], x[:, n_qk:2 * n_qk], x[:, 2 * n_qk:], z_ref[...], ab_ref[...], valid,
                      s_ref.at[0], alog_ref, dtb_ref, nw_ref, nil=CHUNK)
    for h in range(DN_HEADS):
        o_ref[:, h * DN_DV:(h + 1) * DN_DV] = outs[h].astype(BF16)


def _gdn_prompt(p, ab, B, LP, w_conv, alog, dtb, nw):
    nc = LP // CHUNK
    n_qk = DN_HEADS * DN_DK
    blk = lambda col: pl.BlockSpec((CHUNK, n_qk), lambda b, c: (b * nc + c, col))
    vec = lambda n: pl.BlockSpec((1, n), lambda b, c: (0, 0))
    return pl.pallas_call(
        _gdn_prompt_kernel,
        out_shape=(jax.ShapeDtypeStruct((B * LP, n_qk), BF16),
                   jax.ShapeDtypeStruct((B, DN_HEADS, DN_DK, DN_DV), F32)),
        grid=(B, nc),
        in_specs=[blk(2), blk(3), blk(4), blk(5),
                  pl.BlockSpec((CHUNK, LANES), lambda b, c: (b * nc + c, 0)),
                  pl.BlockSpec((SHORT_W, 3 * n_qk), lambda b, c: (0, 0)),
                  vec(LANES), vec(LANES), vec(DN_DV)],
        out_specs=(pl.BlockSpec((CHUNK, n_qk), lambda b, c: (b * nc + c, 0)),
                   pl.BlockSpec((1, DN_HEADS, DN_DK, DN_DV), lambda b, c: (b, 0, 0, 0))),
        scratch_shapes=[pltpu.VMEM((CHUNK + 8, 3 * n_qk), F32)],
        compiler_params=_params(("arbitrary", "arbitrary")),
        name="gdn_prompt",
    )(p, p, p, p, ab, w_conv, alog, dtb, nw)


def _gdn_sample_kernel(st_ref, q_ref, k_ref, v_ref, z_ref, ab_ref, s0_ref, wc_ref, alog_ref, dtb_ref, nw_ref,
                       o_ref, s_ref, xbuf, zbuf, abbuf, *, sb, ls):
    n_qk = DN_HEADS * DN_DK
    C = SAMPLE_CHUNK
    hist = SHORT_W - 1
    xbuf[...] = jnp.zeros_like(xbuf)
    zbuf[...] = jnp.zeros_like(zbuf)
    abbuf[...] = jnp.zeros_like(abbuf)
    s_ref[...] = s0_ref[...]
    valid = (lax.broadcasted_iota(jnp.int32, (C, 1), 0) < ls).astype(F32)
    for s in range(sb):
        xb = xbuf.at[s]
        xb[8 - hist:8, :] = st_ref[s]
        xb[8:8 + ls, 0:n_qk] = q_ref[s]
        xb[8:8 + ls, n_qk:2 * n_qk] = k_ref[s]
        xb[8:8 + ls, 2 * n_qk:] = v_ref[s]
        zbuf[s, 0:ls, :] = z_ref[s]
        abbuf[s, 0:ls, :] = ab_ref[s]
        x = _short_conv_silu(xb, wc_ref, C)
        outs = _gdn_chunk(x[:, 0:n_qk], x[:, n_qk:2 * n_qk], x[:, 2 * n_qk:], zbuf[s], abbuf[s], valid,
                          s_ref.at[s], alog_ref, dtb_ref, nw_ref, nil=ls)
        for h in range(DN_HEADS):
            o_ref[s, :, h * DN_DV:(h + 1) * DN_DV] = outs[h][0:ls, :]


def _gdn_sample(p_s3, ab_s3, st_conv, s0, w_conv, alog, dtb, nw):
    NB, ls, _ = p_s3.shape
    n_qk = DN_HEADS * DN_DK
    hist = SHORT_W - 1
    assert ls <= SAMPLE_CHUNK
    sb = 4 if NB % 4 == 0 else 1
    kern = functools.partial(_gdn_sample_kernel, sb=sb, ls=ls)
    blk = lambda col: pl.BlockSpec((sb, ls, n_qk), lambda i: (i, 0, col))
    vec = lambda n: pl.BlockSpec((1, n), lambda i: (0, 0))
    sspec = lambda: pl.BlockSpec((sb, DN_HEADS, DN_DK, DN_DV), lambda i: (i, 0, 0, 0))
    return pl.pallas_call(
        kern,
        out_shape=(jax.ShapeDtypeStruct((NB, ls, n_qk), F32),
                   jax.ShapeDtypeStruct((NB, DN_HEADS, DN_DK, DN_DV), F32)),
        grid=(NB // sb,),
        in_specs=[pl.BlockSpec((sb, hist, 3 * n_qk), lambda i: (i, 0, 0)),
                  blk(2), blk(3), blk(4), blk(5),
                  pl.BlockSpec((sb, ls, LANES), lambda i: (i, 0, 0)),
                  sspec(),
                  pl.BlockSpec((SHORT_W, 3 * n_qk), lambda i: (0, 0)),
                  vec(LANES), vec(LANES), vec(DN_DV)],
        out_specs=(pl.BlockSpec((sb, ls, n_qk), lambda i: (i, 0, 0)), sspec()),
        scratch_shapes=[pltpu.VMEM((sb, SAMPLE_CHUNK + 8, 3 * n_qk), F32),
                        pltpu.VMEM((sb, SAMPLE_CHUNK, n_qk), F32),
                        pltpu.VMEM((sb, SAMPLE_CHUNK, LANES), F32)],
        compiler_params=_params(("arbitrary",)),
        name="gdn_sample",
    )(st_conv, p_s3, p_s3, p_s3, p_s3, ab_s3, s0, w_conv, alog, dtb, nw)


def _pack_bf16_pairs(x):
    half = x.shape[1] // 2
    lo = lax.bitcast_convert_type(x[:, :half].astype(BF16).astype(F32), jnp.uint32)
    hi = lax.bitcast_convert_type(x[:, half:].astype(BF16).astype(F32), jnp.uint32)
    return jnp.bitwise_or(jnp.bitwise_and(hi, jnp.uint32(0xFFFF0000)), lax.shift_right_logical(lo, jnp.uint32(16)))


def _unpack_bf16_pairs(xp):
    lo = lax.bitcast_convert_type(lax.shift_left(xp, jnp.uint32(16)), F32)
    hi = lax.bitcast_convert_type(jnp.bitwise_and(xp, jnp.uint32(0xFFFF0000)), F32)
    return jnp.concatenate([lo, hi], axis=1).astype(BF16)


def _merge_kernel(c_ref, og_ref, ga_ref, gb_ref, h_ref, wco_ref, bco_ref, wdo_ref, wo_ref, nf_ref, wr_ref, br_ref,
                  h1_ref, xp_ref, ei_ref, gt_ref, rk_ref, cnt_ref, carry):
    step = pl.program_id(0)

    @pl.when(step == 0)
    def _():
        carry[...] = jnp.zeros_like(carry)

    ya = _dot(c_ref[...], wco_ref[...]) + bco_ref[...]
    yb = _dot(og_ref[...], wdo_ref[...])
    mixed = _sigmoid(ga_ref[...]) * ya + _sigmoid(gb_ref[...]) * yb
    h1 = h_ref[...] + _dot(mixed.astype(BF16), wo_ref[...])
    h1_ref[...] = h1
    xn = h1 * lax.rsqrt(jnp.mean(h1 * h1, axis=-1, keepdims=True) + EPS) * nf_ref[...]
    xp_ref[...] = _pack_bf16_pairs(xn)
    logits = _dot(xn.astype(BF16), wr_ref[...]) + br_ref[...]

    tm = logits.shape[0]
    lane = lax.broadcasted_iota(jnp.int32, (tm, LANES), 1)
    work = logits
    sels, vals = [], []
    for _ in range(TOP_K):
        m = jnp.max(work, axis=-1, keepdims=True)
        idx = jnp.min(jnp.where(work == m, lane, N_EXPERTS - 1), axis=-1, keepdims=True)
        sel = lane == idx
        sels.append(sel)
        vals.append(m)
        work = jnp.where(sel, -jnp.inf, work)
    exps = [jnp.exp(v - vals[0]) for v in vals]
    denom = exps[0]
    for e in exps[1:]:
        denom = denom + e
    onehot = jnp.zeros((tm, LANES), F32)
    for sel in sels:
        onehot = onehot + sel.astype(F32)
    ri = lax.broadcasted_iota(jnp.int32, (tm, tm), 0)
    ci = lax.broadcasted_iota(jnp.int32, (tm, tm), 1)
    before = _dot((ri > ci).astype(BF16), onehot.astype(BF16)) + carry[...]
    ei = jnp.zeros((tm, LANES), jnp.int32)
    gt = jnp.zeros((tm, LANES), F32)
    rk = jnp.zeros((tm, LANES), jnp.int32)
    for k in range(TOP_K):
        at_k = lane == k
        e_k = jnp.max(jnp.where(sels[k], lane, 0), axis=-1, keepdims=True)
        r_k = jnp.sum(jnp.where(sels[k], before, 0.0), axis=-1, keepdims=True).astype(jnp.int32)
        ei = jnp.where(at_k, e_k, ei)
        gt = jnp.where(at_k, exps[k] / denom, gt)
        rk = jnp.where(at_k, r_k, rk)
    ei_ref[...] = ei
    gt_ref[...] = gt
    rk_ref[...] = rk
    carry[...] = carry[...] + jnp.sum(onehot, axis=0, keepdims=True)
    cnt_ref[...] = carry[...]


def _merge(c, og, p, h, wco, bco, wdo, wo, nf, wr, br):
    T = h.shape[0]
    tm = _row_tile(T, 512)
    row = lambda col: pl.BlockSpec((tm, D_MODEL), lambda i: (i, col))
    full = lambda a, b: pl.BlockSpec((a, b), lambda i: (0, 0))
    lanes = lambda: pl.BlockSpec((tm, LANES), lambda i: (i, 0))
    return pl.pallas_call(
        _merge_kernel,
        out_shape=(jax.ShapeDtypeStruct((T, D_MODEL), F32),
                   jax.ShapeDtypeStruct((T, D_MODEL // 2), jnp.uint32),
                   jax.ShapeDtypeStruct((T, LANES), jnp.int32),
                   jax.ShapeDtypeStruct((T, LANES), F32),
                   jax.ShapeDtypeStruct((T, LANES), jnp.int32),
                   jax.ShapeDtypeStruct((1, LANES), F32)),
        grid=(T // tm,),
        in_specs=[row(0), row(0), row(6), row(7), row(0),
                  full(D_MODEL, D_MODEL), full(1, D_MODEL), full(D_MODEL, D_MODEL), full(D_MODEL, D_MODEL),
                  full(1, D_MODEL), full(D_MODEL, LANES), full(1, LANES)],
        out_specs=(row(0), pl.BlockSpec((tm, D_MODEL // 2), lambda i: (i, 0)), lanes(), lanes(), lanes(),
                   full(1, LANES)),
        scratch_shapes=[pltpu.VMEM((1, LANES), F32)],
        compiler_params=_params(("arbitrary",)),
        name="merge",
    )(c, og, p, p, h, wco, bco, wdo, wo, nf, wr, br)


ISSUE_UNROLL = 8


def _row_copy(src, src_row, dst, dst_row, sem):
    return pltpu.make_async_copy(src.at[pl.ds(src_row, 1), :], dst.at[pl.ds(dst_row, 1), :], sem)


def _dispatch_kernel(dest_ref, x_ref, xb_in, xb_out, sem, *, tm):
    del xb_in

    def issue(t, carry):
        for k in range(TOP_K):
            _row_copy(x_ref, t, xb_out, dest_ref[t * TOP_K + k], sem).start(priority=k % 2)
        return carry

    lax.fori_loop(0, tm, issue, 0, unroll=ISSUE_UNROLL)
    for k in range(TOP_K):
        pltpu.make_async_copy(x_ref, xb_out.at[pl.ds(0, tm), :], sem).wait()


def _dispatch(dest, xp, xb_init):
    T, W = xp.shape
    tm = _row_tile(T, 256)
    return pl.pallas_call(
        functools.partial(_dispatch_kernel, tm=tm),
        out_shape=jax.ShapeDtypeStruct(xb_init.shape, xb_init.dtype),
        grid=(T // tm,),
        in_specs=[pl.BlockSpec((tm * TOP_K,), lambda i: (i,), memory_space=pltpu.SMEM),
                  pl.BlockSpec((tm, W), lambda i: (i, 0)),
                  pl.BlockSpec(memory_space=pl.ANY)],
        out_specs=pl.BlockSpec(memory_space=pl.ANY),
        scratch_shapes=[pltpu.SemaphoreType.DMA],
        input_output_aliases={2: 0},
        compiler_params=_params(("arbitrary",)),
        name="moe_dispatch",
    )(dest, xp, xb_init)


def _moe_kernel(be_ref, nu_ref, x_ref, wu_ref, bu_ref, wd_ref, bd_ref, o_ref, wub, wdb):
    i = pl.program_id(0)

    @pl.when(jnp.logical_or(i == 0, be_ref[i] != be_ref[jnp.maximum(i - 1, 0)]))
    def _():
        wub[...] = wu_ref[0].astype(BF16)
        wdb[...] = wd_ref[0].astype(BF16)

    @pl.when(i < nu_ref[0])
    def _():
        hmid = _dot(_unpack_bf16_pairs(x_ref[...]), wub[...]) + bu_ref[0]
        hg = jnp.minimum(hmid[:, :D_FF], SWIGLU_LIMIT)
        hl = jnp.clip(hmid[:, D_FF:], -SWIGLU_LIMIT, SWIGLU_LIMIT)
        act = hg * _sigmoid(SWIGLU_ALPHA * hg) * (hl + 1.0)
        o_ref[...] = _dot(act.astype(BF16), wdb[...]) + bd_ref[0]

    @pl.when(i >= nu_ref[0])
    def _():
        o_ref[...] = jnp.zeros_like(o_ref)


def _moe(block_e, n_used, xb, w_up, b_up, w_down, b_down):
    R = xb.shape[0]
    nb = R // MOE_ROWS
    grid_spec = pltpu.PrefetchScalarGridSpec(
        num_scalar_prefetch=2,
        grid=(nb,),
        in_specs=[pl.BlockSpec((MOE_ROWS, D_MODEL // 2), lambda i, be, nu: (i, 0)),
                  pl.BlockSpec((1, D_MODEL, 2 * D_FF), lambda i, be, nu: (be[i], 0, 0)),
                  pl.BlockSpec((1, 1, 2 * D_FF), lambda i, be, nu: (be[i], 0, 0)),
                  pl.BlockSpec((1, D_FF, D_MODEL), lambda i, be, nu: (be[i], 0, 0)),
                  pl.BlockSpec((1, 1, D_MODEL), lambda i, be, nu: (be[i], 0, 0))],
        out_specs=pl.BlockSpec((MOE_ROWS, D_MODEL), lambda i, be, nu: (i, 0)),
        scratch_shapes=[pltpu.VMEM((D_MODEL, 2 * D_FF), BF16), pltpu.VMEM((D_FF, D_MODEL), BF16)],
    )
    return pl.pallas_call(
        _moe_kernel,
        out_shape=jax.ShapeDtypeStruct((R, D_MODEL), F32),
        grid_spec=grid_spec,
        compiler_params=pltpu.CompilerParams(dimension_semantics=("arbitrary",), vmem_limit_bytes=MOE_VMEM_LIMIT),
        name="moe_experts",
    )(block_e, n_used, xb, w_up, b_up, w_down, b_down)


def _dispatch_plan(counts, ei, rk, T):
    A = T * TOP_K
    n_blocks = -(-A // MOE_ROWS) + N_EXPERTS
    counts = counts[0, :N_EXPERTS].astype(jnp.int32)
    padded = (counts + MOE_ROWS - 1) // MOE_ROWS * MOE_ROWS
    pend = jnp.cumsum(padded)
    pstart = pend - padded
    dest = (pstart[ei[:, :TOP_K]] + rk[:, :TOP_K]).reshape(-1).astype(jnp.int32)
    starts = jnp.arange(n_blocks, dtype=jnp.int32) * MOE_ROWS
    block_e = jnp.minimum(jnp.sum((pend[None, :] <= starts[:, None]).astype(jnp.int32), axis=1), N_EXPERTS - 1)
    n_used = (pend[N_EXPERTS - 1:] // MOE_ROWS).astype(jnp.int32)
    return dest, block_e, n_used, n_blocks * MOE_ROWS


def _combine_kernel(dest_ref, gt_ref, h_ref, nw_ref, yb_ref, o_ref, buf, sem, *, tm):
    def issue(t, carry):
        for k in range(TOP_K):
            _row_copy(yb_ref, dest_ref[t * TOP_K + k], buf.at[k], t, sem).start(priority=k % 2)
        return carry

    lax.fori_loop(0, tm, issue, 0, unroll=ISSUE_UNROLL)
    for k in range(TOP_K):
        pltpu.make_async_copy(yb_ref.at[pl.ds(0, tm), :], buf.at[k], sem).wait()
    gt = gt_ref[...]
    x = h_ref[...]
    for k in range(TOP_K):
        x = x + gt[:, k:k + 1] * buf[k]
    o_ref[...] = x * lax.rsqrt(jnp.mean(x * x, axis=-1, keepdims=True) + EPS) * nw_ref[...]


def _combine(dest, gt, h1, nw, yb):
    T = h1.shape[0]
    tm = _row_tile(T, 256)
    row = lambda: pl.BlockSpec((tm, D_MODEL), lambda i: (i, 0))
    return pl.pallas_call(
        functools.partial(_combine_kernel, tm=tm),
        out_shape=jax.ShapeDtypeStruct((T, D_MODEL), F32),
        grid=(T // tm,),
        in_specs=[pl.BlockSpec((tm * TOP_K,), lambda i: (i,), memory_space=pltpu.SMEM),
                  pl.BlockSpec((tm, LANES), lambda i: (i, 0)),
                  row(),
                  pl.BlockSpec((1, D_MODEL), lambda i: (0, 0)),
                  pl.BlockSpec(memory_space=pl.ANY)],
        out_specs=row(),
        scratch_shapes=[pltpu.VMEM((TOP_K, tm, D_MODEL), F32), pltpu.SemaphoreType.DMA],
        compiler_params=_params(("arbitrary",)),
        name="moe_combine_final",
    )(dest, gt, h1, nw, yb)


def _pad_lanes(v, fill=0.0):
    v = v.reshape(1, -1).astype(F32)
    return jnp.pad(v, ((0, 0), (0, LANES - v.shape[1])), constant_values=fill)


def kernel(x_prompt, x_sample, state_conf_conv, state_dn_conv, state_dn_S, meta_tokens, norm_mix, w_in, w_conf_dw, b_conf_dw, ln_conf_g, ln_conf_b, w_conf_out, b_conf_out, w_dn_conv, dn_a_log, dn_dt_bias, dn_norm_w, w_dn_out, w_out, norm_ffn, w_router, b_router, w_up, b_up, w_down, b_down, norm_final):
    B, SEQ, D = x_prompt.shape
    NB, LS, _ = x_sample.shape
    depth = w_in.shape[0]
    assert D == D_MODEL and depth == 1 and SEQ % CHUNK == 0 and LS >= SHORT_W - 1
    LP = FRONT + N_META + SEQ
    TP = B * LP
    T = TP + NB * LS
    n_qk = DN_HEADS * DN_DK
    o_q = 2 * D_MODEL
    o_a = o_q + 4 * n_qk
    o_gate = o_a + 2 * DN_HEADS

    h0 = _tokens(x_prompt, x_sample.reshape(NB * LS, D), meta_tokens.astype(F32))

    w_in0 = w_in[0]
    w_main = jnp.concatenate([w_in0[:, :o_a], w_in0[:, o_gate:]], axis=1)
    w_ab = jnp.pad(w_in0[:, o_a:o_gate], ((0, 0), (0, LANES - 2 * DN_HEADS))).astype(BF16)
    wco = w_conf_out[0].astype(BF16)
    wdo = w_dn_out[0].astype(BF16)
    wo = w_out[0].astype(BF16)
    wr = jnp.pad(w_router[0], ((0, 0), (0, LANES - N_EXPERTS))).astype(BF16)
    br = _pad_lanes(b_router[0], fill=-1e30)
    alog = _pad_lanes(dn_a_log[0])
    dtb = _pad_lanes(dn_dt_bias[0])
    row = lambda v: v.reshape(1, -1).astype(F32)

    xn, ab = _rms_ab(h0, row(norm_mix[0]), w_ab)
    p = _mm_in(xn, w_main)
    p_s3 = p[TP:].reshape(NB, LS, 8 * D_MODEL)
    ab_s3 = ab[TP:].reshape(NB, LS, LANES)

    c_p, ust_p = _conf_prompt(p, B, LP, w_conf_dw[0], row(b_conf_dw[0]), row(ln_conf_g[0]), row(ln_conf_b[0]))
    c_s, conf_state_s = _conf_sample(p_s3, state_conf_conv[0], w_conf_dw[0], row(b_conf_dw[0]),
                                     row(ln_conf_g[0]), row(ln_conf_b[0]))
    c_all = jnp.concatenate([c_p, c_s.reshape(NB * LS, D).astype(BF16)], axis=0)

    og_p, s_p = _gdn_prompt(p, ab, B, LP, w_dn_conv[0], alog, dtb, row(dn_norm_w[0]))
    og_s, s_s = _gdn_sample(p_s3, ab_s3, state_dn_conv[0], state_dn_S[0], w_dn_conv[0], alog, dtb,
                            row(dn_norm_w[0]))
    og_all = jnp.concatenate([og_p, og_s.reshape(NB * LS, n_qk).astype(BF16)], axis=0)

    h1, xp2, ei, gt, rk, counts = _merge(c_all, og_all, p, h0, wco, row(b_conf_out[0]), wdo, wo,
                                         row(norm_ffn[0]), wr, br)

    dest, block_e, n_used, R = _dispatch_plan(counts, ei, rk, T)
    xb = _dispatch(dest, xp2, jnp.zeros((R, D // 2), jnp.uint32))
    yb = _moe(block_e, n_used, xb, w_up[0], b_up[0].reshape(N_EXPERTS, 1, -1),
              w_down[0], b_down[0].reshape(N_EXPERTS, 1, -1))
    y = _combine(dest, gt, h1, row(norm_final), yb)

    y_prompt = y[:TP].reshape(B, LP, D)[:, FRONT + N_META:]
    y_sample = y[TP:].reshape(NB, LS, D)
    hist = CONV_W - 1
    conf_conv_prompt = ust_p[:, HALO - hist:][None]
    dn_conv_prompt = jnp.stack([p[(b + 1) * LP - (SHORT_W - 1):(b + 1) * LP, o_q:o_q + 3 * n_qk]
                                for b in range(B)])[None]
    dn_conv_sample = p_s3[:, LS - (SHORT_W - 1):, o_q:o_q + 3 * n_qk][None]
    return (y_prompt, y_sample, conf_conv_prompt, dn_conv_prompt, s_p[None],
            conf_state_s[None], dn_conv_sample, s_s[None])
```

```python
import functools

import jax
import jax.numpy as jnp
from jax import lax
from jax.experimental import pallas as pl
from jax.experimental.pallas import tpu as pltpu

D_MODEL = 1024
N_META = 16
CONV_W = 31
SHORT_W = 4
DN_HEADS = 8
DN_DK = 128
DN_DV = 128
CHUNK = 64
N_EXPERTS = 32
TOP_K = 4
D_FF = 1024
SWIGLU_LIMIT = 7.0
SWIGLU_ALPHA = 1.702
EPS = 1e-6

FRONT = (-N_META) % CHUNK
SAMPLE_CHUNK = 16
STACK = 128
LANES = 128
HALO = 32
MOE_ROWS = 512
VMEM_LIMIT = 48 * 1024 * 1024
MOE_VMEM_LIMIT = 58 * 1024 * 1024

F32 = jnp.float32
BF16 = jnp.bfloat16


def _row_tile(n, pref):
    best = 16
    for t in range(16, min(n, pref) + 1, 16):
        if n % t == 0:
            best = t
    assert n % best == 0
    return best


def _sigmoid(x):
    return 1.0 / (1.0 + jnp.exp(-x))


def _dot(a, b):
    return jnp.dot(a, b, preferred_element_type=F32)


def _dot_nt(a, b):
    return lax.dot_general(a, b, (((1,), (1,)), ((), ())), preferred_element_type=F32)


def _dot_tn(a, b):
    return lax.dot_general(a, b, (((0,), (0,)), ((), ())), preferred_element_type=F32)


def _params(sem):
    return pltpu.CompilerParams(dimension_semantics=sem, vmem_limit_bytes=VMEM_LIMIT)


def _tokens_kernel(xp_ref, xs_ref, meta_ref, h_ref, head, sems, *, lp, tr, n_prompt_rows):
    b = pl.program_id(0)
    j = pl.program_id(1)
    n_head = FRONT + N_META
    row0 = pl.multiple_of(b * lp, 8)
    body = pltpu.make_async_copy(xp_ref.at[0], h_ref.at[pl.ds(row0 + n_head + j * tr, tr), :], sems.at[0])
    body.start()

    @pl.when(j == 0)
    def _():
        head[0:FRONT, :] = jnp.zeros((FRONT, D_MODEL), F32)
        head[FRONT:, :] = meta_ref[...]
        front = pltpu.make_async_copy(head, h_ref.at[pl.ds(row0, n_head), :], sems.at[1])
        front.start()
        front.wait()

    @pl.when(jnp.logical_and(b == 0, j == 0))
    def _():
        tail = pltpu.make_async_copy(xs_ref, h_ref.at[pl.ds(n_prompt_rows, xs_ref.shape[0]), :], sems.at[2])
        tail.start()
        tail.wait()

    body.wait()


def _tokens(x_prompt, x_sample2, meta):
    B, SEQ, D = x_prompt.shape
    lp = FRONT + N_META + SEQ
    T = B * lp + x_sample2.shape[0]
    tr = _row_tile(SEQ, 1024)
    return pl.pallas_call(
        functools.partial(_tokens_kernel, lp=lp, tr=tr, n_prompt_rows=B * lp),
        out_shape=jax.ShapeDtypeStruct((T, D), F32),
        grid=(B, SEQ // tr),
        in_specs=[pl.BlockSpec((1, tr, D), lambda b, j: (b, j, 0)),
                  pl.BlockSpec(x_sample2.shape, lambda b, j: (0, 0)),
                  pl.BlockSpec((N_META, D), lambda b, j: (0, 0))],
        out_specs=pl.BlockSpec(memory_space=pl.ANY),
        scratch_shapes=[pltpu.VMEM((FRONT + N_META, D), F32), pltpu.SemaphoreType.DMA((3,))],
        compiler_params=_params(("arbitrary", "arbitrary")),
        name="token_layout",
    )(x_prompt, x_sample2, meta)


def _rms_ab_kernel(h_ref, nw_ref, wab_ref, xn_ref, ab_ref):
    x = h_ref[...]
    y = x * lax.rsqrt(jnp.mean(x * x, axis=-1, keepdims=True) + EPS) * nw_ref[...]
    yb = y.astype(BF16)
    xn_ref[...] = yb
    ab_ref[...] = _dot(yb, wab_ref[...])


def _rms_ab(h, norm_w, w_ab):
    T = h.shape[0]
    tm = _row_tile(T, 1024)
    return pl.pallas_call(
        _rms_ab_kernel,
        out_shape=(jax.ShapeDtypeStruct((T, D_MODEL), BF16), jax.ShapeDtypeStruct((T, LANES), F32)),
        grid=(T // tm,),
        in_specs=[pl.BlockSpec((tm, D_MODEL), lambda i: (i, 0)),
                  pl.BlockSpec((1, D_MODEL), lambda i: (0, 0)),
                  pl.BlockSpec((D_MODEL, LANES), lambda i: (0, 0))],
        out_specs=(pl.BlockSpec((tm, D_MODEL), lambda i: (i, 0)),
                   pl.BlockSpec((tm, LANES), lambda i: (i, 0))),
        compiler_params=_params(("arbitrary",)),
        name="rms_ab",
    )(h, norm_w, w_ab)


def _mm_in_kernel(x_ref, w_ref, o_ref, wb_ref):
    @pl.when(pl.program_id(1) == 0)
    def _():
        wb_ref[...] = w_ref[...].astype(BF16)

    o_ref[...] = _dot(x_ref[...], wb_ref[...])


def _mm_in(xn, w):
    T, K = xn.shape
    N = w.shape[1]
    tm = _row_tile(T, 1024)
    tn = 1024
    assert N % tn == 0
    return pl.pallas_call(
        _mm_in_kernel,
        out_shape=jax.ShapeDtypeStruct((T, N), F32),
        grid=(N // tn, T // tm),
        in_specs=[pl.BlockSpec((tm, K), lambda j, i: (i, 0)),
                  pl.BlockSpec((K, tn), lambda j, i: (0, j))],
        out_specs=pl.BlockSpec((tm, tn), lambda j, i: (i, j)),
        scratch_shapes=[pltpu.VMEM((K, tn), BF16)],
        compiler_params=_params(("arbitrary", "arbitrary")),
        name="in_proj",
    )(xn, w)


def _ln_silu(x, g, b):
    mu = jnp.mean(x, axis=-1, keepdims=True)
    xc = x - mu
    var = jnp.mean(xc * xc, axis=-1, keepdims=True)
    y = xc * lax.rsqrt(var + EPS) * g + b
    return y * _sigmoid(y)


def _conf_prompt_kernel(pa_ref, pb_ref, ha_ref, hb_ref, wdw_ref, bdw_ref, lng_ref, lnb_ref,
                        c_ref, ust_ref, ubuf, cbuf, *, tl, rt, ct):
    t = pl.program_id(1)
    u = pa_ref[...] * _sigmoid(pb_ref[...])
    uh = ha_ref[...] * _sigmoid(hb_ref[...])
    ubuf[0:HALO, :] = jnp.where(t > 0, uh, 0.0)
    ubuf[HALO:, :] = u
    first = HALO - (CONV_W - 1)
    for r0 in range(0, tl, rt):
        for c0 in range(0, D_MODEL, ct):
            acc = jnp.zeros((rt, ct), F32)
            for s in range(8):
                part = None
                for w in range(CONV_W):
                    if (first + w) % 8 != s:
                        continue
                    base = r0 + (first + w) // 8 * 8
                    term = ubuf[base:base + rt + (8 if s else 0), c0:c0 + ct] * wdw_ref[w:w + 1, c0:c0 + ct]
                    part = term if part is None else part + term
                if part is not None:
                    acc = acc + part[s:s + rt, :]
            cbuf[r0:r0 + rt, c0:c0 + ct] = acc + bdw_ref[:, c0:c0 + ct]
    c_ref[...] = _ln_silu(cbuf[...], lng_ref[...], lnb_ref[...]).astype(BF16)

    @pl.when(t == pl.num_programs(1) - 1)
    def _():
        ust_ref[0] = ubuf[tl:tl + HALO, :]


def _conf_prompt(p, B, LP, w_dw, b_dw, ln_g, ln_b):
    tl = 192 if LP % 192 == 0 else CHUNK
    nt = LP // tl
    hb = tl // HALO
    kern = functools.partial(_conf_prompt_kernel, tl=tl, rt=64, ct=128)
    halo_idx = lambda b, t: (jnp.maximum((b * nt + t) * hb - 1, 0), 0)
    halo_idx1 = lambda b, t: (jnp.maximum((b * nt + t) * hb - 1, 0), 1)
    vec = lambda: pl.BlockSpec((1, D_MODEL), lambda b, t: (0, 0))
    return pl.pallas_call(
        kern,
        out_shape=(jax.ShapeDtypeStruct((B * LP, D_MODEL), BF16),
                   jax.ShapeDtypeStruct((B, HALO, D_MODEL), F32)),
        grid=(B, nt),
        in_specs=[pl.BlockSpec((tl, D_MODEL), lambda b, t: (b * nt + t, 0)),
                  pl.BlockSpec((tl, D_MODEL), lambda b, t: (b * nt + t, 1)),
                  pl.BlockSpec((HALO, D_MODEL), halo_idx),
                  pl.BlockSpec((HALO, D_MODEL), halo_idx1),
                  pl.BlockSpec((CONV_W, D_MODEL), lambda b, t: (0, 0)),
                  vec(), vec(), vec()],
        out_specs=(pl.BlockSpec((tl, D_MODEL), lambda b, t: (b * nt + t, 0)),
                   pl.BlockSpec((1, HALO, D_MODEL), lambda b, t: (b, 0, 0))),
        scratch_shapes=[pltpu.VMEM((HALO + tl, D_MODEL), F32), pltpu.VMEM((tl, D_MODEL), F32)],
        compiler_params=_params(("arbitrary", "arbitrary")),
        name="conf_prompt",
    )(p, p, p, p, w_dw, b_dw, ln_g, ln_b)


def _conf_sample_kernel(st_ref, pa_ref, pb_ref, wdw_ref, bdw_ref, lng_ref, lnb_ref,
                        c_ref, nst_ref, xh, *, sb, ls):
    hist = CONV_W - 1
    for s in range(sb):
        u = pa_ref[s] * _sigmoid(pb_ref[s])
        xh[0:hist, :] = st_ref[s]
        xh[hist:hist + ls, :] = u
        acc = jnp.zeros((ls, D_MODEL), F32)
        for w in range(CONV_W):
            acc = acc + xh[w:w + ls, :] * wdw_ref[w:w + 1, :]
        c_ref[s] = _ln_silu(acc + bdw_ref[...], lng_ref[...], lnb_ref[...])
        nst_ref[s] = xh[ls:ls + hist, :]


def _conf_sample(p_s3, state, w_dw, b_dw, ln_g, ln_b):
    NB, ls, _ = p_s3.shape
    hist = CONV_W - 1
    sb = 8 if NB % 8 == 0 else 1
    kern = functools.partial(_conf_sample_kernel, sb=sb, ls=ls)
    vec = lambda: pl.BlockSpec((1, D_MODEL), lambda i: (0, 0))
    return pl.pallas_call(
        kern,
        out_shape=(jax.ShapeDtypeStruct((NB, ls, D_MODEL), F32),
                   jax.ShapeDtypeStruct((NB, hist, D_MODEL), F32)),
        grid=(NB // sb,),
        in_specs=[pl.BlockSpec((sb, hist, D_MODEL), lambda i: (i, 0, 0)),
                  pl.BlockSpec((sb, ls, D_MODEL), lambda i: (i, 0, 0)),
                  pl.BlockSpec((sb, ls, D_MODEL), lambda i: (i, 0, 1)),
                  pl.BlockSpec((CONV_W, D_MODEL), lambda i: (0, 0)),
                  vec(), vec(), vec()],
        out_specs=(pl.BlockSpec((sb, ls, D_MODEL), lambda i: (i, 0, 0)),
                   pl.BlockSpec((sb, hist, D_MODEL), lambda i: (i, 0, 0))),
        scratch_shapes=[pltpu.VMEM((hist + ls + 8, D_MODEL), F32)],
        compiler_params=_params(("arbitrary",)),
        name="conf_sample",
    )(state, p_s3, p_s3, w_dw, b_dw, ln_g, ln_b)


def _split(a):
    hi = a.astype(BF16)
    return hi, (a - hi.astype(F32)).astype(BF16)


def _mm3(a, b):
    ah, al = a
    bh, bl = b
    return _dot(jnp.concatenate([ah, al, ah], axis=1), jnp.concatenate([bh, bh, bl], axis=0))


def _tri_inverse(ms, i, j, C, nil):
    same = lambda n: (i >> (n.bit_length() - 1)) == (j >> (n.bit_length() - 1))
    base = min(16, C)
    eye = (i == j).astype(F32)
    dps = [jnp.where(same(base), m, 0.0) for m in ms]
    xs = [eye - d for d in dps]
    for _ in range(max(0, (min(base, nil) - 1).bit_length() - 1)):
        sp = [_split(d) for d in dps]
        dps = [_mm3(s, s) for s in sp]
        xs = [x + _mm3(_split(d), _split(x)) for d, x in zip(dps, xs)]
    blk = base
    while blk < C:
        sel = jnp.logical_and(same(2 * blk), jnp.logical_not(same(blk)))
        xsp = [_split(x) for x in xs]
        ys = [_mm3(_split(jnp.where(sel, m, 0.0)), x) for m, x in zip(ms, xsp)]
        xs = [x - _mm3(xp, _split(y)) for x, xp, y in zip(xs, xsp, ys)]
        blk *= 2
    return xs


def _gdn_chunk(xq, xk, xv, z, ab, valid, s_ref, alog_ref, dtb_ref, nw_ref, nil):
    C = xq.shape[0]
    G = STACK // C
    ok = valid > 0.5
    ri = lax.broadcasted_iota(jnp.int32, (C, C), 0)
    ci = lax.broadcasted_iota(jnp.int32, (C, C), 1)
    tril = (ri >= ci).astype(BF16)
    i = lax.broadcasted_iota(jnp.int32, (STACK, STACK), 0)
    j = lax.broadcasted_iota(jnp.int32, (STACK, STACK), 1)
    shift = C.bit_length() - 1
    same = (i >> shift) == (j >> shift)
    causal = jnp.logical_and(same, i >= j)
    strict = jnp.logical_and(same, i > j)

    xa = ab + dtb_ref[...]
    softplus = jnp.maximum(xa, 0.0) + jnp.log(1.0 + jnp.exp(-jnp.abs(xa)))
    g_all = jnp.where(ok, -jnp.exp(alog_ref[...]) * softplus, 0.0)
    beta_all = jnp.where(ok, _sigmoid(ab), 0.0)
    g1 = g_all.astype(BF16)
    r1 = g_all - g1.astype(F32)
    g2 = r1.astype(BF16)
    g3 = (r1 - g2.astype(F32)).astype(BF16)
    gc_all = _dot(tril, g1) + _dot(tril, g2) + _dot(tril, g3)

    ok_st = jnp.concatenate([valid] * G, axis=0) > 0.5
    stacks = [list(range(h0, h0 + G)) for h0 in range(0, DN_HEADS, G)]
    pre = []
    for heads in stacks:
        stack = lambda x: jnp.concatenate([x[:, h * DN_DK:(h + 1) * DN_DK] for h in heads], axis=0)
        col = lambda a, off: jnp.concatenate([a[:, off + h:off + h + 1] for h in heads], axis=0)
        q = stack(xq)
        k = stack(xk)
        q = jnp.where(ok_st, q * lax.rsqrt(jnp.sum(q * q, axis=-1, keepdims=True) + EPS) * (DN_DK ** -0.5), 0.0)
        k = jnp.where(ok_st, k * lax.rsqrt(jnp.sum(k * k, axis=-1, keepdims=True) + EPS), 0.0)
        v = jnp.where(ok_st, stack(xv), 0.0)
        gc = col(gc_all, 0)
        beta = col(beta_all, DN_HEADS)
        g_last = jnp.concatenate([jnp.broadcast_to(gc_all[C - 1:C, h:h + 1], (C, 1)) for h in heads], axis=0)
        gb = jnp.broadcast_to(gc, (STACK, STACK))
        decay = jnp.where(causal, jnp.exp(jnp.where(causal, gb - gb.T, 0.0)), 0.0)
        egc = jnp.exp(gc)
        kb = k * beta
        pre.append(dict(q=q, k=k, kb=kb, kbf=k.astype(BF16), decay=decay, egc=egc,
                        rhs=jnp.concatenate([v * beta, kb * egc], axis=1),
                        k_dec=(k * jnp.exp(g_last - gc)).astype(BF16), zs=stack(z)))
    ms = [jnp.where(strict, _dot_nt(p["kb"].astype(BF16), p["kbf"]) * p["decay"], 0.0) for p in pre]
    qks = [jnp.where(causal, _dot_nt(p["q"].astype(BF16), p["kbf"]) * p["decay"], 0.0).astype(BF16) for p in pre]
    invs = _tri_inverse(ms, i, j, C, nil)
    sols = [_mm3(_split(inv), _split(p["rhs"])) for inv, p in zip(invs, pre)]
    wss = []
    for heads, p, sol in zip(stacks, pre, sols):
        w = sol[:, DN_DV:].astype(BF16)
        q_dec = (p["q"] * p["egc"]).astype(BF16)
        wss.append([_dot(jnp.concatenate([w[g * C:(g + 1) * C], q_dec[g * C:(g + 1) * C]], axis=0),
                         s_ref[h].astype(BF16)) for g, h in enumerate(heads)])
    outs = [None] * DN_HEADS
    for heads, p, sol, ws, qk in zip(stacks, pre, sols, wss, qks):
        v_new = [(sol[g * C:(g + 1) * C, :DN_DV] - ws[g][:C]).astype(BF16) for g in range(G)]
        for g, h in enumerate(heads):
            s_ref[h] = (s_ref[h] * jnp.exp(gc_all[C - 1:C, h:h + 1])
                        + _dot_tn(p["k_dec"][g * C:(g + 1) * C], v_new[g]))
        o = jnp.concatenate([w[C:] for w in ws], axis=0) + _dot(qk, jnp.concatenate(v_new, axis=0))
        o = o * lax.rsqrt(jnp.mean(o * o, axis=-1, keepdims=True) + EPS) * nw_ref[...]
        og = o * (p["zs"] * _sigmoid(p["zs"]))
        for g, h in enumerate(heads):
            outs[h] = og[g * C:(g + 1) * C]
    return outs


def _short_conv_silu(xbuf, wc_ref, rows):
    first = 8 - (SHORT_W - 1)
    acc = xbuf[first:first + rows, :] * wc_ref[0:1, :]
    for w in range(1, SHORT_W):
        acc = acc + xbuf[first + w:first + w + rows, :] * wc_ref[w:w + 1, :]
    return acc * _sigmoid(acc)


def _gdn_prompt_kernel(q_ref, k_ref, v_ref, z_ref, ab_ref, wc_ref, alog_ref, dtb_ref, nw_ref,
                       o_ref, s_ref, xbuf):
    c = pl.program_id(1)
    n_qk = DN_HEADS * DN_DK

    @pl.when(c == 0)
    def _():
        s_ref[...] = jnp.zeros_like(s_ref)
        xbuf[0:8, :] = jnp.zeros((8, xbuf.shape[1]), F32)

    xbuf[8:8 + CHUNK, 0:n_qk] = q_ref[...]
    xbuf[8:8 + CHUNK, n_qk:2 * n_qk] = k_ref[...]
    xbuf[8:8 + CHUNK, 2 * n_qk:] = v_ref[...]
    x = _short_conv_silu(xbuf, wc_ref, CHUNK)
    xbuf[0:8, :] = xbuf[CHUNK:CHUNK + 8, :]
    rows = lax.broadcasted_iota(jnp.int32, (CHUNK, 1), 0)
    valid = jnp.logical_or(rows >= FRONT, c > 0).astype(F32)
    outs = _gdn_chunk(x[:, 0:n_qk], x[:, n_qk:2 * n_qk], x[:, 2 * n_qk:], z_ref[...], ab_ref[...], valid,
                      s_ref.at[0], alog_ref, dtb_ref, nw_ref, nil=CHUNK)
    for h in range(DN_HEADS):
        o_ref[:, h * DN_DV:(h + 1) * DN_DV] = outs[h].astype(BF16)


def _gdn_prompt(p, ab, B, LP, w_conv, alog, dtb, nw):
    nc = LP // CHUNK
    n_qk = DN_HEADS * DN_DK
    blk = lambda col: pl.BlockSpec((CHUNK, n_qk), lambda b, c: (b * nc + c, col))
    vec = lambda n: pl.BlockSpec((1, n), lambda b, c: (0, 0))
    return pl.pallas_call(
        _gdn_prompt_kernel,
        out_shape=(jax.ShapeDtypeStruct((B * LP, n_qk), BF16),
                   jax.ShapeDtypeStruct((B, DN_HEADS, DN_DK, DN_DV), F32)),
        grid=(B, nc),
        in_specs=[blk(2), blk(3), blk(4), blk(5),
                  pl.BlockSpec((CHUNK, LANES), lambda b, c: (b * nc + c, 0)),
                  pl.BlockSpec((SHORT_W, 3 * n_qk), lambda b, c: (0, 0)),
                  vec(LANES), vec(LANES), vec(DN_DV)],
        out_specs=(pl.BlockSpec((CHUNK, n_qk), lambda b, c: (b * nc + c, 0)),
                   pl.BlockSpec((1, DN_HEADS, DN_DK, DN_DV), lambda b, c: (b, 0, 0, 0))),
        scratch_shapes=[pltpu.VMEM((CHUNK + 8, 3 * n_qk), F32)],
        compiler_params=_params(("arbitrary", "arbitrary")),
        name="gdn_prompt",
    )(p, p, p, p, ab, w_conv, alog, dtb, nw)


def _gdn_sample_kernel(st_ref, q_ref, k_ref, v_ref, z_ref, ab_ref, s0_ref, wc_ref, alog_ref, dtb_ref, nw_ref,
                       o_ref, s_ref, xbuf, zbuf, abbuf, *, sb, ls):
    n_qk = DN_HEADS * DN_DK
    C = SAMPLE_CHUNK
    hist = SHORT_W - 1
    xbuf[...] = jnp.zeros_like(xbuf)
    zbuf[...] = jnp.zeros_like(zbuf)
    abbuf[...] = jnp.zeros_like(abbuf)
    s_ref[...] = s0_ref[...]
    valid = (lax.broadcasted_iota(jnp.int32, (C, 1), 0) < ls).astype(F32)
    for s in range(sb):
        xb = xbuf.at[s]
        xb[8 - hist:8, :] = st_ref[s]
        xb[8:8 + ls, 0:n_qk] = q_ref[s]
        xb[8:8 + ls, n_qk:2 * n_qk] = k_ref[s]
        xb[8:8 + ls, 2 * n_qk:] = v_ref[s]
        zbuf[s, 0:ls, :] = z_ref[s]
        abbuf[s, 0:ls, :] = ab_ref[s]
        x = _short_conv_silu(xb, wc_ref, C)
        outs = _gdn_chunk(x[:, 0:n_qk], x[:, n_qk:2 * n_qk], x[:, 2 * n_qk:], zbuf[s], abbuf[s], valid,
                          s_ref.at[s], alog_ref, dtb_ref, nw_ref, nil=ls)
        for h in range(DN_HEADS):
            o_ref[s, :, h * DN_DV:(h + 1) * DN_DV] = outs[h][0:ls, :]


def _gdn_sample(p_s3, ab_s3, st_conv, s0, w_conv, alog, dtb, nw):
    NB, ls, _ = p_s3.shape
    n_qk = DN_HEADS * DN_DK
    hist = SHORT_W - 1
    assert ls <= SAMPLE_CHUNK
    sb = 4 if NB % 4 == 0 else 1
    kern = functools.partial(_gdn_sample_kernel, sb=sb, ls=ls)
    blk = lambda col: pl.BlockSpec((sb, ls, n_qk), lambda i: (i, 0, col))
    vec = lambda n: pl.BlockSpec((1, n), lambda i: (0, 0))
    sspec = lambda: pl.BlockSpec((sb, DN_HEADS, DN_DK, DN_DV), lambda i: (i, 0, 0, 0))
    return pl.pallas_call(
        kern,
        out_shape=(jax.ShapeDtypeStruct((NB, ls, n_qk), F32),
                   jax.ShapeDtypeStruct((NB, DN_HEADS, DN_DK, DN_DV), F32)),
        grid=(NB // sb,),
        in_specs=[pl.BlockSpec((sb, hist, 3 * n_qk), lambda i: (i, 0, 0)),
                  blk(2), blk(3), blk(4), blk(5),
                  pl.BlockSpec((sb, ls, LANES), lambda i: (i, 0, 0)),
                  sspec(),
                  pl.BlockSpec((SHORT_W, 3 * n_qk), lambda i: (0, 0)),
                  vec(LANES), vec(LANES), vec(DN_DV)],
        out_specs=(pl.BlockSpec((sb, ls, n_qk), lambda i: (i, 0, 0)), sspec()),
        scratch_shapes=[pltpu.VMEM((sb, SAMPLE_CHUNK + 8, 3 * n_qk), F32),
                        pltpu.VMEM((sb, SAMPLE_CHUNK, n_qk), F32),
                        pltpu.VMEM((sb, SAMPLE_CHUNK, LANES), F32)],
        compiler_params=_params(("arbitrary",)),
        name="gdn_sample",
    )(st_conv, p_s3, p_s3, p_s3, p_s3, ab_s3, s0, w_conv, alog, dtb, nw)


def _pack_bf16_pairs(x):
    half = x.shape[1] // 2
    lo = lax.bitcast_convert_type(x[:, :half].astype(BF16).astype(F32), jnp.uint32)
    hi = lax.bitcast_convert_type(x[:, half:].astype(BF16).astype(F32), jnp.uint32)
    return jnp.bitwise_or(jnp.bitwise_and(hi, jnp.uint32(0xFFFF0000)), lax.shift_right_logical(lo, jnp.uint32(16)))


def _unpack_bf16_pairs(xp):
    lo = lax.bitcast_convert_type(lax.shift_left(xp, jnp.uint32(16)), F32)
    hi = lax.bitcast_convert_type(jnp.bitwise_and(xp, jnp.uint32(0xFFFF0000)), F32)
    return jnp.concatenate([lo, hi], axis=1).astype(BF16)


def _merge_kernel(c_ref, og_ref, ga_ref, gb_ref, h_ref, wco_ref, bco_ref, wdo_ref, wo_ref, nf_ref, wr_ref, br_ref,
                  h1_ref, xp_ref, ei_ref, gt_ref, rk_ref, cnt_ref, carry):
    step = pl.program_id(0)

    @pl.when(step == 0)
    def _():
        carry[...] = jnp.zeros_like(carry)

    ya = _dot(c_ref[...], wco_ref[...]) + bco_ref[...]
    yb = _dot(og_ref[...], wdo_ref[...])
    mixed = _sigmoid(ga_ref[...]) * ya + _sigmoid(gb_ref[...]) * yb
    h1 = h_ref[...] + _dot(mixed.astype(BF16), wo_ref[...])
    h1_ref[...] = h1
    xn = h1 * lax.rsqrt(jnp.mean(h1 * h1, axis=-1, keepdims=True) + EPS) * nf_ref[...]
    xp_ref[...] = _pack_bf16_pairs(xn)
    logits = _dot(xn.astype(BF16), wr_ref[...]) + br_ref[...]

    tm = logits.shape[0]
    lane = lax.broadcasted_iota(jnp.int32, (tm, LANES), 1)
    work = logits
    sels, vals = [], []
    for _ in range(TOP_K):
        m = jnp.max(work, axis=-1, keepdims=True)
        idx = jnp.min(jnp.where(work == m, lane, N_EXPERTS - 1), axis=-1, keepdims=True)
        sel = lane == idx
        sels.append(sel)
        vals.append(m)
        work = jnp.where(sel, -jnp.inf, work)
    exps = [jnp.exp(v - vals[0]) for v in vals]
    denom = exps[0]
    for e in exps[1:]:
        denom = denom + e
    onehot = jnp.zeros((tm, LANES), F32)
    for sel in sels:
        onehot = onehot + sel.astype(F32)
    ri = lax.broadcasted_iota(jnp.int32, (tm, tm), 0)
    ci = lax.broadcasted_iota(jnp.int32, (tm, tm), 1)
    before = _dot((ri > ci).astype(BF16), onehot.astype(BF16)) + carry[...]
    ei = jnp.zeros((tm, LANES), jnp.int32)
    gt = jnp.zeros((tm, LANES), F32)
    rk = jnp.zeros((tm, LANES), jnp.int32)
    for k in range(TOP_K):
        at_k = lane == k
        e_k = jnp.max(jnp.where(sels[k], lane, 0), axis=-1, keepdims=True)
        r_k = jnp.sum(jnp.where(sels[k], before, 0.0), axis=-1, keepdims=True).astype(jnp.int32)
        ei = jnp.where(at_k, e_k, ei)
        gt = jnp.where(at_k, exps[k] / denom, gt)
        rk = jnp.where(at_k, r_k, rk)
    ei_ref[...] = ei
    gt_ref[...] = gt
    rk_ref[...] = rk
    carry[...] = carry[...] + jnp.sum(onehot, axis=0, keepdims=True)
    cnt_ref[...] = carry[...]


def _merge(c, og, p, h, wco, bco, wdo, wo, nf, wr, br):
    T = h.shape[0]
    tm = _row_tile(T, 512)
    row = lambda col: pl.BlockSpec((tm, D_MODEL), lambda i: (i, col))
    full = lambda a, b: pl.BlockSpec((a, b), lambda i: (0, 0))
    lanes = lambda: pl.BlockSpec((tm, LANES), lambda i: (i, 0))
    return pl.pallas_call(
        _merge_kernel,
        out_shape=(jax.ShapeDtypeStruct((T, D_MODEL), F32),
                   jax.ShapeDtypeStruct((T, D_MODEL // 2), jnp.uint32),
                   jax.ShapeDtypeStruct((T, LANES), jnp.int32),
                   jax.ShapeDtypeStruct((T, LANES), F32),
                   jax.ShapeDtypeStruct((T, LANES), jnp.int32),
                   jax.ShapeDtypeStruct((1, LANES), F32)),
        grid=(T // tm,),
        in_specs=[row(0), row(0), row(6), row(7), row(0),
                  full(D_MODEL, D_MODEL), full(1, D_MODEL), full(D_MODEL, D_MODEL), full(D_MODEL, D_MODEL),
                  full(1, D_MODEL), full(D_MODEL, LANES), full(1, LANES)],
        out_specs=(row(0), pl.BlockSpec((tm, D_MODEL // 2), lambda i: (i, 0)), lanes(), lanes(), lanes(),
                   full(1, LANES)),
        scratch_shapes=[pltpu.VMEM((1, LANES), F32)],
        compiler_params=_params(("arbitrary",)),
        name="merge",
    )(c, og, p, p, h, wco, bco, wdo, wo, nf, wr, br)


ISSUE_UNROLL = 8


def _row_copy(src, src_row, dst, dst_row, sem):
    return pltpu.make_async_copy(src.at[pl.ds(src_row, 1), :], dst.at[pl.ds(dst_row, 1), :], sem)


def _dispatch_kernel(dest_ref, x_ref, xb_in, xb_out, sem, *, tm):
    del xb_in

    def issue(t, carry):
        for k in range(TOP_K):
            _row_copy(x_ref, t, xb_out, dest_ref[t * TOP_K + k], sem).start(priority=k % 2)
        return carry

    lax.fori_loop(0, tm, issue, 0, unroll=ISSUE_UNROLL)
    for k in range(TOP_K):
        pltpu.make_async_copy(x_ref, xb_out.at[pl.ds(0, tm), :], sem).wait()


def _dispatch(dest, xp, xb_init):
    T, W = xp.shape
    tm = _row_tile(T, 256)
    return pl.pallas_call(
        functools.partial(_dispatch_kernel, tm=tm),
        out_shape=jax.ShapeDtypeStruct(xb_init.shape, xb_init.dtype),
        grid=(T // tm,),
        in_specs=[pl.BlockSpec((tm * TOP_K,), lambda i: (i,), memory_space=pltpu.SMEM),
                  pl.BlockSpec((tm, W), lambda i: (i, 0)),
                  pl.BlockSpec(memory_space=pl.ANY)],
        out_specs=pl.BlockSpec(memory_space=pl.ANY),
        scratch_shapes=[pltpu.SemaphoreType.DMA],
        input_output_aliases={2: 0},
        compiler_params=_params(("arbitrary",)),
        name="moe_dispatch",
    )(dest, xp, xb_init)


def _moe_kernel(be_ref, nu_ref, x_ref, wu_ref, bu_ref, wd_ref, bd_ref, o_ref, wub, wdb):
    i = pl.program_id(0)

    @pl.when(jnp.logical_or(i == 0, be_ref[i] != be_ref[jnp.maximum(i - 1, 0)]))
    def _():
        wub[...] = wu_ref[0].astype(BF16)
        wdb[...] = wd_ref[0].astype(BF16)

    @pl.when(i < nu_ref[0])
    def _():
        hmid = _dot(_unpack_bf16_pairs(x_ref[...]), wub[...]) + bu_ref[0]
        hg = jnp.minimum(hmid[:, :D_FF], SWIGLU_LIMIT)
        hl = jnp.clip(hmid[:, D_FF:], -SWIGLU_LIMIT, SWIGLU_LIMIT)
        act = hg * _sigmoid(SWIGLU_ALPHA * hg) * (hl + 1.0)
        o_ref[...] = _dot(act.astype(BF16), wdb[...]) + bd_ref[0]

    @pl.when(i >= nu_ref[0])
    def _():
        o_ref[...] = jnp.zeros_like(o_ref)


def _moe(block_e, n_used, xb, w_up, b_up, w_down, b_down):
    R = xb.shape[0]
    nb = R // MOE_ROWS
    grid_spec = pltpu.PrefetchScalarGridSpec(
        num_scalar_prefetch=2,
        grid=(nb,),
        in_specs=[pl.BlockSpec((MOE_ROWS, D_MODEL // 2), lambda i, be, nu: (i, 0)),
                  pl.BlockSpec((1, D_MODEL, 2 * D_FF), lambda i, be, nu: (be[i], 0, 0)),
                  pl.BlockSpec((1, 1, 2 * D_FF), lambda i, be, nu: (be[i], 0, 0)),
                  pl.BlockSpec((1, D_FF, D_MODEL), lambda i, be, nu: (be[i], 0, 0)),
                  pl.BlockSpec((1, 1, D_MODEL), lambda i, be, nu: (be[i], 0, 0))],
        out_specs=pl.BlockSpec((MOE_ROWS, D_MODEL), lambda i, be, nu: (i, 0)),
        scratch_shapes=[pltpu.VMEM((D_MODEL, 2 * D_FF), BF16), pltpu.VMEM((D_FF, D_MODEL), BF16)],
    )
    return pl.pallas_call(
        _moe_kernel,
        out_shape=jax.ShapeDtypeStruct((R, D_MODEL), F32),
        grid_spec=grid_spec,
        compiler_params=pltpu.CompilerParams(dimension_semantics=("arbitrary",), vmem_limit_bytes=MOE_VMEM_LIMIT),
        name="moe_experts",
    )(block_e, n_used, xb, w_up, b_up, w_down, b_down)


def _dispatch_plan(counts, ei, rk, T):
    A = T * TOP_K
    n_blocks = -(-A // MOE_ROWS) + N_EXPERTS
    counts = counts[0, :N_EXPERTS].astype(jnp.int32)
    padded = (counts + MOE_ROWS - 1) // MOE_ROWS * MOE_ROWS
    pend = jnp.cumsum(padded)
    pstart = pend - padded
    dest = (pstart[ei[:, :TOP_K]] + rk[:, :TOP_K]).reshape(-1).astype(jnp.int32)
    starts = jnp.arange(n_blocks, dtype=jnp.int32) * MOE_ROWS
    block_e = jnp.minimum(jnp.sum((pend[None, :] <= starts[:, None]).astype(jnp.int32), axis=1), N_EXPERTS - 1)
    n_used = (pend[N_EXPERTS - 1:] // MOE_ROWS).astype(jnp.int32)
    return dest, block_e, n_used, n_blocks * MOE_ROWS


def _combine_kernel(dest_ref, gt_ref, h_ref, nw_ref, yb_ref, o_ref, buf, sem, *, tm):
    def issue(t, carry):
        for k in range(TOP_K):
            _row_copy(yb_ref, dest_ref[t * TOP_K + k], buf.at[k], t, sem).start(priority=k % 2)
        return carry

    lax.fori_loop(0, tm, issue, 0, unroll=ISSUE_UNROLL)
    for k in range(TOP_K):
        pltpu.make_async_copy(yb_ref.at[pl.ds(0, tm), :], buf.at[k], sem).wait()
    gt = gt_ref[...]
    x = h_ref[...]
    for k in range(TOP_K):
        x = x + gt[:, k:k + 1] * buf[k]
    o_ref[...] = x * lax.rsqrt(jnp.mean(x * x, axis=-1, keepdims=True) + EPS) * nw_ref[...]


def _combine(dest, gt, h1, nw, yb):
    T = h1.shape[0]
    tm = _row_tile(T, 256)
    row = lambda: pl.BlockSpec((tm, D_MODEL), lambda i: (i, 0))
    return pl.pallas_call(
        functools.partial(_combine_kernel, tm=tm),
        out_shape=jax.ShapeDtypeStruct((T, D_MODEL), F32),
        grid=(T // tm,),
        in_specs=[pl.BlockSpec((tm * TOP_K,), lambda i: (i,), memory_space=pltpu.SMEM),
                  pl.BlockSpec((tm, LANES), lambda i: (i, 0)),
                  row(),
                  pl.BlockSpec((1, D_MODEL), lambda i: (0, 0)),
                  pl.BlockSpec(memory_space=pl.ANY)],
        out_specs=row(),
        scratch_shapes=[pltpu.VMEM((TOP_K, tm, D_MODEL), F32), pltpu.SemaphoreType.DMA],
        compiler_params=_params(("arbitrary",)),
        name="moe_combine_final",
    )(dest, gt, h1, nw, yb)


def _pad_lanes(v, fill=0.0):
    v = v.reshape(1, -1).astype(F32)
    return jnp.pad(v, ((0, 0), (0, LANES - v.shape[1])), constant_values=fill)


def kernel(x_prompt, x_sample, state_conf_conv, state_dn_conv, state_dn_S, meta_tokens, norm_mix, w_in, w_conf_dw, b_conf_dw, ln_conf_g, ln_conf_b, w_conf_out, b_conf_out, w_dn_conv, dn_a_log, dn_dt_bias, dn_norm_w, w_dn_out, w_out, norm_ffn, w_router, b_router, w_up, b_up, w_down, b_down, norm_final):
    B, SEQ, D = x_prompt.shape
    NB, LS, _ = x_sample.shape
    depth = w_in.shape[0]
    assert D == D_MODEL and depth == 1 and SEQ % CHUNK == 0 and LS >= SHORT_W - 1
    LP = FRONT + N_META + SEQ
    TP = B * LP
    T = TP + NB * LS
    n_qk = DN_HEADS * DN_DK
    o_q = 2 * D_MODEL
    o_a = o_q + 4 * n_qk
    o_gate = o_a + 2 * DN_HEADS

    h0 = _tokens(x_prompt, x_sample.reshape(NB * LS, D), meta_tokens.astype(F32))

    w_in0 = w_in[0]
    w_main = jnp.concatenate([w_in0[:, :o_a], w_in0[:, o_gate:]], axis=1)
    w_ab = jnp.pad(w_in0[:, o_a:o_gate], ((0, 0), (0, LANES - 2 * DN_HEADS))).astype(BF16)
    wco = w_conf_out[0].astype(BF16)
    wdo = w_dn_out[0].astype(BF16)
    wo = w_out[0].astype(BF16)
    wr = jnp.pad(w_router[0], ((0, 0), (0, LANES - N_EXPERTS))).astype(BF16)
    br = _pad_lanes(b_router[0], fill=-1e30)
    alog = _pad_lanes(dn_a_log[0])
    dtb = _pad_lanes(dn_dt_bias[0])
    row = lambda v: v.reshape(1, -1).astype(F32)

    xn, ab = _rms_ab(h0, row(norm_mix[0]), w_ab)
    p = _mm_in(xn, w_main)
    p_s3 = p[TP:].reshape(NB, LS, 8 * D_MODEL)
    ab_s3 = ab[TP:].reshape(NB, LS, LANES)

    c_p, ust_p = _conf_prompt(p, B, LP, w_conf_dw[0], row(b_conf_dw[0]), row(ln_conf_g[0]), row(ln_conf_b[0]))
    c_s, conf_state_s = _conf_sample(p_s3, state_conf_conv[0], w_conf_dw[0], row(b_conf_dw[0]),
                                     row(ln_conf_g[0]), row(ln_conf_b[0]))
    c_all = jnp.concatenate([c_p, c_s.reshape(NB * LS, D).astype(BF16)], axis=0)

    og_p, s_p = _gdn_prompt(p, ab, B, LP, w_dn_conv[0], alog, dtb, row(dn_norm_w[0]))
    og_s, s_s = _gdn_sample(p_s3, ab_s3, state_dn_conv[0], state_dn_S[0], w_dn_conv[0], alog, dtb,
                            row(dn_norm_w[0]))
    og_all = jnp.concatenate([og_p, og_s.reshape(NB * LS, n_qk).astype(BF16)], axis=0)

    h1, xp2, ei, gt, rk, counts = _merge(c_all, og_all, p, h0, wco, row(b_conf_out[0]), wdo, wo,
                                         row(norm_ffn[0]), wr, br)

    dest, block_e, n_used, R = _dispatch_plan(counts, ei, rk, T)
    xb = _dispatch(dest, xp2, jnp.zeros((R, D // 2), jnp.uint32))
    yb = _moe(block_e, n_used, xb, w_up[0], b_up[0].reshape(N_EXPERTS, 1, -1),
              w_down[0], b_down[0].reshape(N_EXPERTS, 1, -1))
    y = _combine(dest, gt, h1, row(norm_final), yb)

    y_prompt = y[:TP].reshape(B, LP, D)[:, FRONT + N_META:]
    y_sample = y[TP:].reshape(NB, LS, D)
    hist = CONV_W - 1
    conf_conv_prompt = ust_p[:, HALO - hist:][None]
    dn_conv_prompt = jnp.stack([p[(b + 1) * LP - (SHORT_W - 1):(b + 1) * LP, o_q:o_q + 3 * n_qk]
                                for b in range(B)])[None]
    dn_conv_sample = p_s3[:, LS - (SHORT_W - 1):, o_q:o_q + 3 * n_qk][None]
    return (y_prompt, y_sample, conf_conv_prompt, dn_conv_prompt, s_p[None],
            conf_state_s[None], dn_conv_sample, s_s[None])
```

```python
import functools

import jax
import jax.numpy as jnp
from jax import lax
from jax.experimental import pallas as pl
from jax.experimental.pallas import tpu as pltpu

D_MODEL = 1024
N_META = 16
CONV_W = 31
SHORT_W = 4
DN_HEADS = 8
DN_DK = 128
DN_DV = 128
CHUNK = 64
N_EXPERTS = 32
TOP_K = 4
D_FF = 1024
SWIGLU_LIMIT = 7.0
SWIGLU_ALPHA = 1.702
EPS = 1e-6

FRONT = (-N_META) % CHUNK
SAMPLE_CHUNK = 16
STACK = 128
LANES = 128
HALO = 32
MOE_ROWS = 512
VMEM_LIMIT = 48 * 1024 * 1024
MOE_VMEM_LIMIT = 58 * 1024 * 1024

F32 = jnp.float32
BF16 = jnp.bfloat16


def _row_tile(n, pref):
    best = 16
    for t in range(16, min(n, pref) + 1, 16):
        if n % t == 0:
            best = t
    assert n % best == 0
    return best


def _sigmoid(x):
    return 1.0 / (1.0 + jnp.exp(-x))


def _dot(a, b):
    return jnp.dot(a, b, preferred_element_type=F32)


def _dot_nt(a, b):
    return lax.dot_general(a, b, (((1,), (1,)), ((), ())), preferred_element_type=F32)


def _dot_tn(a, b):
    return lax.dot_general(a, b, (((0,), (0,)), ((), ())), preferred_element_type=F32)


def _params(sem):
    return pltpu.CompilerParams(dimension_semantics=sem, vmem_limit_bytes=VMEM_LIMIT)


def _tokens_kernel(xp_ref, xs_ref, meta_ref, h_ref, head, sems, *, lp, tr, n_prompt_rows):
    b = pl.program_id(0)
    j = pl.program_id(1)
    n_head = FRONT + N_META
    row0 = pl.multiple_of(b * lp, 8)
    body = pltpu.make_async_copy(xp_ref.at[0], h_ref.at[pl.ds(row0 + n_head + j * tr, tr), :], sems.at[0])
    body.start()

    @pl.when(j == 0)
    def _():
        head[0:FRONT, :] = jnp.zeros((FRONT, D_MODEL), F32)
        head[FRONT:, :] = meta_ref[...]
        front = pltpu.make_async_copy(head, h_ref.at[pl.ds(row0, n_head), :], sems.at[1])
        front.start()
        front.wait()

    @pl.when(jnp.logical_and(b == 0, j == 0))
    def _():
        tail = pltpu.make_async_copy(xs_ref, h_ref.at[pl.ds(n_prompt_rows, xs_ref.shape[0]), :], sems.at[2])
        tail.start()
        tail.wait()

    body.wait()


def _tokens(x_prompt, x_sample2, meta):
    B, SEQ, D = x_prompt.shape
    lp = FRONT + N_META + SEQ
    T = B * lp + x_sample2.shape[0]
    tr = _row_tile(SEQ, 1024)
    return pl.pallas_call(
        functools.partial(_tokens_kernel, lp=lp, tr=tr, n_prompt_rows=B * lp),
        out_shape=jax.ShapeDtypeStruct((T, D), F32),
        grid=(B, SEQ // tr),
        in_specs=[pl.BlockSpec((1, tr, D), lambda b, j: (b, j, 0)),
                  pl.BlockSpec(x_sample2.shape, lambda b, j: (0, 0)),
                  pl.BlockSpec((N_META, D), lambda b, j: (0, 0))],
        out_specs=pl.BlockSpec(memory_space=pl.ANY),
        scratch_shapes=[pltpu.VMEM((FRONT + N_META, D), F32), pltpu.SemaphoreType.DMA((3,))],
        compiler_params=_params(("arbitrary", "arbitrary")),
        name="token_layout",
    )(x_prompt, x_sample2, meta)


def _rms_ab_kernel(h_ref, nw_ref, wab_ref, xn_ref, ab_ref):
    x = h_ref[...]
    y = x * lax.rsqrt(jnp.mean(x * x, axis=-1, keepdims=True) + EPS) * nw_ref[...]
    yb = y.astype(BF16)
    xn_ref[...] = yb
    ab_ref[...] = _dot(yb, wab_ref[...])


def _rms_ab(h, norm_w, w_ab):
    T = h.shape[0]
    tm = _row_tile(T, 1024)
    return pl.pallas_call(
        _rms_ab_kernel,
        out_shape=(jax.ShapeDtypeStruct((T, D_MODEL), BF16), jax.ShapeDtypeStruct((T, LANES), F32)),
        grid=(T // tm,),
        in_specs=[pl.BlockSpec((tm, D_MODEL), lambda i: (i, 0)),
                  pl.BlockSpec((1, D_MODEL), lambda i: (0, 0)),
                  pl.BlockSpec((D_MODEL, LANES), lambda i: (0, 0))],
        out_specs=(pl.BlockSpec((tm, D_MODEL), lambda i: (i, 0)),
                   pl.BlockSpec((tm, LANES), lambda i: (i, 0))),
        compiler_params=_params(("arbitrary",)),
        name="rms_ab",
    )(h, norm_w, w_ab)


def _mm_in_kernel(x_ref, w_ref, o_ref, wb_ref):
    @pl.when(pl.program_id(1) == 0)
    def _():
        wb_ref[...] = w_ref[...].astype(BF16)

    o_ref[...] = _dot(x_ref[...], wb_ref[...])


def _mm_in(xn, w):
    T, K = xn.shape
    N = w.shape[1]
    tm = _row_tile(T, 1024)
    tn = 1024
    assert N % tn == 0
    return pl.pallas_call(
        _mm_in_kernel,
        out_shape=jax.ShapeDtypeStruct((T, N), F32),
        grid=(N // tn, T // tm),
        in_specs=[pl.BlockSpec((tm, K), lambda j, i: (i, 0)),
                  pl.BlockSpec((K, tn), lambda j, i: (0, j))],
        out_specs=pl.BlockSpec((tm, tn), lambda j, i: (i, j)),
        scratch_shapes=[pltpu.VMEM((K, tn), BF16)],
        compiler_params=_params(("arbitrary", "arbitrary")),
        name="in_proj",
    )(xn, w)


def _ln_silu(x, g, b):
    mu = jnp.mean(x, axis=-1, keepdims=True)
    xc = x - mu
    var = jnp.mean(xc * xc, axis=-1, keepdims=True)
    y = xc * lax.rsqrt(var + EPS) * g + b
    return y * _sigmoid(y)


def _conf_prompt_kernel(pa_ref, pb_ref, ha_ref, hb_ref, wdw_ref, bdw_ref, lng_ref, lnb_ref,
                        c_ref, ust_ref, ubuf, cbuf, *, tl, rt, ct):
    t = pl.program_id(1)
    u = pa_ref[...] * _sigmoid(pb_ref[...])
    uh = ha_ref[...] * _sigmoid(hb_ref[...])
    ubuf[0:HALO, :] = jnp.where(t > 0, uh, 0.0)
    ubuf[HALO:, :] = u
    first = HALO - (CONV_W - 1)
    for r0 in range(0, tl, rt):
        for c0 in range(0, D_MODEL, ct):
            acc = jnp.zeros((rt, ct), F32)
            for s in range(8):
                part = None
                for w in range(CONV_W):
                    if (first + w) % 8 != s:
                        continue
                    base = r0 + (first + w) // 8 * 8
                    term = ubuf[base:base + rt + (8 if s else 0), c0:c0 + ct] * wdw_ref[w:w + 1, c0:c0 + ct]
                    part = term if part is None else part + term
                if part is not None:
                    acc = acc + part[s:s + rt, :]
            cbuf[r0:r0 + rt, c0:c0 + ct] = acc + bdw_ref[:, c0:c0 + ct]
    c_ref[...] = _ln_silu(cbuf[...], lng_ref[...], lnb_ref[...]).astype(BF16)

    @pl.when(t == pl.num_programs(1) - 1)
    def _():
        ust_ref[0] = ubuf[tl:tl + HALO, :]


def _conf_prompt(p, B, LP, w_dw, b_dw, ln_g, ln_b):
    tl = 192 if LP % 192 == 0 else CHUNK
    nt = LP // tl
    hb = tl // HALO
    kern = functools.partial(_conf_prompt_kernel, tl=tl, rt=64, ct=128)
    halo_idx = lambda b, t: (jnp.maximum((b * nt + t) * hb - 1, 0), 0)
    halo_idx1 = lambda b, t: (jnp.maximum((b * nt + t) * hb - 1, 0), 1)
    vec = lambda: pl.BlockSpec((1, D_MODEL), lambda b, t: (0, 0))
    return pl.pallas_call(
        kern,
        out_shape=(jax.ShapeDtypeStruct((B * LP, D_MODEL), BF16),
                   jax.ShapeDtypeStruct((B, HALO, D_MODEL), F32)),
        grid=(B, nt),
        in_specs=[pl.BlockSpec((tl, D_MODEL), lambda b, t: (b * nt + t, 0)),
                  pl.BlockSpec((tl, D_MODEL), lambda b, t: (b * nt + t, 1)),
                  pl.BlockSpec((HALO, D_MODEL), halo_idx),
                  pl.BlockSpec((HALO, D_MODEL), halo_idx1),
                  pl.BlockSpec((CONV_W, D_MODEL), lambda b, t: (0, 0)),
                  vec(), vec(), vec()],
        out_specs=(pl.BlockSpec((tl, D_MODEL), lambda b, t: (b * nt + t, 0)),
                   pl.BlockSpec((1, HALO, D_MODEL), lambda b, t: (b, 0, 0))),
        scratch_shapes=[pltpu.VMEM((HALO + tl, D_MODEL), F32), pltpu.VMEM((tl, D_MODEL), F32)],
        compiler_params=_params(("arbitrary", "arbitrary")),
        name="conf_prompt",
    )(p, p, p, p, w_dw, b_dw, ln_g, ln_b)


def _conf_sample_kernel(st_ref, pa_ref, pb_ref, wdw_ref, bdw_ref, lng_ref, lnb_ref,
                        c_ref, nst_ref, xh, *, sb, ls):
    hist = CONV_W - 1
    for s in range(sb):
        u = pa_ref[s] * _sigmoid(pb_ref[s])
        xh[0:hist, :] = st_ref[s]
        xh[hist:hist + ls, :] = u
        acc = jnp.zeros((ls, D_MODEL), F32)
        for w in range(CONV_W):
            acc = acc + xh[w:w + ls, :] * wdw_ref[w:w + 1, :]
        c_ref[s] = _ln_silu(acc + bdw_ref[...], lng_ref[...], lnb_ref[...])
        nst_ref[s] = xh[ls:ls + hist, :]


def _conf_sample(p_s3, state, w_dw, b_dw, ln_g, ln_b):
    NB, ls, _ = p_s3.shape
    hist = CONV_W - 1
    sb = 8 if NB % 8 == 0 else 1
    kern = functools.partial(_conf_sample_kernel, sb=sb, ls=ls)
    vec = lambda: pl.BlockSpec((1, D_MODEL), lambda i: (0, 0))
    return pl.pallas_call(
        kern,
        out_shape=(jax.ShapeDtypeStruct((NB, ls, D_MODEL), F32),
                   jax.ShapeDtypeStruct((NB, hist, D_MODEL), F32)),
        grid=(NB // sb,),
        in_specs=[pl.BlockSpec((sb, hist, D_MODEL), lambda i: (i, 0, 0)),
                  pl.BlockSpec((sb, ls, D_MODEL), lambda i: (i, 0, 0)),
                  pl.BlockSpec((sb, ls, D_MODEL), lambda i: (i, 0, 1)),
                  pl.BlockSpec((CONV_W, D_MODEL), lambda i: (0, 0)),
                  vec(), vec(), vec()],
        out_specs=(pl.BlockSpec((sb, ls, D_MODEL), lambda i: (i, 0, 0)),
                   pl.BlockSpec((sb, hist, D_MODEL), lambda i: (i, 0, 0))),
        scratch_shapes=[pltpu.VMEM((hist + ls + 8, D_MODEL), F32)],
        compiler_params=_params(("arbitrary",)),
        name="conf_sample",
    )(state, p_s3, p_s3, w_dw, b_dw, ln_g, ln_b)


def _split(a):
    hi = a.astype(BF16)
    return hi, (a - hi.astype(F32)).astype(BF16)


def _mm3(a, b):
    ah, al = a
    bh, bl = b
    return _dot(jnp.concatenate([ah, al, ah], axis=1), jnp.concatenate([bh, bh, bl], axis=0))


def _tri_inverse(ms, i, j, C, nil):
    same = lambda n: (i >> (n.bit_length() - 1)) == (j >> (n.bit_length() - 1))
    base = min(16, C)
    eye = (i == j).astype(F32)
    bdot = lambda a, b: _dot(a.astype(BF16), b.astype(BF16))
    dps = [jnp.where(same(base), m, 0.0) for m in ms]
    xs = [eye - d for d in dps]
    for _ in range(max(0, (min(base, nil) - 1).bit_length() - 1)):
        dps = [bdot(d, d) for d in dps]
        xs = [x + bdot(d, x) for d, x in zip(dps, xs)]
    blk = base
    while blk < C:
        sel = jnp.logical_and(same(2 * blk), jnp.logical_not(same(blk)))
        ys = [bdot(jnp.where(sel, m, 0.0), x) for m, x in zip(ms, xs)]
        xs = [x - bdot(x, y) for x, y in zip(xs, ys)]
        blk *= 2
    xsp = [_split(x) for x in xs]
    res = [eye - x - _mm3(_split(m), xp) for m, x, xp in zip(ms, xs, xsp)]
    return [x + _dot(xp[0], r.astype(BF16)) for x, xp, r in zip(xs, xsp, res)]


def _gdn_chunks(seqs, alog_ref, dtb_ref, nw_ref, nil):
    C = seqs[0][0].shape[0]
    G = STACK // C
    ri = lax.broadcasted_iota(jnp.int32, (C, C), 0)
    ci = lax.broadcasted_iota(jnp.int32, (C, C), 1)
    tril = (ri >= ci).astype(BF16)
    i = lax.broadcasted_iota(jnp.int32, (STACK, STACK), 0)
    j = lax.broadcasted_iota(jnp.int32, (STACK, STACK), 1)
    shift = C.bit_length() - 1
    same = (i >> shift) == (j >> shift)
    causal = jnp.logical_and(same, i >= j)
    strict = jnp.logical_and(same, i > j)

    pre = []
    for n, (xq, xk, xv, z, ab, valid, s_ref) in enumerate(seqs):
        ok = valid > 0.5
        xa = ab + dtb_ref[...]
        softplus = jnp.maximum(xa, 0.0) + jnp.log(1.0 + jnp.exp(-jnp.abs(xa)))
        g_all = jnp.where(ok, -jnp.exp(alog_ref[...]) * softplus, 0.0)
        beta_all = jnp.where(ok, _sigmoid(ab), 0.0)
        g1 = g_all.astype(BF16)
        r1 = g_all - g1.astype(F32)
        g2 = r1.astype(BF16)
        g3 = (r1 - g2.astype(F32)).astype(BF16)
        gc_all = _dot(tril, g1) + _dot(tril, g2) + _dot(tril, g3)
        ok_st = jnp.concatenate([valid] * G, axis=0) > 0.5
        for h0 in range(0, DN_HEADS, G):
            heads = list(range(h0, h0 + G))
            stack = lambda x: jnp.concatenate([x[:, h * DN_DK:(h + 1) * DN_DK] for h in heads], axis=0)
            col = lambda a, off: jnp.concatenate([a[:, off + h:off + h + 1] for h in heads], axis=0)
            q = stack(xq)
            k = stack(xk)
            q = jnp.where(ok_st, q * lax.rsqrt(jnp.sum(q * q, axis=-1, keepdims=True) + EPS) * (DN_DK ** -0.5), 0.0)
            k = jnp.where(ok_st, k * lax.rsqrt(jnp.sum(k * k, axis=-1, keepdims=True) + EPS), 0.0)
            v = jnp.where(ok_st, stack(xv), 0.0)
            gc = col(gc_all, 0)
            beta = col(beta_all, DN_HEADS)
            g_last = jnp.concatenate([jnp.broadcast_to(gc_all[C - 1:C, h:h + 1], (C, 1)) for h in heads], axis=0)
            gb = jnp.broadcast_to(gc, (STACK, STACK))
            decay = jnp.where(causal, jnp.exp(jnp.where(causal, gb - gb.T, 0.0)), 0.0)
            egc = jnp.exp(gc)
            kb = k * beta
            pre.append(dict(n=n, heads=heads, s_ref=s_ref, q=q, kb=kb, kbf=k.astype(BF16), decay=decay, egc=egc,
                            rhs=jnp.concatenate([v * beta, kb * egc], axis=1),
                            k_dec=(k * jnp.exp(g_last - gc)).astype(BF16), zs=stack(z),
                            s_decay=[jnp.exp(gc_all[C - 1:C, h:h + 1]) for h in heads]))
    ms = [jnp.where(strict, _dot_nt(p["kb"].astype(BF16), p["kbf"]) * p["decay"], 0.0) for p in pre]
    qks = [jnp.where(causal, _dot_nt(p["q"].astype(BF16), p["kbf"]) * p["decay"], 0.0).astype(BF16) for p in pre]
    invs = _tri_inverse(ms, i, j, C, nil)
    sols = [_mm3(_split(inv), _split(p["rhs"])) for inv, p in zip(invs, pre)]
    wss = []
    for p, sol in zip(pre, sols):
        w = sol[:, DN_DV:].astype(BF16)
        q_dec = (p["q"] * p["egc"]).astype(BF16)
        wss.append([_dot(jnp.concatenate([w[g * C:(g + 1) * C], q_dec[g * C:(g + 1) * C]], axis=0),
                         p["s_ref"][h].astype(BF16)) for g, h in enumerate(p["heads"])])
    outs = [[None] * DN_HEADS for _ in seqs]
    for p, sol, ws, qk in zip(pre, sols, wss, qks):
        s_ref = p["s_ref"]
        v_new = [(sol[g * C:(g + 1) * C, :DN_DV] - ws[g][:C]).astype(BF16) for g in range(G)]
        for g, h in enumerate(p["heads"]):
            s_ref[h] = s_ref[h] * p["s_decay"][g] + _dot_tn(p["k_dec"][g * C:(g + 1) * C], v_new[g])
        o = jnp.concatenate([w[C:] for w in ws], axis=0) + _dot(qk, jnp.concatenate(v_new, axis=0))
        o = o * lax.rsqrt(jnp.mean(o * o, axis=-1, keepdims=True) + EPS) * nw_ref[...]
        og = o * (p["zs"] * _sigmoid(p["zs"]))
        for g, h in enumerate(p["heads"]):
            outs[p["n"]][h] = og[g * C:(g + 1) * C]
    return outs


def _short_conv_silu(xbuf, wc_ref, rows):
    first = 8 - (SHORT_W - 1)
    acc = xbuf[first:first + rows, :] * wc_ref[0:1, :]
    for w in range(1, SHORT_W):
        acc = acc + xbuf[first + w:first + w + rows, :] * wc_ref[w:w + 1, :]
    return acc * _sigmoid(acc)


def _gdn_prompt_kernel(*refs, ns):
    seq_refs = [refs[5 * s:5 * s + 5] for s in range(ns)]
    wc_ref, alog_ref, dtb_ref, nw_ref, o_ref, s_ref, xbuf = refs[5 * ns:]
    c = pl.program_id(1)
    n_qk = DN_HEADS * DN_DK

    @pl.when(c == 0)
    def _():
        s_ref[...] = jnp.zeros_like(s_ref)
        xbuf[:, 0:8, :] = jnp.zeros((ns, 8, xbuf.shape[2]), F32)

    rows = lax.broadcasted_iota(jnp.int32, (CHUNK, 1), 0)
    valid = jnp.logical_or(rows >= FRONT, c > 0).astype(F32)
    seqs = []
    for s, (q_ref, k_ref, v_ref, z_ref, ab_ref) in enumerate(seq_refs):
        xb = xbuf.at[s]
        xb[8:8 + CHUNK, 0:n_qk] = q_ref[...]
        xb[8:8 + CHUNK, n_qk:2 * n_qk] = k_ref[...]
        xb[8:8 + CHUNK, 2 * n_qk:] = v_ref[...]
        x = _short_conv_silu(xb, wc_ref, CHUNK)
        xb[0:8, :] = xb[CHUNK:CHUNK + 8, :]
        seqs.append((x[:, 0:n_qk], x[:, n_qk:2 * n_qk], x[:, 2 * n_qk:], z_ref[...], ab_ref[...], valid,
                     s_ref.at[s]))
    outs = _gdn_chunks(seqs, alog_ref, dtb_ref, nw_ref, nil=CHUNK)
    for s in range(ns):
        for h in range(DN_HEADS):
            o_ref[s, :, h * DN_DV:(h + 1) * DN_DV] = outs[s][h].astype(BF16)


def _gdn_prompt(p, ab, B, LP, w_conv, alog, dtb, nw):
    nc = LP // CHUNK
    n_qk = DN_HEADS * DN_DK
    ns = 2 if B % 2 == 0 else 1
    vec = lambda n: pl.BlockSpec((1, n), lambda g, c: (0, 0))
    in_specs, args = [], []
    for s in range(ns):
        for col in (2, 3, 4, 5):
            in_specs.append(pl.BlockSpec((CHUNK, n_qk), lambda g, c, s=s, col=col: ((g * ns + s) * nc + c, col)))
            args.append(p)
        in_specs.append(pl.BlockSpec((CHUNK, LANES), lambda g, c, s=s: ((g * ns + s) * nc + c, 0)))
        args.append(ab)
    in_specs += [pl.BlockSpec((SHORT_W, 3 * n_qk), lambda g, c: (0, 0)), vec(LANES), vec(LANES), vec(DN_DV)]
    og, s_out = pl.pallas_call(
        functools.partial(_gdn_prompt_kernel, ns=ns),
        out_shape=(jax.ShapeDtypeStruct((B, LP, n_qk), BF16),
                   jax.ShapeDtypeStruct((B, DN_HEADS, DN_DK, DN_DV), F32)),
        grid=(B // ns, nc),
        in_specs=in_specs,
        out_specs=(pl.BlockSpec((ns, CHUNK, n_qk), lambda g, c: (g, c, 0)),
                   pl.BlockSpec((ns, DN_HEADS, DN_DK, DN_DV), lambda g, c: (g, 0, 0, 0))),
        scratch_shapes=[pltpu.VMEM((ns, CHUNK + 8, 3 * n_qk), F32)],
        compiler_params=_params(("arbitrary", "arbitrary")),
        name="gdn_prompt",
    )(*args, w_conv, alog, dtb, nw)
    return og.reshape(B * LP, n_qk), s_out


def _gdn_sample_kernel(st_ref, q_ref, k_ref, v_ref, z_ref, ab_ref, s0_ref, wc_ref, alog_ref, dtb_ref, nw_ref,
                       o_ref, s_ref, xbuf, zbuf, abbuf, *, sb, ls):
    n_qk = DN_HEADS * DN_DK
    C = SAMPLE_CHUNK
    hist = SHORT_W - 1
    xbuf[...] = jnp.zeros_like(xbuf)
    zbuf[...] = jnp.zeros_like(zbuf)
    abbuf[...] = jnp.zeros_like(abbuf)
    s_ref[...] = s0_ref[...]
    valid = (lax.broadcasted_iota(jnp.int32, (C, 1), 0) < ls).astype(F32)
    seqs = []
    for s in range(sb):
        xb = xbuf.at[s]
        xb[8 - hist:8, :] = st_ref[s]
        xb[8:8 + ls, 0:n_qk] = q_ref[s]
        xb[8:8 + ls, n_qk:2 * n_qk] = k_ref[s]
        xb[8:8 + ls, 2 * n_qk:] = v_ref[s]
        zbuf[s, 0:ls, :] = z_ref[s]
        abbuf[s, 0:ls, :] = ab_ref[s]
        x = _short_conv_silu(xb, wc_ref, C)
        seqs.append((x[:, 0:n_qk], x[:, n_qk:2 * n_qk], x[:, 2 * n_qk:], zbuf[s], abbuf[s], valid, s_ref.at[s]))
    outs = _gdn_chunks(seqs, alog_ref, dtb_ref, nw_ref, nil=ls)
    for s in range(sb):
        for h in range(DN_HEADS):
            o_ref[s, :, h * DN_DV:(h + 1) * DN_DV] = outs[s][h][0:ls, :]


def _gdn_sample(p_s3, ab_s3, st_conv, s0, w_conv, alog, dtb, nw):
    NB, ls, _ = p_s3.shape
    n_qk = DN_HEADS * DN_DK
    hist = SHORT_W - 1
    assert ls <= SAMPLE_CHUNK
    sb = 4 if NB % 4 == 0 else 1
    kern = functools.partial(_gdn_sample_kernel, sb=sb, ls=ls)
    blk = lambda col: pl.BlockSpec((sb, ls, n_qk), lambda i: (i, 0, col))
    vec = lambda n: pl.BlockSpec((1, n), lambda i: (0, 0))
    sspec = lambda: pl.BlockSpec((sb, DN_HEADS, DN_DK, DN_DV), lambda i: (i, 0, 0, 0))
    return pl.pallas_call(
        kern,
        out_shape=(jax.ShapeDtypeStruct((NB, ls, n_qk), F32),
                   jax.ShapeDtypeStruct((NB, DN_HEADS, DN_DK, DN_DV), F32)),
        grid=(NB // sb,),
        in_specs=[pl.BlockSpec((sb, hist, 3 * n_qk), lambda i: (i, 0, 0)),
                  blk(2), blk(3), blk(4), blk(5),
                  pl.BlockSpec((sb, ls, LANES), lambda i: (i, 0, 0)),
                  sspec(),
                  pl.BlockSpec((SHORT_W, 3 * n_qk), lambda i: (0, 0)),
                  vec(LANES), vec(LANES), vec(DN_DV)],
        out_specs=(pl.BlockSpec((sb, ls, n_qk), lambda i: (i, 0, 0)), sspec()),
        scratch_shapes=[pltpu.VMEM((sb, SAMPLE_CHUNK + 8, 3 * n_qk), F32),
                        pltpu.VMEM((sb, SAMPLE_CHUNK, n_qk), F32),
                        pltpu.VMEM((sb, SAMPLE_CHUNK, LANES), F32)],
        compiler_params=_params(("arbitrary",)),
        name="gdn_sample",
    )(st_conv, p_s3, p_s3, p_s3, p_s3, ab_s3, s0, w_conv, alog, dtb, nw)


def _pack_bf16_pairs(x):
    half = x.shape[1] // 2
    lo = lax.bitcast_convert_type(x[:, :half].astype(BF16).astype(F32), jnp.uint32)
    hi = lax.bitcast_convert_type(x[:, half:].astype(BF16).astype(F32), jnp.uint32)
    return jnp.bitwise_or(jnp.bitwise_and(hi, jnp.uint32(0xFFFF0000)), lax.shift_right_logical(lo, jnp.uint32(16)))


def _unpack_bf16_pairs(xp):
    lo = lax.bitcast_convert_type(lax.shift_left(xp, jnp.uint32(16)), F32)
    hi = lax.bitcast_convert_type(jnp.bitwise_and(xp, jnp.uint32(0xFFFF0000)), F32)
    return jnp.concatenate([lo, hi], axis=1).astype(BF16)


def _merge_kernel(c_ref, og_ref, ga_ref, gb_ref, h_ref, wco_ref, bco_ref, wdo_ref, wo_ref, nf_ref, wr_ref, br_ref,
                  h1_ref, xp_ref, ei_ref, gt_ref, rk_ref, cnt_ref, carry):
    step = pl.program_id(0)

    @pl.when(step == 0)
    def _():
        carry[...] = jnp.zeros_like(carry)

    ya = _dot(c_ref[...], wco_ref[...]) + bco_ref[...]
    yb = _dot(og_ref[...], wdo_ref[...])
    mixed = _sigmoid(ga_ref[...]) * ya + _sigmoid(gb_ref[...]) * yb
    h1 = h_ref[...] + _dot(mixed.astype(BF16), wo_ref[...])
    h1_ref[...] = h1
    xn = h1 * lax.rsqrt(jnp.mean(h1 * h1, axis=-1, keepdims=True) + EPS) * nf_ref[...]
    xp_ref[...] = _pack_bf16_pairs(xn)
    logits = _dot(xn.astype(BF16), wr_ref[...]) + br_ref[...]

    tm = logits.shape[0]
    lane = lax.broadcasted_iota(jnp.int32, (tm, LANES), 1)
    work = logits
    sels, vals = [], []
    for _ in range(TOP_K):
        m = jnp.max(work, axis=-1, keepdims=True)
        idx = jnp.min(jnp.where(work == m, lane, N_EXPERTS - 1), axis=-1, keepdims=True)
        sel = lane == idx
        sels.append(sel)
        vals.append(m)
        work = jnp.where(sel, -jnp.inf, work)
    exps = [jnp.exp(v - vals[0]) for v in vals]
    denom = exps[0]
    for e in exps[1:]:
        denom = denom + e
    onehot = jnp.zeros((tm, LANES), F32)
    for sel in sels:
        onehot = onehot + sel.astype(F32)
    ri = lax.broadcasted_iota(jnp.int32, (tm, tm), 0)
    ci = lax.broadcasted_iota(jnp.int32, (tm, tm), 1)
    before = _dot((ri > ci).astype(BF16), onehot.astype(BF16)) + carry[...]
    ei = jnp.zeros((tm, LANES), jnp.int32)
    gt = jnp.zeros((tm, LANES), F32)
    rk = jnp.zeros((tm, LANES), jnp.int32)
    for k in range(TOP_K):
        at_k = lane == k
        e_k = jnp.max(jnp.where(sels[k], lane, 0), axis=-1, keepdims=True)
        r_k = jnp.sum(jnp.where(sels[k], before, 0.0), axis=-1, keepdims=True).astype(jnp.int32)
        ei = jnp.where(at_k, e_k, ei)
        gt = jnp.where(at_k, exps[k] / denom, gt)
        rk = jnp.where(at_k, r_k, rk)
    ei_ref[...] = ei
    gt_ref[...] = gt
    rk_ref[...] = rk
    carry[...] = carry[...] + jnp.sum(onehot, axis=0, keepdims=True)
    cnt_ref[...] = carry[...]


def _merge(c, og, p, h, wco, bco, wdo, wo, nf, wr, br):
    T = h.shape[0]
    tm = _row_tile(T, 512)
    row = lambda col: pl.BlockSpec((tm, D_MODEL), lambda i: (i, col))
    full = lambda a, b: pl.BlockSpec((a, b), lambda i: (0, 0))
    lanes = lambda: pl.BlockSpec((tm, LANES), lambda i: (i, 0))
    return pl.pallas_call(
        _merge_kernel,
        out_shape=(jax.ShapeDtypeStruct((T, D_MODEL), F32),
                   jax.ShapeDtypeStruct((T, D_MODEL // 2), jnp.uint32),
                   jax.ShapeDtypeStruct((T, LANES), jnp.int32),
                   jax.ShapeDtypeStruct((T, LANES), F32),
                   jax.ShapeDtypeStruct((T, LANES), jnp.int32),
                   jax.ShapeDtypeStruct((1, LANES), F32)),
        grid=(T // tm,),
        in_specs=[row(0), row(0), row(6), row(7), row(0),
                  full(D_MODEL, D_MODEL), full(1, D_MODEL), full(D_MODEL, D_MODEL), full(D_MODEL, D_MODEL),
                  full(1, D_MODEL), full(D_MODEL, LANES), full(1, LANES)],
        out_specs=(row(0), pl.BlockSpec((tm, D_MODEL // 2), lambda i: (i, 0)), lanes(), lanes(), lanes(),
                   full(1, LANES)),
        scratch_shapes=[pltpu.VMEM((1, LANES), F32)],
        compiler_params=_params(("arbitrary",)),
        name="merge",
    )(c, og, p, p, h, wco, bco, wdo, wo, nf, wr, br)


ISSUE_UNROLL = 8


def _row_copy(src, src_row, dst, dst_row, sem):
    return pltpu.make_async_copy(src.at[pl.ds(src_row, 1), :], dst.at[pl.ds(dst_row, 1), :], sem)


def _dispatch_kernel(dest_ref, x_ref, xb_in, xb_out, sem, *, tm):
    del xb_in

    def issue(t, carry):
        for k in range(TOP_K):
            _row_copy(x_ref, t, xb_out, dest_ref[t * TOP_K + k], sem).start(priority=k % 2)
        return carry

    lax.fori_loop(0, tm, issue, 0, unroll=ISSUE_UNROLL)
    for k in range(TOP_K):
        pltpu.make_async_copy(x_ref, xb_out.at[pl.ds(0, tm), :], sem).wait()


def _dispatch(dest, xp, xb_init):
    T, W = xp.shape
    tm = _row_tile(T, 256)
    return pl.pallas_call(
        functools.partial(_dispatch_kernel, tm=tm),
        out_shape=jax.ShapeDtypeStruct(xb_init.shape, xb_init.dtype),
        grid=(T // tm,),
        in_specs=[pl.BlockSpec((tm * TOP_K,), lambda i: (i,), memory_space=pltpu.SMEM),
                  pl.BlockSpec((tm, W), lambda i: (i, 0)),
                  pl.BlockSpec(memory_space=pl.ANY)],
        out_specs=pl.BlockSpec(memory_space=pl.ANY),
        scratch_shapes=[pltpu.SemaphoreType.DMA],
        input_output_aliases={2: 0},
        compiler_params=_params(("arbitrary",)),
        name="moe_dispatch",
    )(dest, xp, xb_init)


def _moe_kernel(be_ref, nu_ref, x_ref, wu_ref, bu_ref, wd_ref, bd_ref, o_ref, wub, wdb):
    i = pl.program_id(0)

    @pl.when(jnp.logical_or(i == 0, be_ref[i] != be_ref[jnp.maximum(i - 1, 0)]))
    def _():
        wub[...] = wu_ref[0].astype(BF16)
        wdb[...] = wd_ref[0].astype(BF16)

    @pl.when(i < nu_ref[0])
    def _():
        hmid = _dot(_unpack_bf16_pairs(x_ref[...]), wub[...]) + bu_ref[0]
        hg = jnp.minimum(hmid[:, :D_FF], SWIGLU_LIMIT)
        hl = jnp.clip(hmid[:, D_FF:], -SWIGLU_LIMIT, SWIGLU_LIMIT)
        act = hg * _sigmoid(SWIGLU_ALPHA * hg) * (hl + 1.0)
        o_ref[...] = _dot(act.astype(BF16), wdb[...]) + bd_ref[0]

    @pl.when(i >= nu_ref[0])
    def _():
        o_ref[...] = jnp.zeros_like(o_ref)


def _moe(block_e, n_used, xb, w_up, b_up, w_down, b_down):
    R = xb.shape[0]
    nb = R // MOE_ROWS
    grid_spec = pltpu.PrefetchScalarGridSpec(
        num_scalar_prefetch=2,
        grid=(nb,),
        in_specs=[pl.BlockSpec((MOE_ROWS, D_MODEL // 2), lambda i, be, nu: (i, 0)),
                  pl.BlockSpec((1, D_MODEL, 2 * D_FF), lambda i, be, nu: (be[i], 0, 0)),
                  pl.BlockSpec((1, 1, 2 * D_FF), lambda i, be, nu: (be[i], 0, 0)),
                  pl.BlockSpec((1, D_FF, D_MODEL), lambda i, be, nu: (be[i], 0, 0)),
                  pl.BlockSpec((1, 1, D_MODEL), lambda i, be, nu: (be[i], 0, 0))],
        out_specs=pl.BlockSpec((MOE_ROWS, D_MODEL), lambda i, be, nu: (i, 0)),
        scratch_shapes=[pltpu.VMEM((D_MODEL, 2 * D_FF), BF16), pltpu.VMEM((D_FF, D_MODEL), BF16)],
    )
    return pl.pallas_call(
        _moe_kernel,
        out_shape=jax.ShapeDtypeStruct((R, D_MODEL), F32),
        grid_spec=grid_spec,
        compiler_params=pltpu.CompilerParams(dimension_semantics=("arbitrary",), vmem_limit_bytes=MOE_VMEM_LIMIT),
        name="moe_experts",
    )(block_e, n_used, xb, w_up, b_up, w_down, b_down)


def _dispatch_plan(counts, ei, rk, T):
    A = T * TOP_K
    n_blocks = -(-A // MOE_ROWS) + N_EXPERTS
    counts = counts[0, :N_EXPERTS].astype(jnp.int32)
    padded = (counts + MOE_ROWS - 1) // MOE_ROWS * MOE_ROWS
    pend = jnp.cumsum(padded)
    pstart = pend - padded
    dest = (pstart[ei[:, :TOP_K]] + rk[:, :TOP_K]).reshape(-1).astype(jnp.int32)
    starts = jnp.arange(n_blocks, dtype=jnp.int32) * MOE_ROWS
    block_e = jnp.minimum(jnp.sum((pend[None, :] <= starts[:, None]).astype(jnp.int32), axis=1), N_EXPERTS - 1)
    n_used = (pend[N_EXPERTS - 1:] // MOE_ROWS).astype(jnp.int32)
    return dest, block_e, n_used, n_blocks * MOE_ROWS


def _combine_kernel(dest_ref, gt_ref, h_ref, nw_ref, yb_ref, o_ref, buf, sem, *, tm):
    def issue(t, carry):
        for k in range(TOP_K):
            _row_copy(yb_ref, dest_ref[t * TOP_K + k], buf.at[k], t, sem).start(priority=k % 2)
        return carry

    lax.fori_loop(0, tm, issue, 0, unroll=ISSUE_UNROLL)
    for k in range(TOP_K):
        pltpu.make_async_copy(yb_ref.at[pl.ds(0, tm), :], buf.at[k], sem).wait()
    gt = gt_ref[...]
    x = h_ref[...]
    for k in range(TOP_K):
        x = x + gt[:, k:k + 1] * buf[k]
    o_ref[...] = x * lax.rsqrt(jnp.mean(x * x, axis=-1, keepdims=True) + EPS) * nw_ref[...]


def _combine(dest, gt, h1, nw, yb):
    T = h1.shape[0]
    tm = _row_tile(T, 256)
    row = lambda: pl.BlockSpec((tm, D_MODEL), lambda i: (i, 0))
    return pl.pallas_call(
        functools.partial(_combine_kernel, tm=tm),
        out_shape=jax.ShapeDtypeStruct((T, D_MODEL), F32),
        grid=(T // tm,),
        in_specs=[pl.BlockSpec((tm * TOP_K,), lambda i: (i,), memory_space=pltpu.SMEM),
                  pl.BlockSpec((tm, LANES), lambda i: (i, 0)),
                  row(),
                  pl.BlockSpec((1, D_MODEL), lambda i: (0, 0)),
                  pl.BlockSpec(memory_space=pl.ANY)],
        out_specs=row(),
        scratch_shapes=[pltpu.VMEM((TOP_K, tm, D_MODEL), F32), pltpu.SemaphoreType.DMA],
        compiler_params=_params(("arbitrary",)),
        name="moe_combine_final",
    )(dest, gt, h1, nw, yb)


def _pad_lanes(v, fill=0.0):
    v = v.reshape(1, -1).astype(F32)
    return jnp.pad(v, ((0, 0), (0, LANES - v.shape[1])), constant_values=fill)


def kernel(x_prompt, x_sample, state_conf_conv, state_dn_conv, state_dn_S, meta_tokens, norm_mix, w_in, w_conf_dw, b_conf_dw, ln_conf_g, ln_conf_b, w_conf_out, b_conf_out, w_dn_conv, dn_a_log, dn_dt_bias, dn_norm_w, w_dn_out, w_out, norm_ffn, w_router, b_router, w_up, b_up, w_down, b_down, norm_final):
    B, SEQ, D = x_prompt.shape
    NB, LS, _ = x_sample.shape
    depth = w_in.shape[0]
    assert D == D_MODEL and depth == 1 and SEQ % CHUNK == 0 and LS >= SHORT_W - 1
    LP = FRONT + N_META + SEQ
    TP = B * LP
    T = TP + NB * LS
    n_qk = DN_HEADS * DN_DK
    o_q = 2 * D_MODEL
    o_a = o_q + 4 * n_qk
    o_gate = o_a + 2 * DN_HEADS

    h0 = _tokens(x_prompt, x_sample.reshape(NB * LS, D), meta_tokens.astype(F32))

    w_in0 = w_in[0]
    w_main = jnp.concatenate([w_in0[:, :o_a], w_in0[:, o_gate:]], axis=1)
    w_ab = jnp.pad(w_in0[:, o_a:o_gate], ((0, 0), (0, LANES - 2 * DN_HEADS))).astype(BF16)
    wco = w_conf_out[0].astype(BF16)
    wdo = w_dn_out[0].astype(BF16)
    wo = w_out[0].astype(BF16)
    wr = jnp.pad(w_router[0], ((0, 0), (0, LANES - N_EXPERTS))).astype(BF16)
    br = _pad_lanes(b_router[0], fill=-1e30)
    alog = _pad_lanes(dn_a_log[0])
    dtb = _pad_lanes(dn_dt_bias[0])
    row = lambda v: v.reshape(1, -1).astype(F32)

    xn, ab = _rms_ab(h0, row(norm_mix[0]), w_ab)
    p = _mm_in(xn, w_main)
    p_s3 = p[TP:].reshape(NB, LS, 8 * D_MODEL)
    ab_s3 = ab[TP:].reshape(NB, LS, LANES)

    c_p, ust_p = _conf_prompt(p, B, LP, w_conf_dw[0], row(b_conf_dw[0]), row(ln_conf_g[0]), row(ln_conf_b[0]))
    c_s, conf_state_s = _conf_sample(p_s3, state_conf_conv[0], w_conf_dw[0], row(b_conf_dw[0]),
                                     row(ln_conf_g[0]), row(ln_conf_b[0]))
    c_all = jnp.concatenate([c_p, c_s.reshape(NB * LS, D).astype(BF16)], axis=0)

    og_p, s_p = _gdn_prompt(p, ab, B, LP, w_dn_conv[0], alog, dtb, row(dn_norm_w[0]))
    og_s, s_s = _gdn_sample(p_s3, ab_s3, state_dn_conv[0], state_dn_S[0], w_dn_conv[0], alog, dtb,
                            row(dn_norm_w[0]))
    og_all = jnp.concatenate([og_p, og_s.reshape(NB * LS, n_qk).astype(BF16)], axis=0)

    h1, xp2, ei, gt, rk, counts = _merge(c_all, og_all, p, h0, wco, row(b_conf_out[0]), wdo, wo,
                                         row(norm_ffn[0]), wr, br)

    dest, block_e, n_used, R = _dispatch_plan(counts, ei, rk, T)
    xb = _dispatch(dest, xp2, jnp.zeros((R, D // 2), jnp.uint32))
    yb = _moe(block_e, n_used, xb, w_up[0], b_up[0].reshape(N_EXPERTS, 1, -1),
              w_down[0], b_down[0].reshape(N_EXPERTS, 1, -1))
    y = _combine(dest, gt, h1, row(norm_final), yb)

    y_prompt = y[:TP].reshape(B, LP, D)[:, FRONT + N_META:]
    y_sample = y[TP:].reshape(NB, LS, D)
    hist = CONV_W - 1
    conf_conv_prompt = ust_p[:, HALO - hist:][None]
    dn_conv_prompt = jnp.stack([p[(b + 1) * LP - (SHORT_W - 1):(b + 1) * LP, o_q:o_q + 3 * n_qk]
                                for b in range(B)])[None]
    dn_conv_sample = p_s3[:, LS - (SHORT_W - 1):, o_q:o_q + 3 * n_qk][None]
    return (y_prompt, y_sample, conf_conv_prompt, dn_conv_prompt, s_p[None],
            conf_state_s[None], dn_conv_sample, s_s[None])
```

```python
import functools
import math

import jax
import jax.numpy as jnp
from jax import lax
from jax.experimental import pallas as pl
from jax.experimental.pallas import tpu as pltpu

D_MODEL = 1024
N_META = 16
CONV_W = 31
SHORT_W = 4
DN_HEADS = 8
DN_DK = 128
DN_DV = 128
CHUNK = 64
N_EXPERTS = 32
TOP_K = 4
D_FF = 1024
SWIGLU_LIMIT = 7.0
SWIGLU_ALPHA = 1.702
EPS = 1e-6

FRONT = (-N_META) % CHUNK
SAMPLE_CHUNK = 16
STACK = 128
LANES = 128
HALO = 32
MOE_ROWS = 512
VMEM_LIMIT = 48 * 1024 * 1024
MOE_VMEM_LIMIT = 58 * 1024 * 1024

F32 = jnp.float32
BF16 = jnp.bfloat16


def _row_tile(n, pref):
    best = 16
    for t in range(16, min(n, pref) + 1, 16):
        if n % t == 0:
            best = t
    assert n % best == 0
    return best


def _sigmoid(x):
    return 1.0 / (1.0 + jnp.exp(-x))


def _dot(a, b):
    return jnp.dot(a, b, preferred_element_type=F32)


def _dot_nt(a, b):
    return lax.dot_general(a, b, (((1,), (1,)), ((), ())), preferred_element_type=F32)


def _dot_tn(a, b):
    return lax.dot_general(a, b, (((0,), (0,)), ((), ())), preferred_element_type=F32)


def _params(sem):
    return pltpu.CompilerParams(dimension_semantics=sem, vmem_limit_bytes=VMEM_LIMIT)


def _tokens_kernel(xp_ref, xs_ref, meta_ref, h_ref, head, sems, *, lp, tr, n_prompt_rows):
    b = pl.program_id(0)
    j = pl.program_id(1)
    n_head = FRONT + N_META
    row0 = pl.multiple_of(b * lp, 8)
    body = pltpu.make_async_copy(xp_ref.at[0], h_ref.at[pl.ds(row0 + n_head + j * tr, tr), :], sems.at[0])
    body.start()

    @pl.when(j == 0)
    def _():
        head[0:FRONT, :] = jnp.zeros((FRONT, D_MODEL), F32)
        head[FRONT:, :] = meta_ref[...]
        front = pltpu.make_async_copy(head, h_ref.at[pl.ds(row0, n_head), :], sems.at[1])
        front.start()
        front.wait()

    @pl.when(jnp.logical_and(b == 0, j == 0))
    def _():
        tail = pltpu.make_async_copy(xs_ref, h_ref.at[pl.ds(n_prompt_rows, xs_ref.shape[0]), :], sems.at[2])
        tail.start()
        tail.wait()

    body.wait()


def _tokens(x_prompt, x_sample2, meta):
    B, SEQ, D = x_prompt.shape
    lp = FRONT + N_META + SEQ
    T = B * lp + x_sample2.shape[0]
    tr = _row_tile(SEQ, 1024)
    return pl.pallas_call(
        functools.partial(_tokens_kernel, lp=lp, tr=tr, n_prompt_rows=B * lp),
        out_shape=jax.ShapeDtypeStruct((T, D), F32),
        grid=(B, SEQ // tr),
        in_specs=[pl.BlockSpec((1, tr, D), lambda b, j: (b, j, 0)),
                  pl.BlockSpec(x_sample2.shape, lambda b, j: (0, 0)),
                  pl.BlockSpec((N_META, D), lambda b, j: (0, 0))],
        out_specs=pl.BlockSpec(memory_space=pl.ANY),
        scratch_shapes=[pltpu.VMEM((FRONT + N_META, D), F32), pltpu.SemaphoreType.DMA((3,))],
        compiler_params=_params(("arbitrary", "arbitrary")),
        name="token_layout",
    )(x_prompt, x_sample2, meta)


def _rms_ab_kernel(h_ref, nw_ref, wab_ref, xn_ref, ab_ref):
    x = h_ref[...]
    y = x * lax.rsqrt(jnp.mean(x * x, axis=-1, keepdims=True) + EPS) * nw_ref[...]
    yb = y.astype(BF16)
    xn_ref[...] = yb
    ab_ref[...] = _dot(yb, wab_ref[...])


def _rms_ab(h, norm_w, w_ab):
    T = h.shape[0]
    tm = _row_tile(T, 1024)
    return pl.pallas_call(
        _rms_ab_kernel,
        out_shape=(jax.ShapeDtypeStruct((T, D_MODEL), BF16), jax.ShapeDtypeStruct((T, LANES), F32)),
        grid=(T // tm,),
        in_specs=[pl.BlockSpec((tm, D_MODEL), lambda i: (i, 0)),
                  pl.BlockSpec((1, D_MODEL), lambda i: (0, 0)),
                  pl.BlockSpec((D_MODEL, LANES), lambda i: (0, 0))],
        out_specs=(pl.BlockSpec((tm, D_MODEL), lambda i: (i, 0)),
                   pl.BlockSpec((tm, LANES), lambda i: (i, 0))),
        compiler_params=_params(("arbitrary",)),
        name="rms_ab",
    )(h, norm_w, w_ab)


def _mm_in_kernel(x_ref, w_ref, o_ref, wb_ref):
    @pl.when(pl.program_id(1) == 0)
    def _():
        wb_ref[...] = w_ref[...].astype(BF16)

    o_ref[...] = _dot(x_ref[...], wb_ref[...]).astype(o_ref.dtype)


def _mm_in(xn, w, n_tiles, out_dtype):
    T, K = xn.shape
    tm = _row_tile(T, 1024)
    tn = 1024
    N = n_tiles * tn
    assert N <= w.shape[1]
    return pl.pallas_call(
        _mm_in_kernel,
        out_shape=jax.ShapeDtypeStruct((T, N), out_dtype),
        grid=(N // tn, T // tm),
        in_specs=[pl.BlockSpec((tm, K), lambda j, i: (i, 0)),
                  pl.BlockSpec((K, tn), lambda j, i: (0, j))],
        out_specs=pl.BlockSpec((tm, tn), lambda j, i: (i, j)),
        scratch_shapes=[pltpu.VMEM((K, tn), BF16)],
        compiler_params=_params(("arbitrary", "arbitrary")),
        name="in_proj",
    )(xn, w)


def _ln_silu(x, g, b):
    mu = jnp.mean(x, axis=-1, keepdims=True)
    xc = x - mu
    var = jnp.mean(xc * xc, axis=-1, keepdims=True)
    y = xc * lax.rsqrt(var + EPS) * g + b
    return y * _sigmoid(y)


def _conf_prompt_kernel(pa_ref, pb_ref, ha_ref, hb_ref, wdw_ref, bdw_ref, lng_ref, lnb_ref,
                        c_ref, ust_ref, ubuf, cbuf, *, tl, rt, ct):
    t = pl.program_id(1)
    u = pa_ref[...] * _sigmoid(pb_ref[...])
    uh = ha_ref[...] * _sigmoid(hb_ref[...])
    ubuf[0:HALO, :] = jnp.where(t > 0, uh, 0.0)
    ubuf[HALO:, :] = u
    first = HALO - (CONV_W - 1)
    for r0 in range(0, tl, rt):
        for c0 in range(0, D_MODEL, ct):
            acc = jnp.zeros((rt, ct), F32)
            for s in range(8):
                part = None
                for w in range(CONV_W):
                    if (first + w) % 8 != s:
                        continue
                    base = r0 + (first + w) // 8 * 8
                    term = ubuf[base:base + rt + (8 if s else 0), c0:c0 + ct] * wdw_ref[w:w + 1, c0:c0 + ct]
                    part = term if part is None else part + term
                if part is not None:
                    acc = acc + part[s:s + rt, :]
            cbuf[r0:r0 + rt, c0:c0 + ct] = acc + bdw_ref[:, c0:c0 + ct]
    c_ref[...] = _ln_silu(cbuf[...], lng_ref[...], lnb_ref[...]).astype(BF16)

    @pl.when(t == pl.num_programs(1) - 1)
    def _():
        ust_ref[0] = ubuf[tl:tl + HALO, :]


def _conf_prompt(p, B, LP, w_dw, b_dw, ln_g, ln_b):
    tl = 192 if LP % 192 == 0 else CHUNK
    nt = LP // tl
    hb = tl // HALO
    kern = functools.partial(_conf_prompt_kernel, tl=tl, rt=64, ct=128)
    halo_idx = lambda b, t: (jnp.maximum((b * nt + t) * hb - 1, 0), 0)
    halo_idx1 = lambda b, t: (jnp.maximum((b * nt + t) * hb - 1, 0), 1)
    vec = lambda: pl.BlockSpec((1, D_MODEL), lambda b, t: (0, 0))
    return pl.pallas_call(
        kern,
        out_shape=(jax.ShapeDtypeStruct((B * LP, D_MODEL), BF16),
                   jax.ShapeDtypeStruct((B, HALO, D_MODEL), F32)),
        grid=(B, nt),
        in_specs=[pl.BlockSpec((tl, D_MODEL), lambda b, t: (b * nt + t, 0)),
                  pl.BlockSpec((tl, D_MODEL), lambda b, t: (b * nt + t, 1)),
                  pl.BlockSpec((HALO, D_MODEL), halo_idx),
                  pl.BlockSpec((HALO, D_MODEL), halo_idx1),
                  pl.BlockSpec((CONV_W, D_MODEL), lambda b, t: (0, 0)),
                  vec(), vec(), vec()],
        out_specs=(pl.BlockSpec((tl, D_MODEL), lambda b, t: (b * nt + t, 0)),
                   pl.BlockSpec((1, HALO, D_MODEL), lambda b, t: (b, 0, 0))),
        scratch_shapes=[pltpu.VMEM((HALO + tl, D_MODEL), F32), pltpu.VMEM((tl, D_MODEL), F32)],
        compiler_params=_params(("arbitrary", "arbitrary")),
        name="conf_prompt",
    )(p, p, p, p, w_dw, b_dw, ln_g, ln_b)


def _conf_sample_kernel(st_ref, pa_ref, pb_ref, wdw_ref, bdw_ref, lng_ref, lnb_ref,
                        c_ref, nst_ref, xh, *, sb, ls):
    hist = CONV_W - 1
    for s in range(sb):
        u = pa_ref[s] * _sigmoid(pb_ref[s])
        xh[0:hist, :] = st_ref[s]
        xh[hist:hist + ls, :] = u
        acc = jnp.zeros((ls, D_MODEL), F32)
        for w in range(CONV_W):
            acc = acc + xh[w:w + ls, :] * wdw_ref[w:w + 1, :]
        c_ref[s] = _ln_silu(acc + bdw_ref[...], lng_ref[...], lnb_ref[...])
        nst_ref[s] = xh[ls:ls + hist, :]


def _conf_sample(p_s3, state, w_dw, b_dw, ln_g, ln_b):
    NB, ls, _ = p_s3.shape
    hist = CONV_W - 1
    sb = 8 if NB % 8 == 0 else 1
    kern = functools.partial(_conf_sample_kernel, sb=sb, ls=ls)
    vec = lambda: pl.BlockSpec((1, D_MODEL), lambda i: (0, 0))
    return pl.pallas_call(
        kern,
        out_shape=(jax.ShapeDtypeStruct((NB, ls, D_MODEL), F32),
                   jax.ShapeDtypeStruct((NB, hist, D_MODEL), F32)),
        grid=(NB // sb,),
        in_specs=[pl.BlockSpec((sb, hist, D_MODEL), lambda i: (i, 0, 0)),
                  pl.BlockSpec((sb, ls, D_MODEL), lambda i: (i, 0, 0)),
                  pl.BlockSpec((sb, ls, D_MODEL), lambda i: (i, 0, 1)),
                  pl.BlockSpec((CONV_W, D_MODEL), lambda i: (0, 0)),
                  vec(), vec(), vec()],
        out_specs=(pl.BlockSpec((sb, ls, D_MODEL), lambda i: (i, 0, 0)),
                   pl.BlockSpec((sb, hist, D_MODEL), lambda i: (i, 0, 0))),
        scratch_shapes=[pltpu.VMEM((hist + ls + 8, D_MODEL), F32)],
        compiler_params=_params(("arbitrary",)),
        name="conf_sample",
    )(state, p_s3, p_s3, w_dw, b_dw, ln_g, ln_b)


def _split(a):
    hi = a.astype(BF16)
    return hi, (a - hi.astype(F32)).astype(BF16)


def _mm3(a, b):
    ah, al = a
    bh, bl = b
    return _dot(jnp.concatenate([ah, al, ah], axis=1), jnp.concatenate([bh, bh, bl], axis=0))


def _tri_inverse(ms, i, j, C, nil):
    same = lambda n: (i >> (n.bit_length() - 1)) == (j >> (n.bit_length() - 1))
    base = min(16, C)
    eye = (i == j).astype(F32)
    bdot = lambda a, b: _dot(a.astype(BF16), b.astype(BF16))
    dps = [jnp.where(same(base), m, 0.0) for m in ms]
    xs = [eye - d for d in dps]
    for _ in range(max(0, (min(base, nil) - 1).bit_length() - 1)):
        dps = [bdot(d, d) for d in dps]
        xs = [x + bdot(d, x) for d, x in zip(dps, xs)]
    blk = base
    while blk < C:
        sel = jnp.logical_and(same(2 * blk), jnp.logical_not(same(blk)))
        ys = [bdot(jnp.where(sel, m, 0.0), x) for m, x in zip(ms, xs)]
        xs = [x - bdot(x, y) for x, y in zip(xs, ys)]
        blk *= 2
    xsp = [_split(x) for x in xs]
    res = [eye - x - _mm3(_split(m), xp) for m, x, xp in zip(ms, xs, xsp)]
    return [x + _dot(xp[0], r.astype(BF16)) for x, xp, r in zip(xs, xsp, res)]


def _gdn_chunks(seqs, alog_ref, dtb_ref, nw_ref, nil):
    C = seqs[0][0].shape[0]
    G = STACK // C
    ri = lax.broadcasted_iota(jnp.int32, (C, C), 0)
    ci = lax.broadcasted_iota(jnp.int32, (C, C), 1)
    tril = (ri >= ci).astype(BF16)
    i = lax.broadcasted_iota(jnp.int32, (STACK, STACK), 0)
    j = lax.broadcasted_iota(jnp.int32, (STACK, STACK), 1)
    shift = C.bit_length() - 1
    same = (i >> shift) == (j >> shift)
    causal = jnp.logical_and(same, i >= j)
    strict = jnp.logical_and(same, i > j)

    pre = []
    for n, (xq, xk, xv, z, ab, valid, s_ref) in enumerate(seqs):
        ok = valid > 0.5
        xa = ab + dtb_ref[...]
        softplus = jnp.maximum(xa, 0.0) + jnp.log(1.0 + jnp.exp(-jnp.abs(xa)))
        g_all = jnp.where(ok, -jnp.exp(alog_ref[...]) * softplus, 0.0)
        beta_all = jnp.where(ok, _sigmoid(ab), 0.0)
        g1 = g_all.astype(BF16)
        r1 = g_all - g1.astype(F32)
        g2 = r1.astype(BF16)
        g3 = (r1 - g2.astype(F32)).astype(BF16)
        gc_all = _dot(tril, g1) + _dot(tril, g2) + _dot(tril, g3)
        ok_st = jnp.concatenate([valid] * G, axis=0) > 0.5
        for h0 in range(0, DN_HEADS, G):
            heads = list(range(h0, h0 + G))
            stack = lambda x: jnp.concatenate([x[:, h * DN_DK:(h + 1) * DN_DK] for h in heads], axis=0)
            col = lambda a, off: jnp.concatenate([a[:, off + h:off + h + 1] for h in heads], axis=0)
            q = stack(xq)
            k = stack(xk)
            q = jnp.where(ok_st, q * lax.rsqrt(jnp.sum(q * q, axis=-1, keepdims=True) + EPS) * (DN_DK ** -0.5), 0.0)
            k = jnp.where(ok_st, k * lax.rsqrt(jnp.sum(k * k, axis=-1, keepdims=True) + EPS), 0.0)
            v = jnp.where(ok_st, stack(xv), 0.0)
            gc = col(gc_all, 0)
            beta = col(beta_all, DN_HEADS)
            g_last = jnp.concatenate([jnp.broadcast_to(gc_all[C - 1:C, h:h + 1], (C, 1)) for h in heads], axis=0)
            gb = jnp.broadcast_to(gc, (STACK, STACK))
            decay = jnp.where(causal, jnp.exp(jnp.where(causal, gb - gb.T, 0.0)), 0.0)
            egc = jnp.exp(gc)
            kb = k * beta
            pre.append(dict(n=n, heads=heads, s_ref=s_ref, q=q, kb=kb, kbf=k.astype(BF16), decay=decay, egc=egc,
                            rhs=jnp.concatenate([v * beta, kb * egc], axis=1),
                            k_dec=(k * jnp.exp(g_last - gc)).astype(BF16), zs=stack(z),
                            s_decay=[jnp.exp(gc_all[C - 1:C, h:h + 1]) for h in heads]))
    ms = [jnp.where(strict, _dot_nt(p["kb"].astype(BF16), p["kbf"]) * p["decay"], 0.0) for p in pre]
    qks = [jnp.where(causal, _dot_nt(p["q"].astype(BF16), p["kbf"]) * p["decay"], 0.0).astype(BF16) for p in pre]
    invs = _tri_inverse(ms, i, j, C, nil)
    sols = [_mm3(_split(inv), _split(p["rhs"])) for inv, p in zip(invs, pre)]
    wss = []
    for p, sol in zip(pre, sols):
        w = sol[:, DN_DV:].astype(BF16)
        q_dec = (p["q"] * p["egc"]).astype(BF16)
        wss.append([_dot(jnp.concatenate([w[g * C:(g + 1) * C], q_dec[g * C:(g + 1) * C]], axis=0),
                         p["s_ref"][h].astype(BF16)) for g, h in enumerate(p["heads"])])
    outs = [[None] * DN_HEADS for _ in seqs]
    for p, sol, ws, qk in zip(pre, sols, wss, qks):
        s_ref = p["s_ref"]
        v_new = [(sol[g * C:(g + 1) * C, :DN_DV] - ws[g][:C]).astype(BF16) for g in range(G)]
        for g, h in enumerate(p["heads"]):
            s_ref[h] = s_ref[h] * p["s_decay"][g] + _dot_tn(p["k_dec"][g * C:(g + 1) * C], v_new[g])
        o = jnp.concatenate([w[C:] for w in ws], axis=0) + _dot(qk, jnp.concatenate(v_new, axis=0))
        o = o * lax.rsqrt(jnp.mean(o * o, axis=-1, keepdims=True) + EPS) * nw_ref[...]
        og = o * (p["zs"] * _sigmoid(p["zs"]))
        for g, h in enumerate(p["heads"]):
            outs[p["n"]][h] = og[g * C:(g + 1) * C]
    return outs


def _short_conv_silu(xbuf, wc_ref, rows):
    first = 8 - (SHORT_W - 1)
    acc = xbuf[first:first + rows, :] * wc_ref[0:1, :]
    for w in range(1, SHORT_W):
        acc = acc + xbuf[first + w:first + w + rows, :] * wc_ref[w:w + 1, :]
    return acc * _sigmoid(acc)


def _gdn_prompt_kernel(*refs, ns):
    seq_refs = [refs[5 * s:5 * s + 5] for s in range(ns)]
    wc_ref, alog_ref, dtb_ref, nw_ref, o_ref, s_ref, xbuf = refs[5 * ns:]
    c = pl.program_id(1)
    n_qk = DN_HEADS * DN_DK

    @pl.when(c == 0)
    def _():
        s_ref[...] = jnp.zeros_like(s_ref)
        xbuf[:, 0:8, :] = jnp.zeros((ns, 8, xbuf.shape[2]), F32)

    rows = lax.broadcasted_iota(jnp.int32, (CHUNK, 1), 0)
    valid = jnp.logical_or(rows >= FRONT, c > 0).astype(F32)
    seqs = []
    for s, (q_ref, k_ref, v_ref, z_ref, ab_ref) in enumerate(seq_refs):
        xb = xbuf.at[s]
        xb[8:8 + CHUNK, 0:n_qk] = q_ref[...]
        xb[8:8 + CHUNK, n_qk:2 * n_qk] = k_ref[...]
        xb[8:8 + CHUNK, 2 * n_qk:] = v_ref[...]
        x = _short_conv_silu(xb, wc_ref, CHUNK)
        xb[0:8, :] = xb[CHUNK:CHUNK + 8, :]
        seqs.append((x[:, 0:n_qk], x[:, n_qk:2 * n_qk], x[:, 2 * n_qk:], z_ref[...].astype(F32), ab_ref[...],
                     valid, s_ref.at[s]))
    outs = _gdn_chunks(seqs, alog_ref, dtb_ref, nw_ref, nil=CHUNK)
    for s in range(ns):
        for h in range(DN_HEADS):
            o_ref[s, :, h * DN_DV:(h + 1) * DN_DV] = outs[s][h].astype(BF16)


def _gdn_prompt(p, pg, ab, B, LP, w_conv, alog, dtb, nw):
    nc = LP // CHUNK
    n_qk = DN_HEADS * DN_DK
    ns = 4 if B % 4 == 0 else (2 if B % 2 == 0 else 1)
    vec = lambda n: pl.BlockSpec((1, n), lambda g, c: (0, 0))
    in_specs, args = [], []
    for s in range(ns):
        for src, col in ((p, 2), (p, 3), (p, 4), (pg, 0)):
            in_specs.append(pl.BlockSpec((CHUNK, n_qk), lambda g, c, s=s, col=col: ((g * ns + s) * nc + c, col)))
            args.append(src)
        in_specs.append(pl.BlockSpec((CHUNK, LANES), lambda g, c, s=s: ((g * ns + s) * nc + c, 0)))
        args.append(ab)
    in_specs += [pl.BlockSpec((SHORT_W, 3 * n_qk), lambda g, c: (0, 0)), vec(LANES), vec(LANES), vec(DN_DV)]
    og, s_out = pl.pallas_call(
        functools.partial(_gdn_prompt_kernel, ns=ns),
        out_shape=(jax.ShapeDtypeStruct((B, LP, n_qk), BF16),
                   jax.ShapeDtypeStruct((B, DN_HEADS, DN_DK, DN_DV), F32)),
        grid=(B // ns, nc),
        in_specs=in_specs,
        out_specs=(pl.BlockSpec((ns, CHUNK, n_qk), lambda g, c: (g, c, 0)),
                   pl.BlockSpec((ns, DN_HEADS, DN_DK, DN_DV), lambda g, c: (g, 0, 0, 0))),
        scratch_shapes=[pltpu.VMEM((ns, CHUNK + 8, 3 * n_qk), F32)],
        compiler_params=_params(("arbitrary", "arbitrary")),
        name="gdn_prompt",
    )(*args, w_conv, alog, dtb, nw)
    return og.reshape(B * LP, n_qk), s_out


def _gdn_sample_kernel(st_ref, q_ref, k_ref, v_ref, z_ref, ab_ref, s0_ref, wc_ref, alog_ref, dtb_ref, nw_ref,
                       o_ref, s_ref, xbuf, zbuf, abbuf, *, sb, ls):
    n_qk = DN_HEADS * DN_DK
    C = SAMPLE_CHUNK
    hist = SHORT_W - 1
    xbuf[...] = jnp.zeros_like(xbuf)
    zbuf[...] = jnp.zeros_like(zbuf)
    abbuf[...] = jnp.zeros_like(abbuf)
    s_ref[...] = s0_ref[...]
    valid = (lax.broadcasted_iota(jnp.int32, (C, 1), 0) < ls).astype(F32)
    seqs = []
    for s in range(sb):
        xb = xbuf.at[s]
        xb[8 - hist:8, :] = st_ref[s]
        xb[8:8 + ls, 0:n_qk] = q_ref[s]
        xb[8:8 + ls, n_qk:2 * n_qk] = k_ref[s]
        xb[8:8 + ls, 2 * n_qk:] = v_ref[s]
        zbuf[s, 0:ls, :] = z_ref[s]
        abbuf[s, 0:ls, :] = ab_ref[s]
        x = _short_conv_silu(xb, wc_ref, C)
        seqs.append((x[:, 0:n_qk], x[:, n_qk:2 * n_qk], x[:, 2 * n_qk:], zbuf[s], abbuf[s], valid, s_ref.at[s]))
    outs = _gdn_chunks(seqs, alog_ref, dtb_ref, nw_ref, nil=ls)
    for s in range(sb):
        for h in range(DN_HEADS):
            o_ref[s, :, h * DN_DV:(h + 1) * DN_DV] = outs[s][h][0:ls, :]


def _gdn_sample(p_s3, z_s3, ab_s3, st_conv, s0, w_conv, alog, dtb, nw):
    NB, ls, _ = p_s3.shape
    n_qk = DN_HEADS * DN_DK
    hist = SHORT_W - 1
    assert ls <= SAMPLE_CHUNK
    sb = 4 if NB % 4 == 0 else 1
    kern = functools.partial(_gdn_sample_kernel, sb=sb, ls=ls)
    blk = lambda col: pl.BlockSpec((sb, ls, n_qk), lambda i: (i, 0, col))
    vec = lambda n: pl.BlockSpec((1, n), lambda i: (0, 0))
    sspec = lambda: pl.BlockSpec((sb, DN_HEADS, DN_DK, DN_DV), lambda i: (i, 0, 0, 0))
    return pl.pallas_call(
        kern,
        out_shape=(jax.ShapeDtypeStruct((NB, ls, n_qk), F32),
                   jax.ShapeDtypeStruct((NB, DN_HEADS, DN_DK, DN_DV), F32)),
        grid=(NB // sb,),
        in_specs=[pl.BlockSpec((sb, hist, 3 * n_qk), lambda i: (i, 0, 0)),
                  blk(2), blk(3), blk(4), blk(0),
                  pl.BlockSpec((sb, ls, LANES), lambda i: (i, 0, 0)),
                  sspec(),
                  pl.BlockSpec((SHORT_W, 3 * n_qk), lambda i: (0, 0)),
                  vec(LANES), vec(LANES), vec(DN_DV)],
        out_specs=(pl.BlockSpec((sb, ls, n_qk), lambda i: (i, 0, 0)), sspec()),
        scratch_shapes=[pltpu.VMEM((sb, SAMPLE_CHUNK + 8, 3 * n_qk), F32),
                        pltpu.VMEM((sb, SAMPLE_CHUNK, n_qk), F32),
                        pltpu.VMEM((sb, SAMPLE_CHUNK, LANES), F32)],
        compiler_params=_params(("arbitrary",)),
        name="gdn_sample",
    )(st_conv, p_s3, p_s3, p_s3, z_s3, ab_s3, s0, w_conv, alog, dtb, nw)


def _pack_bf16_pairs(x):
    half = x.shape[1] // 2
    lo = lax.bitcast_convert_type(x[:, :half].astype(BF16).astype(F32), jnp.uint32)
    hi = lax.bitcast_convert_type(x[:, half:].astype(BF16).astype(F32), jnp.uint32)
    return jnp.bitwise_or(jnp.bitwise_and(hi, jnp.uint32(0xFFFF0000)), lax.shift_right_logical(lo, jnp.uint32(16)))


def _unpack_bf16_pairs(xp):
    lo = lax.bitcast_convert_type(lax.shift_left(xp, jnp.uint32(16)), F32)
    hi = lax.bitcast_convert_type(jnp.bitwise_and(xp, jnp.uint32(0xFFFF0000)), F32)
    return jnp.concatenate([lo, hi], axis=1).astype(BF16)


def _merge_kernel(cp_ref, cs_ref, ogp_ref, ogs_ref, ga_ref, gb_ref, h_ref, wco_ref, bco_ref, wdo_ref, wo_ref, nf_ref,
                  wr_ref, br_ref, h1_ref, xp_ref, ei_ref, gt_ref, rk_ref, cnt_ref, carry, *, n_p):
    step = pl.program_id(0)

    @pl.when(step == 0)
    def _():
        carry[...] = jnp.zeros_like(carry)

    in_prompt = step < n_p
    c = jnp.where(in_prompt, cp_ref[...], cs_ref[...])
    og = jnp.where(in_prompt, ogp_ref[...], ogs_ref[...])
    ya = _dot(c, wco_ref[...]) + bco_ref[...]
    yb = _dot(og, wdo_ref[...])
    mixed = _sigmoid(ga_ref[...].astype(F32)) * ya + _sigmoid(gb_ref[...].astype(F32)) * yb
    h1 = h_ref[...] + _dot(mixed.astype(BF16), wo_ref[...])
    h1_ref[...] = h1
    xn = h1 * lax.rsqrt(jnp.mean(h1 * h1, axis=-1, keepdims=True) + EPS) * nf_ref[...]
    xp_ref[...] = _pack_bf16_pairs(xn)
    logits = _dot(xn.astype(BF16), wr_ref[...]) + br_ref[...]

    tm = logits.shape[0]
    lane = lax.broadcasted_iota(jnp.int32, (tm, LANES), 1)
    work = logits
    sels, vals = [], []
    for _ in range(TOP_K):
        m = jnp.max(work, axis=-1, keepdims=True)
        idx = jnp.min(jnp.where(work == m, lane, N_EXPERTS - 1), axis=-1, keepdims=True)
        sel = lane == idx
        sels.append(sel)
        vals.append(m)
        work = jnp.where(sel, -jnp.inf, work)
    exps = [jnp.exp(v - vals[0]) for v in vals]
    denom = exps[0]
    for e in exps[1:]:
        denom = denom + e
    onehot = jnp.zeros((tm, LANES), F32)
    for sel in sels:
        onehot = onehot + sel.astype(F32)
    ri = lax.broadcasted_iota(jnp.int32, (tm, tm), 0)
    ci = lax.broadcasted_iota(jnp.int32, (tm, tm), 1)
    before = _dot((ri > ci).astype(BF16), onehot.astype(BF16)) + carry[...]
    ei = jnp.zeros((tm, LANES), jnp.int32)
    gt = jnp.zeros((tm, LANES), F32)
    rk = jnp.zeros((tm, LANES), jnp.int32)
    for k in range(TOP_K):
        at_k = lane == k
        e_k = jnp.max(jnp.where(sels[k], lane, 0), axis=-1, keepdims=True)
        r_k = jnp.sum(jnp.where(sels[k], before, 0.0), axis=-1, keepdims=True).astype(jnp.int32)
        ei = jnp.where(at_k, e_k, ei)
        gt = jnp.where(at_k, exps[k] / denom, gt)
        rk = jnp.where(at_k, r_k, rk)
    ei_ref[...] = ei
    gt_ref[...] = gt
    rk_ref[...] = rk
    carry[...] = carry[...] + jnp.sum(onehot, axis=0, keepdims=True)
    cnt_ref[...] = carry[...]


def _merge(c_p, c_s, og_p, og_s, pg, h, wco, bco, wdo, wo, nf, wr, br):
    T = h.shape[0]
    tm = _row_tile(math.gcd(c_p.shape[0], c_s.shape[0]), 512)
    n_p = c_p.shape[0] // tm
    row = lambda col: pl.BlockSpec((tm, D_MODEL), lambda i: (i, col))
    part_p = lambda: pl.BlockSpec((tm, D_MODEL), lambda i: (jnp.minimum(i, n_p - 1), 0))
    part_s = lambda: pl.BlockSpec((tm, D_MODEL), lambda i: (jnp.maximum(i - n_p, 0), 0))
    full = lambda a, b: pl.BlockSpec((a, b), lambda i: (0, 0))
    lanes = lambda: pl.BlockSpec((tm, LANES), lambda i: (i, 0))
    return pl.pallas_call(
        functools.partial(_merge_kernel, n_p=n_p),
        out_shape=(jax.ShapeDtypeStruct((T, D_MODEL), F32),
                   jax.ShapeDtypeStruct((T, D_MODEL // 2), jnp.uint32),
                   jax.ShapeDtypeStruct((T, LANES), jnp.int32),
                   jax.ShapeDtypeStruct((T, LANES), F32),
                   jax.ShapeDtypeStruct((T, LANES), jnp.int32),
                   jax.ShapeDtypeStruct((1, LANES), F32)),
        grid=(T // tm,),
        in_specs=[part_p(), part_s(), part_p(), part_s(), row(1), row(2), row(0),
                  full(D_MODEL, D_MODEL), full(1, D_MODEL), full(D_MODEL, D_MODEL), full(D_MODEL, D_MODEL),
                  full(1, D_MODEL), full(D_MODEL, LANES), full(1, LANES)],
        out_specs=(row(0), pl.BlockSpec((tm, D_MODEL // 2), lambda i: (i, 0)), lanes(), lanes(), lanes(),
                   full(1, LANES)),
        scratch_shapes=[pltpu.VMEM((1, LANES), F32)],
        compiler_params=_params(("arbitrary",)),
        name="merge",
    )(c_p, c_s, og_p, og_s, pg, pg, h, wco, bco, wdo, wo, nf, wr, br)


ISSUE_UNROLL = 8


def _row_copy(src, src_row, dst, dst_row, sem):
    return pltpu.make_async_copy(src.at[pl.ds(src_row, 1), :], dst.at[pl.ds(dst_row, 1), :], sem)


def _dispatch_kernel(dest_ref, x_ref, xb_in, xb_out, sem, *, tm):
    del xb_in

    def issue(t, carry):
        for k in range(TOP_K):
            _row_copy(x_ref, t, xb_out, dest_ref[t * TOP_K + k], sem).start(priority=k % 2)
        return carry

    lax.fori_loop(0, tm, issue, 0, unroll=ISSUE_UNROLL)
    for k in range(TOP_K):
        pltpu.make_async_copy(x_ref, xb_out.at[pl.ds(0, tm), :], sem).wait()


def _dispatch(dest, xp, xb_init):
    T, W = xp.shape
    tm = _row_tile(T, 256)
    return pl.pallas_call(
        functools.partial(_dispatch_kernel, tm=tm),
        out_shape=jax.ShapeDtypeStruct(xb_init.shape, xb_init.dtype),
        grid=(T // tm,),
        in_specs=[pl.BlockSpec((tm * TOP_K,), lambda i: (i,), memory_space=pltpu.SMEM),
                  pl.BlockSpec((tm, W), lambda i: (i, 0)),
                  pl.BlockSpec(memory_space=pl.ANY)],
        out_specs=pl.BlockSpec(memory_space=pl.ANY),
        scratch_shapes=[pltpu.SemaphoreType.DMA],
        input_output_aliases={2: 0},
        compiler_params=_params(("arbitrary",)),
        name="moe_dispatch",
    )(dest, xp, xb_init)


def _moe_kernel(be_ref, nu_ref, x_ref, wu_ref, bu_ref, wd_ref, bd_ref, o_ref, wub, wdb):
    i = pl.program_id(0)

    @pl.when(jnp.logical_or(i == 0, be_ref[i] != be_ref[jnp.maximum(i - 1, 0)]))
    def _():
        wub[...] = wu_ref[0].astype(BF16)
        wdb[...] = wd_ref[0].astype(BF16)

    @pl.when(i < nu_ref[0])
    def _():
        hmid = _dot(_unpack_bf16_pairs(x_ref[...]), wub[...]) + bu_ref[0]
        hg = jnp.minimum(hmid[:, :D_FF], SWIGLU_LIMIT)
        hl = jnp.clip(hmid[:, D_FF:], -SWIGLU_LIMIT, SWIGLU_LIMIT)
        act = hg * _sigmoid(SWIGLU_ALPHA * hg) * (hl + 1.0)
        o_ref[...] = _dot(act.astype(BF16), wdb[...]) + bd_ref[0]

    @pl.when(i >= nu_ref[0])
    def _():
        o_ref[...] = jnp.zeros_like(o_ref)


def _moe(block_e, n_used, xb, w_up, b_up, w_down, b_down):
    R = xb.shape[0]
    nb = R // MOE_ROWS
    grid_spec = pltpu.PrefetchScalarGridSpec(
        num_scalar_prefetch=2,
        grid=(nb,),
        in_specs=[pl.BlockSpec((MOE_ROWS, D_MODEL // 2), lambda i, be, nu: (i, 0)),
                  pl.BlockSpec((1, D_MODEL, 2 * D_FF), lambda i, be, nu: (be[i], 0, 0)),
                  pl.BlockSpec((1, 1, 2 * D_FF), lambda i, be, nu: (be[i], 0, 0)),
                  pl.BlockSpec((1, D_FF, D_MODEL), lambda i, be, nu: (be[i], 0, 0)),
                  pl.BlockSpec((1, 1, D_MODEL), lambda i, be, nu: (be[i], 0, 0))],
        out_specs=pl.BlockSpec((MOE_ROWS, D_MODEL), lambda i, be, nu: (i, 0)),
        scratch_shapes=[pltpu.VMEM((D_MODEL, 2 * D_FF), BF16), pltpu.VMEM((D_FF, D_MODEL), BF16)],
    )
    return pl.pallas_call(
        _moe_kernel,
        out_shape=jax.ShapeDtypeStruct((R, D_MODEL), F32),
        grid_spec=grid_spec,
        compiler_params=pltpu.CompilerParams(dimension_semantics=("arbitrary",), vmem_limit_bytes=MOE_VMEM_LIMIT),
        name="moe_experts",
    )(block_e, n_used, xb, w_up, b_up, w_down, b_down)


def _dispatch_plan(counts, ei, rk, T):
    A = T * TOP_K
    n_blocks = -(-A // MOE_ROWS) + N_EXPERTS
    counts = counts[0, :N_EXPERTS].astype(jnp.int32)
    padded = (counts + MOE_ROWS - 1) // MOE_ROWS * MOE_ROWS
    pend = jnp.cumsum(padded)
    pstart = pend - padded
    dest = (pstart[ei[:, :TOP_K]] + rk[:, :TOP_K]).reshape(-1).astype(jnp.int32)
    starts = jnp.arange(n_blocks, dtype=jnp.int32) * MOE_ROWS
    block_e = jnp.minimum(jnp.sum((pend[None, :] <= starts[:, None]).astype(jnp.int32), axis=1), N_EXPERTS - 1)
    n_used = (pend[N_EXPERTS - 1:] // MOE_ROWS).astype(jnp.int32)
    return dest, block_e, n_used, n_blocks * MOE_ROWS


def _combine_kernel(dest_ref, gt_ref, h_ref, nw_ref, yb_ref, o_ref, buf, sem, *, tm):
    def issue(t, carry):
        for k in range(TOP_K):
            _row_copy(yb_ref, dest_ref[t * TOP_K + k], buf.at[k], t, sem).start(priority=k % 2)
        return carry

    lax.fori_loop(0, tm, issue, 0, unroll=ISSUE_UNROLL)
    for k in range(TOP_K):
        pltpu.make_async_copy(yb_ref.at[pl.ds(0, tm), :], buf.at[k], sem).wait()
    gt = gt_ref[...]
    x = h_ref[...]
    for k in range(TOP_K):
        x = x + gt[:, k:k + 1] * buf[k]
    o_ref[...] = x * lax.rsqrt(jnp.mean(x * x, axis=-1, keepdims=True) + EPS) * nw_ref[...]


def _combine(dest, gt, h1, nw, yb):
    T = h1.shape[0]
    tm = _row_tile(T, 256)
    row = lambda: pl.BlockSpec((tm, D_MODEL), lambda i: (i, 0))
    return pl.pallas_call(
        functools.partial(_combine_kernel, tm=tm),
        out_shape=jax.ShapeDtypeStruct((T, D_MODEL), F32),
        grid=(T // tm,),
        in_specs=[pl.BlockSpec((tm * TOP_K,), lambda i: (i,), memory_space=pltpu.SMEM),
                  pl.BlockSpec((tm, LANES), lambda i: (i, 0)),
                  row(),
                  pl.BlockSpec((1, D_MODEL), lambda i: (0, 0)),
                  pl.BlockSpec(memory_space=pl.ANY)],
        out_specs=row(),
        scratch_shapes=[pltpu.VMEM((TOP_K, tm, D_MODEL), F32), pltpu.SemaphoreType.DMA],
        compiler_params=_params(("arbitrary",)),
        name="moe_combine_final",
    )(dest, gt, h1, nw, yb)


def _pad_lanes(v, fill=0.0):
    v = v.reshape(1, -1).astype(F32)
    return jnp.pad(v, ((0, 0), (0, LANES - v.shape[1])), constant_values=fill)


def kernel(x_prompt, x_sample, state_conf_conv, state_dn_conv, state_dn_S, meta_tokens, norm_mix, w_in, w_conf_dw, b_conf_dw, ln_conf_g, ln_conf_b, w_conf_out, b_conf_out, w_dn_conv, dn_a_log, dn_dt_bias, dn_norm_w, w_dn_out, w_out, norm_ffn, w_router, b_router, w_up, b_up, w_down, b_down, norm_final):
    B, SEQ, D = x_prompt.shape
    NB, LS, _ = x_sample.shape
    depth = w_in.shape[0]
    assert D == D_MODEL and depth == 1 and SEQ % CHUNK == 0 and LS >= SHORT_W - 1
    LP = FRONT + N_META + SEQ
    TP = B * LP
    T = TP + NB * LS
    n_qk = DN_HEADS * DN_DK
    o_q = 2 * D_MODEL
    o_a = o_q + 4 * n_qk
    o_gate = o_a + 2 * DN_HEADS

    h0 = _tokens(x_prompt, x_sample.reshape(NB * LS, D), meta_tokens.astype(F32))

    w_in0 = w_in[0]
    o_z = o_q + 3 * n_qk
    w_gate = jnp.concatenate([w_in0[:, o_z:o_a], w_in0[:, o_gate:]], axis=1)
    w_ab = jnp.pad(w_in0[:, o_a:o_gate], ((0, 0), (0, LANES - 2 * DN_HEADS))).astype(BF16)
    wco = w_conf_out[0].astype(BF16)
    wdo = w_dn_out[0].astype(BF16)
    wo = w_out[0].astype(BF16)
    wr = jnp.pad(w_router[0], ((0, 0), (0, LANES - N_EXPERTS))).astype(BF16)
    br = _pad_lanes(b_router[0], fill=-1e30)
    alog = _pad_lanes(dn_a_log[0])
    dtb = _pad_lanes(dn_dt_bias[0])
    row = lambda v: v.reshape(1, -1).astype(F32)

    xn, ab = _rms_ab(h0, row(norm_mix[0]), w_ab)
    n_main = o_z // D_MODEL
    p = _mm_in(xn, w_in0, n_main, F32)
    pg = _mm_in(xn, w_gate, 3, BF16)
    p_s3 = p[TP:].reshape(NB, LS, n_main * D_MODEL)
    z_s3 = pg[TP:, :n_qk].astype(F32).reshape(NB, LS, n_qk)
    ab_s3 = ab[TP:].reshape(NB, LS, LANES)

    c_p, ust_p = _conf_prompt(p, B, LP, w_conf_dw[0], row(b_conf_dw[0]), row(ln_conf_g[0]), row(ln_conf_b[0]))
    c_s, conf_state_s = _conf_sample(p_s3, state_conf_conv[0], w_conf_dw[0], row(b_conf_dw[0]),
                                     row(ln_conf_g[0]), row(ln_conf_b[0]))

    og_p, s_p = _gdn_prompt(p, pg, ab, B, LP, w_dn_conv[0], alog, dtb, row(dn_norm_w[0]))
    og_s, s_s = _gdn_sample(p_s3, z_s3, ab_s3, state_dn_conv[0], state_dn_S[0], w_dn_conv[0], alog, dtb,
                            row(dn_norm_w[0]))

    h1, xp2, ei, gt, rk, counts = _merge(c_p, c_s.reshape(NB * LS, D).astype(BF16),
                                         og_p, og_s.reshape(NB * LS, n_qk).astype(BF16), pg, h0,
                                         wco, row(b_conf_out[0]), wdo, wo, row(norm_ffn[0]), wr, br)

    dest, block_e, n_used, R = _dispatch_plan(counts, ei, rk, T)
    xb = _dispatch(dest, xp2, jnp.zeros((R, D // 2), jnp.uint32))
    yb = _moe(block_e, n_used, xb, w_up[0], b_up[0].reshape(N_EXPERTS, 1, -1),
              w_down[0], b_down[0].reshape(N_EXPERTS, 1, -1))
    y = _combine(dest, gt, h1, row(norm_final), yb)

    y_prompt = y[:TP].reshape(B, LP, D)[:, FRONT + N_META:]
    y_sample = y[TP:].reshape(NB, LS, D)
    hist = CONV_W - 1
    conf_conv_prompt = ust_p[:, HALO - hist:][None]
    dn_conv_prompt = jnp.stack([p[(b + 1) * LP - (SHORT_W - 1):(b + 1) * LP, o_q:o_q + 3 * n_qk]
                                for b in range(B)])[None]
    dn_conv_sample = p_s3[:, LS - (SHORT_W - 1):, o_q:o_q + 3 * n_qk][None]
    return (y_prompt, y_sample, conf_conv_prompt, dn_conv_prompt, s_p[None],
            conf_state_s[None], dn_conv_sample, s_s[None])
```

```python
import functools
import math

import jax
import jax.numpy as jnp
from jax import lax
from jax.experimental import pallas as pl
from jax.experimental.pallas import tpu as pltpu

D_MODEL = 1024
N_META = 16
CONV_W = 31
SHORT_W = 4
DN_HEADS = 8
DN_DK = 128
DN_DV = 128
CHUNK = 64
N_EXPERTS = 32
TOP_K = 4
D_FF = 1024
SWIGLU_LIMIT = 7.0
SWIGLU_ALPHA = 1.702
EPS = 1e-6

FRONT = (-N_META) % CHUNK
SAMPLE_CHUNK = 16
STACK = 128
LANES = 128
HALO = 32
MOE_ROWS = 512
VMEM_LIMIT = 48 * 1024 * 1024
MOE_VMEM_LIMIT = 58 * 1024 * 1024

F32 = jnp.float32
BF16 = jnp.bfloat16


def _row_tile(n, pref):
    best = 16
    for t in range(16, min(n, pref) + 1, 16):
        if n % t == 0:
            best = t
    assert n % best == 0
    return best


def _sigmoid(x):
    return 1.0 / (1.0 + jnp.exp(-x))


def _dot(a, b):
    return jnp.dot(a, b, preferred_element_type=F32)


def _dot_nt(a, b):
    return lax.dot_general(a, b, (((1,), (1,)), ((), ())), preferred_element_type=F32)


def _dot_tn(a, b):
    return lax.dot_general(a, b, (((0,), (0,)), ((), ())), preferred_element_type=F32)


def _params(sem):
    return pltpu.CompilerParams(dimension_semantics=sem, vmem_limit_bytes=VMEM_LIMIT)


def _tokens_kernel(xp_ref, xs_ref, meta_ref, h_ref, head, sems, *, lp, tr, n_prompt_rows):
    b = pl.program_id(0)
    j = pl.program_id(1)
    n_head = FRONT + N_META
    row0 = pl.multiple_of(b * lp, 8)
    body = pltpu.make_async_copy(xp_ref.at[0], h_ref.at[pl.ds(row0 + n_head + j * tr, tr), :], sems.at[0])
    body.start()

    @pl.when(j == 0)
    def _():
        head[0:FRONT, :] = jnp.zeros((FRONT, D_MODEL), F32)
        head[FRONT:, :] = meta_ref[...]
        front = pltpu.make_async_copy(head, h_ref.at[pl.ds(row0, n_head), :], sems.at[1])
        front.start()
        front.wait()

    @pl.when(jnp.logical_and(b == 0, j == 0))
    def _():
        tail = pltpu.make_async_copy(xs_ref, h_ref.at[pl.ds(n_prompt_rows, xs_ref.shape[0]), :], sems.at[2])
        tail.start()
        tail.wait()

    body.wait()


def _tokens(x_prompt, x_sample2, meta):
    B, SEQ, D = x_prompt.shape
    lp = FRONT + N_META + SEQ
    T = B * lp + x_sample2.shape[0]
    tr = _row_tile(SEQ, 1024)
    return pl.pallas_call(
        functools.partial(_tokens_kernel, lp=lp, tr=tr, n_prompt_rows=B * lp),
        out_shape=jax.ShapeDtypeStruct((T, D), F32),
        grid=(B, SEQ // tr),
        in_specs=[pl.BlockSpec((1, tr, D), lambda b, j: (b, j, 0)),
                  pl.BlockSpec(x_sample2.shape, lambda b, j: (0, 0)),
                  pl.BlockSpec((N_META, D), lambda b, j: (0, 0))],
        out_specs=pl.BlockSpec(memory_space=pl.ANY),
        scratch_shapes=[pltpu.VMEM((FRONT + N_META, D), F32), pltpu.SemaphoreType.DMA((3,))],
        compiler_params=_params(("arbitrary", "arbitrary")),
        name="token_layout",
    )(x_prompt, x_sample2, meta)


def _rms_ab_kernel(h_ref, nw_ref, wab_ref, xn_ref, ab_ref):
    x = h_ref[...]
    y = x * lax.rsqrt(jnp.mean(x * x, axis=-1, keepdims=True) + EPS) * nw_ref[...]
    yb = y.astype(BF16)
    xn_ref[...] = yb
    ab_ref[...] = _dot(yb, wab_ref[...])


def _rms_ab(h, norm_w, w_ab):
    T = h.shape[0]
    tm = _row_tile(T, 1024)
    return pl.pallas_call(
        _rms_ab_kernel,
        out_shape=(jax.ShapeDtypeStruct((T, D_MODEL), BF16), jax.ShapeDtypeStruct((T, LANES), F32)),
        grid=(T // tm,),
        in_specs=[pl.BlockSpec((tm, D_MODEL), lambda i: (i, 0)),
                  pl.BlockSpec((1, D_MODEL), lambda i: (0, 0)),
                  pl.BlockSpec((D_MODEL, LANES), lambda i: (0, 0))],
        out_specs=(pl.BlockSpec((tm, D_MODEL), lambda i: (i, 0)),
                   pl.BlockSpec((tm, LANES), lambda i: (i, 0))),
        compiler_params=_params(("arbitrary",)),
        name="rms_ab",
    )(h, norm_w, w_ab)


def _mm_in_kernel(x_ref, w_ref, o_ref, wb_ref):
    @pl.when(pl.program_id(1) == 0)
    def _():
        wb_ref[...] = w_ref[...].astype(BF16)

    o_ref[...] = _dot(x_ref[...], wb_ref[...]).astype(o_ref.dtype)


def _mm_in(xn, w, n_tiles, out_dtype):
    T, K = xn.shape
    tm = _row_tile(T, 2304)
    tn = 1024
    N = n_tiles * tn
    assert N <= w.shape[1]
    return pl.pallas_call(
        _mm_in_kernel,
        out_shape=jax.ShapeDtypeStruct((T, N), out_dtype),
        grid=(N // tn, T // tm),
        in_specs=[pl.BlockSpec((tm, K), lambda j, i: (i, 0)),
                  pl.BlockSpec((K, tn), lambda j, i: (0, j))],
        out_specs=pl.BlockSpec((tm, tn), lambda j, i: (i, j)),
        scratch_shapes=[pltpu.VMEM((K, tn), BF16)],
        compiler_params=_params(("arbitrary", "arbitrary")),
        name="in_proj",
    )(xn, w)


def _ln_silu(x, g, b):
    mu = jnp.mean(x, axis=-1, keepdims=True)
    xc = x - mu
    var = jnp.mean(xc * xc, axis=-1, keepdims=True)
    y = xc * lax.rsqrt(var + EPS) * g + b
    return y * _sigmoid(y)


def _conf_prompt_kernel(pa_ref, pb_ref, ha_ref, hb_ref, wdw_ref, bdw_ref, lng_ref, lnb_ref,
                        c_ref, ust_ref, ubuf, cbuf, *, tl, rt, ct):
    t = pl.program_id(1)
    u = pa_ref[...] * _sigmoid(pb_ref[...])
    uh = ha_ref[...] * _sigmoid(hb_ref[...])
    ubuf[0:HALO, :] = jnp.where(t > 0, uh, 0.0)
    ubuf[HALO:, :] = u
    first = HALO - (CONV_W - 1)
    for r0 in range(0, tl, rt):
        for c0 in range(0, D_MODEL, ct):
            acc = jnp.zeros((rt, ct), F32)
            for s in range(8):
                part = None
                for w in range(CONV_W):
                    if (first + w) % 8 != s:
                        continue
                    base = r0 + (first + w) // 8 * 8
                    term = ubuf[base:base + rt + (8 if s else 0), c0:c0 + ct] * wdw_ref[w:w + 1, c0:c0 + ct]
                    part = term if part is None else part + term
                if part is not None:
                    acc = acc + part[s:s + rt, :]
            cbuf[r0:r0 + rt, c0:c0 + ct] = acc + bdw_ref[:, c0:c0 + ct]
    c_ref[...] = _ln_silu(cbuf[...], lng_ref[...], lnb_ref[...]).astype(BF16)

    @pl.when(t == pl.num_programs(1) - 1)
    def _():
        ust_ref[0] = ubuf[tl:tl + HALO, :]


def _conf_prompt(p, B, LP, w_dw, b_dw, ln_g, ln_b):
    tl = 192 if LP % 192 == 0 else CHUNK
    nt = LP // tl
    hb = tl // HALO
    kern = functools.partial(_conf_prompt_kernel, tl=tl, rt=64, ct=128)
    halo_idx = lambda b, t: (jnp.maximum((b * nt + t) * hb - 1, 0), 0)
    halo_idx1 = lambda b, t: (jnp.maximum((b * nt + t) * hb - 1, 0), 1)
    vec = lambda: pl.BlockSpec((1, D_MODEL), lambda b, t: (0, 0))
    return pl.pallas_call(
        kern,
        out_shape=(jax.ShapeDtypeStruct((B * LP, D_MODEL), BF16),
                   jax.ShapeDtypeStruct((B, HALO, D_MODEL), F32)),
        grid=(B, nt),
        in_specs=[pl.BlockSpec((tl, D_MODEL), lambda b, t: (b * nt + t, 0)),
                  pl.BlockSpec((tl, D_MODEL), lambda b, t: (b * nt + t, 1)),
                  pl.BlockSpec((HALO, D_MODEL), halo_idx),
                  pl.BlockSpec((HALO, D_MODEL), halo_idx1),
                  pl.BlockSpec((CONV_W, D_MODEL), lambda b, t: (0, 0)),
                  vec(), vec(), vec()],
        out_specs=(pl.BlockSpec((tl, D_MODEL), lambda b, t: (b * nt + t, 0)),
                   pl.BlockSpec((1, HALO, D_MODEL), lambda b, t: (b, 0, 0))),
        scratch_shapes=[pltpu.VMEM((HALO + tl, D_MODEL), F32), pltpu.VMEM((tl, D_MODEL), F32)],
        compiler_params=_params(("arbitrary", "arbitrary")),
        name="conf_prompt",
    )(p, p, p, p, w_dw, b_dw, ln_g, ln_b)


def _conf_sample_kernel(st_ref, pa_ref, pb_ref, wdw_ref, bdw_ref, lng_ref, lnb_ref,
                        c_ref, nst_ref, xh, *, sb, ls):
    hist = CONV_W - 1
    for s in range(sb):
        u = pa_ref[s] * _sigmoid(pb_ref[s])
        xh[0:hist, :] = st_ref[s]
        xh[hist:hist + ls, :] = u
        acc = jnp.zeros((ls, D_MODEL), F32)
        for w in range(CONV_W):
            acc = acc + xh[w:w + ls, :] * wdw_ref[w:w + 1, :]
        c_ref[s] = _ln_silu(acc + bdw_ref[...], lng_ref[...], lnb_ref[...])
        nst_ref[s] = xh[ls:ls + hist, :]


def _conf_sample(p_s3, state, w_dw, b_dw, ln_g, ln_b):
    NB, ls, _ = p_s3.shape
    hist = CONV_W - 1
    sb = 8 if NB % 8 == 0 else 1
    kern = functools.partial(_conf_sample_kernel, sb=sb, ls=ls)
    vec = lambda: pl.BlockSpec((1, D_MODEL), lambda i: (0, 0))
    return pl.pallas_call(
        kern,
        out_shape=(jax.ShapeDtypeStruct((NB, ls, D_MODEL), F32),
                   jax.ShapeDtypeStruct((NB, hist, D_MODEL), F32)),
        grid=(NB // sb,),
        in_specs=[pl.BlockSpec((sb, hist, D_MODEL), lambda i: (i, 0, 0)),
                  pl.BlockSpec((sb, ls, D_MODEL), lambda i: (i, 0, 0)),
                  pl.BlockSpec((sb, ls, D_MODEL), lambda i: (i, 0, 1)),
                  pl.BlockSpec((CONV_W, D_MODEL), lambda i: (0, 0)),
                  vec(), vec(), vec()],
        out_specs=(pl.BlockSpec((sb, ls, D_MODEL), lambda i: (i, 0, 0)),
                   pl.BlockSpec((sb, hist, D_MODEL), lambda i: (i, 0, 0))),
        scratch_shapes=[pltpu.VMEM((hist + ls + 8, D_MODEL), F32)],
        compiler_params=_params(("arbitrary",)),
        name="conf_sample",
    )(state, p_s3, p_s3, w_dw, b_dw, ln_g, ln_b)


def _split(a):
    hi = a.astype(BF16)
    return hi, (a - hi.astype(F32)).astype(BF16)


def _mm3(a, b):
    ah, al = a
    bh, bl = b
    return _dot(jnp.concatenate([ah, al, ah], axis=1), jnp.concatenate([bh, bh, bl], axis=0))


def _tri_inverse(ms, i, j, C, nil):
    same = lambda n: (i >> (n.bit_length() - 1)) == (j >> (n.bit_length() - 1))
    base = min(16, C)
    eye = (i == j).astype(F32)
    bdot = lambda a, b: _dot(a.astype(BF16), b.astype(BF16))
    dps = [jnp.where(same(base), m, 0.0) for m in ms]
    xs = [eye - d for d in dps]
    for _ in range(max(0, (min(base, nil) - 1).bit_length() - 1)):
        dps = [bdot(d, d) for d in dps]
        xs = [x + bdot(d, x) for d, x in zip(dps, xs)]
    blk = base
    while blk < C:
        sel = jnp.logical_and(same(2 * blk), jnp.logical_not(same(blk)))
        ys = [bdot(jnp.where(sel, m, 0.0), x) for m, x in zip(ms, xs)]
        xs = [x - bdot(x, y) for x, y in zip(xs, ys)]
        blk *= 2
    xsp = [_split(x) for x in xs]
    res = [eye - x - _mm3(_split(m), xp) for m, x, xp in zip(ms, xs, xsp)]
    return [x + _dot(xp[0], r.astype(BF16)) for x, xp, r in zip(xs, xsp, res)]


def _gdn_chunks(seqs, alog_ref, dtb_ref, nw_ref, nil):
    C = seqs[0][0].shape[0]
    G = STACK // C
    ri = lax.broadcasted_iota(jnp.int32, (C, C), 0)
    ci = lax.broadcasted_iota(jnp.int32, (C, C), 1)
    tril = (ri >= ci).astype(BF16)
    i = lax.broadcasted_iota(jnp.int32, (STACK, STACK), 0)
    j = lax.broadcasted_iota(jnp.int32, (STACK, STACK), 1)
    shift = C.bit_length() - 1
    same = (i >> shift) == (j >> shift)
    causal = jnp.logical_and(same, i >= j)
    strict = jnp.logical_and(same, i > j)

    pre = []
    for n, (xq, xk, xv, z, ab, valid, s_ref) in enumerate(seqs):
        ok = valid > 0.5
        xa = ab + dtb_ref[...]
        softplus = jnp.maximum(xa, 0.0) + jnp.log(1.0 + jnp.exp(-jnp.abs(xa)))
        g_all = jnp.where(ok, -jnp.exp(alog_ref[...]) * softplus, 0.0)
        beta_all = jnp.where(ok, _sigmoid(ab), 0.0)
        g1 = g_all.astype(BF16)
        r1 = g_all - g1.astype(F32)
        g2 = r1.astype(BF16)
        g3 = (r1 - g2.astype(F32)).astype(BF16)
        gc_all = _dot(tril, g1) + _dot(tril, g2) + _dot(tril, g3)
        ok_st = jnp.concatenate([valid] * G, axis=0) > 0.5
        for h0 in range(0, DN_HEADS, G):
            heads = list(range(h0, h0 + G))
            stack = lambda x: jnp.concatenate([x[:, h * DN_DK:(h + 1) * DN_DK] for h in heads], axis=0)
            col = lambda a, off: jnp.concatenate([a[:, off + h:off + h + 1] for h in heads], axis=0)
            q = stack(xq)
            k = stack(xk)
            q = jnp.where(ok_st, q * lax.rsqrt(jnp.sum(q * q, axis=-1, keepdims=True) + EPS) * (DN_DK ** -0.5), 0.0)
            k = jnp.where(ok_st, k * lax.rsqrt(jnp.sum(k * k, axis=-1, keepdims=True) + EPS), 0.0)
            v = jnp.where(ok_st, stack(xv), 0.0)
            gc = col(gc_all, 0)
            beta = col(beta_all, DN_HEADS)
            g_last = jnp.concatenate([jnp.broadcast_to(gc_all[C - 1:C, h:h + 1], (C, 1)) for h in heads], axis=0)
            gb = jnp.broadcast_to(gc, (STACK, STACK))
            decay = jnp.where(causal, jnp.exp(jnp.where(causal, gb - gb.T, 0.0)), 0.0)
            egc = jnp.exp(gc)
            kb = k * beta
            pre.append(dict(n=n, heads=heads, s_ref=s_ref, q=q, kb=kb, kbf=k.astype(BF16), decay=decay, egc=egc,
                            rhs=jnp.concatenate([v * beta, kb * egc], axis=1),
                            k_dec=(k * jnp.exp(g_last - gc)).astype(BF16), zs=stack(z),
                            s_decay=[jnp.exp(gc_all[C - 1:C, h:h + 1]) for h in heads]))
    ms = [jnp.where(strict, _dot_nt(p["kb"].astype(BF16), p["kbf"]) * p["decay"], 0.0) for p in pre]
    qks = [jnp.where(causal, _dot_nt(p["q"].astype(BF16), p["kbf"]) * p["decay"], 0.0).astype(BF16) for p in pre]
    invs = _tri_inverse(ms, i, j, C, nil)
    sols = [_mm3(_split(inv), _split(p["rhs"])) for inv, p in zip(invs, pre)]
    wss = []
    for p, sol in zip(pre, sols):
        w = sol[:, DN_DV:].astype(BF16)
        q_dec = (p["q"] * p["egc"]).astype(BF16)
        wss.append([_dot(jnp.concatenate([w[g * C:(g + 1) * C], q_dec[g * C:(g + 1) * C]], axis=0),
                         p["s_ref"][h].astype(BF16)) for g, h in enumerate(p["heads"])])
    outs = [[None] * DN_HEADS for _ in seqs]
    for p, sol, ws, qk in zip(pre, sols, wss, qks):
        s_ref = p["s_ref"]
        v_new = [(sol[g * C:(g + 1) * C, :DN_DV] - ws[g][:C]).astype(BF16) for g in range(G)]
        for g, h in enumerate(p["heads"]):
            s_ref[h] = s_ref[h] * p["s_decay"][g] + _dot_tn(p["k_dec"][g * C:(g + 1) * C], v_new[g])
        o = jnp.concatenate([w[C:] for w in ws], axis=0) + _dot(qk, jnp.concatenate(v_new, axis=0))
        o = o * lax.rsqrt(jnp.mean(o * o, axis=-1, keepdims=True) + EPS) * nw_ref[...]
        og = o * (p["zs"] * _sigmoid(p["zs"]))
        for g, h in enumerate(p["heads"]):
            outs[p["n"]][h] = og[g * C:(g + 1) * C]
    return outs


def _short_conv_silu(xbuf, wc_ref, rows):
    first = 8 - (SHORT_W - 1)
    acc = xbuf[first:first + rows, :] * wc_ref[0:1, :]
    for w in range(1, SHORT_W):
        acc = acc + xbuf[first + w:first + w + rows, :] * wc_ref[w:w + 1, :]
    return acc * _sigmoid(acc)


def _gdn_prompt_kernel(*refs, ns):
    seq_refs = [refs[5 * s:5 * s + 5] for s in range(ns)]
    wc_ref, alog_ref, dtb_ref, nw_ref, o_ref, s_ref, xbuf = refs[5 * ns:]
    c = pl.program_id(1)
    n_qk = DN_HEADS * DN_DK

    @pl.when(c == 0)
    def _():
        s_ref[...] = jnp.zeros_like(s_ref)
        xbuf[:, 0:8, :] = jnp.zeros((ns, 8, xbuf.shape[2]), F32)

    rows = lax.broadcasted_iota(jnp.int32, (CHUNK, 1), 0)
    valid = jnp.logical_or(rows >= FRONT, c > 0).astype(F32)
    seqs = []
    for s, (q_ref, k_ref, v_ref, z_ref, ab_ref) in enumerate(seq_refs):
        xb = xbuf.at[s]
        xb[8:8 + CHUNK, 0:n_qk] = q_ref[...]
        xb[8:8 + CHUNK, n_qk:2 * n_qk] = k_ref[...]
        xb[8:8 + CHUNK, 2 * n_qk:] = v_ref[...]
        x = _short_conv_silu(xb, wc_ref, CHUNK)
        xb[0:8, :] = xb[CHUNK:CHUNK + 8, :]
        seqs.append((x[:, 0:n_qk], x[:, n_qk:2 * n_qk], x[:, 2 * n_qk:], z_ref[...].astype(F32), ab_ref[...],
                     valid, s_ref.at[s]))
    outs = _gdn_chunks(seqs, alog_ref, dtb_ref, nw_ref, nil=CHUNK)
    for s in range(ns):
        for h in range(DN_HEADS):
            o_ref[s, :, h * DN_DV:(h + 1) * DN_DV] = outs[s][h].astype(BF16)


def _gdn_prompt(p, pg, ab, B, LP, w_conv, alog, dtb, nw):
    nc = LP // CHUNK
    n_qk = DN_HEADS * DN_DK
    ns = 4 if B % 4 == 0 else (2 if B % 2 == 0 else 1)
    vec = lambda n: pl.BlockSpec((1, n), lambda g, c: (0, 0))
    in_specs, args = [], []
    for s in range(ns):
        for src, col in ((p, 2), (p, 3), (p, 4), (pg, 0)):
            in_specs.append(pl.BlockSpec((CHUNK, n_qk), lambda g, c, s=s, col=col: ((g * ns + s) * nc + c, col)))
            args.append(src)
        in_specs.append(pl.BlockSpec((CHUNK, LANES), lambda g, c, s=s: ((g * ns + s) * nc + c, 0)))
        args.append(ab)
    in_specs += [pl.BlockSpec((SHORT_W, 3 * n_qk), lambda g, c: (0, 0)), vec(LANES), vec(LANES), vec(DN_DV)]
    og, s_out = pl.pallas_call(
        functools.partial(_gdn_prompt_kernel, ns=ns),
        out_shape=(jax.ShapeDtypeStruct((B, LP, n_qk), BF16),
                   jax.ShapeDtypeStruct((B, DN_HEADS, DN_DK, DN_DV), F32)),
        grid=(B // ns, nc),
        in_specs=in_specs,
        out_specs=(pl.BlockSpec((ns, CHUNK, n_qk), lambda g, c: (g, c, 0)),
                   pl.BlockSpec((ns, DN_HEADS, DN_DK, DN_DV), lambda g, c: (g, 0, 0, 0))),
        scratch_shapes=[pltpu.VMEM((ns, CHUNK + 8, 3 * n_qk), F32)],
        compiler_params=_params(("arbitrary", "arbitrary")),
        name="gdn_prompt",
    )(*args, w_conv, alog, dtb, nw)
    return og.reshape(B * LP, n_qk), s_out


def _gdn_sample_kernel(st_ref, q_ref, k_ref, v_ref, z_ref, ab_ref, s0_ref, wc_ref, alog_ref, dtb_ref, nw_ref,
                       o_ref, s_ref, xbuf, zbuf, abbuf, *, sb, ls):
    n_qk = DN_HEADS * DN_DK
    C = SAMPLE_CHUNK
    hist = SHORT_W - 1
    xbuf[...] = jnp.zeros_like(xbuf)
    zbuf[...] = jnp.zeros_like(zbuf)
    abbuf[...] = jnp.zeros_like(abbuf)
    s_ref[...] = s0_ref[...]
    valid = (lax.broadcasted_iota(jnp.int32, (C, 1), 0) < ls).astype(F32)
    seqs = []
    for s in range(sb):
        xb = xbuf.at[s]
        xb[8 - hist:8, :] = st_ref[s]
        xb[8:8 + ls, 0:n_qk] = q_ref[s]
        xb[8:8 + ls, n_qk:2 * n_qk] = k_ref[s]
        xb[8:8 + ls, 2 * n_qk:] = v_ref[s]
        zbuf[s, 0:ls, :] = z_ref[s]
        abbuf[s, 0:ls, :] = ab_ref[s]
        x = _short_conv_silu(xb, wc_ref, C)
        seqs.append((x[:, 0:n_qk], x[:, n_qk:2 * n_qk], x[:, 2 * n_qk:], zbuf[s], abbuf[s], valid, s_ref.at[s]))
    outs = _gdn_chunks(seqs, alog_ref, dtb_ref, nw_ref, nil=ls)
    for s in range(sb):
        for h in range(DN_HEADS):
            o_ref[s, :, h * DN_DV:(h + 1) * DN_DV] = outs[s][h][0:ls, :]


def _gdn_sample(p_s3, z_s3, ab_s3, st_conv, s0, w_conv, alog, dtb, nw):
    NB, ls, _ = p_s3.shape
    n_qk = DN_HEADS * DN_DK
    hist = SHORT_W - 1
    assert ls <= SAMPLE_CHUNK
    sb = 4 if NB % 4 == 0 else 1
    kern = functools.partial(_gdn_sample_kernel, sb=sb, ls=ls)
    blk = lambda col: pl.BlockSpec((sb, ls, n_qk), lambda i: (i, 0, col))
    vec = lambda n: pl.BlockSpec((1, n), lambda i: (0, 0))
    sspec = lambda: pl.BlockSpec((sb, DN_HEADS, DN_DK, DN_DV), lambda i: (i, 0, 0, 0))
    return pl.pallas_call(
        kern,
        out_shape=(jax.ShapeDtypeStruct((NB, ls, n_qk), F32),
                   jax.ShapeDtypeStruct((NB, DN_HEADS, DN_DK, DN_DV), F32)),
        grid=(NB // sb,),
        in_specs=[pl.BlockSpec((sb, hist, 3 * n_qk), lambda i: (i, 0, 0)),
                  blk(2), blk(3), blk(4), blk(0),
                  pl.BlockSpec((sb, ls, LANES), lambda i: (i, 0, 0)),
                  sspec(),
                  pl.BlockSpec((SHORT_W, 3 * n_qk), lambda i: (0, 0)),
                  vec(LANES), vec(LANES), vec(DN_DV)],
        out_specs=(pl.BlockSpec((sb, ls, n_qk), lambda i: (i, 0, 0)), sspec()),
        scratch_shapes=[pltpu.VMEM((sb, SAMPLE_CHUNK + 8, 3 * n_qk), F32),
                        pltpu.VMEM((sb, SAMPLE_CHUNK, n_qk), F32),
                        pltpu.VMEM((sb, SAMPLE_CHUNK, LANES), F32)],
        compiler_params=_params(("arbitrary",)),
        name="gdn_sample",
    )(st_conv, p_s3, p_s3, p_s3, z_s3, ab_s3, s0, w_conv, alog, dtb, nw)


def _pack_bf16_pairs(x):
    half = x.shape[1] // 2
    lo = lax.bitcast_convert_type(x[:, :half].astype(BF16).astype(F32), jnp.uint32)
    hi = lax.bitcast_convert_type(x[:, half:].astype(BF16).astype(F32), jnp.uint32)
    return jnp.bitwise_or(jnp.bitwise_and(hi, jnp.uint32(0xFFFF0000)), lax.shift_right_logical(lo, jnp.uint32(16)))


def _unpack_bf16_pairs(xp):
    lo = lax.bitcast_convert_type(lax.shift_left(xp, jnp.uint32(16)), F32)
    hi = lax.bitcast_convert_type(jnp.bitwise_and(xp, jnp.uint32(0xFFFF0000)), F32)
    return jnp.concatenate([lo, hi], axis=1).astype(BF16)


def _merge_kernel(cp_ref, cs_ref, ogp_ref, ogs_ref, ga_ref, gb_ref, h_ref, wco_ref, bco_ref, wdo_ref, wo_ref, nf_ref,
                  wr_ref, br_ref, h1_ref, xp_ref, ei_ref, gt_ref, rk_ref, cnt_ref, carry, *, n_p):
    step = pl.program_id(0)

    @pl.when(step == 0)
    def _():
        carry[...] = jnp.zeros_like(carry)

    in_prompt = step < n_p
    c = jnp.where(in_prompt, cp_ref[...], cs_ref[...])
    og = jnp.where(in_prompt, ogp_ref[...], ogs_ref[...])
    ya = _dot(c, wco_ref[...]) + bco_ref[...]
    yb = _dot(og, wdo_ref[...])
    mixed = _sigmoid(ga_ref[...].astype(F32)) * ya + _sigmoid(gb_ref[...].astype(F32)) * yb
    h1 = h_ref[...] + _dot(mixed.astype(BF16), wo_ref[...])
    h1_ref[...] = h1
    xn = h1 * lax.rsqrt(jnp.mean(h1 * h1, axis=-1, keepdims=True) + EPS) * nf_ref[...]
    xp_ref[...] = _pack_bf16_pairs(xn)
    logits = _dot(xn.astype(BF16), wr_ref[...]) + br_ref[...]

    tm = logits.shape[0]
    lane = lax.broadcasted_iota(jnp.int32, (tm, LANES), 1)
    work = logits
    sels, vals = [], []
    for _ in range(TOP_K):
        m = jnp.max(work, axis=-1, keepdims=True)
        idx = jnp.min(jnp.where(work == m, lane, N_EXPERTS - 1), axis=-1, keepdims=True)
        sel = lane == idx
        sels.append(sel)
        vals.append(m)
        work = jnp.where(sel, -jnp.inf, work)
    exps = [jnp.exp(v - vals[0]) for v in vals]
    denom = exps[0]
    for e in exps[1:]:
        denom = denom + e
    onehot = jnp.zeros((tm, LANES), F32)
    for sel in sels:
        onehot = onehot + sel.astype(F32)
    ri = lax.broadcasted_iota(jnp.int32, (tm, tm), 0)
    ci = lax.broadcasted_iota(jnp.int32, (tm, tm), 1)
    before = _dot((ri > ci).astype(BF16), onehot.astype(BF16)) + carry[...]
    ei = jnp.zeros((tm, LANES), jnp.int32)
    gt = jnp.zeros((tm, LANES), F32)
    rk = jnp.zeros((tm, LANES), jnp.int32)
    for k in range(TOP_K):
        at_k = lane == k
        e_k = jnp.max(jnp.where(sels[k], lane, 0), axis=-1, keepdims=True)
        r_k = jnp.sum(jnp.where(sels[k], before, 0.0), axis=-1, keepdims=True).astype(jnp.int32)
        ei = jnp.where(at_k, e_k, ei)
        gt = jnp.where(at_k, exps[k] / denom, gt)
        rk = jnp.where(at_k, r_k, rk)
    ei_ref[...] = ei
    gt_ref[...] = gt
    rk_ref[...] = rk
    carry[...] = carry[...] + jnp.sum(onehot, axis=0, keepdims=True)
    cnt_ref[...] = carry[...]


def _merge(c_p, c_s, og_p, og_s, pg, h, wco, bco, wdo, wo, nf, wr, br):
    T = h.shape[0]
    tm = _row_tile(math.gcd(c_p.shape[0], c_s.shape[0]), 512)
    n_p = c_p.shape[0] // tm
    row = lambda col: pl.BlockSpec((tm, D_MODEL), lambda i: (i, col))
    part_p = lambda: pl.BlockSpec((tm, D_MODEL), lambda i: (jnp.minimum(i, n_p - 1), 0))
    part_s = lambda: pl.BlockSpec((tm, D_MODEL), lambda i: (jnp.maximum(i - n_p, 0), 0))
    full = lambda a, b: pl.BlockSpec((a, b), lambda i: (0, 0))
    lanes = lambda: pl.BlockSpec((tm, LANES), lambda i: (i, 0))
    return pl.pallas_call(
        functools.partial(_merge_kernel, n_p=n_p),
        out_shape=(jax.ShapeDtypeStruct((T, D_MODEL), F32),
                   jax.ShapeDtypeStruct((T, D_MODEL // 2), jnp.uint32),
                   jax.ShapeDtypeStruct((T, LANES), jnp.int32),
                   jax.ShapeDtypeStruct((T, LANES), F32),
                   jax.ShapeDtypeStruct((T, LANES), jnp.int32),
                   jax.ShapeDtypeStruct((1, LANES), F32)),
        grid=(T // tm,),
        in_specs=[part_p(), part_s(), part_p(), part_s(), row(1), row(2), row(0),
                  full(D_MODEL, D_MODEL), full(1, D_MODEL), full(D_MODEL, D_MODEL), full(D_MODEL, D_MODEL),
                  full(1, D_MODEL), full(D_MODEL, LANES), full(1, LANES)],
        out_specs=(row(0), pl.BlockSpec((tm, D_MODEL // 2), lambda i: (i, 0)), lanes(), lanes(), lanes(),
                   full(1, LANES)),
        scratch_shapes=[pltpu.VMEM((1, LANES), F32)],
        compiler_params=_params(("arbitrary",)),
        name="merge",
    )(c_p, c_s, og_p, og_s, pg, pg, h, wco, bco, wdo, wo, nf, wr, br)


ISSUE_UNROLL = 8


def _row_copy(src, src_row, dst, dst_row, sem):
    return pltpu.make_async_copy(src.at[pl.ds(src_row, 1), :], dst.at[pl.ds(dst_row, 1), :], sem)


def _dispatch_kernel(dest_ref, x_ref, xb_in, xb_out, sem, *, tm):
    del xb_in

    def issue(t, carry):
        for k in range(TOP_K):
            _row_copy(x_ref, t, xb_out, dest_ref[t * TOP_K + k], sem).start(priority=k % 2)
        return carry

    lax.fori_loop(0, tm, issue, 0, unroll=ISSUE_UNROLL)
    for k in range(TOP_K):
        pltpu.make_async_copy(x_ref, xb_out.at[pl.ds(0, tm), :], sem).wait()


def _dispatch(dest, xp, xb_init):
    T, W = xp.shape
    tm = _row_tile(T, 256)
    return pl.pallas_call(
        functools.partial(_dispatch_kernel, tm=tm),
        out_shape=jax.ShapeDtypeStruct(xb_init.shape, xb_init.dtype),
        grid=(T // tm,),
        in_specs=[pl.BlockSpec((tm * TOP_K,), lambda i: (i,), memory_space=pltpu.SMEM),
                  pl.BlockSpec((tm, W), lambda i: (i, 0)),
                  pl.BlockSpec(memory_space=pl.ANY)],
        out_specs=pl.BlockSpec(memory_space=pl.ANY),
        scratch_shapes=[pltpu.SemaphoreType.DMA],
        input_output_aliases={2: 0},
        compiler_params=_params(("arbitrary",)),
        name="moe_dispatch",
    )(dest, xp, xb_init)


def _moe_kernel(be_ref, nu_ref, x_ref, wu_ref, bu_ref, wd_ref, bd_ref, o_ref, wub, wdb):
    i = pl.program_id(0)

    @pl.when(jnp.logical_or(i == 0, be_ref[i] != be_ref[jnp.maximum(i - 1, 0)]))
    def _():
        wub[...] = wu_ref[0].astype(BF16)
        wdb[...] = wd_ref[0].astype(BF16)

    @pl.when(i < nu_ref[0])
    def _():
        hmid = _dot(_unpack_bf16_pairs(x_ref[...]), wub[...]) + bu_ref[0]
        hg = jnp.minimum(hmid[:, :D_FF], SWIGLU_LIMIT)
        hl = jnp.clip(hmid[:, D_FF:], -SWIGLU_LIMIT, SWIGLU_LIMIT)
        act = hg * _sigmoid(SWIGLU_ALPHA * hg) * (hl + 1.0)
        o_ref[...] = _dot(act.astype(BF16), wdb[...]) + bd_ref[0]

    @pl.when(i >= nu_ref[0])
    def _():
        o_ref[...] = jnp.zeros_like(o_ref)


def _moe(block_e, n_used, xb, w_up, b_up, w_down, b_down):
    R = xb.shape[0]
    nb = R // MOE_ROWS
    grid_spec = pltpu.PrefetchScalarGridSpec(
        num_scalar_prefetch=2,
        grid=(nb,),
        in_specs=[pl.BlockSpec((MOE_ROWS, D_MODEL // 2), lambda i, be, nu: (i, 0)),
                  pl.BlockSpec((1, D_MODEL, 2 * D_FF), lambda i, be, nu: (be[i], 0, 0)),
                  pl.BlockSpec((1, 1, 2 * D_FF), lambda i, be, nu: (be[i], 0, 0)),
                  pl.BlockSpec((1, D_FF, D_MODEL), lambda i, be, nu: (be[i], 0, 0)),
                  pl.BlockSpec((1, 1, D_MODEL), lambda i, be, nu: (be[i], 0, 0))],
        out_specs=pl.BlockSpec((MOE_ROWS, D_MODEL), lambda i, be, nu: (i, 0)),
        scratch_shapes=[pltpu.VMEM((D_MODEL, 2 * D_FF), BF16), pltpu.VMEM((D_FF, D_MODEL), BF16)],
    )
    return pl.pallas_call(
        _moe_kernel,
        out_shape=jax.ShapeDtypeStruct((R, D_MODEL), F32),
        grid_spec=grid_spec,
        compiler_params=pltpu.CompilerParams(dimension_semantics=("arbitrary",), vmem_limit_bytes=MOE_VMEM_LIMIT),
        name="moe_experts",
    )(block_e, n_used, xb, w_up, b_up, w_down, b_down)


def _dispatch_plan(counts, ei, rk, T):
    A = T * TOP_K
    n_blocks = -(-A // MOE_ROWS) + N_EXPERTS
    counts = counts[0, :N_EXPERTS].astype(jnp.int32)
    padded = (counts + MOE_ROWS - 1) // MOE_ROWS * MOE_ROWS
    pend = jnp.cumsum(padded)
    pstart = pend - padded
    dest = (pstart[ei[:, :TOP_K]] + rk[:, :TOP_K]).reshape(-1).astype(jnp.int32)
    starts = jnp.arange(n_blocks, dtype=jnp.int32) * MOE_ROWS
    block_e = jnp.minimum(jnp.sum((pend[None, :] <= starts[:, None]).astype(jnp.int32), axis=1), N_EXPERTS - 1)
    n_used = (pend[N_EXPERTS - 1:] // MOE_ROWS).astype(jnp.int32)
    return dest, block_e, n_used, n_blocks * MOE_ROWS


def _combine_kernel(dest_ref, dnext_ref, gt_ref, h_ref, nw_ref, yb_ref, o_ref, buf, sems, *, tm):
    i = pl.program_id(0)
    slot = lax.rem(i, 2)

    def request(d_ref, s):
        def issue(t, carry):
            for k in range(TOP_K):
                _row_copy(yb_ref, d_ref[t * TOP_K + k], buf.at[s, k], t, sems.at[s]).start(priority=k % 2)
            return carry

        lax.fori_loop(0, tm, issue, 0, unroll=ISSUE_UNROLL)

    @pl.when(i == 0)
    def _():
        request(dest_ref, 0)

    @pl.when(i + 1 < pl.num_programs(0))
    def _():
        request(dnext_ref, 1 - slot)

    for k in range(TOP_K):
        pltpu.make_async_copy(yb_ref.at[pl.ds(0, tm), :], buf.at[slot, k], sems.at[slot]).wait()
    gt = gt_ref[...]
    x = h_ref[...]
    for k in range(TOP_K):
        x = x + gt[:, k:k + 1] * buf[slot, k]
    o_ref[...] = x * lax.rsqrt(jnp.mean(x * x, axis=-1, keepdims=True) + EPS) * nw_ref[...]


def _combine(dest, gt, h1, nw, yb):
    T = h1.shape[0]
    tm = _row_tile(T, 256)
    n = T // tm
    row = lambda: pl.BlockSpec((tm, D_MODEL), lambda i: (i, 0))
    return pl.pallas_call(
        functools.partial(_combine_kernel, tm=tm),
        out_shape=jax.ShapeDtypeStruct((T, D_MODEL), F32),
        grid=(T // tm,),
        in_specs=[pl.BlockSpec((tm * TOP_K,), lambda i: (i,), memory_space=pltpu.SMEM),
                  pl.BlockSpec((tm * TOP_K,), lambda i: (jnp.minimum(i + 1, n - 1),), memory_space=pltpu.SMEM),
                  pl.BlockSpec((tm, LANES), lambda i: (i, 0)),
                  row(),
                  pl.BlockSpec((1, D_MODEL), lambda i: (0, 0)),
                  pl.BlockSpec(memory_space=pl.ANY)],
        out_specs=row(),
        scratch_shapes=[pltpu.VMEM((2, TOP_K, tm, D_MODEL), F32), pltpu.SemaphoreType.DMA((2,))],
        compiler_params=_params(("arbitrary",)),
        name="moe_combine_final",
    )(dest, dest, gt, h1, nw, yb)


def _pad_lanes(v, fill=0.0):
    v = v.reshape(1, -1).astype(F32)
    return jnp.pad(v, ((0, 0), (0, LANES - v.shape[1])), constant_values=fill)


def kernel(x_prompt, x_sample, state_conf_conv, state_dn_conv, state_dn_S, meta_tokens, norm_mix, w_in, w_conf_dw, b_conf_dw, ln_conf_g, ln_conf_b, w_conf_out, b_conf_out, w_dn_conv, dn_a_log, dn_dt_bias, dn_norm_w, w_dn_out, w_out, norm_ffn, w_router, b_router, w_up, b_up, w_down, b_down, norm_final):
    B, SEQ, D = x_prompt.shape
    NB, LS, _ = x_sample.shape
    depth = w_in.shape[0]
    assert D == D_MODEL and depth == 1 and SEQ % CHUNK == 0 and LS >= SHORT_W - 1
    LP = FRONT + N_META + SEQ
    TP = B * LP
    T = TP + NB * LS
    n_qk = DN_HEADS * DN_DK
    o_q = 2 * D_MODEL
    o_a = o_q + 4 * n_qk
    o_gate = o_a + 2 * DN_HEADS

    h0 = _tokens(x_prompt, x_sample.reshape(NB * LS, D), meta_tokens.astype(F32))

    w_in0 = w_in[0]
    o_z = o_q + 3 * n_qk
    w_gate = jnp.concatenate([w_in0[:, o_z:o_a], w_in0[:, o_gate:]], axis=1)
    w_ab = jnp.pad(w_in0[:, o_a:o_gate], ((0, 0), (0, LANES - 2 * DN_HEADS))).astype(BF16)
    wco = w_conf_out[0].astype(BF16)
    wdo = w_dn_out[0].astype(BF16)
    wo = w_out[0].astype(BF16)
    wr = jnp.pad(w_router[0], ((0, 0), (0, LANES - N_EXPERTS))).astype(BF16)
    br = _pad_lanes(b_router[0], fill=-1e30)
    alog = _pad_lanes(dn_a_log[0])
    dtb = _pad_lanes(dn_dt_bias[0])
    row = lambda v: v.reshape(1, -1).astype(F32)

    xn, ab = _rms_ab(h0, row(norm_mix[0]), w_ab)
    n_main = o_z // D_MODEL
    p = _mm_in(xn, w_in0, n_main, F32)
    pg = _mm_in(xn, w_gate, 3, BF16)
    p_s3 = p[TP:].reshape(NB, LS, n_main * D_MODEL)
    z_s3 = pg[TP:, :n_qk].astype(F32).reshape(NB, LS, n_qk)
    ab_s3 = ab[TP:].reshape(NB, LS, LANES)

    c_p, ust_p = _conf_prompt(p, B, LP, w_conf_dw[0], row(b_conf_dw[0]), row(ln_conf_g[0]), row(ln_conf_b[0]))
    c_s, conf_state_s = _conf_sample(p_s3, state_conf_conv[0], w_conf_dw[0], row(b_conf_dw[0]),
                                     row(ln_conf_g[0]), row(ln_conf_b[0]))

    og_p, s_p = _gdn_prompt(p, pg, ab, B, LP, w_dn_conv[0], alog, dtb, row(dn_norm_w[0]))
    og_s, s_s = _gdn_sample(p_s3, z_s3, ab_s3, state_dn_conv[0], state_dn_S[0], w_dn_conv[0], alog, dtb,
                            row(dn_norm_w[0]))

    h1, xp2, ei, gt, rk, counts = _merge(c_p, c_s.reshape(NB * LS, D).astype(BF16),
                                         og_p, og_s.reshape(NB * LS, n_qk).astype(BF16), pg, h0,
                                         wco, row(b_conf_out[0]), wdo, wo, row(norm_ffn[0]), wr, br)

    dest, block_e, n_used, R = _dispatch_plan(counts, ei, rk, T)
    xb = _dispatch(dest, xp2, jnp.zeros((R, D // 2), jnp.uint32))
    yb = _moe(block_e, n_used, xb, w_up[0], b_up[0].reshape(N_EXPERTS, 1, -1),
              w_down[0], b_down[0].reshape(N_EXPERTS, 1, -1))
    y = _combine(dest, gt, h1, row(norm_final), yb)

    y_prompt = jnp.stack([y[b * LP + FRONT + N_META:(b + 1) * LP] for b in range(B)])
    y_sample = y[TP:].reshape(NB, LS, D)
    hist = CONV_W - 1
    conf_conv_prompt = ust_p[:, HALO - hist:][None]
    dn_conv_prompt = jnp.stack([p[(b + 1) * LP - (SHORT_W - 1):(b + 1) * LP, o_q:o_q + 3 * n_qk]
                                for b in range(B)])[None]
    dn_conv_sample = p_s3[:, LS - (SHORT_W - 1):, o_q:o_q + 3 * n_qk][None]
    return (y_prompt, y_sample, conf_conv_prompt, dn_conv_prompt, s_p[None],
            conf_state_s[None], dn_conv_sample, s_s[None])
```

```python
import functools
import math

import jax
import jax.numpy as jnp
from jax import lax
from jax.experimental import pallas as pl
from jax.experimental.pallas import tpu as pltpu

D_MODEL = 1024
N_META = 16
CONV_W = 31
SHORT_W = 4
DN_HEADS = 8
DN_DK = 128
DN_DV = 128
CHUNK = 64
N_EXPERTS = 32
TOP_K = 4
D_FF = 1024
SWIGLU_LIMIT = 7.0
SWIGLU_ALPHA = 1.702
EPS = 1e-6

FRONT = (-N_META) % CHUNK
SAMPLE_CHUNK = 16
STACK = 128
LANES = 128
HALO = 32
MOE_ROWS = 512
VMEM_LIMIT = 48 * 1024 * 1024
MOE_VMEM_LIMIT = 58 * 1024 * 1024

F32 = jnp.float32
BF16 = jnp.bfloat16


def _row_tile(n, pref):
    best = 16
    for t in range(16, min(n, pref) + 1, 16):
        if n % t == 0:
            best = t
    assert n % best == 0
    return best


def _sigmoid(x):
    return 1.0 / (1.0 + jnp.exp(-x))


def _dot(a, b):
    return jnp.dot(a, b, preferred_element_type=F32)


def _dot_nt(a, b):
    return lax.dot_general(a, b, (((1,), (1,)), ((), ())), preferred_element_type=F32)


def _dot_tn(a, b):
    return lax.dot_general(a, b, (((0,), (0,)), ((), ())), preferred_element_type=F32)


def _params(sem):
    return pltpu.CompilerParams(dimension_semantics=sem, vmem_limit_bytes=VMEM_LIMIT)


def _tokens_kernel(xp_ref, xs_ref, meta_ref, h_ref, head, sems, *, lp, tr, n_prompt_rows):
    b = pl.program_id(0)
    j = pl.program_id(1)
    n_head = FRONT + N_META
    row0 = pl.multiple_of(b * lp, 8)
    body = pltpu.make_async_copy(xp_ref.at[0], h_ref.at[pl.ds(row0 + n_head + j * tr, tr), :], sems.at[0])
    body.start()

    @pl.when(j == 0)
    def _():
        head[0:FRONT, :] = jnp.zeros((FRONT, D_MODEL), F32)
        head[FRONT:, :] = meta_ref[...]
        front = pltpu.make_async_copy(head, h_ref.at[pl.ds(row0, n_head), :], sems.at[1])
        front.start()
        front.wait()

    @pl.when(jnp.logical_and(b == 0, j == 0))
    def _():
        tail = pltpu.make_async_copy(xs_ref, h_ref.at[pl.ds(n_prompt_rows, xs_ref.shape[0]), :], sems.at[2])
        tail.start()
        tail.wait()

    body.wait()


def _tokens(x_prompt, x_sample2, meta):
    B, SEQ, D = x_prompt.shape
    lp = FRONT + N_META + SEQ
    T = B * lp + x_sample2.shape[0]
    tr = _row_tile(SEQ, 1024)
    return pl.pallas_call(
        functools.partial(_tokens_kernel, lp=lp, tr=tr, n_prompt_rows=B * lp),
        out_shape=jax.ShapeDtypeStruct((T, D), F32),
        grid=(B, SEQ // tr),
        in_specs=[pl.BlockSpec((1, tr, D), lambda b, j: (b, j, 0)),
                  pl.BlockSpec(x_sample2.shape, lambda b, j: (0, 0)),
                  pl.BlockSpec((N_META, D), lambda b, j: (0, 0))],
        out_specs=pl.BlockSpec(memory_space=pl.ANY),
        scratch_shapes=[pltpu.VMEM((FRONT + N_META, D), F32), pltpu.SemaphoreType.DMA((3,))],
        compiler_params=_params(("arbitrary", "arbitrary")),
        name="token_layout",
    )(x_prompt, x_sample2, meta)


def _rms_ab_kernel(h_ref, nw_ref, wab_ref, xn_ref, ab_ref):
    x = h_ref[...]
    y = x * lax.rsqrt(jnp.mean(x * x, axis=-1, keepdims=True) + EPS) * nw_ref[...]
    yb = y.astype(BF16)
    xn_ref[...] = yb
    ab_ref[...] = _dot(yb, wab_ref[...])


def _rms_ab(h, norm_w, w_ab):
    T = h.shape[0]
    tm = _row_tile(T, 1024)
    return pl.pallas_call(
        _rms_ab_kernel,
        out_shape=(jax.ShapeDtypeStruct((T, D_MODEL), BF16), jax.ShapeDtypeStruct((T, LANES), F32)),
        grid=(T // tm,),
        in_specs=[pl.BlockSpec((tm, D_MODEL), lambda i: (i, 0)),
                  pl.BlockSpec((1, D_MODEL), lambda i: (0, 0)),
                  pl.BlockSpec((D_MODEL, LANES), lambda i: (0, 0))],
        out_specs=(pl.BlockSpec((tm, D_MODEL), lambda i: (i, 0)),
                   pl.BlockSpec((tm, LANES), lambda i: (i, 0))),
        compiler_params=_params(("arbitrary",)),
        name="rms_ab",
    )(h, norm_w, w_ab)


def _mm_in_kernel(x_ref, w_ref, o_ref, wb_ref):
    @pl.when(pl.program_id(1) == 0)
    def _():
        wb_ref[...] = w_ref[...].astype(BF16)

    o_ref[...] = _dot(x_ref[...], wb_ref[...]).astype(o_ref.dtype)


def _mm_in(xn, w, n_tiles, out_dtype):
    T, K = xn.shape
    tm = _row_tile(T, 2304)
    tn = 1024
    N = n_tiles * tn
    assert N <= w.shape[1]
    return pl.pallas_call(
        _mm_in_kernel,
        out_shape=jax.ShapeDtypeStruct((T, N), out_dtype),
        grid=(N // tn, T // tm),
        in_specs=[pl.BlockSpec((tm, K), lambda j, i: (i, 0)),
                  pl.BlockSpec((K, tn), lambda j, i: (0, j))],
        out_specs=pl.BlockSpec((tm, tn), lambda j, i: (i, j)),
        scratch_shapes=[pltpu.VMEM((K, tn), BF16)],
        compiler_params=_params(("arbitrary", "arbitrary")),
        name="in_proj",
    )(xn, w)


def _ln_silu(x, g, b):
    mu = jnp.mean(x, axis=-1, keepdims=True)
    xc = x - mu
    var = jnp.mean(xc * xc, axis=-1, keepdims=True)
    y = xc * lax.rsqrt(var + EPS) * g + b
    return y * _sigmoid(y)


def _conf_prompt_kernel(pa_ref, pb_ref, ha_ref, hb_ref, wdw_ref, bdw_ref, lng_ref, lnb_ref,
                        c_ref, ust_ref, ubuf, cbuf, *, tl, rt, ct):
    t = pl.program_id(1)
    u = pa_ref[...] * _sigmoid(pb_ref[...])
    uh = ha_ref[...] * _sigmoid(hb_ref[...])
    ubuf[0:HALO, :] = jnp.where(t > 0, uh, 0.0)
    ubuf[HALO:, :] = u
    first = HALO - (CONV_W - 1)
    for r0 in range(0, tl, rt):
        for c0 in range(0, D_MODEL, ct):
            acc = jnp.zeros((rt, ct), F32)
            for s in range(8):
                part = None
                for w in range(CONV_W):
                    if (first + w) % 8 != s:
                        continue
                    base = r0 + (first + w) // 8 * 8
                    term = ubuf[base:base + rt + (8 if s else 0), c0:c0 + ct] * wdw_ref[w:w + 1, c0:c0 + ct]
                    part = term if part is None else part + term
                if part is not None:
                    acc = acc + part[s:s + rt, :]
            cbuf[r0:r0 + rt, c0:c0 + ct] = acc + bdw_ref[:, c0:c0 + ct]
    c_ref[...] = _ln_silu(cbuf[...], lng_ref[...], lnb_ref[...]).astype(BF16)

    @pl.when(t == pl.num_programs(1) - 1)
    def _():
        ust_ref[0] = ubuf[tl:tl + HALO, :]


def _conf_prompt(p, B, LP, w_dw, b_dw, ln_g, ln_b):
    tl = 192 if LP % 192 == 0 else CHUNK
    nt = LP // tl
    hb = tl // HALO
    kern = functools.partial(_conf_prompt_kernel, tl=tl, rt=64, ct=128)
    halo_idx = lambda b, t: (jnp.maximum((b * nt + t) * hb - 1, 0), 0)
    halo_idx1 = lambda b, t: (jnp.maximum((b * nt + t) * hb - 1, 0), 1)
    vec = lambda: pl.BlockSpec((1, D_MODEL), lambda b, t: (0, 0))
    return pl.pallas_call(
        kern,
        out_shape=(jax.ShapeDtypeStruct((B * LP, D_MODEL), BF16),
                   jax.ShapeDtypeStruct((B, HALO, D_MODEL), F32)),
        grid=(B, nt),
        in_specs=[pl.BlockSpec((tl, D_MODEL), lambda b, t: (b * nt + t, 0)),
                  pl.BlockSpec((tl, D_MODEL), lambda b, t: (b * nt + t, 1)),
                  pl.BlockSpec((HALO, D_MODEL), halo_idx),
                  pl.BlockSpec((HALO, D_MODEL), halo_idx1),
                  pl.BlockSpec((CONV_W, D_MODEL), lambda b, t: (0, 0)),
                  vec(), vec(), vec()],
        out_specs=(pl.BlockSpec((tl, D_MODEL), lambda b, t: (b * nt + t, 0)),
                   pl.BlockSpec((1, HALO, D_MODEL), lambda b, t: (b, 0, 0))),
        scratch_shapes=[pltpu.VMEM((HALO + tl, D_MODEL), F32), pltpu.VMEM((tl, D_MODEL), F32)],
        compiler_params=_params(("arbitrary", "arbitrary")),
        name="conf_prompt",
    )(p, p, p, p, w_dw, b_dw, ln_g, ln_b)


def _conf_sample_kernel(st_ref, pa_ref, pb_ref, wdw_ref, bdw_ref, lng_ref, lnb_ref,
                        c_ref, nst_ref, xh, *, sb, ls):
    hist = CONV_W - 1
    for s in range(sb):
        u = pa_ref[s] * _sigmoid(pb_ref[s])
        xh[0:hist, :] = st_ref[s]
        xh[hist:hist + ls, :] = u
        acc = jnp.zeros((ls, D_MODEL), F32)
        for w in range(CONV_W):
            acc = acc + xh[w:w + ls, :] * wdw_ref[w:w + 1, :]
        c_ref[s] = _ln_silu(acc + bdw_ref[...], lng_ref[...], lnb_ref[...])
        nst_ref[s] = xh[ls:ls + hist, :]


def _conf_sample(p_s3, state, w_dw, b_dw, ln_g, ln_b):
    NB, ls, _ = p_s3.shape
    hist = CONV_W - 1
    sb = 8 if NB % 8 == 0 else 1
    kern = functools.partial(_conf_sample_kernel, sb=sb, ls=ls)
    vec = lambda: pl.BlockSpec((1, D_MODEL), lambda i: (0, 0))
    return pl.pallas_call(
        kern,
        out_shape=(jax.ShapeDtypeStruct((NB, ls, D_MODEL), F32),
                   jax.ShapeDtypeStruct((NB, hist, D_MODEL), F32)),
        grid=(NB // sb,),
        in_specs=[pl.BlockSpec((sb, hist, D_MODEL), lambda i: (i, 0, 0)),
                  pl.BlockSpec((sb, ls, D_MODEL), lambda i: (i, 0, 0)),
                  pl.BlockSpec((sb, ls, D_MODEL), lambda i: (i, 0, 1)),
                  pl.BlockSpec((CONV_W, D_MODEL), lambda i: (0, 0)),
                  vec(), vec(), vec()],
        out_specs=(pl.BlockSpec((sb, ls, D_MODEL), lambda i: (i, 0, 0)),
                   pl.BlockSpec((sb, hist, D_MODEL), lambda i: (i, 0, 0))),
        scratch_shapes=[pltpu.VMEM((hist + ls + 8, D_MODEL), F32)],
        compiler_params=_params(("arbitrary",)),
        name="conf_sample",
    )(state, p_s3, p_s3, w_dw, b_dw, ln_g, ln_b)


def _split(a):
    hi = a.astype(BF16)
    return hi, (a - hi.astype(F32)).astype(BF16)


def _mm3(a, b):
    ah, al = a
    bh, bl = b
    return _dot(jnp.concatenate([ah, al, ah], axis=1), jnp.concatenate([bh, bh, bl], axis=0))


def _tri_inverse(ms, i, j, C, nil):
    same = lambda n: (i >> (n.bit_length() - 1)) == (j >> (n.bit_length() - 1))
    base = min(16, C)
    eye = (i == j).astype(F32)
    bdot = lambda a, b: _dot(a.astype(BF16), b.astype(BF16))
    dps = [jnp.where(same(base), m, 0.0) for m in ms]
    xs = [eye - d for d in dps]
    for _ in range(max(0, (min(base, nil) - 1).bit_length() - 1)):
        dps = [bdot(d, d) for d in dps]
        xs = [x + bdot(d, x) for d, x in zip(dps, xs)]
    blk = base
    while blk < C:
        sel = jnp.logical_and(same(2 * blk), jnp.logical_not(same(blk)))
        ys = [bdot(jnp.where(sel, m, 0.0), x) for m, x in zip(ms, xs)]
        xs = [x - bdot(x, y) for x, y in zip(xs, ys)]
        blk *= 2
    xsp = [_split(x) for x in xs]
    res = [eye - x - _mm3(_split(m), xp) for m, x, xp in zip(ms, xs, xsp)]
    return [x + _dot(xp[0], r.astype(BF16)) for x, xp, r in zip(xs, xsp, res)]


def _gdn_chunks(seqs, alog_ref, dtb_ref, nw_ref, nil):
    C = seqs[0][0].shape[0]
    G = STACK // C
    ri = lax.broadcasted_iota(jnp.int32, (C, C), 0)
    ci = lax.broadcasted_iota(jnp.int32, (C, C), 1)
    tril = (ri >= ci).astype(BF16)
    i = lax.broadcasted_iota(jnp.int32, (STACK, STACK), 0)
    j = lax.broadcasted_iota(jnp.int32, (STACK, STACK), 1)
    shift = C.bit_length() - 1
    same = (i >> shift) == (j >> shift)
    causal = jnp.logical_and(same, i >= j)
    strict = jnp.logical_and(same, i > j)

    pre = []
    for n, (xq, xk, xv, z, ab, valid, s_ref) in enumerate(seqs):
        ok = valid > 0.5
        xa = ab + dtb_ref[...]
        softplus = jnp.maximum(xa, 0.0) + jnp.log(1.0 + jnp.exp(-jnp.abs(xa)))
        g_all = jnp.where(ok, -jnp.exp(alog_ref[...]) * softplus, 0.0)
        beta_all = jnp.where(ok, _sigmoid(ab), 0.0)
        g1 = g_all.astype(BF16)
        r1 = g_all - g1.astype(F32)
        g2 = r1.astype(BF16)
        g3 = (r1 - g2.astype(F32)).astype(BF16)
        gc_all = _dot(tril, g1) + _dot(tril, g2) + _dot(tril, g3)
        ok_st = jnp.concatenate([valid] * G, axis=0) > 0.5
        for h0 in range(0, DN_HEADS, G):
            heads = list(range(h0, h0 + G))
            stack = lambda x: jnp.concatenate([x[:, h * DN_DK:(h + 1) * DN_DK] for h in heads], axis=0)
            col = lambda a, off: jnp.concatenate([a[:, off + h:off + h + 1] for h in heads], axis=0)
            q = stack(xq)
            k = stack(xk)
            q = jnp.where(ok_st, q * lax.rsqrt(jnp.sum(q * q, axis=-1, keepdims=True) + EPS) * (DN_DK ** -0.5), 0.0)
            k = jnp.where(ok_st, k * lax.rsqrt(jnp.sum(k * k, axis=-1, keepdims=True) + EPS), 0.0)
            v = jnp.where(ok_st, stack(xv), 0.0)
            gc = col(gc_all, 0)
            beta = col(beta_all, DN_HEADS)
            g_last = jnp.concatenate([jnp.broadcast_to(gc_all[C - 1:C, h:h + 1], (C, 1)) for h in heads], axis=0)
            gb = jnp.broadcast_to(gc, (STACK, STACK))
            decay = jnp.where(causal, jnp.exp(jnp.where(causal, gb - gb.T, 0.0)), 0.0)
            egc = jnp.exp(gc)
            kb = k * beta
            pre.append(dict(n=n, heads=heads, s_ref=s_ref, q=q, kb=kb, kbf=k.astype(BF16), decay=decay, egc=egc,
                            rhs=jnp.concatenate([v * beta, kb * egc], axis=1),
                            k_dec=(k * jnp.exp(g_last - gc)).astype(BF16), zs=stack(z),
                            s_decay=[jnp.exp(gc_all[C - 1:C, h:h + 1]) for h in heads]))
    ms = [jnp.where(strict, _dot_nt(p["kb"].astype(BF16), p["kbf"]) * p["decay"], 0.0) for p in pre]
    qks = [jnp.where(causal, _dot_nt(p["q"].astype(BF16), p["kbf"]) * p["decay"], 0.0).astype(BF16) for p in pre]
    invs = _tri_inverse(ms, i, j, C, nil)
    sols = [_mm3(_split(inv), _split(p["rhs"])) for inv, p in zip(invs, pre)]
    wss = []
    for p, sol in zip(pre, sols):
        w = sol[:, DN_DV:].astype(BF16)
        q_dec = (p["q"] * p["egc"]).astype(BF16)
        wss.append([_dot(jnp.concatenate([w[g * C:(g + 1) * C], q_dec[g * C:(g + 1) * C]], axis=0),
                         p["s_ref"][h].astype(BF16)) for g, h in enumerate(p["heads"])])
    outs = [[None] * DN_HEADS for _ in seqs]
    for p, sol, ws, qk in zip(pre, sols, wss, qks):
        s_ref = p["s_ref"]
        v_new = [(sol[g * C:(g + 1) * C, :DN_DV] - ws[g][:C]).astype(BF16) for g in range(G)]
        for g, h in enumerate(p["heads"]):
            s_ref[h] = s_ref[h] * p["s_decay"][g] + _dot_tn(p["k_dec"][g * C:(g + 1) * C], v_new[g])
        o = jnp.concatenate([w[C:] for w in ws], axis=0) + _dot(qk, jnp.concatenate(v_new, axis=0))
        o = o * lax.rsqrt(jnp.mean(o * o, axis=-1, keepdims=True) + EPS) * nw_ref[...]
        og = o * (p["zs"] * _sigmoid(p["zs"]))
        for g, h in enumerate(p["heads"]):
            outs[p["n"]][h] = og[g * C:(g + 1) * C]
    return outs


def _short_conv_silu(xbuf, wc_ref, rows):
    first = 8 - (SHORT_W - 1)
    acc = xbuf[first:first + rows, :] * wc_ref[0:1, :]
    for w in range(1, SHORT_W):
        acc = acc + xbuf[first + w:first + w + rows, :] * wc_ref[w:w + 1, :]
    return acc * _sigmoid(acc)


def _gdn_prompt_kernel(*refs, ns):
    seq_refs = [refs[5 * s:5 * s + 5] for s in range(ns)]
    wc_ref, alog_ref, dtb_ref, nw_ref, o_ref, s_ref, xbuf = refs[5 * ns:]
    c = pl.program_id(1)
    n_qk = DN_HEADS * DN_DK

    @pl.when(c == 0)
    def _():
        s_ref[...] = jnp.zeros_like(s_ref)
        xbuf[:, 0:8, :] = jnp.zeros((ns, 8, xbuf.shape[2]), F32)

    rows = lax.broadcasted_iota(jnp.int32, (CHUNK, 1), 0)
    valid = jnp.logical_or(rows >= FRONT, c > 0).astype(F32)
    seqs = []
    for s, (q_ref, k_ref, v_ref, z_ref, ab_ref) in enumerate(seq_refs):
        xb = xbuf.at[s]
        xb[8:8 + CHUNK, 0:n_qk] = q_ref[...]
        xb[8:8 + CHUNK, n_qk:2 * n_qk] = k_ref[...]
        xb[8:8 + CHUNK, 2 * n_qk:] = v_ref[...]
        x = _short_conv_silu(xb, wc_ref, CHUNK)
        xb[0:8, :] = xb[CHUNK:CHUNK + 8, :]
        seqs.append((x[:, 0:n_qk], x[:, n_qk:2 * n_qk], x[:, 2 * n_qk:], z_ref[...].astype(F32), ab_ref[...],
                     valid, s_ref.at[s]))
    outs = _gdn_chunks(seqs, alog_ref, dtb_ref, nw_ref, nil=CHUNK)
    for s in range(ns):
        for h in range(DN_HEADS):
            o_ref[s, :, h * DN_DV:(h + 1) * DN_DV] = outs[s][h].astype(BF16)


def _gdn_prompt(p, pg, ab, B, LP, w_conv, alog, dtb, nw):
    nc = LP // CHUNK
    n_qk = DN_HEADS * DN_DK
    ns = 4 if B % 4 == 0 else (2 if B % 2 == 0 else 1)
    vec = lambda n: pl.BlockSpec((1, n), lambda g, c: (0, 0))
    in_specs, args = [], []
    for s in range(ns):
        for src, col in ((p, 2), (p, 3), (p, 4), (pg, 0)):
            in_specs.append(pl.BlockSpec((CHUNK, n_qk), lambda g, c, s=s, col=col: ((g * ns + s) * nc + c, col)))
            args.append(src)
        in_specs.append(pl.BlockSpec((CHUNK, LANES), lambda g, c, s=s: ((g * ns + s) * nc + c, 0)))
        args.append(ab)
    in_specs += [pl.BlockSpec((SHORT_W, 3 * n_qk), lambda g, c: (0, 0)), vec(LANES), vec(LANES), vec(DN_DV)]
    og, s_out = pl.pallas_call(
        functools.partial(_gdn_prompt_kernel, ns=ns),
        out_shape=(jax.ShapeDtypeStruct((B, LP, n_qk), BF16),
                   jax.ShapeDtypeStruct((B, DN_HEADS, DN_DK, DN_DV), F32)),
        grid=(B // ns, nc),
        in_specs=in_specs,
        out_specs=(pl.BlockSpec((ns, CHUNK, n_qk), lambda g, c: (g, c, 0)),
                   pl.BlockSpec((ns, DN_HEADS, DN_DK, DN_DV), lambda g, c: (g, 0, 0, 0))),
        scratch_shapes=[pltpu.VMEM((ns, CHUNK + 8, 3 * n_qk), F32)],
        compiler_params=_params(("arbitrary", "arbitrary")),
        name="gdn_prompt",
    )(*args, w_conv, alog, dtb, nw)
    return og.reshape(B * LP, n_qk), s_out


def _gdn_sample_kernel(st_ref, q_ref, k_ref, v_ref, z_ref, ab_ref, s0_ref, wc_ref, alog_ref, dtb_ref, nw_ref,
                       o_ref, s_ref, xbuf, zbuf, abbuf, *, sb, ls):
    n_qk = DN_HEADS * DN_DK
    C = SAMPLE_CHUNK
    hist = SHORT_W - 1
    xbuf[...] = jnp.zeros_like(xbuf)
    zbuf[...] = jnp.zeros_like(zbuf)
    abbuf[...] = jnp.zeros_like(abbuf)
    s_ref[...] = s0_ref[...]
    valid = (lax.broadcasted_iota(jnp.int32, (C, 1), 0) < ls).astype(F32)
    seqs = []
    for s in range(sb):
        xb = xbuf.at[s]
        xb[8 - hist:8, :] = st_ref[s]
        xb[8:8 + ls, 0:n_qk] = q_ref[s]
        xb[8:8 + ls, n_qk:2 * n_qk] = k_ref[s]
        xb[8:8 + ls, 2 * n_qk:] = v_ref[s]
        zbuf[s, 0:ls, :] = z_ref[s]
        abbuf[s, 0:ls, :] = ab_ref[s]
        x = _short_conv_silu(xb, wc_ref, C)
        seqs.append((x[:, 0:n_qk], x[:, n_qk:2 * n_qk], x[:, 2 * n_qk:], zbuf[s], abbuf[s], valid, s_ref.at[s]))
    outs = _gdn_chunks(seqs, alog_ref, dtb_ref, nw_ref, nil=ls)
    for s in range(sb):
        for h in range(DN_HEADS):
            o_ref[s, :, h * DN_DV:(h + 1) * DN_DV] = outs[s][h][0:ls, :]


def _gdn_sample(p_s3, z_s3, ab_s3, st_conv, s0, w_conv, alog, dtb, nw):
    NB, ls, _ = p_s3.shape
    n_qk = DN_HEADS * DN_DK
    hist = SHORT_W - 1
    assert ls <= SAMPLE_CHUNK
    sb = 4 if NB % 4 == 0 else 1
    kern = functools.partial(_gdn_sample_kernel, sb=sb, ls=ls)
    blk = lambda col: pl.BlockSpec((sb, ls, n_qk), lambda i: (i, 0, col))
    vec = lambda n: pl.BlockSpec((1, n), lambda i: (0, 0))
    sspec = lambda: pl.BlockSpec((sb, DN_HEADS, DN_DK, DN_DV), lambda i: (i, 0, 0, 0))
    return pl.pallas_call(
        kern,
        out_shape=(jax.ShapeDtypeStruct((NB, ls, n_qk), F32),
                   jax.ShapeDtypeStruct((NB, DN_HEADS, DN_DK, DN_DV), F32)),
        grid=(NB // sb,),
        in_specs=[pl.BlockSpec((sb, hist, 3 * n_qk), lambda i: (i, 0, 0)),
                  blk(2), blk(3), blk(4), blk(0),
                  pl.BlockSpec((sb, ls, LANES), lambda i: (i, 0, 0)),
                  sspec(),
                  pl.BlockSpec((SHORT_W, 3 * n_qk), lambda i: (0, 0)),
                  vec(LANES), vec(LANES), vec(DN_DV)],
        out_specs=(pl.BlockSpec((sb, ls, n_qk), lambda i: (i, 0, 0)), sspec()),
        scratch_shapes=[pltpu.VMEM((sb, SAMPLE_CHUNK + 8, 3 * n_qk), F32),
                        pltpu.VMEM((sb, SAMPLE_CHUNK, n_qk), F32),
                        pltpu.VMEM((sb, SAMPLE_CHUNK, LANES), F32)],
        compiler_params=_params(("arbitrary",)),
        name="gdn_sample",
    )(st_conv, p_s3, p_s3, p_s3, z_s3, ab_s3, s0, w_conv, alog, dtb, nw)


def _pack_bf16_pairs(x):
    half = x.shape[1] // 2
    lo = lax.bitcast_convert_type(x[:, :half].astype(BF16).astype(F32), jnp.uint32)
    hi = lax.bitcast_convert_type(x[:, half:].astype(BF16).astype(F32), jnp.uint32)
    return jnp.bitwise_or(jnp.bitwise_and(hi, jnp.uint32(0xFFFF0000)), lax.shift_right_logical(lo, jnp.uint32(16)))


def _unpack_bf16_pairs(xp):
    lo = lax.bitcast_convert_type(lax.shift_left(xp, jnp.uint32(16)), F32)
    hi = lax.bitcast_convert_type(jnp.bitwise_and(xp, jnp.uint32(0xFFFF0000)), F32)
    return jnp.concatenate([lo, hi], axis=1).astype(BF16)


def _merge_kernel(cp_ref, cs_ref, ogp_ref, ogs_ref, ga_ref, gb_ref, h_ref, wco_ref, bco_ref, wdo_ref, wo_ref, nf_ref,
                  wr_ref, br_ref, h1_ref, xp_ref, ei_ref, gt_ref, rk_ref, cnt_ref, carry, *, n_p):
    step = pl.program_id(0)

    @pl.when(step == 0)
    def _():
        carry[...] = jnp.zeros_like(carry)

    in_prompt = step < n_p
    c = jnp.where(in_prompt, cp_ref[...], cs_ref[...])
    og = jnp.where(in_prompt, ogp_ref[...], ogs_ref[...])
    ya = _dot(c, wco_ref[...]) + bco_ref[...]
    yb = _dot(og, wdo_ref[...])
    mixed = _sigmoid(ga_ref[...].astype(F32)) * ya + _sigmoid(gb_ref[...].astype(F32)) * yb
    h1 = h_ref[...] + _dot(mixed.astype(BF16), wo_ref[...])
    h1_ref[...] = h1
    xn = h1 * lax.rsqrt(jnp.mean(h1 * h1, axis=-1, keepdims=True) + EPS) * nf_ref[...]
    xp_ref[...] = _pack_bf16_pairs(xn)
    logits = _dot(xn.astype(BF16), wr_ref[...]) + br_ref[...]

    tm = logits.shape[0]
    lane = lax.broadcasted_iota(jnp.int32, (tm, LANES), 1)
    work = logits
    sels, vals = [], []
    for _ in range(TOP_K):
        m = jnp.max(work, axis=-1, keepdims=True)
        idx = jnp.min(jnp.where(work == m, lane, N_EXPERTS - 1), axis=-1, keepdims=True)
        sel = lane == idx
        sels.append(sel)
        vals.append(m)
        work = jnp.where(sel, -jnp.inf, work)
    exps = [jnp.exp(v - vals[0]) for v in vals]
    denom = exps[0]
    for e in exps[1:]:
        denom = denom + e
    onehot = jnp.zeros((tm, LANES), F32)
    for sel in sels:
        onehot = onehot + sel.astype(F32)
    ri = lax.broadcasted_iota(jnp.int32, (tm, tm), 0)
    ci = lax.broadcasted_iota(jnp.int32, (tm, tm), 1)
    before = _dot((ri > ci).astype(BF16), onehot.astype(BF16)) + carry[...]
    ei = jnp.zeros((tm, LANES), jnp.int32)
    gt = jnp.zeros((tm, LANES), F32)
    rk = jnp.zeros((tm, LANES), jnp.int32)
    for k in range(TOP_K):
        at_k = lane == k
        e_k = jnp.max(jnp.where(sels[k], lane, 0), axis=-1, keepdims=True)
        r_k = jnp.sum(jnp.where(sels[k], before, 0.0), axis=-1, keepdims=True).astype(jnp.int32)
        ei = jnp.where(at_k, e_k, ei)
        gt = jnp.where(at_k, exps[k] / denom, gt)
        rk = jnp.where(at_k, r_k, rk)
    ei_ref[...] = ei
    gt_ref[...] = gt
    rk_ref[...] = rk
    carry[...] = carry[...] + jnp.sum(onehot, axis=0, keepdims=True)
    cnt_ref[...] = carry[...]


def _merge(c_p, c_s, og_p, og_s, pg, h, wco, bco, wdo, wo, nf, wr, br):
    T = h.shape[0]
    tm = _row_tile(math.gcd(c_p.shape[0], c_s.shape[0]), 512)
    n_p = c_p.shape[0] // tm
    row = lambda col: pl.BlockSpec((tm, D_MODEL), lambda i: (i, col))
    part_p = lambda: pl.BlockSpec((tm, D_MODEL), lambda i: (jnp.minimum(i, n_p - 1), 0))
    part_s = lambda: pl.BlockSpec((tm, D_MODEL), lambda i: (jnp.maximum(i - n_p, 0), 0))
    full = lambda a, b: pl.BlockSpec((a, b), lambda i: (0, 0))
    lanes = lambda: pl.BlockSpec((tm, LANES), lambda i: (i, 0))
    return pl.pallas_call(
        functools.partial(_merge_kernel, n_p=n_p),
        out_shape=(jax.ShapeDtypeStruct((T, D_MODEL), F32),
                   jax.ShapeDtypeStruct((T, D_MODEL // 2), jnp.uint32),
                   jax.ShapeDtypeStruct((T, LANES), jnp.int32),
                   jax.ShapeDtypeStruct((T, LANES), F32),
                   jax.ShapeDtypeStruct((T, LANES), jnp.int32),
                   jax.ShapeDtypeStruct((1, LANES), F32)),
        grid=(T // tm,),
        in_specs=[part_p(), part_s(), part_p(), part_s(), row(1), row(2), row(0),
                  full(D_MODEL, D_MODEL), full(1, D_MODEL), full(D_MODEL, D_MODEL), full(D_MODEL, D_MODEL),
                  full(1, D_MODEL), full(D_MODEL, LANES), full(1, LANES)],
        out_specs=(row(0), pl.BlockSpec((tm, D_MODEL // 2), lambda i: (i, 0)), lanes(), lanes(), lanes(),
                   full(1, LANES)),
        scratch_shapes=[pltpu.VMEM((1, LANES), F32)],
        compiler_params=_params(("arbitrary",)),
        name="merge",
    )(c_p, c_s, og_p, og_s, pg, pg, h, wco, bco, wdo, wo, nf, wr, br)


ISSUE_UNROLL = 8


def _row_copy(src, src_row, dst, dst_row, sem):
    return pltpu.make_async_copy(src.at[pl.ds(src_row, 1), :], dst.at[pl.ds(dst_row, 1), :], sem)


def _dispatch_kernel(dest_ref, x_ref, xb_out, sem, *, tm):
    def issue(t, carry):
        for k in range(TOP_K):
            _row_copy(x_ref, t, xb_out, dest_ref[t * TOP_K + k], sem).start(priority=k % 2)
        return carry

    lax.fori_loop(0, tm, issue, 0, unroll=ISSUE_UNROLL)
    for k in range(TOP_K):
        pltpu.make_async_copy(x_ref, xb_out.at[pl.ds(0, tm), :], sem).wait()


def _dispatch(dest, xp, n_rows):
    T, W = xp.shape
    tm = _row_tile(T, 256)
    return pl.pallas_call(
        functools.partial(_dispatch_kernel, tm=tm),
        out_shape=jax.ShapeDtypeStruct((n_rows, W), xp.dtype),
        grid=(T // tm,),
        in_specs=[pl.BlockSpec((tm * TOP_K,), lambda i: (i,), memory_space=pltpu.SMEM),
                  pl.BlockSpec((tm, W), lambda i: (i, 0))],
        out_specs=pl.BlockSpec(memory_space=pl.ANY),
        scratch_shapes=[pltpu.SemaphoreType.DMA],
        compiler_params=_params(("arbitrary",)),
        name="moe_dispatch",
    )(dest, xp)


def _moe_kernel(be_ref, nv_ref, first_ref, slot_ref, next_ref, x_ref, wu_hbm, bu_ref, wd_hbm, bd_ref, o_ref,
                wu32, wd32, wub, wdb, sems):
    i = pl.program_id(0)

    def fetch(e, s):
        pltpu.make_async_copy(wu_hbm.at[e], wu32.at[s], sems.at[0, s]).start()
        pltpu.make_async_copy(wd_hbm.at[e], wd32.at[s], sems.at[1, s]).start()

    @pl.when(i == 0)
    def _():
        fetch(be_ref[0], 0)

    @pl.when(first_ref[i] == 1)
    def _():
        s = slot_ref[i]
        pltpu.make_async_copy(wu_hbm.at[0], wu32.at[s], sems.at[0, s]).wait()
        pltpu.make_async_copy(wd_hbm.at[0], wd32.at[s], sems.at[1, s]).wait()
        wub[...] = wu32[s].astype(BF16)
        wdb[...] = wd32[s].astype(BF16)

        @pl.when(next_ref[i] >= 0)
        def _():
            fetch(next_ref[i], 1 - s)

    @pl.when(nv_ref[i] > 0)
    def _():
        rows = lax.broadcasted_iota(jnp.int32, (MOE_ROWS, 1), 0)
        xp = jnp.where(rows < nv_ref[i], x_ref[...], jnp.uint32(0))
        hmid = _dot(_unpack_bf16_pairs(xp), wub[...]) + bu_ref[0]
        hg = jnp.minimum(hmid[:, :D_FF], SWIGLU_LIMIT)
        hl = jnp.clip(hmid[:, D_FF:], -SWIGLU_LIMIT, SWIGLU_LIMIT)
        act = hg * _sigmoid(SWIGLU_ALPHA * hg) * (hl + 1.0)
        o_ref[...] = _dot(act.astype(BF16), wdb[...]) + bd_ref[0]

    @pl.when(nv_ref[i] == 0)
    def _():
        o_ref[...] = jnp.zeros_like(o_ref)


def _moe(plan, xb, w_up, b_up, w_down, b_down):
    R = xb.shape[0]
    nb = R // MOE_ROWS
    bias = lambda n: pl.BlockSpec((1, 1, n), lambda i, be, *_: (be[i], 0, 0))
    grid_spec = pltpu.PrefetchScalarGridSpec(
        num_scalar_prefetch=len(plan),
        grid=(nb,),
        in_specs=[pl.BlockSpec((MOE_ROWS, D_MODEL // 2), lambda i, *_: (i, 0)),
                  pl.BlockSpec(memory_space=pl.ANY), bias(2 * D_FF),
                  pl.BlockSpec(memory_space=pl.ANY), bias(D_MODEL)],
        out_specs=pl.BlockSpec((MOE_ROWS, D_MODEL), lambda i, *_: (i, 0)),
        scratch_shapes=[pltpu.VMEM((2, D_MODEL, 2 * D_FF), F32), pltpu.VMEM((2, D_FF, D_MODEL), F32),
                        pltpu.VMEM((D_MODEL, 2 * D_FF), BF16), pltpu.VMEM((D_FF, D_MODEL), BF16),
                        pltpu.SemaphoreType.DMA((2, 2))],
    )
    return pl.pallas_call(
        _moe_kernel,
        out_shape=jax.ShapeDtypeStruct((R, D_MODEL), F32),
        grid_spec=grid_spec,
        compiler_params=pltpu.CompilerParams(dimension_semantics=("arbitrary",), vmem_limit_bytes=MOE_VMEM_LIMIT),
        name="moe_experts",
    )(*plan, xb, w_up, b_up, w_down, b_down)


def _dispatch_plan(counts, ei, rk, T):
    A = T * TOP_K
    n_blocks = -(-A // MOE_ROWS) + N_EXPERTS
    counts = counts[0, :N_EXPERTS].astype(jnp.int32)
    padded = (counts + MOE_ROWS - 1) // MOE_ROWS * MOE_ROWS
    pend = jnp.cumsum(padded)
    pstart = pend - padded
    experts = jnp.arange(N_EXPERTS, dtype=jnp.int32)
    seg_start = jnp.sum(jnp.where(ei[:, :TOP_K, None] == experts, pstart, 0), axis=-1)
    dest = (seg_start + rk[:, :TOP_K]).reshape(-1).astype(jnp.int32)
    starts = jnp.arange(n_blocks, dtype=jnp.int32) * MOE_ROWS
    block_e = jnp.minimum(jnp.sum((pend[None, :] <= starts[:, None]).astype(jnp.int32), axis=1), N_EXPERTS - 1)
    seg_end = jnp.sum(jnp.where(block_e[:, None] == experts, pstart + counts, 0), axis=-1)
    rows_valid = jnp.clip(seg_end - starts, 0, MOE_ROWS).astype(jnp.int32)
    prev_e = jnp.concatenate([jnp.full((1,), -1, jnp.int32), block_e[:-1]])
    first = jnp.logical_and(block_e != prev_e, rows_valid > 0).astype(jnp.int32)
    slot = (jnp.cumsum(first) - 1) % 2
    later = jnp.logical_and(experts[None, :] > experts[:, None], (counts > 0)[None, :])
    next_of = jnp.min(jnp.where(later, experts[None, :], N_EXPERTS), axis=1)
    next_of = jnp.where(next_of == N_EXPERTS, -1, next_of)
    next_e = jnp.sum(jnp.where(block_e[:, None] == experts, next_of, 0), axis=-1)
    plan = tuple(a.astype(jnp.int32) for a in (block_e, rows_valid, first, slot, next_e))
    return dest, plan, n_blocks * MOE_ROWS


def _combine_kernel(dest_ref, dnext_ref, gt_ref, h_ref, nw_ref, yb_ref, o_ref, buf, sems, *, tm):
    i = pl.program_id(0)
    slot = lax.rem(i, 2)

    def request(d_ref, s):
        def issue(t, carry):
            for k in range(TOP_K):
                _row_copy(yb_ref, d_ref[t * TOP_K + k], buf.at[s, k], t, sems.at[s]).start(priority=k % 2)
            return carry

        lax.fori_loop(0, tm, issue, 0, unroll=ISSUE_UNROLL)

    @pl.when(i == 0)
    def _():
        request(dest_ref, 0)

    @pl.when(i + 1 < pl.num_programs(0))
    def _():
        request(dnext_ref, 1 - slot)

    for k in range(TOP_K):
        pltpu.make_async_copy(yb_ref.at[pl.ds(0, tm), :], buf.at[slot, k], sems.at[slot]).wait()
    gt = gt_ref[...]
    x = h_ref[...]
    for k in range(TOP_K):
        x = x + gt[:, k:k + 1] * buf[slot, k]
    o_ref[...] = x * lax.rsqrt(jnp.mean(x * x, axis=-1, keepdims=True) + EPS) * nw_ref[...]


def _combine(dest, gt, h1, nw, yb):
    T = h1.shape[0]
    tm = _row_tile(T, 256)
    n = T // tm
    row = lambda: pl.BlockSpec((tm, D_MODEL), lambda i: (i, 0))
    return pl.pallas_call(
        functools.partial(_combine_kernel, tm=tm),
        out_shape=jax.ShapeDtypeStruct((T, D_MODEL), F32),
        grid=(T // tm,),
        in_specs=[pl.BlockSpec((tm * TOP_K,), lambda i: (i,), memory_space=pltpu.SMEM),
                  pl.BlockSpec((tm * TOP_K,), lambda i: (jnp.minimum(i + 1, n - 1),), memory_space=pltpu.SMEM),
                  pl.BlockSpec((tm, LANES), lambda i: (i, 0)),
                  row(),
                  pl.BlockSpec((1, D_MODEL), lambda i: (0, 0)),
                  pl.BlockSpec(memory_space=pl.ANY)],
        out_specs=row(),
        scratch_shapes=[pltpu.VMEM((2, TOP_K, tm, D_MODEL), F32), pltpu.SemaphoreType.DMA((2,))],
        compiler_params=_params(("arbitrary",)),
        name="moe_combine_final",
    )(dest, dest, gt, h1, nw, yb)


def _pad_lanes(v, fill=0.0):
    v = v.reshape(1, -1).astype(F32)
    return jnp.pad(v, ((0, 0), (0, LANES - v.shape[1])), constant_values=fill)


def kernel(x_prompt, x_sample, state_conf_conv, state_dn_conv, state_dn_S, meta_tokens, norm_mix, w_in, w_conf_dw, b_conf_dw, ln_conf_g, ln_conf_b, w_conf_out, b_conf_out, w_dn_conv, dn_a_log, dn_dt_bias, dn_norm_w, w_dn_out, w_out, norm_ffn, w_router, b_router, w_up, b_up, w_down, b_down, norm_final):
    B, SEQ, D = x_prompt.shape
    NB, LS, _ = x_sample.shape
    depth = w_in.shape[0]
    assert D == D_MODEL and depth == 1 and SEQ % CHUNK == 0 and LS >= SHORT_W - 1
    LP = FRONT + N_META + SEQ
    TP = B * LP
    T = TP + NB * LS
    n_qk = DN_HEADS * DN_DK
    o_q = 2 * D_MODEL
    o_a = o_q + 4 * n_qk
    o_gate = o_a + 2 * DN_HEADS

    h0 = _tokens(x_prompt, x_sample.reshape(NB * LS, D), meta_tokens.astype(F32))

    w_in0 = w_in[0]
    o_z = o_q + 3 * n_qk
    w_gate = jnp.concatenate([w_in0[:, o_z:o_a], w_in0[:, o_gate:]], axis=1)
    w_ab = jnp.pad(w_in0[:, o_a:o_gate], ((0, 0), (0, LANES - 2 * DN_HEADS))).astype(BF16)
    wco = w_conf_out[0].astype(BF16)
    wdo = w_dn_out[0].astype(BF16)
    wo = w_out[0].astype(BF16)
    wr = jnp.pad(w_router[0], ((0, 0), (0, LANES - N_EXPERTS))).astype(BF16)
    br = _pad_lanes(b_router[0], fill=-1e30)
    alog = _pad_lanes(dn_a_log[0])
    dtb = _pad_lanes(dn_dt_bias[0])
    row = lambda v: v.reshape(1, -1).astype(F32)

    xn, ab = _rms_ab(h0, row(norm_mix[0]), w_ab)
    n_main = o_z // D_MODEL
    p = _mm_in(xn, w_in0, n_main, F32)
    pg = _mm_in(xn, w_gate, 3, BF16)
    p_s3 = p[TP:].reshape(NB, LS, n_main * D_MODEL)
    z_s3 = pg[TP:, :n_qk].astype(F32).reshape(NB, LS, n_qk)
    ab_s3 = ab[TP:].reshape(NB, LS, LANES)

    c_p, ust_p = _conf_prompt(p, B, LP, w_conf_dw[0], row(b_conf_dw[0]), row(ln_conf_g[0]), row(ln_conf_b[0]))
    c_s, conf_state_s = _conf_sample(p_s3, state_conf_conv[0], w_conf_dw[0], row(b_conf_dw[0]),
                                     row(ln_conf_g[0]), row(ln_conf_b[0]))

    og_p, s_p = _gdn_prompt(p, pg, ab, B, LP, w_dn_conv[0], alog, dtb, row(dn_norm_w[0]))
    og_s, s_s = _gdn_sample(p_s3, z_s3, ab_s3, state_dn_conv[0], state_dn_S[0], w_dn_conv[0], alog, dtb,
                            row(dn_norm_w[0]))

    h1, xp2, ei, gt, rk, counts = _merge(c_p, c_s.reshape(NB * LS, D).astype(BF16),
                                         og_p, og_s.reshape(NB * LS, n_qk).astype(BF16), pg, h0,
                                         wco, row(b_conf_out[0]), wdo, wo, row(norm_ffn[0]), wr, br)

    dest, plan, R = _dispatch_plan(counts, ei, rk, T)
    xb = _dispatch(dest, xp2, R)
    yb = _moe(plan, xb, w_up[0], b_up[0].reshape(N_EXPERTS, 1, -1), w_down[0], b_down[0].reshape(N_EXPERTS, 1, -1))
    y = _combine(dest, gt, h1, row(norm_final), yb)

    y_prompt = jnp.stack([y[b * LP + FRONT + N_META:(b + 1) * LP] for b in range(B)])
    y_sample = y[TP:].reshape(NB, LS, D)
    hist = CONV_W - 1
    conf_conv_prompt = ust_p[:, HALO - hist:][None]
    dn_conv_prompt = jnp.stack([p[(b + 1) * LP - (SHORT_W - 1):(b + 1) * LP, o_q:o_q + 3 * n_qk]
                                for b in range(B)])[None]
    dn_conv_sample = p_s3[:, LS - (SHORT_W - 1):, o_q:o_q + 3 * n_qk][None]
    return (y_prompt, y_sample, conf_conv_prompt, dn_conv_prompt, s_p[None],
            conf_state_s[None], dn_conv_sample, s_s[None])
```

```python
import functools
import math

import jax
import jax.numpy as jnp
from jax import lax
from jax.experimental import pallas as pl
from jax.experimental.pallas import tpu as pltpu

D_MODEL = 1024
N_META = 16
CONV_W = 31
SHORT_W = 4
DN_HEADS = 8
DN_DK = 128
DN_DV = 128
CHUNK = 64
N_EXPERTS = 32
TOP_K = 4
D_FF = 1024
SWIGLU_LIMIT = 7.0
SWIGLU_ALPHA = 1.702
EPS = 1e-6

FRONT = (-N_META) % CHUNK
SAMPLE_CHUNK = 16
STACK = 128
LANES = 128
HALO = 32
MOE_ROWS = 512
VMEM_LIMIT = 48 * 1024 * 1024
MOE_VMEM_LIMIT = 58 * 1024 * 1024

F32 = jnp.float32
BF16 = jnp.bfloat16


def _row_tile(n, pref):
    best = 16
    for t in range(16, min(n, pref) + 1, 16):
        if n % t == 0:
            best = t
    assert n % best == 0
    return best


def _sigmoid(x):
    return 1.0 / (1.0 + jnp.exp(-x))


def _dot(a, b):
    return jnp.dot(a, b, preferred_element_type=F32)


def _dot_nt(a, b):
    return lax.dot_general(a, b, (((1,), (1,)), ((), ())), preferred_element_type=F32)


def _dot_tn(a, b):
    return lax.dot_general(a, b, (((0,), (0,)), ((), ())), preferred_element_type=F32)


def _params(sem):
    return pltpu.CompilerParams(dimension_semantics=sem, vmem_limit_bytes=VMEM_LIMIT)


def _tokens_kernel(xp_ref, xs_ref, meta_ref, h_ref, head, sems, *, lp, tr, n_prompt_rows):
    b = pl.program_id(0)
    j = pl.program_id(1)
    n_head = FRONT + N_META
    row0 = pl.multiple_of(b * lp, 8)
    body = pltpu.make_async_copy(xp_ref.at[0], h_ref.at[pl.ds(row0 + n_head + j * tr, tr), :], sems.at[0])
    body.start()

    @pl.when(j == 0)
    def _():
        head[0:FRONT, :] = jnp.zeros((FRONT, D_MODEL), F32)
        head[FRONT:, :] = meta_ref[...]
        front = pltpu.make_async_copy(head, h_ref.at[pl.ds(row0, n_head), :], sems.at[1])
        front.start()
        front.wait()

    @pl.when(jnp.logical_and(b == 0, j == 0))
    def _():
        tail = pltpu.make_async_copy(xs_ref, h_ref.at[pl.ds(n_prompt_rows, xs_ref.shape[0]), :], sems.at[2])
        tail.start()
        tail.wait()

    body.wait()


def _tokens(x_prompt, x_sample2, meta):
    B, SEQ, D = x_prompt.shape
    lp = FRONT + N_META + SEQ
    T = B * lp + x_sample2.shape[0]
    tr = _row_tile(SEQ, 1024)
    return pl.pallas_call(
        functools.partial(_tokens_kernel, lp=lp, tr=tr, n_prompt_rows=B * lp),
        out_shape=jax.ShapeDtypeStruct((T, D), F32),
        grid=(B, SEQ // tr),
        in_specs=[pl.BlockSpec((1, tr, D), lambda b, j: (b, j, 0)),
                  pl.BlockSpec(x_sample2.shape, lambda b, j: (0, 0)),
                  pl.BlockSpec((N_META, D), lambda b, j: (0, 0))],
        out_specs=pl.BlockSpec(memory_space=pl.ANY),
        scratch_shapes=[pltpu.VMEM((FRONT + N_META, D), F32), pltpu.SemaphoreType.DMA((3,))],
        compiler_params=_params(("arbitrary", "arbitrary")),
        name="token_layout",
    )(x_prompt, x_sample2, meta)


def _rms_ab_kernel(h_ref, nw_ref, wab_ref, xn_ref, ab_ref):
    x = h_ref[...]
    y = x * lax.rsqrt(jnp.mean(x * x, axis=-1, keepdims=True) + EPS) * nw_ref[...]
    yb = y.astype(BF16)
    xn_ref[...] = yb
    ab_ref[...] = _dot(yb, wab_ref[...])


def _rms_ab(h, norm_w, w_ab):
    T = h.shape[0]
    tm = _row_tile(T, 1024)
    return pl.pallas_call(
        _rms_ab_kernel,
        out_shape=(jax.ShapeDtypeStruct((T, D_MODEL), BF16), jax.ShapeDtypeStruct((T, LANES), F32)),
        grid=(T // tm,),
        in_specs=[pl.BlockSpec((tm, D_MODEL), lambda i: (i, 0)),
                  pl.BlockSpec((1, D_MODEL), lambda i: (0, 0)),
                  pl.BlockSpec((D_MODEL, LANES), lambda i: (0, 0))],
        out_specs=(pl.BlockSpec((tm, D_MODEL), lambda i: (i, 0)),
                   pl.BlockSpec((tm, LANES), lambda i: (i, 0))),
        compiler_params=_params(("arbitrary",)),
        name="rms_ab",
    )(h, norm_w, w_ab)


def _mm_in_kernel(x_ref, w_ref, o_ref, wb_ref):
    @pl.when(pl.program_id(1) == 0)
    def _():
        wb_ref[...] = w_ref[...].astype(BF16)

    o_ref[...] = _dot(x_ref[...], wb_ref[...]).astype(o_ref.dtype)


def _mm_in(xn, w, n_tiles, out_dtype):
    T, K = xn.shape
    tm = _row_tile(T, 2304)
    tn = 1024
    N = n_tiles * tn
    assert N <= w.shape[1]
    return pl.pallas_call(
        _mm_in_kernel,
        out_shape=jax.ShapeDtypeStruct((T, N), out_dtype),
        grid=(N // tn, T // tm),
        in_specs=[pl.BlockSpec((tm, K), lambda j, i: (i, 0)),
                  pl.BlockSpec((K, tn), lambda j, i: (0, j))],
        out_specs=pl.BlockSpec((tm, tn), lambda j, i: (i, j)),
        scratch_shapes=[pltpu.VMEM((K, tn), BF16)],
        compiler_params=_params(("arbitrary", "arbitrary")),
        name="in_proj",
    )(xn, w)


def _ln_silu(x, g, b):
    mu = jnp.mean(x, axis=-1, keepdims=True)
    xc = x - mu
    var = jnp.mean(xc * xc, axis=-1, keepdims=True)
    y = xc * lax.rsqrt(var + EPS) * g + b
    return y * _sigmoid(y)


def _conf_prompt_kernel(pa_ref, pb_ref, ha_ref, hb_ref, wdw_ref, bdw_ref, lng_ref, lnb_ref,
                        c_ref, ust_ref, ubuf, cbuf, *, tl, rt, ct):
    t = pl.program_id(1)
    u = pa_ref[...] * _sigmoid(pb_ref[...])
    uh = ha_ref[...] * _sigmoid(hb_ref[...])
    ubuf[0:HALO, :] = jnp.where(t > 0, uh, 0.0)
    ubuf[HALO:, :] = u
    first = HALO - (CONV_W - 1)
    for r0 in range(0, tl, rt):
        for c0 in range(0, D_MODEL, ct):
            acc = jnp.zeros((rt, ct), F32)
            for s in range(8):
                part = None
                for w in range(CONV_W):
                    if (first + w) % 8 != s:
                        continue
                    base = r0 + (first + w) // 8 * 8
                    term = ubuf[base:base + rt + (8 if s else 0), c0:c0 + ct] * wdw_ref[w:w + 1, c0:c0 + ct]
                    part = term if part is None else part + term
                if part is not None:
                    acc = acc + part[s:s + rt, :]
            cbuf[r0:r0 + rt, c0:c0 + ct] = acc + bdw_ref[:, c0:c0 + ct]
    c_ref[...] = _ln_silu(cbuf[...], lng_ref[...], lnb_ref[...]).astype(BF16)

    @pl.when(t == pl.num_programs(1) - 1)
    def _():
        ust_ref[0] = ubuf[tl:tl + HALO, :]


def _conf_prompt(p, B, LP, w_dw, b_dw, ln_g, ln_b):
    tl = 192 if LP % 192 == 0 else CHUNK
    nt = LP // tl
    hb = tl // HALO
    kern = functools.partial(_conf_prompt_kernel, tl=tl, rt=64, ct=128)
    halo_idx = lambda b, t: (jnp.maximum((b * nt + t) * hb - 1, 0), 0)
    halo_idx1 = lambda b, t: (jnp.maximum((b * nt + t) * hb - 1, 0), 1)
    vec = lambda: pl.BlockSpec((1, D_MODEL), lambda b, t: (0, 0))
    return pl.pallas_call(
        kern,
        out_shape=(jax.ShapeDtypeStruct((B * LP, D_MODEL), BF16),
                   jax.ShapeDtypeStruct((B, HALO, D_MODEL), F32)),
        grid=(B, nt),
        in_specs=[pl.BlockSpec((tl, D_MODEL), lambda b, t: (b * nt + t, 0)),
                  pl.BlockSpec((tl, D_MODEL), lambda b, t: (b * nt + t, 1)),
                  pl.BlockSpec((HALO, D_MODEL), halo_idx),
                  pl.BlockSpec((HALO, D_MODEL), halo_idx1),
                  pl.BlockSpec((CONV_W, D_MODEL), lambda b, t: (0, 0)),
                  vec(), vec(), vec()],
        out_specs=(pl.BlockSpec((tl, D_MODEL), lambda b, t: (b * nt + t, 0)),
                   pl.BlockSpec((1, HALO, D_MODEL), lambda b, t: (b, 0, 0))),
        scratch_shapes=[pltpu.VMEM((HALO + tl, D_MODEL), F32), pltpu.VMEM((tl, D_MODEL), F32)],
        compiler_params=_params(("arbitrary", "arbitrary")),
        name="conf_prompt",
    )(p, p, p, p, w_dw, b_dw, ln_g, ln_b)


def _conf_sample_kernel(st_ref, pa_ref, pb_ref, wdw_ref, bdw_ref, lng_ref, lnb_ref,
                        c_ref, nst_ref, xh, *, sb, ls):
    hist = CONV_W - 1
    for s in range(sb):
        u = pa_ref[s] * _sigmoid(pb_ref[s])
        xh[0:hist, :] = st_ref[s]
        xh[hist:hist + ls, :] = u
        acc = jnp.zeros((ls, D_MODEL), F32)
        for w in range(CONV_W):
            acc = acc + xh[w:w + ls, :] * wdw_ref[w:w + 1, :]
        c_ref[s] = _ln_silu(acc + bdw_ref[...], lng_ref[...], lnb_ref[...])
        nst_ref[s] = xh[ls:ls + hist, :]


def _conf_sample(p_s3, state, w_dw, b_dw, ln_g, ln_b):
    NB, ls, _ = p_s3.shape
    hist = CONV_W - 1
    sb = 8 if NB % 8 == 0 else 1
    kern = functools.partial(_conf_sample_kernel, sb=sb, ls=ls)
    vec = lambda: pl.BlockSpec((1, D_MODEL), lambda i: (0, 0))
    return pl.pallas_call(
        kern,
        out_shape=(jax.ShapeDtypeStruct((NB, ls, D_MODEL), F32),
                   jax.ShapeDtypeStruct((NB, hist, D_MODEL), F32)),
        grid=(NB // sb,),
        in_specs=[pl.BlockSpec((sb, hist, D_MODEL), lambda i: (i, 0, 0)),
                  pl.BlockSpec((sb, ls, D_MODEL), lambda i: (i, 0, 0)),
                  pl.BlockSpec((sb, ls, D_MODEL), lambda i: (i, 0, 1)),
                  pl.BlockSpec((CONV_W, D_MODEL), lambda i: (0, 0)),
                  vec(), vec(), vec()],
        out_specs=(pl.BlockSpec((sb, ls, D_MODEL), lambda i: (i, 0, 0)),
                   pl.BlockSpec((sb, hist, D_MODEL), lambda i: (i, 0, 0))),
        scratch_shapes=[pltpu.VMEM((hist + ls + 8, D_MODEL), F32)],
        compiler_params=_params(("arbitrary",)),
        name="conf_sample",
    )(state, p_s3, p_s3, w_dw, b_dw, ln_g, ln_b)


def _split(a):
    hi = a.astype(BF16)
    return hi, (a - hi.astype(F32)).astype(BF16)


def _mm3(a, b):
    ah, al = a
    bh, bl = b
    return _dot(jnp.concatenate([ah, al, ah], axis=1), jnp.concatenate([bh, bh, bl], axis=0))


def _tri_inverse(ms, i, j, C, nil):
    same = lambda n: (i >> (n.bit_length() - 1)) == (j >> (n.bit_length() - 1))
    base = min(16, C)
    eye = (i == j).astype(F32)
    bdot = lambda a, b: _dot(a.astype(BF16), b.astype(BF16))
    dps = [jnp.where(same(base), m, 0.0) for m in ms]
    xs = [eye - d for d in dps]
    for _ in range(max(0, (min(base, nil) - 1).bit_length() - 1)):
        dps = [bdot(d, d) for d in dps]
        xs = [x + bdot(d, x) for d, x in zip(dps, xs)]
    blk = base
    while blk < C:
        sel = jnp.logical_and(same(2 * blk), jnp.logical_not(same(blk)))
        ys = [bdot(jnp.where(sel, m, 0.0), x) for m, x in zip(ms, xs)]
        xs = [x - bdot(x, y) for x, y in zip(xs, ys)]
        blk *= 2
    xsp = [_split(x) for x in xs]
    res = [eye - x - _mm3(_split(m), xp) for m, x, xp in zip(ms, xs, xsp)]
    return [x + _dot(xp[0], r.astype(BF16)) for x, xp, r in zip(xs, xsp, res)]


def _gdn_chunks(seqs, alog_ref, dtb_ref, nw_ref, nil):
    C = seqs[0][0].shape[0]
    G = STACK // C
    ri = lax.broadcasted_iota(jnp.int32, (C, C), 0)
    ci = lax.broadcasted_iota(jnp.int32, (C, C), 1)
    tril = (ri >= ci).astype(BF16)
    i = lax.broadcasted_iota(jnp.int32, (STACK, STACK), 0)
    j = lax.broadcasted_iota(jnp.int32, (STACK, STACK), 1)
    shift = C.bit_length() - 1
    same = (i >> shift) == (j >> shift)
    causal = jnp.logical_and(same, i >= j)
    strict = jnp.logical_and(same, i > j)

    pre = []
    for n, (xq, xk, xv, z, ab, valid, s_ref) in enumerate(seqs):
        ok = valid > 0.5
        xa = ab + dtb_ref[...]
        softplus = jnp.maximum(xa, 0.0) + jnp.log(1.0 + jnp.exp(-jnp.abs(xa)))
        g_all = jnp.where(ok, -jnp.exp(alog_ref[...]) * softplus, 0.0)
        beta_all = jnp.where(ok, _sigmoid(ab), 0.0)
        g1 = g_all.astype(BF16)
        r1 = g_all - g1.astype(F32)
        g2 = r1.astype(BF16)
        g3 = (r1 - g2.astype(F32)).astype(BF16)
        gc_all = _dot(tril, g1) + _dot(tril, g2) + _dot(tril, g3)
        ok_st = jnp.concatenate([valid] * G, axis=0) > 0.5
        for h0 in range(0, DN_HEADS, G):
            heads = list(range(h0, h0 + G))
            stack = lambda x: jnp.concatenate([x[:, h * DN_DK:(h + 1) * DN_DK] for h in heads], axis=0)
            col = lambda a, off: jnp.concatenate([a[:, off + h:off + h + 1] for h in heads], axis=0)
            q = stack(xq)
            k = stack(xk)
            q = jnp.where(ok_st, q * lax.rsqrt(jnp.sum(q * q, axis=-1, keepdims=True) + EPS) * (DN_DK ** -0.5), 0.0)
            k = jnp.where(ok_st, k * lax.rsqrt(jnp.sum(k * k, axis=-1, keepdims=True) + EPS), 0.0)
            v = jnp.where(ok_st, stack(xv), 0.0)
            gc = col(gc_all, 0)
            beta = col(beta_all, DN_HEADS)
            g_last = jnp.concatenate([jnp.broadcast_to(gc_all[C - 1:C, h:h + 1], (C, 1)) for h in heads], axis=0)
            gb = jnp.broadcast_to(gc, (STACK, STACK))
            decay = jnp.where(causal, jnp.exp(jnp.where(causal, gb - gb.T, 0.0)), 0.0)
            egc = jnp.exp(gc)
            kb = k * beta
            pre.append(dict(n=n, heads=heads, s_ref=s_ref, q=q, kb=kb, kbf=k.astype(BF16), decay=decay, egc=egc,
                            rhs=jnp.concatenate([v * beta, kb * egc], axis=1),
                            k_dec=(k * jnp.exp(g_last - gc)).astype(BF16), zs=stack(z),
                            s_decay=[jnp.exp(gc_all[C - 1:C, h:h + 1]) for h in heads]))
    ms = [jnp.where(strict, _dot_nt(p["kb"].astype(BF16), p["kbf"]) * p["decay"], 0.0) for p in pre]
    qks = [jnp.where(causal, _dot_nt(p["q"].astype(BF16), p["kbf"]) * p["decay"], 0.0).astype(BF16) for p in pre]
    invs = _tri_inverse(ms, i, j, C, nil)
    sols = [_mm3(_split(inv), _split(p["rhs"])) for inv, p in zip(invs, pre)]
    wss = []
    for p, sol in zip(pre, sols):
        w = sol[:, DN_DV:].astype(BF16)
        q_dec = (p["q"] * p["egc"]).astype(BF16)
        wss.append([_dot(jnp.concatenate([w[g * C:(g + 1) * C], q_dec[g * C:(g + 1) * C]], axis=0),
                         p["s_ref"][h].astype(BF16)) for g, h in enumerate(p["heads"])])
    outs = [[None] * DN_HEADS for _ in seqs]
    for p, sol, ws, qk in zip(pre, sols, wss, qks):
        s_ref = p["s_ref"]
        v_new = [(sol[g * C:(g + 1) * C, :DN_DV] - ws[g][:C]).astype(BF16) for g in range(G)]
        for g, h in enumerate(p["heads"]):
            s_ref[h] = s_ref[h] * p["s_decay"][g] + _dot_tn(p["k_dec"][g * C:(g + 1) * C], v_new[g])
        o = jnp.concatenate([w[C:] for w in ws], axis=0) + _dot(qk, jnp.concatenate(v_new, axis=0))
        o = o * lax.rsqrt(jnp.mean(o * o, axis=-1, keepdims=True) + EPS) * nw_ref[...]
        og = o * (p["zs"] * _sigmoid(p["zs"]))
        for g, h in enumerate(p["heads"]):
            outs[p["n"]][h] = og[g * C:(g + 1) * C]
    return outs


def _short_conv_silu(xbuf, wc_ref, rows):
    first = 8 - (SHORT_W - 1)
    acc = xbuf[first:first + rows, :] * wc_ref[0:1, :]
    for w in range(1, SHORT_W):
        acc = acc + xbuf[first + w:first + w + rows, :] * wc_ref[w:w + 1, :]
    return acc * _sigmoid(acc)


def _gdn_prompt_kernel(*refs, ns):
    seq_refs = [refs[5 * s:5 * s + 5] for s in range(ns)]
    wc_ref, alog_ref, dtb_ref, nw_ref, o_ref, s_ref, xbuf = refs[5 * ns:]
    c = pl.program_id(1)
    n_qk = DN_HEADS * DN_DK

    @pl.when(c == 0)
    def _():
        s_ref[...] = jnp.zeros_like(s_ref)
        xbuf[:, 0:8, :] = jnp.zeros((ns, 8, xbuf.shape[2]), F32)

    rows = lax.broadcasted_iota(jnp.int32, (CHUNK, 1), 0)
    valid = jnp.logical_or(rows >= FRONT, c > 0).astype(F32)
    seqs = []
    for s, (q_ref, k_ref, v_ref, z_ref, ab_ref) in enumerate(seq_refs):
        xb = xbuf.at[s]
        xb[8:8 + CHUNK, 0:n_qk] = q_ref[...]
        xb[8:8 + CHUNK, n_qk:2 * n_qk] = k_ref[...]
        xb[8:8 + CHUNK, 2 * n_qk:] = v_ref[...]
        x = _short_conv_silu(xb, wc_ref, CHUNK)
        xb[0:8, :] = xb[CHUNK:CHUNK + 8, :]
        seqs.append((x[:, 0:n_qk], x[:, n_qk:2 * n_qk], x[:, 2 * n_qk:], z_ref[...].astype(F32), ab_ref[...],
                     valid, s_ref.at[s]))
    outs = _gdn_chunks(seqs, alog_ref, dtb_ref, nw_ref, nil=CHUNK)
    for s in range(ns):
        for h in range(DN_HEADS):
            o_ref[s, :, h * DN_DV:(h + 1) * DN_DV] = outs[s][h].astype(BF16)


def _gdn_prompt(p, pg, ab, B, LP, w_conv, alog, dtb, nw):
    nc = LP // CHUNK
    n_qk = DN_HEADS * DN_DK
    ns = 4 if B % 4 == 0 else (2 if B % 2 == 0 else 1)
    vec = lambda n: pl.BlockSpec((1, n), lambda g, c: (0, 0))
    in_specs, args = [], []
    for s in range(ns):
        for src, col in ((p, 2), (p, 3), (p, 4), (pg, 0)):
            in_specs.append(pl.BlockSpec((CHUNK, n_qk), lambda g, c, s=s, col=col: ((g * ns + s) * nc + c, col)))
            args.append(src)
        in_specs.append(pl.BlockSpec((CHUNK, LANES), lambda g, c, s=s: ((g * ns + s) * nc + c, 0)))
        args.append(ab)
    in_specs += [pl.BlockSpec((SHORT_W, 3 * n_qk), lambda g, c: (0, 0)), vec(LANES), vec(LANES), vec(DN_DV)]
    og, s_out = pl.pallas_call(
        functools.partial(_gdn_prompt_kernel, ns=ns),
        out_shape=(jax.ShapeDtypeStruct((B, LP, n_qk), BF16),
                   jax.ShapeDtypeStruct((B, DN_HEADS, DN_DK, DN_DV), F32)),
        grid=(B // ns, nc),
        in_specs=in_specs,
        out_specs=(pl.BlockSpec((ns, CHUNK, n_qk), lambda g, c: (g, c, 0)),
                   pl.BlockSpec((ns, DN_HEADS, DN_DK, DN_DV), lambda g, c: (g, 0, 0, 0))),
        scratch_shapes=[pltpu.VMEM((ns, CHUNK + 8, 3 * n_qk), F32)],
        compiler_params=_params(("arbitrary", "arbitrary")),
        name="gdn_prompt",
    )(*args, w_conv, alog, dtb, nw)
    return og.reshape(B * LP, n_qk), s_out


def _gdn_sample_kernel(st_ref, q_ref, k_ref, v_ref, z_ref, ab_ref, s0_ref, wc_ref, alog_ref, dtb_ref, nw_ref,
                       o_ref, s_ref, xbuf, zbuf, abbuf, *, sb, ls):
    n_qk = DN_HEADS * DN_DK
    C = SAMPLE_CHUNK
    hist = SHORT_W - 1
    xbuf[...] = jnp.zeros_like(xbuf)
    zbuf[...] = jnp.zeros_like(zbuf)
    abbuf[...] = jnp.zeros_like(abbuf)
    s_ref[...] = s0_ref[...]
    valid = (lax.broadcasted_iota(jnp.int32, (C, 1), 0) < ls).astype(F32)
    seqs = []
    for s in range(sb):
        xb = xbuf.at[s]
        xb[8 - hist:8, :] = st_ref[s]
        xb[8:8 + ls, 0:n_qk] = q_ref[s]
        xb[8:8 + ls, n_qk:2 * n_qk] = k_ref[s]
        xb[8:8 + ls, 2 * n_qk:] = v_ref[s]
        zbuf[s, 0:ls, :] = z_ref[s]
        abbuf[s, 0:ls, :] = ab_ref[s]
        x = _short_conv_silu(xb, wc_ref, C)
        seqs.append((x[:, 0:n_qk], x[:, n_qk:2 * n_qk], x[:, 2 * n_qk:], zbuf[s], abbuf[s], valid, s_ref.at[s]))
    outs = _gdn_chunks(seqs, alog_ref, dtb_ref, nw_ref, nil=ls)
    for s in range(sb):
        for h in range(DN_HEADS):
            o_ref[s, :, h * DN_DV:(h + 1) * DN_DV] = outs[s][h][0:ls, :]


def _gdn_sample(p_s3, z_s3, ab_s3, st_conv, s0, w_conv, alog, dtb, nw):
    NB, ls, _ = p_s3.shape
    n_qk = DN_HEADS * DN_DK
    hist = SHORT_W - 1
    assert ls <= SAMPLE_CHUNK
    sb = 4 if NB % 4 == 0 else 1
    kern = functools.partial(_gdn_sample_kernel, sb=sb, ls=ls)
    blk = lambda col: pl.BlockSpec((sb, ls, n_qk), lambda i: (i, 0, col))
    vec = lambda n: pl.BlockSpec((1, n), lambda i: (0, 0))
    sspec = lambda: pl.BlockSpec((sb, DN_HEADS, DN_DK, DN_DV), lambda i: (i, 0, 0, 0))
    return pl.pallas_call(
        kern,
        out_shape=(jax.ShapeDtypeStruct((NB, ls, n_qk), F32),
                   jax.ShapeDtypeStruct((NB, DN_HEADS, DN_DK, DN_DV), F32)),
        grid=(NB // sb,),
        in_specs=[pl.BlockSpec((sb, hist, 3 * n_qk), lambda i: (i, 0, 0)),
                  blk(2), blk(3), blk(4), blk(0),
                  pl.BlockSpec((sb, ls, LANES), lambda i: (i, 0, 0)),
                  sspec(),
                  pl.BlockSpec((SHORT_W, 3 * n_qk), lambda i: (0, 0)),
                  vec(LANES), vec(LANES), vec(DN_DV)],
        out_specs=(pl.BlockSpec((sb, ls, n_qk), lambda i: (i, 0, 0)), sspec()),
        scratch_shapes=[pltpu.VMEM((sb, SAMPLE_CHUNK + 8, 3 * n_qk), F32),
                        pltpu.VMEM((sb, SAMPLE_CHUNK, n_qk), F32),
                        pltpu.VMEM((sb, SAMPLE_CHUNK, LANES), F32)],
        compiler_params=_params(("arbitrary",)),
        name="gdn_sample",
    )(st_conv, p_s3, p_s3, p_s3, z_s3, ab_s3, s0, w_conv, alog, dtb, nw)


def _pack_bf16_pairs(x):
    half = x.shape[1] // 2
    lo = lax.bitcast_convert_type(x[:, :half].astype(BF16).astype(F32), jnp.uint32)
    hi = lax.bitcast_convert_type(x[:, half:].astype(BF16).astype(F32), jnp.uint32)
    return jnp.bitwise_or(jnp.bitwise_and(hi, jnp.uint32(0xFFFF0000)), lax.shift_right_logical(lo, jnp.uint32(16)))


def _unpack_bf16_pairs(xp):
    lo = lax.bitcast_convert_type(lax.shift_left(xp, jnp.uint32(16)), F32)
    hi = lax.bitcast_convert_type(jnp.bitwise_and(xp, jnp.uint32(0xFFFF0000)), F32)
    return jnp.concatenate([lo, hi], axis=1).astype(BF16)


def _merge_kernel(cp_ref, cs_ref, ogp_ref, ogs_ref, ga_ref, gb_ref, h_ref, wco_ref, bco_ref, wdo_ref, wo_ref, nf_ref,
                  wr_ref, br_ref, h1_ref, xp_ref, ei_ref, gt_ref, rk_ref, cnt_ref, carry, below, *, n_p):
    step = pl.program_id(0)

    @pl.when(step == 0)
    def _():
        carry[...] = jnp.zeros_like(carry)
        ri = lax.broadcasted_iota(jnp.int32, below.shape, 0)
        ci = lax.broadcasted_iota(jnp.int32, below.shape, 1)
        below[...] = (ri > ci).astype(BF16)

    in_prompt = step < n_p
    c = jnp.where(in_prompt, cp_ref[...], cs_ref[...])
    og = jnp.where(in_prompt, ogp_ref[...], ogs_ref[...])
    ya = _dot(c, wco_ref[...]) + bco_ref[...]
    yb = _dot(og, wdo_ref[...])
    mixed = _sigmoid(ga_ref[...].astype(F32)) * ya + _sigmoid(gb_ref[...].astype(F32)) * yb
    h1 = h_ref[...] + _dot(mixed.astype(BF16), wo_ref[...])
    h1_ref[...] = h1
    xn = h1 * lax.rsqrt(jnp.mean(h1 * h1, axis=-1, keepdims=True) + EPS) * nf_ref[...]
    xp_ref[...] = _pack_bf16_pairs(xn)
    logits = _dot(xn.astype(BF16), wr_ref[...]) + br_ref[...]

    tm = logits.shape[0]
    lane = lax.broadcasted_iota(jnp.int32, (tm, LANES), 1)
    work = logits
    sels, idxs, vals = [], [], []
    for _ in range(TOP_K):
        m = jnp.max(work, axis=-1, keepdims=True)
        idx = jnp.min(jnp.where(work == m, lane, N_EXPERTS - 1), axis=-1, keepdims=True)
        sel = lane == idx
        sels.append(sel)
        idxs.append(idx)
        vals.append(m)
        work = jnp.where(sel, -jnp.inf, work)
    exps = [jnp.exp(v - vals[0]) for v in vals]
    denom = exps[0]
    for e in exps[1:]:
        denom = denom + e
    onehot = jnp.zeros((tm, LANES), F32)
    for sel in sels:
        onehot = onehot + sel.astype(F32)
    before = _dot(below[...], onehot.astype(BF16)) + carry[...]
    ei = jnp.zeros((tm, LANES), jnp.int32)
    gt = jnp.zeros((tm, LANES), F32)
    rk = jnp.zeros((tm, LANES), jnp.int32)
    for k in range(TOP_K):
        at_k = lane == k
        r_k = jnp.sum(jnp.where(sels[k], before, 0.0), axis=-1, keepdims=True).astype(jnp.int32)
        ei = jnp.where(at_k, idxs[k], ei)
        gt = jnp.where(at_k, exps[k] / denom, gt)
        rk = jnp.where(at_k, r_k, rk)
    ei_ref[...] = ei
    gt_ref[...] = gt
    rk_ref[...] = rk
    carry[...] = carry[...] + jnp.sum(onehot, axis=0, keepdims=True)
    cnt_ref[...] = carry[...]


def _merge(c_p, c_s, og_p, og_s, pg, h, wco, bco, wdo, wo, nf, wr, br):
    T = h.shape[0]
    tm = _row_tile(math.gcd(c_p.shape[0], c_s.shape[0]), 512)
    n_p = c_p.shape[0] // tm
    row = lambda col: pl.BlockSpec((tm, D_MODEL), lambda i: (i, col))
    part_p = lambda: pl.BlockSpec((tm, D_MODEL), lambda i: (jnp.minimum(i, n_p - 1), 0))
    part_s = lambda: pl.BlockSpec((tm, D_MODEL), lambda i: (jnp.maximum(i - n_p, 0), 0))
    full = lambda a, b: pl.BlockSpec((a, b), lambda i: (0, 0))
    lanes = lambda: pl.BlockSpec((tm, LANES), lambda i: (i, 0))
    return pl.pallas_call(
        functools.partial(_merge_kernel, n_p=n_p),
        out_shape=(jax.ShapeDtypeStruct((T, D_MODEL), F32),
                   jax.ShapeDtypeStruct((T, D_MODEL // 2), jnp.uint32),
                   jax.ShapeDtypeStruct((T, LANES), jnp.int32),
                   jax.ShapeDtypeStruct((T, LANES), F32),
                   jax.ShapeDtypeStruct((T, LANES), jnp.int32),
                   jax.ShapeDtypeStruct((1, LANES), F32)),
        grid=(T // tm,),
        in_specs=[part_p(), part_s(), part_p(), part_s(), row(1), row(2), row(0),
                  full(D_MODEL, D_MODEL), full(1, D_MODEL), full(D_MODEL, D_MODEL), full(D_MODEL, D_MODEL),
                  full(1, D_MODEL), full(D_MODEL, LANES), full(1, LANES)],
        out_specs=(row(0), pl.BlockSpec((tm, D_MODEL // 2), lambda i: (i, 0)), lanes(), lanes(), lanes(),
                   full(1, LANES)),
        scratch_shapes=[pltpu.VMEM((1, LANES), F32), pltpu.VMEM((tm, tm), BF16)],
        compiler_params=_params(("arbitrary",)),
        name="merge",
    )(c_p, c_s, og_p, og_s, pg, pg, h, wco, bco, wdo, wo, nf, wr, br)


ISSUE_UNROLL = 8


def _row_copy(src, src_row, dst, dst_row, sem):
    return pltpu.make_async_copy(src.at[pl.ds(src_row, 1), :], dst.at[pl.ds(dst_row, 1), :], sem)


def _dispatch_kernel(dest_ref, x_ref, xb_out, sem, *, tm):
    def issue(t, carry):
        for k in range(TOP_K):
            _row_copy(x_ref, t, xb_out, dest_ref[t * TOP_K + k], sem).start(priority=k % 2)
        return carry

    lax.fori_loop(0, tm, issue, 0, unroll=ISSUE_UNROLL)
    for k in range(TOP_K):
        pltpu.make_async_copy(x_ref, xb_out.at[pl.ds(0, tm), :], sem).wait()


def _dispatch(dest, xp, n_rows):
    T, W = xp.shape
    tm = _row_tile(T, 512)
    return pl.pallas_call(
        functools.partial(_dispatch_kernel, tm=tm),
        out_shape=jax.ShapeDtypeStruct((n_rows, W), xp.dtype),
        grid=(T // tm,),
        in_specs=[pl.BlockSpec((tm * TOP_K,), lambda i: (i,), memory_space=pltpu.SMEM),
                  pl.BlockSpec((tm, W), lambda i: (i, 0))],
        out_specs=pl.BlockSpec(memory_space=pl.ANY),
        scratch_shapes=[pltpu.SemaphoreType.DMA],
        compiler_params=_params(("arbitrary",)),
        name="moe_dispatch",
    )(dest, xp)


def _moe_kernel(be_ref, nv_ref, first_ref, slot_ref, next_ref, x_ref, wu_hbm, bu_ref, wd_hbm, bd_ref, o_ref,
                wu32, wd32, wub, wdb, sems):
    i = pl.program_id(0)

    def fetch(e, s):
        pltpu.make_async_copy(wu_hbm.at[e], wu32.at[s], sems.at[0, s]).start()
        pltpu.make_async_copy(wd_hbm.at[e], wd32.at[s], sems.at[1, s]).start()

    @pl.when(i == 0)
    def _():
        fetch(be_ref[0], 0)

    @pl.when(first_ref[i] == 1)
    def _():
        s = slot_ref[i]
        pltpu.make_async_copy(wu_hbm.at[0], wu32.at[s], sems.at[0, s]).wait()
        pltpu.make_async_copy(wd_hbm.at[0], wd32.at[s], sems.at[1, s]).wait()
        wub[...] = wu32[s].astype(BF16)
        wdb[...] = wd32[s].astype(BF16)

        @pl.when(next_ref[i] >= 0)
        def _():
            fetch(next_ref[i], 1 - s)

    @pl.when(nv_ref[i] > 0)
    def _():
        rows = lax.broadcasted_iota(jnp.int32, (MOE_ROWS, 1), 0)
        xp = jnp.where(rows < nv_ref[i], x_ref[...], jnp.uint32(0))
        hmid = _dot(_unpack_bf16_pairs(xp), wub[...]) + bu_ref[0]
        hg = jnp.minimum(hmid[:, :D_FF], SWIGLU_LIMIT)
        hl = jnp.clip(hmid[:, D_FF:], -SWIGLU_LIMIT, SWIGLU_LIMIT)
        act = hg * _sigmoid(SWIGLU_ALPHA * hg) * (hl + 1.0)
        o_ref[...] = _dot(act.astype(BF16), wdb[...]) + bd_ref[0]

    @pl.when(nv_ref[i] == 0)
    def _():
        o_ref[...] = jnp.zeros_like(o_ref)


def _moe(plan, xb, w_up, b_up, w_down, b_down):
    R = xb.shape[0]
    nb = R // MOE_ROWS
    bias = lambda n: pl.BlockSpec((1, 1, n), lambda i, be, *_: (be[i], 0, 0))
    grid_spec = pltpu.PrefetchScalarGridSpec(
        num_scalar_prefetch=len(plan),
        grid=(nb,),
        in_specs=[pl.BlockSpec((MOE_ROWS, D_MODEL // 2), lambda i, *_: (i, 0)),
                  pl.BlockSpec(memory_space=pl.ANY), bias(2 * D_FF),
                  pl.BlockSpec(memory_space=pl.ANY), bias(D_MODEL)],
        out_specs=pl.BlockSpec((MOE_ROWS, D_MODEL), lambda i, *_: (i, 0)),
        scratch_shapes=[pltpu.VMEM((2, D_MODEL, 2 * D_FF), F32), pltpu.VMEM((2, D_FF, D_MODEL), F32),
                        pltpu.VMEM((D_MODEL, 2 * D_FF), BF16), pltpu.VMEM((D_FF, D_MODEL), BF16),
                        pltpu.SemaphoreType.DMA((2, 2))],
    )
    return pl.pallas_call(
        _moe_kernel,
        out_shape=jax.ShapeDtypeStruct((R, D_MODEL), F32),
        grid_spec=grid_spec,
        compiler_params=pltpu.CompilerParams(dimension_semantics=("arbitrary",), vmem_limit_bytes=MOE_VMEM_LIMIT),
        name="moe_experts",
    )(*plan, xb, w_up, b_up, w_down, b_down)


def _dispatch_plan(counts, ei, rk, T):
    A = T * TOP_K
    n_blocks = -(-A // MOE_ROWS) + N_EXPERTS
    counts = counts[0, :N_EXPERTS].astype(jnp.int32)
    padded = (counts + MOE_ROWS - 1) // MOE_ROWS * MOE_ROWS
    pend = jnp.cumsum(padded)
    pstart = pend - padded
    experts = jnp.arange(N_EXPERTS, dtype=jnp.int32)
    seg_start = jnp.sum(jnp.where(ei[:, :TOP_K, None] == experts, pstart, 0), axis=-1)
    dest = (seg_start + rk[:, :TOP_K]).reshape(-1).astype(jnp.int32)
    starts = jnp.arange(n_blocks, dtype=jnp.int32) * MOE_ROWS
    block_e = jnp.minimum(jnp.sum((pend[None, :] <= starts[:, None]).astype(jnp.int32), axis=1), N_EXPERTS - 1)
    seg_end = jnp.sum(jnp.where(block_e[:, None] == experts, pstart + counts, 0), axis=-1)
    rows_valid = jnp.clip(seg_end - starts, 0, MOE_ROWS).astype(jnp.int32)
    prev_e = jnp.concatenate([jnp.full((1,), -1, jnp.int32), block_e[:-1]])
    first = jnp.logical_and(block_e != prev_e, rows_valid > 0).astype(jnp.int32)
    slot = (jnp.cumsum(first) - 1) % 2
    later = jnp.logical_and(experts[None, :] > experts[:, None], (counts > 0)[None, :])
    next_of = jnp.min(jnp.where(later, experts[None, :], N_EXPERTS), axis=1)
    next_of = jnp.where(next_of == N_EXPERTS, -1, next_of)
    next_e = jnp.sum(jnp.where(block_e[:, None] == experts, next_of, 0), axis=-1)
    plan = tuple(a.astype(jnp.int32) for a in (block_e, rows_valid, first, slot, next_e))
    return dest, plan, n_blocks * MOE_ROWS


def _combine_kernel(dest_ref, dnext_ref, gt_ref, h_ref, nw_ref, yb_ref, o_ref, buf, sems, *, tm):
    i = pl.program_id(0)
    slot = lax.rem(i, 2)

    def request(d_ref, s):
        def issue(t, carry):
            for k in range(TOP_K):
                _row_copy(yb_ref, d_ref[t * TOP_K + k], buf.at[s, k], t, sems.at[s]).start(priority=k % 2)
            return carry

        lax.fori_loop(0, tm, issue, 0, unroll=ISSUE_UNROLL)

    @pl.when(i == 0)
    def _():
        request(dest_ref, 0)

    @pl.when(i + 1 < pl.num_programs(0))
    def _():
        request(dnext_ref, 1 - slot)

    for k in range(TOP_K):
        pltpu.make_async_copy(yb_ref.at[pl.ds(0, tm), :], buf.at[slot, k], sems.at[slot]).wait()
    gt = gt_ref[...]
    x = h_ref[...]
    for k in range(TOP_K):
        x = x + gt[:, k:k + 1] * buf[slot, k]
    o_ref[...] = x * lax.rsqrt(jnp.mean(x * x, axis=-1, keepdims=True) + EPS) * nw_ref[...]


def _combine(dest, gt, h1, nw, yb):
    T = h1.shape[0]
    tm = _row_tile(T, 512)
    n = T // tm
    row = lambda: pl.BlockSpec((tm, D_MODEL), lambda i: (i, 0))
    return pl.pallas_call(
        functools.partial(_combine_kernel, tm=tm),
        out_shape=jax.ShapeDtypeStruct((T, D_MODEL), F32),
        grid=(T // tm,),
        in_specs=[pl.BlockSpec((tm * TOP_K,), lambda i: (i,), memory_space=pltpu.SMEM),
                  pl.BlockSpec((tm * TOP_K,), lambda i: (jnp.minimum(i + 1, n - 1),), memory_space=pltpu.SMEM),
                  pl.BlockSpec((tm, LANES), lambda i: (i, 0)),
                  row(),
                  pl.BlockSpec((1, D_MODEL), lambda i: (0, 0)),
                  pl.BlockSpec(memory_space=pl.ANY)],
        out_specs=row(),
        scratch_shapes=[pltpu.VMEM((2, TOP_K, tm, D_MODEL), F32), pltpu.SemaphoreType.DMA((2,))],
        compiler_params=_params(("arbitrary",)),
        name="moe_combine_final",
    )(dest, dest, gt, h1, nw, yb)


def _pad_lanes(v, fill=0.0):
    v = v.reshape(1, -1).astype(F32)
    return jnp.pad(v, ((0, 0), (0, LANES - v.shape[1])), constant_values=fill)


def kernel(x_prompt, x_sample, state_conf_conv, state_dn_conv, state_dn_S, meta_tokens, norm_mix, w_in, w_conf_dw, b_conf_dw, ln_conf_g, ln_conf_b, w_conf_out, b_conf_out, w_dn_conv, dn_a_log, dn_dt_bias, dn_norm_w, w_dn_out, w_out, norm_ffn, w_router, b_router, w_up, b_up, w_down, b_down, norm_final):
    B, SEQ, D = x_prompt.shape
    NB, LS, _ = x_sample.shape
    depth = w_in.shape[0]
    assert D == D_MODEL and depth == 1 and SEQ % CHUNK == 0 and LS >= SHORT_W - 1
    LP = FRONT + N_META + SEQ
    TP = B * LP
    T = TP + NB * LS
    n_qk = DN_HEADS * DN_DK
    o_q = 2 * D_MODEL
    o_a = o_q + 4 * n_qk
    o_gate = o_a + 2 * DN_HEADS

    h0 = _tokens(x_prompt, x_sample.reshape(NB * LS, D), meta_tokens.astype(F32))

    w_in0 = w_in[0]
    o_z = o_q + 3 * n_qk
    w_gate = jnp.concatenate([w_in0[:, o_z:o_a], w_in0[:, o_gate:]], axis=1)
    w_ab = jnp.pad(w_in0[:, o_a:o_gate], ((0, 0), (0, LANES - 2 * DN_HEADS))).astype(BF16)
    wco = w_conf_out[0].astype(BF16)
    wdo = w_dn_out[0].astype(BF16)
    wo = w_out[0].astype(BF16)
    wr = jnp.pad(w_router[0], ((0, 0), (0, LANES - N_EXPERTS))).astype(BF16)
    br = _pad_lanes(b_router[0], fill=-1e30)
    alog = _pad_lanes(dn_a_log[0])
    dtb = _pad_lanes(dn_dt_bias[0])
    row = lambda v: v.reshape(1, -1).astype(F32)

    xn, ab = _rms_ab(h0, row(norm_mix[0]), w_ab)
    n_main = o_z // D_MODEL
    p = _mm_in(xn, w_in0, n_main, F32)
    pg = _mm_in(xn, w_gate, 3, BF16)
    p_s3 = p[TP:].reshape(NB, LS, n_main * D_MODEL)
    z_s3 = pg[TP:, :n_qk].astype(F32).reshape(NB, LS, n_qk)
    ab_s3 = ab[TP:].reshape(NB, LS, LANES)

    c_p, ust_p = _conf_prompt(p, B, LP, w_conf_dw[0], row(b_conf_dw[0]), row(ln_conf_g[0]), row(ln_conf_b[0]))
    c_s, conf_state_s = _conf_sample(p_s3, state_conf_conv[0], w_conf_dw[0], row(b_conf_dw[0]),
                                     row(ln_conf_g[0]), row(ln_conf_b[0]))

    og_p, s_p = _gdn_prompt(p, pg, ab, B, LP, w_dn_conv[0], alog, dtb, row(dn_norm_w[0]))
    og_s, s_s = _gdn_sample(p_s3, z_s3, ab_s3, state_dn_conv[0], state_dn_S[0], w_dn_conv[0], alog, dtb,
                            row(dn_norm_w[0]))

    h1, xp2, ei, gt, rk, counts = _merge(c_p, c_s.reshape(NB * LS, D).astype(BF16),
                                         og_p, og_s.reshape(NB * LS, n_qk).astype(BF16), pg, h0,
                                         wco, row(b_conf_out[0]), wdo, wo, row(norm_ffn[0]), wr, br)

    dest, plan, R = _dispatch_plan(counts, ei, rk, T)
    xb = _dispatch(dest, xp2, R)
    yb = _moe(plan, xb, w_up[0], b_up[0].reshape(N_EXPERTS, 1, -1), w_down[0], b_down[0].reshape(N_EXPERTS, 1, -1))
    y = _combine(dest, gt, h1, row(norm_final), yb)

    y_prompt = jnp.stack([y[b * LP + FRONT + N_META:(b + 1) * LP] for b in range(B)])
    y_sample = y[TP:].reshape(NB, LS, D)
    hist = CONV_W - 1
    conf_conv_prompt = ust_p[:, HALO - hist:][None]
    dn_conv_prompt = jnp.stack([p[(b + 1) * LP - (SHORT_W - 1):(b + 1) * LP, o_q:o_q + 3 * n_qk]
                                for b in range(B)])[None]
    dn_conv_sample = p_s3[:, LS - (SHORT_W - 1):, o_q:o_q + 3 * n_qk][None]
    return (y_prompt, y_sample, conf_conv_prompt, dn_conv_prompt, s_p[None],
            conf_state_s[None], dn_conv_sample, s_s[None])
```

```python
import functools
import math

import jax
import jax.numpy as jnp
from jax import lax
from jax.experimental import pallas as pl
from jax.experimental.pallas import tpu as pltpu

D_MODEL = 1024
N_META = 16
CONV_W = 31
SHORT_W = 4
DN_HEADS = 8
DN_DK = 128
DN_DV = 128
CHUNK = 64
N_EXPERTS = 32
TOP_K = 4
D_FF = 1024
SWIGLU_LIMIT = 7.0
SWIGLU_ALPHA = 1.702
EPS = 1e-6

FRONT = (-N_META) % CHUNK
SAMPLE_CHUNK = 16
STACK = 128
LANES = 128
HALO = 32
MOE_ROWS = 512
VMEM_LIMIT = 48 * 1024 * 1024
MOE_VMEM_LIMIT = 58 * 1024 * 1024

F32 = jnp.float32
BF16 = jnp.bfloat16


def _row_tile(n, pref):
    best = 16
    for t in range(16, min(n, pref) + 1, 16):
        if n % t == 0:
            best = t
    assert n % best == 0
    return best


def _sigmoid(x):
    return 1.0 / (1.0 + jnp.exp(-x))


def _dot(a, b):
    return jnp.dot(a, b, preferred_element_type=F32)


def _dot_nt(a, b):
    return lax.dot_general(a, b, (((1,), (1,)), ((), ())), preferred_element_type=F32)


def _dot_tn(a, b):
    return lax.dot_general(a, b, (((0,), (0,)), ((), ())), preferred_element_type=F32)


def _params(sem):
    return pltpu.CompilerParams(dimension_semantics=sem, vmem_limit_bytes=VMEM_LIMIT)


def _tokens_kernel(xp_ref, xs_ref, meta_ref, h_ref, head, sems, *, lp, tr, n_prompt_rows):
    b = pl.program_id(0)
    j = pl.program_id(1)
    n_head = FRONT + N_META
    row0 = pl.multiple_of(b * lp, 8)
    body = pltpu.make_async_copy(xp_ref.at[0], h_ref.at[pl.ds(row0 + n_head + j * tr, tr), :], sems.at[0])
    body.start()

    @pl.when(j == 0)
    def _():
        head[0:FRONT, :] = jnp.zeros((FRONT, D_MODEL), F32)
        head[FRONT:, :] = meta_ref[...]
        front = pltpu.make_async_copy(head, h_ref.at[pl.ds(row0, n_head), :], sems.at[1])
        front.start()
        front.wait()

    @pl.when(jnp.logical_and(b == 0, j == 0))
    def _():
        tail = pltpu.make_async_copy(xs_ref, h_ref.at[pl.ds(n_prompt_rows, xs_ref.shape[0]), :], sems.at[2])
        tail.start()
        tail.wait()

    body.wait()


def _tokens(x_prompt, x_sample2, meta):
    B, SEQ, D = x_prompt.shape
    lp = FRONT + N_META + SEQ
    T = B * lp + x_sample2.shape[0]
    tr = _row_tile(SEQ, 1024)
    return pl.pallas_call(
        functools.partial(_tokens_kernel, lp=lp, tr=tr, n_prompt_rows=B * lp),
        out_shape=jax.ShapeDtypeStruct((T, D), F32),
        grid=(B, SEQ // tr),
        in_specs=[pl.BlockSpec((1, tr, D), lambda b, j: (b, j, 0)),
                  pl.BlockSpec(x_sample2.shape, lambda b, j: (0, 0)),
                  pl.BlockSpec((N_META, D), lambda b, j: (0, 0))],
        out_specs=pl.BlockSpec(memory_space=pl.ANY),
        scratch_shapes=[pltpu.VMEM((FRONT + N_META, D), F32), pltpu.SemaphoreType.DMA((3,))],
        compiler_params=_params(("arbitrary", "arbitrary")),
        name="token_layout",
    )(x_prompt, x_sample2, meta)


def _rms_ab_kernel(h_ref, nw_ref, wab_ref, xn_ref, ab_ref):
    x = h_ref[...]
    y = x * lax.rsqrt(jnp.mean(x * x, axis=-1, keepdims=True) + EPS) * nw_ref[...]
    yb = y.astype(BF16)
    xn_ref[...] = yb
    ab_ref[...] = _dot(yb, wab_ref[...])


def _rms_ab(h, norm_w, w_ab):
    T = h.shape[0]
    tm = _row_tile(T, 1024)
    return pl.pallas_call(
        _rms_ab_kernel,
        out_shape=(jax.ShapeDtypeStruct((T, D_MODEL), BF16), jax.ShapeDtypeStruct((T, LANES), F32)),
        grid=(T // tm,),
        in_specs=[pl.BlockSpec((tm, D_MODEL), lambda i: (i, 0)),
                  pl.BlockSpec((1, D_MODEL), lambda i: (0, 0)),
                  pl.BlockSpec((D_MODEL, LANES), lambda i: (0, 0))],
        out_specs=(pl.BlockSpec((tm, D_MODEL), lambda i: (i, 0)),
                   pl.BlockSpec((tm, LANES), lambda i: (i, 0))),
        compiler_params=_params(("arbitrary",)),
        name="rms_ab",
    )(h, norm_w, w_ab)


def _mm_in_kernel(x_ref, w_ref, o_ref, wb_ref):
    @pl.when(pl.program_id(1) == 0)
    def _():
        wb_ref[...] = w_ref[...].astype(BF16)

    o_ref[...] = _dot(x_ref[...], wb_ref[...]).astype(o_ref.dtype)


def _mm_in(xn, w, n_tiles, out_dtype):
    T, K = xn.shape
    tm = _row_tile(T, 2304)
    tn = 1024
    N = n_tiles * tn
    assert N <= w.shape[1]
    return pl.pallas_call(
        _mm_in_kernel,
        out_shape=jax.ShapeDtypeStruct((T, N), out_dtype),
        grid=(N // tn, T // tm),
        in_specs=[pl.BlockSpec((tm, K), lambda j, i: (i, 0)),
                  pl.BlockSpec((K, tn), lambda j, i: (0, j))],
        out_specs=pl.BlockSpec((tm, tn), lambda j, i: (i, j)),
        scratch_shapes=[pltpu.VMEM((K, tn), BF16)],
        compiler_params=_params(("arbitrary", "arbitrary")),
        name="in_proj",
    )(xn, w)


def _ln_silu(x, g, b):
    mu = jnp.mean(x, axis=-1, keepdims=True)
    xc = x - mu
    var = jnp.mean(xc * xc, axis=-1, keepdims=True)
    y = xc * lax.rsqrt(var + EPS) * g + b
    return y * _sigmoid(y)


def _conf_prompt_kernel(pa_ref, pb_ref, ha_ref, hb_ref, wdw_ref, bdw_ref, lng_ref, lnb_ref,
                        c_ref, ust_ref, ubuf, cbuf, *, tl, rt, ct):
    t = pl.program_id(1)
    u = pa_ref[...] * _sigmoid(pb_ref[...])
    uh = ha_ref[...] * _sigmoid(hb_ref[...])
    ubuf[0:HALO, :] = jnp.where(t > 0, uh, 0.0)
    ubuf[HALO:, :] = u
    first = HALO - (CONV_W - 1)
    for r0 in range(0, tl, rt):
        for c0 in range(0, D_MODEL, ct):
            acc = jnp.zeros((rt, ct), F32)
            for s in range(8):
                part = None
                for w in range(CONV_W):
                    if (first + w) % 8 != s:
                        continue
                    base = r0 + (first + w) // 8 * 8
                    term = ubuf[base:base + rt + (8 if s else 0), c0:c0 + ct] * wdw_ref[w:w + 1, c0:c0 + ct]
                    part = term if part is None else part + term
                if part is not None:
                    acc = acc + part[s:s + rt, :]
            cbuf[r0:r0 + rt, c0:c0 + ct] = acc + bdw_ref[:, c0:c0 + ct]
    c_ref[...] = _ln_silu(cbuf[...], lng_ref[...], lnb_ref[...]).astype(BF16)

    @pl.when(t == pl.num_programs(1) - 1)
    def _():
        ust_ref[0] = ubuf[tl:tl + HALO, :]


def _conf_prompt(p, B, LP, w_dw, b_dw, ln_g, ln_b):
    tl = 192 if LP % 192 == 0 else CHUNK
    nt = LP // tl
    hb = tl // HALO
    kern = functools.partial(_conf_prompt_kernel, tl=tl, rt=64, ct=128)
    halo_idx = lambda b, t: (jnp.maximum((b * nt + t) * hb - 1, 0), 0)
    halo_idx1 = lambda b, t: (jnp.maximum((b * nt + t) * hb - 1, 0), 1)
    vec = lambda: pl.BlockSpec((1, D_MODEL), lambda b, t: (0, 0))
    return pl.pallas_call(
        kern,
        out_shape=(jax.ShapeDtypeStruct((B * LP, D_MODEL), BF16),
                   jax.ShapeDtypeStruct((B, HALO, D_MODEL), F32)),
        grid=(B, nt),
        in_specs=[pl.BlockSpec((tl, D_MODEL), lambda b, t: (b * nt + t, 0)),
                  pl.BlockSpec((tl, D_MODEL), lambda b, t: (b * nt + t, 1)),
                  pl.BlockSpec((HALO, D_MODEL), halo_idx),
                  pl.BlockSpec((HALO, D_MODEL), halo_idx1),
                  pl.BlockSpec((CONV_W, D_MODEL), lambda b, t: (0, 0)),
                  vec(), vec(), vec()],
        out_specs=(pl.BlockSpec((tl, D_MODEL), lambda b, t: (b * nt + t, 0)),
                   pl.BlockSpec((1, HALO, D_MODEL), lambda b, t: (b, 0, 0))),
        scratch_shapes=[pltpu.VMEM((HALO + tl, D_MODEL), F32), pltpu.VMEM((tl, D_MODEL), F32)],
        compiler_params=_params(("arbitrary", "arbitrary")),
        name="conf_prompt",
    )(p, p, p, p, w_dw, b_dw, ln_g, ln_b)


def _conf_sample_kernel(st_ref, pa_ref, pb_ref, wdw_ref, bdw_ref, lng_ref, lnb_ref,
                        c_ref, nst_ref, xh, *, sb, ls):
    hist = CONV_W - 1
    for s in range(sb):
        u = pa_ref[s] * _sigmoid(pb_ref[s])
        xh[0:hist, :] = st_ref[s]
        xh[hist:hist + ls, :] = u
        acc = jnp.zeros((ls, D_MODEL), F32)
        for w in range(CONV_W):
            acc = acc + xh[w:w + ls, :] * wdw_ref[w:w + 1, :]
        c_ref[s] = _ln_silu(acc + bdw_ref[...], lng_ref[...], lnb_ref[...])
        nst_ref[s] = xh[ls:ls + hist, :]


def _conf_sample(p_s3, state, w_dw, b_dw, ln_g, ln_b):
    NB, ls, _ = p_s3.shape
    hist = CONV_W - 1
    sb = 8 if NB % 8 == 0 else 1
    kern = functools.partial(_conf_sample_kernel, sb=sb, ls=ls)
    vec = lambda: pl.BlockSpec((1, D_MODEL), lambda i: (0, 0))
    return pl.pallas_call(
        kern,
        out_shape=(jax.ShapeDtypeStruct((NB, ls, D_MODEL), F32),
                   jax.ShapeDtypeStruct((NB, hist, D_MODEL), F32)),
        grid=(NB // sb,),
        in_specs=[pl.BlockSpec((sb, hist, D_MODEL), lambda i: (i, 0, 0)),
                  pl.BlockSpec((sb, ls, D_MODEL), lambda i: (i, 0, 0)),
                  pl.BlockSpec((sb, ls, D_MODEL), lambda i: (i, 0, 1)),
                  pl.BlockSpec((CONV_W, D_MODEL), lambda i: (0, 0)),
                  vec(), vec(), vec()],
        out_specs=(pl.BlockSpec((sb, ls, D_MODEL), lambda i: (i, 0, 0)),
                   pl.BlockSpec((sb, hist, D_MODEL), lambda i: (i, 0, 0))),
        scratch_shapes=[pltpu.VMEM((hist + ls + 8, D_MODEL), F32)],
        compiler_params=_params(("arbitrary",)),
        name="conf_sample",
    )(state, p_s3, p_s3, w_dw, b_dw, ln_g, ln_b)


def _split(a):
    hi = a.astype(BF16)
    return hi, (a - hi.astype(F32)).astype(BF16)


def _mm3(a, b):
    ah, al = a
    bh, bl = b
    return _dot(jnp.concatenate([ah, al, ah], axis=1), jnp.concatenate([bh, bh, bl], axis=0))


def _tri_inverse(ms, i, j, C, nil):
    same = lambda n: (i >> (n.bit_length() - 1)) == (j >> (n.bit_length() - 1))
    base = min(16, C)
    eye = (i == j).astype(F32)
    bdot = lambda a, b: _dot(a.astype(BF16), b.astype(BF16))
    dps = [jnp.where(same(base), m, 0.0) for m in ms]
    xs = [eye - d for d in dps]
    for _ in range(max(0, (min(base, nil) - 1).bit_length() - 1)):
        dps = [bdot(d, d) for d in dps]
        xs = [x + bdot(d, x) for d, x in zip(dps, xs)]
    blk = base
    while blk < C:
        sel = jnp.logical_and(same(2 * blk), jnp.logical_not(same(blk)))
        ys = [bdot(jnp.where(sel, m, 0.0), x) for m, x in zip(ms, xs)]
        xs = [x - bdot(x, y) for x, y in zip(xs, ys)]
        blk *= 2
    xsp = [_split(x) for x in xs]
    res = [eye - x - _mm3(_split(m), xp) for m, x, xp in zip(ms, xs, xsp)]
    return [x + _dot(xp[0], r.astype(BF16)) for x, xp, r in zip(xs, xsp, res)]


def _gdn_chunks(seqs, alog_ref, dtb_ref, nw_ref, nil):
    C = seqs[0][0].shape[0]
    G = STACK // C
    ri = lax.broadcasted_iota(jnp.int32, (C, C), 0)
    ci = lax.broadcasted_iota(jnp.int32, (C, C), 1)
    tril = (ri >= ci).astype(BF16)
    i = lax.broadcasted_iota(jnp.int32, (STACK, STACK), 0)
    j = lax.broadcasted_iota(jnp.int32, (STACK, STACK), 1)
    shift = C.bit_length() - 1
    same = (i >> shift) == (j >> shift)
    causal = jnp.logical_and(same, i >= j)
    strict = jnp.logical_and(same, i > j)

    pre = []
    for n, (xq, xk, xv, z, ab, valid, s_ref) in enumerate(seqs):
        ok = valid > 0.5
        xa = ab + dtb_ref[...]
        softplus = jnp.maximum(xa, 0.0) + jnp.log(1.0 + jnp.exp(-jnp.abs(xa)))
        g_all = jnp.where(ok, -jnp.exp(alog_ref[...]) * softplus, 0.0)
        beta_all = jnp.where(ok, _sigmoid(ab), 0.0)
        g1 = g_all.astype(BF16)
        r1 = g_all - g1.astype(F32)
        g2 = r1.astype(BF16)
        g3 = (r1 - g2.astype(F32)).astype(BF16)
        gc_all = _dot(tril, g1) + _dot(tril, g2) + _dot(tril, g3)
        ok_st = jnp.concatenate([valid] * G, axis=0) > 0.5
        for h0 in range(0, DN_HEADS, G):
            heads = list(range(h0, h0 + G))
            stack = lambda x: jnp.concatenate([x[:, h * DN_DK:(h + 1) * DN_DK] for h in heads], axis=0)
            col = lambda a, off: jnp.concatenate([a[:, off + h:off + h + 1] for h in heads], axis=0)
            q = stack(xq)
            k = stack(xk)
            q = jnp.where(ok_st, q * lax.rsqrt(jnp.sum(q * q, axis=-1, keepdims=True) + EPS) * (DN_DK ** -0.5), 0.0)
            k = jnp.where(ok_st, k * lax.rsqrt(jnp.sum(k * k, axis=-1, keepdims=True) + EPS), 0.0)
            v = jnp.where(ok_st, stack(xv), 0.0)
            gc = col(gc_all, 0)
            beta = col(beta_all, DN_HEADS)
            g_last = jnp.concatenate([jnp.broadcast_to(gc_all[C - 1:C, h:h + 1], (C, 1)) for h in heads], axis=0)
            gb = jnp.broadcast_to(gc, (STACK, STACK))
            decay = jnp.where(causal, jnp.exp(jnp.where(causal, gb - gb.T, 0.0)), 0.0)
            egc = jnp.exp(gc)
            kb = k * beta
            pre.append(dict(n=n, heads=heads, s_ref=s_ref, q=q, kb=kb, kbf=k.astype(BF16), decay=decay, egc=egc,
                            rhs=jnp.concatenate([v * beta, kb * egc], axis=1),
                            k_dec=(k * jnp.exp(g_last - gc)).astype(BF16), zs=stack(z),
                            s_decay=[jnp.exp(gc_all[C - 1:C, h:h + 1]) for h in heads]))
    ms = [jnp.where(strict, _dot_nt(p["kb"].astype(BF16), p["kbf"]) * p["decay"], 0.0) for p in pre]
    qks = [jnp.where(causal, _dot_nt(p["q"].astype(BF16), p["kbf"]) * p["decay"], 0.0).astype(BF16) for p in pre]
    invs = _tri_inverse(ms, i, j, C, nil)
    sols = [_mm3(_split(inv), _split(p["rhs"])) for inv, p in zip(invs, pre)]
    wss = []
    for p, sol in zip(pre, sols):
        w = sol[:, DN_DV:].astype(BF16)
        q_dec = (p["q"] * p["egc"]).astype(BF16)
        wss.append([_dot(jnp.concatenate([w[g * C:(g + 1) * C], q_dec[g * C:(g + 1) * C]], axis=0),
                         p["s_ref"][h].astype(BF16)) for g, h in enumerate(p["heads"])])
    outs = [[None] * DN_HEADS for _ in seqs]
    for p, sol, ws, qk in zip(pre, sols, wss, qks):
        s_ref = p["s_ref"]
        v_new = [(sol[g * C:(g + 1) * C, :DN_DV] - ws[g][:C]).astype(BF16) for g in range(G)]
        for g, h in enumerate(p["heads"]):
            s_ref[h] = s_ref[h] * p["s_decay"][g] + _dot_tn(p["k_dec"][g * C:(g + 1) * C], v_new[g])
        o = jnp.concatenate([w[C:] for w in ws], axis=0) + _dot(qk, jnp.concatenate(v_new, axis=0))
        o = o * lax.rsqrt(jnp.mean(o * o, axis=-1, keepdims=True) + EPS) * nw_ref[...]
        og = o * (p["zs"] * _sigmoid(p["zs"]))
        for g, h in enumerate(p["heads"]):
            outs[p["n"]][h] = og[g * C:(g + 1) * C]
    return outs


def _short_conv_silu(xbuf, wc_ref, rows):
    first = 8 - (SHORT_W - 1)
    acc = xbuf[first:first + rows, :] * wc_ref[0:1, :]
    for w in range(1, SHORT_W):
        acc = acc + xbuf[first + w:first + w + rows, :] * wc_ref[w:w + 1, :]
    return acc * _sigmoid(acc)


def _gdn_prompt_kernel(*refs, ns):
    seq_refs = [refs[5 * s:5 * s + 5] for s in range(ns)]
    wc_ref, alog_ref, dtb_ref, nw_ref, o_ref, s_ref, xbuf = refs[5 * ns:]
    c = pl.program_id(1)
    n_qk = DN_HEADS * DN_DK

    @pl.when(c == 0)
    def _():
        s_ref[...] = jnp.zeros_like(s_ref)
        xbuf[:, 0:8, :] = jnp.zeros((ns, 8, xbuf.shape[2]), F32)

    rows = lax.broadcasted_iota(jnp.int32, (CHUNK, 1), 0)
    valid = jnp.logical_or(rows >= FRONT, c > 0).astype(F32)
    seqs = []
    for s, (q_ref, k_ref, v_ref, z_ref, ab_ref) in enumerate(seq_refs):
        xb = xbuf.at[s]
        xb[8:8 + CHUNK, 0:n_qk] = q_ref[...]
        xb[8:8 + CHUNK, n_qk:2 * n_qk] = k_ref[...]
        xb[8:8 + CHUNK, 2 * n_qk:] = v_ref[...]
        x = _short_conv_silu(xb, wc_ref, CHUNK)
        xb[0:8, :] = xb[CHUNK:CHUNK + 8, :]
        seqs.append((x[:, 0:n_qk], x[:, n_qk:2 * n_qk], x[:, 2 * n_qk:], z_ref[...].astype(F32), ab_ref[...],
                     valid, s_ref.at[s, 0]))
    outs = _gdn_chunks(seqs, alog_ref, dtb_ref, nw_ref, nil=CHUNK)
    for s in range(ns):
        for h in range(DN_HEADS):
            o_ref[s, :, h * DN_DV:(h + 1) * DN_DV] = outs[s][h].astype(BF16)


def _gdn_prompt(p, pg, ab, B, LP, w_conv, alog, dtb, nw):
    nc = LP // CHUNK
    n_qk = DN_HEADS * DN_DK
    ns = 4 if B % 4 == 0 else (2 if B % 2 == 0 else 1)
    groups = B // ns
    vec = lambda n: pl.BlockSpec((1, n), lambda g, c: (0, 0))
    in_specs, args = [], []
    for s in range(ns):
        for src, col in ((p, 2), (p, 3), (p, 4), (pg, 0)):
            in_specs.append(pl.BlockSpec((CHUNK, n_qk), lambda g, c, s=s, col=col: ((s * groups + g) * nc + c, col)))
            args.append(src)
        in_specs.append(pl.BlockSpec((CHUNK, LANES), lambda g, c, s=s: ((s * groups + g) * nc + c, 0)))
        args.append(ab)
    in_specs += [pl.BlockSpec((SHORT_W, 3 * n_qk), lambda g, c: (0, 0)), vec(LANES), vec(LANES), vec(DN_DV)]
    og, s_out = pl.pallas_call(
        functools.partial(_gdn_prompt_kernel, ns=ns),
        out_shape=(jax.ShapeDtypeStruct((ns, groups * LP, n_qk), BF16),
                   jax.ShapeDtypeStruct((ns, groups, DN_HEADS, DN_DK, DN_DV), F32)),
        grid=(groups, nc),
        in_specs=in_specs,
        out_specs=(pl.BlockSpec((ns, CHUNK, n_qk), lambda g, c: (0, g * nc + c, 0)),
                   pl.BlockSpec((ns, 1, DN_HEADS, DN_DK, DN_DV), lambda g, c: (0, g, 0, 0, 0))),
        scratch_shapes=[pltpu.VMEM((ns, CHUNK + 8, 3 * n_qk), F32)],
        compiler_params=_params(("arbitrary", "arbitrary")),
        name="gdn_prompt",
    )(*args, w_conv, alog, dtb, nw)
    return og.reshape(B * LP, n_qk), s_out.reshape(B, DN_HEADS, DN_DK, DN_DV)


def _gdn_sample_kernel(st_ref, q_ref, k_ref, v_ref, z_ref, ab_ref, s0_ref, wc_ref, alog_ref, dtb_ref, nw_ref,
                       o_ref, s_ref, xbuf, zbuf, abbuf, *, sb, ls):
    n_qk = DN_HEADS * DN_DK
    C = SAMPLE_CHUNK
    hist = SHORT_W - 1
    xbuf[...] = jnp.zeros_like(xbuf)
    zbuf[...] = jnp.zeros_like(zbuf)
    abbuf[...] = jnp.zeros_like(abbuf)
    s_ref[...] = s0_ref[...]
    valid = (lax.broadcasted_iota(jnp.int32, (C, 1), 0) < ls).astype(F32)
    seqs = []
    for s in range(sb):
        xb = xbuf.at[s]
        xb[8 - hist:8, :] = st_ref[s]
        xb[8:8 + ls, 0:n_qk] = q_ref[s]
        xb[8:8 + ls, n_qk:2 * n_qk] = k_ref[s]
        xb[8:8 + ls, 2 * n_qk:] = v_ref[s]
        zbuf[s, 0:ls, :] = z_ref[s]
        abbuf[s, 0:ls, :] = ab_ref[s]
        x = _short_conv_silu(xb, wc_ref, C)
        seqs.append((x[:, 0:n_qk], x[:, n_qk:2 * n_qk], x[:, 2 * n_qk:], zbuf[s], abbuf[s], valid, s_ref.at[s]))
    outs = _gdn_chunks(seqs, alog_ref, dtb_ref, nw_ref, nil=ls)
    for s in range(sb):
        for h in range(DN_HEADS):
            o_ref[s, :, h * DN_DV:(h + 1) * DN_DV] = outs[s][h][0:ls, :]


def _gdn_sample(p_s3, z_s3, ab_s3, st_conv, s0, w_conv, alog, dtb, nw):
    NB, ls, _ = p_s3.shape
    n_qk = DN_HEADS * DN_DK
    hist = SHORT_W - 1
    assert ls <= SAMPLE_CHUNK
    sb = 4 if NB % 4 == 0 else 1
    kern = functools.partial(_gdn_sample_kernel, sb=sb, ls=ls)
    blk = lambda col: pl.BlockSpec((sb, ls, n_qk), lambda i: (i, 0, col))
    vec = lambda n: pl.BlockSpec((1, n), lambda i: (0, 0))
    sspec = lambda: pl.BlockSpec((sb, DN_HEADS, DN_DK, DN_DV), lambda i: (i, 0, 0, 0))
    return pl.pallas_call(
        kern,
        out_shape=(jax.ShapeDtypeStruct((NB, ls, n_qk), F32),
                   jax.ShapeDtypeStruct((NB, DN_HEADS, DN_DK, DN_DV), F32)),
        grid=(NB // sb,),
        in_specs=[pl.BlockSpec((sb, hist, 3 * n_qk), lambda i: (i, 0, 0)),
                  blk(2), blk(3), blk(4), blk(0),
                  pl.BlockSpec((sb, ls, LANES), lambda i: (i, 0, 0)),
                  sspec(),
                  pl.BlockSpec((SHORT_W, 3 * n_qk), lambda i: (0, 0)),
                  vec(LANES), vec(LANES), vec(DN_DV)],
        out_specs=(pl.BlockSpec((sb, ls, n_qk), lambda i: (i, 0, 0)), sspec()),
        scratch_shapes=[pltpu.VMEM((sb, SAMPLE_CHUNK + 8, 3 * n_qk), F32),
                        pltpu.VMEM((sb, SAMPLE_CHUNK, n_qk), F32),
                        pltpu.VMEM((sb, SAMPLE_CHUNK, LANES), F32)],
        compiler_params=_params(("arbitrary",)),
        name="gdn_sample",
    )(st_conv, p_s3, p_s3, p_s3, z_s3, ab_s3, s0, w_conv, alog, dtb, nw)


def _pack_bf16_pairs(x):
    half = x.shape[1] // 2
    lo = lax.bitcast_convert_type(x[:, :half].astype(BF16).astype(F32), jnp.uint32)
    hi = lax.bitcast_convert_type(x[:, half:].astype(BF16).astype(F32), jnp.uint32)
    return jnp.bitwise_or(jnp.bitwise_and(hi, jnp.uint32(0xFFFF0000)), lax.shift_right_logical(lo, jnp.uint32(16)))


def _unpack_bf16_pairs(xp):
    lo = lax.bitcast_convert_type(lax.shift_left(xp, jnp.uint32(16)), F32)
    hi = lax.bitcast_convert_type(jnp.bitwise_and(xp, jnp.uint32(0xFFFF0000)), F32)
    return jnp.concatenate([lo, hi], axis=1).astype(BF16)


def _merge_kernel(cp_ref, cs_ref, ogp_ref, ogs_ref, ga_ref, gb_ref, h_ref, wco_ref, bco_ref, wdo_ref, wo_ref, nf_ref,
                  wr_ref, br_ref, h1_ref, xp_ref, ei_ref, gt_ref, rk_ref, cnt_ref, carry, below, *, n_p):
    step = pl.program_id(0)

    @pl.when(step == 0)
    def _():
        carry[...] = jnp.zeros_like(carry)
        ri = lax.broadcasted_iota(jnp.int32, below.shape, 0)
        ci = lax.broadcasted_iota(jnp.int32, below.shape, 1)
        below[...] = (ri > ci).astype(BF16)

    in_prompt = step < n_p
    c = jnp.where(in_prompt, cp_ref[...], cs_ref[...])
    og = jnp.where(in_prompt, ogp_ref[...], ogs_ref[...])
    ya = _dot(c, wco_ref[...]) + bco_ref[...]
    yb = _dot(og, wdo_ref[...])
    mixed = _sigmoid(ga_ref[...].astype(F32)) * ya + _sigmoid(gb_ref[...].astype(F32)) * yb
    h1 = h_ref[...] + _dot(mixed.astype(BF16), wo_ref[...])
    h1_ref[...] = h1
    xn = h1 * lax.rsqrt(jnp.mean(h1 * h1, axis=-1, keepdims=True) + EPS) * nf_ref[...]
    xp_ref[...] = _pack_bf16_pairs(xn)
    logits = _dot(xn.astype(BF16), wr_ref[...]) + br_ref[...]

    tm = logits.shape[0]
    lane = lax.broadcasted_iota(jnp.int32, (tm, LANES), 1)
    work = logits
    sels, idxs, vals = [], [], []
    for _ in range(TOP_K):
        m = jnp.max(work, axis=-1, keepdims=True)
        idx = jnp.min(jnp.where(work == m, lane, N_EXPERTS - 1), axis=-1, keepdims=True)
        sel = lane == idx
        sels.append(sel)
        idxs.append(idx)
        vals.append(m)
        work = jnp.where(sel, -jnp.inf, work)
    exps = [jnp.exp(v - vals[0]) for v in vals]
    denom = exps[0]
    for e in exps[1:]:
        denom = denom + e
    onehot = jnp.zeros((tm, LANES), F32)
    for sel in sels:
        onehot = onehot + sel.astype(F32)
    before = _dot(below[...], onehot.astype(BF16)) + carry[...]
    ei = jnp.zeros((tm, LANES), jnp.int32)
    gt = jnp.zeros((tm, LANES), F32)
    rk = jnp.zeros((tm, LANES), jnp.int32)
    for k in range(TOP_K):
        at_k = lane == k
        r_k = jnp.sum(jnp.where(sels[k], before, 0.0), axis=-1, keepdims=True).astype(jnp.int32)
        ei = jnp.where(at_k, idxs[k], ei)
        gt = jnp.where(at_k, exps[k] / denom, gt)
        rk = jnp.where(at_k, r_k, rk)
    ei_ref[...] = ei
    gt_ref[...] = gt
    rk_ref[...] = rk
    carry[...] = carry[...] + jnp.sum(onehot, axis=0, keepdims=True)
    cnt_ref[...] = carry[...]


def _merge(c_p, c_s, og_p, og_s, pg, h, wco, bco, wdo, wo, nf, wr, br):
    T = h.shape[0]
    tm = _row_tile(math.gcd(c_p.shape[0], c_s.shape[0]), 512)
    n_p = c_p.shape[0] // tm
    row = lambda col: pl.BlockSpec((tm, D_MODEL), lambda i: (i, col))
    part_p = lambda: pl.BlockSpec((tm, D_MODEL), lambda i: (jnp.minimum(i, n_p - 1), 0))
    part_s = lambda: pl.BlockSpec((tm, D_MODEL), lambda i: (jnp.maximum(i - n_p, 0), 0))
    full = lambda a, b: pl.BlockSpec((a, b), lambda i: (0, 0))
    lanes = lambda: pl.BlockSpec((tm, LANES), lambda i: (i, 0))
    return pl.pallas_call(
        functools.partial(_merge_kernel, n_p=n_p),
        out_shape=(jax.ShapeDtypeStruct((T, D_MODEL), F32),
                   jax.ShapeDtypeStruct((T, D_MODEL // 2), jnp.uint32),
                   jax.ShapeDtypeStruct((T, LANES), jnp.int32),
                   jax.ShapeDtypeStruct((T, LANES), F32),
                   jax.ShapeDtypeStruct((T, LANES), jnp.int32),
                   jax.ShapeDtypeStruct((1, LANES), F32)),
        grid=(T // tm,),
        in_specs=[part_p(), part_s(), part_p(), part_s(), row(1), row(2), row(0),
                  full(D_MODEL, D_MODEL), full(1, D_MODEL), full(D_MODEL, D_MODEL), full(D_MODEL, D_MODEL),
                  full(1, D_MODEL), full(D_MODEL, LANES), full(1, LANES)],
        out_specs=(row(0), pl.BlockSpec((tm, D_MODEL // 2), lambda i: (i, 0)), lanes(), lanes(), lanes(),
                   full(1, LANES)),
        scratch_shapes=[pltpu.VMEM((1, LANES), F32), pltpu.VMEM((tm, tm), BF16)],
        compiler_params=_params(("arbitrary",)),
        name="merge",
    )(c_p, c_s, og_p, og_s, pg, pg, h, wco, bco, wdo, wo, nf, wr, br)


ISSUE_UNROLL = 8


def _row_copy(src, src_row, dst, dst_row, sem):
    return pltpu.make_async_copy(src.at[pl.ds(src_row, 1), :], dst.at[pl.ds(dst_row, 1), :], sem)


def _dispatch_kernel(dest_ref, x_ref, xb_out, sem, *, tm):
    def issue(t, carry):
        for k in range(TOP_K):
            _row_copy(x_ref, t, xb_out, dest_ref[t * TOP_K + k], sem).start(priority=k % 2)
        return carry

    lax.fori_loop(0, tm, issue, 0, unroll=ISSUE_UNROLL)
    for k in range(TOP_K):
        pltpu.make_async_copy(x_ref, xb_out.at[pl.ds(0, tm), :], sem).wait()


def _dispatch(dest, xp, n_rows):
    T, W = xp.shape
    tm = _row_tile(T, 512)
    return pl.pallas_call(
        functools.partial(_dispatch_kernel, tm=tm),
        out_shape=jax.ShapeDtypeStruct((n_rows, W), xp.dtype),
        grid=(T // tm,),
        in_specs=[pl.BlockSpec((tm * TOP_K,), lambda i: (i,), memory_space=pltpu.SMEM),
                  pl.BlockSpec((tm, W), lambda i: (i, 0))],
        out_specs=pl.BlockSpec(memory_space=pl.ANY),
        scratch_shapes=[pltpu.SemaphoreType.DMA],
        compiler_params=_params(("arbitrary",)),
        name="moe_dispatch",
    )(dest, xp)


def _moe_kernel(be_ref, nv_ref, first_ref, slot_ref, next_ref, x_ref, wu_hbm, bu_ref, wd_hbm, bd_ref, o_ref,
                wu32, wd32, wub, wdb, sems):
    i = pl.program_id(0)

    def fetch(e, s):
        pltpu.make_async_copy(wu_hbm.at[e], wu32.at[s], sems.at[0, s]).start()
        pltpu.make_async_copy(wd_hbm.at[e], wd32.at[s], sems.at[1, s]).start()

    @pl.when(i == 0)
    def _():
        fetch(be_ref[0], 0)

    @pl.when(first_ref[i] == 1)
    def _():
        s = slot_ref[i]
        pltpu.make_async_copy(wu_hbm.at[0], wu32.at[s], sems.at[0, s]).wait()
        pltpu.make_async_copy(wd_hbm.at[0], wd32.at[s], sems.at[1, s]).wait()
        wub[...] = wu32[s].astype(BF16)
        wdb[...] = wd32[s].astype(BF16)

        @pl.when(next_ref[i] >= 0)
        def _():
            fetch(next_ref[i], 1 - s)

    @pl.when(nv_ref[i] > 0)
    def _():
        rows = lax.broadcasted_iota(jnp.int32, (MOE_ROWS, 1), 0)
        xp = jnp.where(rows < nv_ref[i], x_ref[...], jnp.uint32(0))
        hmid = _dot(_unpack_bf16_pairs(xp), wub[...]) + bu_ref[0]
        hg = jnp.minimum(hmid[:, :D_FF], SWIGLU_LIMIT)
        hl = jnp.clip(hmid[:, D_FF:], -SWIGLU_LIMIT, SWIGLU_LIMIT)
        act = hg * _sigmoid(SWIGLU_ALPHA * hg) * (hl + 1.0)
        o_ref[...] = _dot(act.astype(BF16), wdb[...]) + bd_ref[0]

    @pl.when(nv_ref[i] == 0)
    def _():
        o_ref[...] = jnp.zeros_like(o_ref)


def _moe(plan, xb, w_up, b_up, w_down, b_down):
    R = xb.shape[0]
    nb = R // MOE_ROWS
    bias = lambda n: pl.BlockSpec((1, 1, n), lambda i, be, *_: (be[i], 0, 0))
    grid_spec = pltpu.PrefetchScalarGridSpec(
        num_scalar_prefetch=len(plan),
        grid=(nb,),
        in_specs=[pl.BlockSpec((MOE_ROWS, D_MODEL // 2), lambda i, *_: (i, 0)),
                  pl.BlockSpec(memory_space=pl.ANY), bias(2 * D_FF),
                  pl.BlockSpec(memory_space=pl.ANY), bias(D_MODEL)],
        out_specs=pl.BlockSpec((MOE_ROWS, D_MODEL), lambda i, *_: (i, 0)),
        scratch_shapes=[pltpu.VMEM((2, D_MODEL, 2 * D_FF), F32), pltpu.VMEM((2, D_FF, D_MODEL), F32),
                        pltpu.VMEM((D_MODEL, 2 * D_FF), BF16), pltpu.VMEM((D_FF, D_MODEL), BF16),
                        pltpu.SemaphoreType.DMA((2, 2))],
    )
    return pl.pallas_call(
        _moe_kernel,
        out_shape=jax.ShapeDtypeStruct((R, D_MODEL), F32),
        grid_spec=grid_spec,
        compiler_params=pltpu.CompilerParams(dimension_semantics=("arbitrary",), vmem_limit_bytes=MOE_VMEM_LIMIT),
        name="moe_experts",
    )(*plan, xb, w_up, b_up, w_down, b_down)


def _dispatch_plan(counts, ei, rk, T):
    A = T * TOP_K
    n_blocks = -(-A // MOE_ROWS) + N_EXPERTS
    counts = counts[0, :N_EXPERTS].astype(jnp.int32)
    padded = (counts + MOE_ROWS - 1) // MOE_ROWS * MOE_ROWS
    pend = jnp.cumsum(padded)
    pstart = pend - padded
    experts = jnp.arange(N_EXPERTS, dtype=jnp.int32)
    seg_start = jnp.sum(jnp.where(ei[:, :TOP_K, None] == experts, pstart, 0), axis=-1)
    dest = (seg_start + rk[:, :TOP_K]).reshape(-1).astype(jnp.int32)
    starts = jnp.arange(n_blocks, dtype=jnp.int32) * MOE_ROWS
    block_e = jnp.minimum(jnp.sum((pend[None, :] <= starts[:, None]).astype(jnp.int32), axis=1), N_EXPERTS - 1)
    seg_end = jnp.sum(jnp.where(block_e[:, None] == experts, pstart + counts, 0), axis=-1)
    rows_valid = jnp.clip(seg_end - starts, 0, MOE_ROWS).astype(jnp.int32)
    prev_e = jnp.concatenate([jnp.full((1,), -1, jnp.int32), block_e[:-1]])
    first = jnp.logical_and(block_e != prev_e, rows_valid > 0).astype(jnp.int32)
    slot = (jnp.cumsum(first) - 1) % 2
    later = jnp.logical_and(experts[None, :] > experts[:, None], (counts > 0)[None, :])
    next_of = jnp.min(jnp.where(later, experts[None, :], N_EXPERTS), axis=1)
    next_of = jnp.where(next_of == N_EXPERTS, -1, next_of)
    next_e = jnp.sum(jnp.where(block_e[:, None] == experts, next_of, 0), axis=-1)
    plan = tuple(a.astype(jnp.int32) for a in (block_e, rows_valid, first, slot, next_e))
    return dest, plan, n_blocks * MOE_ROWS


def _combine_kernel(dest_ref, dnext_ref, gt_ref, h_ref, nw_ref, yb_ref, o_ref, buf, sems, *, tm):
    i = pl.program_id(0)
    slot = lax.rem(i, 2)

    def request(d_ref, s):
        def issue(t, carry):
            for k in range(TOP_K):
                _row_copy(yb_ref, d_ref[t * TOP_K + k], buf.at[s, k], t, sems.at[s]).start(priority=k % 2)
            return carry

        lax.fori_loop(0, tm, issue, 0, unroll=ISSUE_UNROLL)

    @pl.when(i == 0)
    def _():
        request(dest_ref, 0)

    @pl.when(i + 1 < pl.num_programs(0))
    def _():
        request(dnext_ref, 1 - slot)

    for k in range(TOP_K):
        pltpu.make_async_copy(yb_ref.at[pl.ds(0, tm), :], buf.at[slot, k], sems.at[slot]).wait()
    gt = gt_ref[...]
    x = h_ref[...]
    for k in range(TOP_K):
        x = x + gt[:, k:k + 1] * buf[slot, k]
    o_ref[...] = x * lax.rsqrt(jnp.mean(x * x, axis=-1, keepdims=True) + EPS) * nw_ref[...]


def _combine(dest, gt, h1, nw, yb):
    T = h1.shape[0]
    tm = _row_tile(T, 512)
    n = T // tm
    row = lambda: pl.BlockSpec((tm, D_MODEL), lambda i: (i, 0))
    return pl.pallas_call(
        functools.partial(_combine_kernel, tm=tm),
        out_shape=jax.ShapeDtypeStruct((T, D_MODEL), F32),
        grid=(T // tm,),
        in_specs=[pl.BlockSpec((tm * TOP_K,), lambda i: (i,), memory_space=pltpu.SMEM),
                  pl.BlockSpec((tm * TOP_K,), lambda i: (jnp.minimum(i + 1, n - 1),), memory_space=pltpu.SMEM),
                  pl.BlockSpec((tm, LANES), lambda i: (i, 0)),
                  row(),
                  pl.BlockSpec((1, D_MODEL), lambda i: (0, 0)),
                  pl.BlockSpec(memory_space=pl.ANY)],
        out_specs=row(),
        scratch_shapes=[pltpu.VMEM((2, TOP_K, tm, D_MODEL), F32), pltpu.SemaphoreType.DMA((2,))],
        compiler_params=_params(("arbitrary",)),
        name="moe_combine_final",
    )(dest, dest, gt, h1, nw, yb)


def _pad_lanes(v, fill=0.0):
    v = v.reshape(1, -1).astype(F32)
    return jnp.pad(v, ((0, 0), (0, LANES - v.shape[1])), constant_values=fill)


def kernel(x_prompt, x_sample, state_conf_conv, state_dn_conv, state_dn_S, meta_tokens, norm_mix, w_in, w_conf_dw, b_conf_dw, ln_conf_g, ln_conf_b, w_conf_out, b_conf_out, w_dn_conv, dn_a_log, dn_dt_bias, dn_norm_w, w_dn_out, w_out, norm_ffn, w_router, b_router, w_up, b_up, w_down, b_down, norm_final):
    B, SEQ, D = x_prompt.shape
    NB, LS, _ = x_sample.shape
    depth = w_in.shape[0]
    assert D == D_MODEL and depth == 1 and SEQ % CHUNK == 0 and LS >= SHORT_W - 1
    LP = FRONT + N_META + SEQ
    TP = B * LP
    T = TP + NB * LS
    n_qk = DN_HEADS * DN_DK
    o_q = 2 * D_MODEL
    o_a = o_q + 4 * n_qk
    o_gate = o_a + 2 * DN_HEADS

    h0 = _tokens(x_prompt, x_sample.reshape(NB * LS, D), meta_tokens.astype(F32))

    w_in0 = w_in[0]
    o_z = o_q + 3 * n_qk
    w_gate = jnp.concatenate([w_in0[:, o_z:o_a], w_in0[:, o_gate:]], axis=1)
    w_ab = jnp.pad(w_in0[:, o_a:o_gate], ((0, 0), (0, LANES - 2 * DN_HEADS))).astype(BF16)
    wco = w_conf_out[0].astype(BF16)
    wdo = w_dn_out[0].astype(BF16)
    wo = w_out[0].astype(BF16)
    wr = jnp.pad(w_router[0], ((0, 0), (0, LANES - N_EXPERTS))).astype(BF16)
    br = _pad_lanes(b_router[0], fill=-1e30)
    alog = _pad_lanes(dn_a_log[0])
    dtb = _pad_lanes(dn_dt_bias[0])
    row = lambda v: v.reshape(1, -1).astype(F32)

    xn, ab = _rms_ab(h0, row(norm_mix[0]), w_ab)
    n_main = o_z // D_MODEL
    p = _mm_in(xn, w_in0, n_main, F32)
    pg = _mm_in(xn, w_gate, 3, BF16)
    p_s3 = p[TP:].reshape(NB, LS, n_main * D_MODEL)
    z_s3 = pg[TP:, :n_qk].astype(F32).reshape(NB, LS, n_qk)
    ab_s3 = ab[TP:].reshape(NB, LS, LANES)

    c_p, ust_p = _conf_prompt(p, B, LP, w_conf_dw[0], row(b_conf_dw[0]), row(ln_conf_g[0]), row(ln_conf_b[0]))
    c_s, conf_state_s = _conf_sample(p_s3, state_conf_conv[0], w_conf_dw[0], row(b_conf_dw[0]),
                                     row(ln_conf_g[0]), row(ln_conf_b[0]))

    og_p, s_p = _gdn_prompt(p, pg, ab, B, LP, w_dn_conv[0], alog, dtb, row(dn_norm_w[0]))
    og_s, s_s = _gdn_sample(p_s3, z_s3, ab_s3, state_dn_conv[0], state_dn_S[0], w_dn_conv[0], alog, dtb,
                            row(dn_norm_w[0]))

    h1, xp2, ei, gt, rk, counts = _merge(c_p, c_s.reshape(NB * LS, D).astype(BF16),
                                         og_p, og_s.reshape(NB * LS, n_qk).astype(BF16), pg, h0,
                                         wco, row(b_conf_out[0]), wdo, wo, row(norm_ffn[0]), wr, br)

    dest, plan, R = _dispatch_plan(counts, ei, rk, T)
    xb = _dispatch(dest, xp2, R)
    yb = _moe(plan, xb, w_up[0], b_up[0].reshape(N_EXPERTS, 1, -1), w_down[0], b_down[0].reshape(N_EXPERTS, 1, -1))
    y = _combine(dest, gt, h1, row(norm_final), yb)

    y_prompt = jnp.stack([y[b * LP + FRONT + N_META:(b + 1) * LP] for b in range(B)])
    y_sample = y[TP:].reshape(NB, LS, D)
    hist = CONV_W - 1
    conf_conv_prompt = ust_p[:, HALO - hist:][None]
    dn_conv_prompt = jnp.stack([p[(b + 1) * LP - (SHORT_W - 1):(b + 1) * LP, o_q:o_q + 3 * n_qk]
                                for b in range(B)])[None]
    dn_conv_sample = p_s3[:, LS - (SHORT_W - 1):, o_q:o_q + 3 * n_qk][None]
    return (y_prompt, y_sample, conf_conv_prompt, dn_conv_prompt, s_p[None],
            conf_state_s[None], dn_conv_sample, s_s[None])
```

```python
import functools
import math

import jax
import jax.numpy as jnp
from jax import lax
from jax.experimental import pallas as pl
from jax.experimental.pallas import tpu as pltpu

D_MODEL = 1024
N_META = 16
CONV_W = 31
SHORT_W = 4
DN_HEADS = 8
DN_DK = 128
DN_DV = 128
CHUNK = 64
N_EXPERTS = 32
TOP_K = 4
D_FF = 1024
SWIGLU_LIMIT = 7.0
SWIGLU_ALPHA = 1.702
EPS = 1e-6

FRONT = (-N_META) % CHUNK
SAMPLE_CHUNK = 16
STACK = 128
LANES = 128
HALO = 32
MOE_ROWS = 512
VMEM_LIMIT = 48 * 1024 * 1024
MOE_VMEM_LIMIT = 58 * 1024 * 1024

F32 = jnp.float32
BF16 = jnp.bfloat16


def _row_tile(n, pref):
    best = 16
    for t in range(16, min(n, pref) + 1, 16):
        if n % t == 0:
            best = t
    assert n % best == 0
    return best


def _sigmoid(x):
    return 1.0 / (1.0 + jnp.exp(-x))


def _dot(a, b):
    return jnp.dot(a, b, preferred_element_type=F32)


def _dot_nt(a, b):
    return lax.dot_general(a, b, (((1,), (1,)), ((), ())), preferred_element_type=F32)


def _dot_tn(a, b):
    return lax.dot_general(a, b, (((0,), (0,)), ((), ())), preferred_element_type=F32)


def _params(sem):
    return pltpu.CompilerParams(dimension_semantics=sem, vmem_limit_bytes=VMEM_LIMIT)


def _tokens_kernel(xp_ref, xs_ref, meta_ref, h_ref, head, sems, *, lp, tr, n_prompt_rows):
    b = pl.program_id(0)
    j = pl.program_id(1)
    n_head = FRONT + N_META
    row0 = pl.multiple_of(b * lp, 8)
    body = pltpu.make_async_copy(xp_ref.at[0], h_ref.at[pl.ds(row0 + n_head + j * tr, tr), :], sems.at[0])
    body.start()

    @pl.when(j == 0)
    def _():
        head[0:FRONT, :] = jnp.zeros((FRONT, D_MODEL), F32)
        head[FRONT:, :] = meta_ref[...]
        front = pltpu.make_async_copy(head, h_ref.at[pl.ds(row0, n_head), :], sems.at[1])
        front.start()
        front.wait()

    @pl.when(jnp.logical_and(b == 0, j == 0))
    def _():
        tail = pltpu.make_async_copy(xs_ref, h_ref.at[pl.ds(n_prompt_rows, xs_ref.shape[0]), :], sems.at[2])
        tail.start()
        tail.wait()

    body.wait()


def _tokens(x_prompt, x_sample2, meta):
    B, SEQ, D = x_prompt.shape
    lp = FRONT + N_META + SEQ
    T = B * lp + x_sample2.shape[0]
    tr = _row_tile(SEQ, 1024)
    return pl.pallas_call(
        functools.partial(_tokens_kernel, lp=lp, tr=tr, n_prompt_rows=B * lp),
        out_shape=jax.ShapeDtypeStruct((T, D), F32),
        grid=(B, SEQ // tr),
        in_specs=[pl.BlockSpec((1, tr, D), lambda b, j: (b, j, 0)),
                  pl.BlockSpec(x_sample2.shape, lambda b, j: (0, 0)),
                  pl.BlockSpec((N_META, D), lambda b, j: (0, 0))],
        out_specs=pl.BlockSpec(memory_space=pl.ANY),
        scratch_shapes=[pltpu.VMEM((FRONT + N_META, D), F32), pltpu.SemaphoreType.DMA((3,))],
        compiler_params=_params(("arbitrary", "arbitrary")),
        name="token_layout",
    )(x_prompt, x_sample2, meta)


def _rms_ab_kernel(h_ref, nw_ref, wab_ref, xn_ref, ab_ref):
    x = h_ref[...]
    y = x * lax.rsqrt(jnp.mean(x * x, axis=-1, keepdims=True) + EPS) * nw_ref[...]
    yb = y.astype(BF16)
    xn_ref[...] = yb
    ab_ref[...] = _dot(yb, wab_ref[...])


def _rms_ab(h, norm_w, w_ab):
    T = h.shape[0]
    tm = _row_tile(T, 1024)
    return pl.pallas_call(
        _rms_ab_kernel,
        out_shape=(jax.ShapeDtypeStruct((T, D_MODEL), BF16), jax.ShapeDtypeStruct((T, LANES), F32)),
        grid=(T // tm,),
        in_specs=[pl.BlockSpec((tm, D_MODEL), lambda i: (i, 0)),
                  pl.BlockSpec((1, D_MODEL), lambda i: (0, 0)),
                  pl.BlockSpec((D_MODEL, LANES), lambda i: (0, 0))],
        out_specs=(pl.BlockSpec((tm, D_MODEL), lambda i: (i, 0)),
                   pl.BlockSpec((tm, LANES), lambda i: (i, 0))),
        compiler_params=_params(("arbitrary",)),
        name="rms_ab",
    )(h, norm_w, w_ab)


def _mm_in_kernel(x_ref, w_ref, o_ref, wb_ref):
    @pl.when(pl.program_id(1) == 0)
    def _():
        wb_ref[...] = w_ref[...].astype(BF16)

    o_ref[...] = _dot(x_ref[...], wb_ref[...]).astype(o_ref.dtype)


def _mm_in(xn, w, n_tiles, out_dtype):
    T, K = xn.shape
    tm = _row_tile(T, 2304)
    tn = 1024
    N = n_tiles * tn
    assert N <= w.shape[1]
    return pl.pallas_call(
        _mm_in_kernel,
        out_shape=jax.ShapeDtypeStruct((T, N), out_dtype),
        grid=(N // tn, T // tm),
        in_specs=[pl.BlockSpec((tm, K), lambda j, i: (i, 0)),
                  pl.BlockSpec((K, tn), lambda j, i: (0, j))],
        out_specs=pl.BlockSpec((tm, tn), lambda j, i: (i, j)),
        scratch_shapes=[pltpu.VMEM((K, tn), BF16)],
        compiler_params=_params(("arbitrary", "arbitrary")),
        name="in_proj",
    )(xn, w)


def _ln_silu(x, g, b):
    mu = jnp.mean(x, axis=-1, keepdims=True)
    xc = x - mu
    var = jnp.mean(xc * xc, axis=-1, keepdims=True)
    y = xc * lax.rsqrt(var + EPS) * g + b
    return y * _sigmoid(y)


def _conf_prompt_kernel(pa_ref, pb_ref, ha_ref, hb_ref, wdw_ref, bdw_ref, lng_ref, lnb_ref,
                        c_ref, ust_ref, ubuf, cbuf, *, tl, rt, ct):
    t = pl.program_id(1)
    u = pa_ref[...] * _sigmoid(pb_ref[...])
    uh = ha_ref[...] * _sigmoid(hb_ref[...])
    ubuf[0:HALO, :] = jnp.where(t > 0, uh, 0.0)
    ubuf[HALO:, :] = u
    first = HALO - (CONV_W - 1)
    for r0 in range(0, tl, rt):
        for c0 in range(0, D_MODEL, ct):
            acc = jnp.zeros((rt, ct), F32)
            for s in range(8):
                part = None
                for w in range(CONV_W):
                    if (first + w) % 8 != s:
                        continue
                    base = r0 + (first + w) // 8 * 8
                    term = ubuf[base:base + rt + (8 if s else 0), c0:c0 + ct] * wdw_ref[w:w + 1, c0:c0 + ct]
                    part = term if part is None else part + term
                if part is not None:
                    acc = acc + part[s:s + rt, :]
            cbuf[r0:r0 + rt, c0:c0 + ct] = acc + bdw_ref[:, c0:c0 + ct]
    c_ref[...] = _ln_silu(cbuf[...], lng_ref[...], lnb_ref[...]).astype(BF16)

    @pl.when(t == pl.num_programs(1) - 1)
    def _():
        ust_ref[0] = ubuf[tl:tl + HALO, :]


def _conf_prompt(p, B, LP, w_dw, b_dw, ln_g, ln_b):
    tl = 192 if LP % 192 == 0 else CHUNK
    nt = LP // tl
    hb = tl // HALO
    kern = functools.partial(_conf_prompt_kernel, tl=tl, rt=64, ct=128)
    halo_idx = lambda b, t: (jnp.maximum((b * nt + t) * hb - 1, 0), 0)
    halo_idx1 = lambda b, t: (jnp.maximum((b * nt + t) * hb - 1, 0), 1)
    vec = lambda: pl.BlockSpec((1, D_MODEL), lambda b, t: (0, 0))
    return pl.pallas_call(
        kern,
        out_shape=(jax.ShapeDtypeStruct((B * LP, D_MODEL), BF16),
                   jax.ShapeDtypeStruct((B, HALO, D_MODEL), F32)),
        grid=(B, nt),
        in_specs=[pl.BlockSpec((tl, D_MODEL), lambda b, t: (b * nt + t, 0)),
                  pl.BlockSpec((tl, D_MODEL), lambda b, t: (b * nt + t, 1)),
                  pl.BlockSpec((HALO, D_MODEL), halo_idx),
                  pl.BlockSpec((HALO, D_MODEL), halo_idx1),
                  pl.BlockSpec((CONV_W, D_MODEL), lambda b, t: (0, 0)),
                  vec(), vec(), vec()],
        out_specs=(pl.BlockSpec((tl, D_MODEL), lambda b, t: (b * nt + t, 0)),
                   pl.BlockSpec((1, HALO, D_MODEL), lambda b, t: (b, 0, 0))),
        scratch_shapes=[pltpu.VMEM((HALO + tl, D_MODEL), F32), pltpu.VMEM((tl, D_MODEL), F32)],
        compiler_params=_params(("arbitrary", "arbitrary")),
        name="conf_prompt",
    )(p, p, p, p, w_dw, b_dw, ln_g, ln_b)


def _conf_sample_kernel(st_ref, pa_ref, pb_ref, wdw_ref, bdw_ref, lng_ref, lnb_ref,
                        c_ref, nst_ref, xh, *, sb, ls):
    hist = CONV_W - 1
    for s in range(sb):
        u = pa_ref[s] * _sigmoid(pb_ref[s])
        xh[0:hist, :] = st_ref[s]
        xh[hist:hist + ls, :] = u
        acc = jnp.zeros((ls, D_MODEL), F32)
        for w in range(CONV_W):
            acc = acc + xh[w:w + ls, :] * wdw_ref[w:w + 1, :]
        c_ref[s] = _ln_silu(acc + bdw_ref[...], lng_ref[...], lnb_ref[...])
        nst_ref[s] = xh[ls:ls + hist, :]


def _conf_sample(p_s3, state, w_dw, b_dw, ln_g, ln_b):
    NB, ls, _ = p_s3.shape
    hist = CONV_W - 1
    sb = 8 if NB % 8 == 0 else 1
    kern = functools.partial(_conf_sample_kernel, sb=sb, ls=ls)
    vec = lambda: pl.BlockSpec((1, D_MODEL), lambda i: (0, 0))
    return pl.pallas_call(
        kern,
        out_shape=(jax.ShapeDtypeStruct((NB, ls, D_MODEL), F32),
                   jax.ShapeDtypeStruct((NB, hist, D_MODEL), F32)),
        grid=(NB // sb,),
        in_specs=[pl.BlockSpec((sb, hist, D_MODEL), lambda i: (i, 0, 0)),
                  pl.BlockSpec((sb, ls, D_MODEL), lambda i: (i, 0, 0)),
                  pl.BlockSpec((sb, ls, D_MODEL), lambda i: (i, 0, 1)),
                  pl.BlockSpec((CONV_W, D_MODEL), lambda i: (0, 0)),
                  vec(), vec(), vec()],
        out_specs=(pl.BlockSpec((sb, ls, D_MODEL), lambda i: (i, 0, 0)),
                   pl.BlockSpec((sb, hist, D_MODEL), lambda i: (i, 0, 0))),
        scratch_shapes=[pltpu.VMEM((hist + ls + 8, D_MODEL), F32)],
        compiler_params=_params(("arbitrary",)),
        name="conf_sample",
    )(state, p_s3, p_s3, w_dw, b_dw, ln_g, ln_b)


def _split(a):
    hi = a.astype(BF16)
    return hi, (a - hi.astype(F32)).astype(BF16)


def _mm3(a, b):
    ah, al = a
    bh, bl = b
    return _dot(jnp.concatenate([ah, al, ah], axis=1), jnp.concatenate([bh, bh, bl], axis=0))


def _tri_inverse(ms, i, j, C, nil):
    same = lambda n: (i >> (n.bit_length() - 1)) == (j >> (n.bit_length() - 1))
    base = min(16, C)
    eye = (i == j).astype(F32)
    bdot = lambda a, b: _dot(a.astype(BF16), b.astype(BF16))
    dps = [jnp.where(same(base), m, 0.0) for m in ms]
    xs = [eye - d for d in dps]
    for _ in range(max(0, (min(base, nil) - 1).bit_length() - 1)):
        dps = [bdot(d, d) for d in dps]
        xs = [x + bdot(d, x) for d, x in zip(dps, xs)]
    blk = base
    while blk < C:
        sel = jnp.logical_and(same(2 * blk), jnp.logical_not(same(blk)))
        ys = [bdot(jnp.where(sel, m, 0.0), x) for m, x in zip(ms, xs)]
        xs = [x - bdot(x, y) for x, y in zip(xs, ys)]
        blk *= 2
    xsp = [_split(x) for x in xs]
    res = [eye - x - _mm3(_split(m), xp) for m, x, xp in zip(ms, xs, xsp)]
    return [x + _dot(xp[0], r.astype(BF16)) for x, xp, r in zip(xs, xsp, res)]


def _gdn_chunks(seqs, alog_ref, dtb_ref, nw_ref, nil):
    C = seqs[0][0].shape[0]
    G = STACK // C
    ri = lax.broadcasted_iota(jnp.int32, (C, C), 0)
    ci = lax.broadcasted_iota(jnp.int32, (C, C), 1)
    tril = (ri >= ci).astype(BF16)
    i = lax.broadcasted_iota(jnp.int32, (STACK, STACK), 0)
    j = lax.broadcasted_iota(jnp.int32, (STACK, STACK), 1)
    shift = C.bit_length() - 1
    same = (i >> shift) == (j >> shift)
    causal = jnp.logical_and(same, i >= j)
    strict = jnp.logical_and(same, i > j)

    pre = []
    for n, (xq, xk, xv, z, ab, valid, s_ref) in enumerate(seqs):
        ok = valid > 0.5
        xa = ab + dtb_ref[...]
        softplus = jnp.maximum(xa, 0.0) + jnp.log(1.0 + jnp.exp(-jnp.abs(xa)))
        g_all = jnp.where(ok, -jnp.exp(alog_ref[...]) * softplus, 0.0)
        beta_all = jnp.where(ok, _sigmoid(ab), 0.0)
        g1 = g_all.astype(BF16)
        r1 = g_all - g1.astype(F32)
        g2 = r1.astype(BF16)
        g3 = (r1 - g2.astype(F32)).astype(BF16)
        gc_all = _dot(tril, g1) + _dot(tril, g2) + _dot(tril, g3)
        ok_st = jnp.concatenate([valid] * G, axis=0) > 0.5
        for h0 in range(0, DN_HEADS, G):
            heads = list(range(h0, h0 + G))
            stack = lambda x: jnp.concatenate([x[:, h * DN_DK:(h + 1) * DN_DK] for h in heads], axis=0)
            col = lambda a, off: jnp.concatenate([a[:, off + h:off + h + 1] for h in heads], axis=0)
            q = stack(xq)
            k = stack(xk)
            q = jnp.where(ok_st, q * lax.rsqrt(jnp.sum(q * q, axis=-1, keepdims=True) + EPS) * (DN_DK ** -0.5), 0.0)
            k = jnp.where(ok_st, k * lax.rsqrt(jnp.sum(k * k, axis=-1, keepdims=True) + EPS), 0.0)
            v = jnp.where(ok_st, stack(xv), 0.0)
            gc = col(gc_all, 0)
            beta = col(beta_all, DN_HEADS)
            g_last = jnp.concatenate([jnp.broadcast_to(gc_all[C - 1:C, h:h + 1], (C, 1)) for h in heads], axis=0)
            gb = jnp.broadcast_to(gc, (STACK, STACK))
            decay = jnp.where(causal, jnp.exp(jnp.where(causal, gb - gb.T, 0.0)), 0.0)
            egc = jnp.exp(gc)
            kb = k * beta
            pre.append(dict(n=n, heads=heads, s_ref=s_ref, q=q, kb=kb, kbf=k.astype(BF16), decay=decay, egc=egc,
                            rhs=jnp.concatenate([v * beta, kb * egc], axis=1),
                            k_dec=(k * jnp.exp(g_last - gc)).astype(BF16), zs=stack(z),
                            s_decay=[jnp.exp(gc_all[C - 1:C, h:h + 1]) for h in heads]))
    ms = [jnp.where(strict, _dot_nt(p["kb"].astype(BF16), p["kbf"]) * p["decay"], 0.0) for p in pre]
    qks = [jnp.where(causal, _dot_nt(p["q"].astype(BF16), p["kbf"]) * p["decay"], 0.0).astype(BF16) for p in pre]
    invs = _tri_inverse(ms, i, j, C, nil)
    sols = [_mm3(_split(inv), _split(p["rhs"])) for inv, p in zip(invs, pre)]
    wss = []
    for p, sol in zip(pre, sols):
        w = sol[:, DN_DV:].astype(BF16)
        q_dec = (p["q"] * p["egc"]).astype(BF16)
        wss.append([_dot(jnp.concatenate([w[g * C:(g + 1) * C], q_dec[g * C:(g + 1) * C]], axis=0),
                         p["s_ref"][h].astype(BF16)) for g, h in enumerate(p["heads"])])
    outs = [[None] * DN_HEADS for _ in seqs]
    for p, sol, ws, qk in zip(pre, sols, wss, qks):
        s_ref = p["s_ref"]
        v_new = [(sol[g * C:(g + 1) * C, :DN_DV] - ws[g][:C]).astype(BF16) for g in range(G)]
        for g, h in enumerate(p["heads"]):
            s_ref[h] = s_ref[h] * p["s_decay"][g] + _dot_tn(p["k_dec"][g * C:(g + 1) * C], v_new[g])
        o = jnp.concatenate([w[C:] for w in ws], axis=0) + _dot(qk, jnp.concatenate(v_new, axis=0))
        o = o * lax.rsqrt(jnp.mean(o * o, axis=-1, keepdims=True) + EPS) * nw_ref[...]
        og = o * (p["zs"] * _sigmoid(p["zs"]))
        for g, h in enumerate(p["heads"]):
            outs[p["n"]][h] = og[g * C:(g + 1) * C]
    return outs


def _short_conv_silu(xbuf, wc_ref, rows):
    first = 8 - (SHORT_W - 1)
    acc = xbuf[first:first + rows, :] * wc_ref[0:1, :]
    for w in range(1, SHORT_W):
        acc = acc + xbuf[first + w:first + w + rows, :] * wc_ref[w:w + 1, :]
    return acc * _sigmoid(acc)


def _gdn_prompt_kernel(*refs, ns):
    seq_refs = [refs[5 * s:5 * s + 5] for s in range(ns)]
    wc_ref, alog_ref, dtb_ref, nw_ref, o_ref, s_ref, xbuf = refs[5 * ns:]
    c = pl.program_id(1)
    n_qk = DN_HEADS * DN_DK

    @pl.when(c == 0)
    def _():
        s_ref[...] = jnp.zeros_like(s_ref)
        xbuf[:, 0:8, :] = jnp.zeros((ns, 8, xbuf.shape[2]), F32)

    rows = lax.broadcasted_iota(jnp.int32, (CHUNK, 1), 0)
    valid = jnp.logical_or(rows >= FRONT, c > 0).astype(F32)
    seqs = []
    for s, (q_ref, k_ref, v_ref, z_ref, ab_ref) in enumerate(seq_refs):
        xb = xbuf.at[s]
        xb[8:8 + CHUNK, 0:n_qk] = q_ref[...]
        xb[8:8 + CHUNK, n_qk:2 * n_qk] = k_ref[...]
        xb[8:8 + CHUNK, 2 * n_qk:] = v_ref[...]
        x = _short_conv_silu(xb, wc_ref, CHUNK)
        xb[0:8, :] = xb[CHUNK:CHUNK + 8, :]
        seqs.append((x[:, 0:n_qk], x[:, n_qk:2 * n_qk], x[:, 2 * n_qk:], z_ref[...].astype(F32), ab_ref[...],
                     valid, s_ref.at[s, 0]))
    outs = _gdn_chunks(seqs, alog_ref, dtb_ref, nw_ref, nil=CHUNK)
    for s in range(ns):
        for h in range(DN_HEADS):
            o_ref[s, :, h * DN_DV:(h + 1) * DN_DV] = outs[s][h].astype(BF16)


def _gdn_prompt(p, pg, ab, B, LP, w_conv, alog, dtb, nw):
    nc = LP // CHUNK
    n_qk = DN_HEADS * DN_DK
    ns = 4 if B % 4 == 0 else (2 if B % 2 == 0 else 1)
    groups = B // ns
    vec = lambda n: pl.BlockSpec((1, n), lambda g, c: (0, 0))
    in_specs, args = [], []
    for s in range(ns):
        for src, col in ((p, 2), (p, 3), (p, 4), (pg, 0)):
            in_specs.append(pl.BlockSpec((CHUNK, n_qk), lambda g, c, s=s, col=col: ((s * groups + g) * nc + c, col)))
            args.append(src)
        in_specs.append(pl.BlockSpec((CHUNK, LANES), lambda g, c, s=s: ((s * groups + g) * nc + c, 0)))
        args.append(ab)
    in_specs += [pl.BlockSpec((SHORT_W, 3 * n_qk), lambda g, c: (0, 0)), vec(LANES), vec(LANES), vec(DN_DV)]
    og, s_out = pl.pallas_call(
        functools.partial(_gdn_prompt_kernel, ns=ns),
        out_shape=(jax.ShapeDtypeStruct((ns, groups * LP, n_qk), BF16),
                   jax.ShapeDtypeStruct((ns, groups, DN_HEADS, DN_DK, DN_DV), F32)),
        grid=(groups, nc),
        in_specs=in_specs,
        out_specs=(pl.BlockSpec((ns, CHUNK, n_qk), lambda g, c: (0, g * nc + c, 0)),
                   pl.BlockSpec((ns, 1, DN_HEADS, DN_DK, DN_DV), lambda g, c: (0, g, 0, 0, 0))),
        scratch_shapes=[pltpu.VMEM((ns, CHUNK + 8, 3 * n_qk), F32)],
        compiler_params=_params(("arbitrary", "arbitrary")),
        name="gdn_prompt",
    )(*args, w_conv, alog, dtb, nw)
    return og.reshape(B * LP, n_qk), s_out.reshape(B, DN_HEADS, DN_DK, DN_DV)


def _gdn_sample_kernel(st_ref, q_ref, k_ref, v_ref, z_ref, ab_ref, s0_ref, wc_ref, alog_ref, dtb_ref, nw_ref,
                       o_ref, s_ref, xbuf, zbuf, abbuf, *, sb, ls):
    n_qk = DN_HEADS * DN_DK
    C = SAMPLE_CHUNK
    hist = SHORT_W - 1
    xbuf[...] = jnp.zeros_like(xbuf)
    zbuf[...] = jnp.zeros_like(zbuf)
    abbuf[...] = jnp.zeros_like(abbuf)
    s_ref[...] = s0_ref[...]
    valid = (lax.broadcasted_iota(jnp.int32, (C, 1), 0) < ls).astype(F32)
    seqs = []
    for s in range(sb):
        xb = xbuf.at[s]
        xb[8 - hist:8, :] = st_ref[s]
        xb[8:8 + ls, 0:n_qk] = q_ref[s]
        xb[8:8 + ls, n_qk:2 * n_qk] = k_ref[s]
        xb[8:8 + ls, 2 * n_qk:] = v_ref[s]
        zbuf[s, 0:ls, :] = z_ref[s]
        abbuf[s, 0:ls, :] = ab_ref[s]
        x = _short_conv_silu(xb, wc_ref, C)
        seqs.append((x[:, 0:n_qk], x[:, n_qk:2 * n_qk], x[:, 2 * n_qk:], zbuf[s], abbuf[s], valid, s_ref.at[s]))
    outs = _gdn_chunks(seqs, alog_ref, dtb_ref, nw_ref, nil=ls)
    for s in range(sb):
        for h in range(DN_HEADS):
            o_ref[s, :, h * DN_DV:(h + 1) * DN_DV] = outs[s][h][0:ls, :]


def _gdn_sample(p_s3, z_s3, ab_s3, st_conv, s0, w_conv, alog, dtb, nw):
    NB, ls, _ = p_s3.shape
    n_qk = DN_HEADS * DN_DK
    hist = SHORT_W - 1
    assert ls <= SAMPLE_CHUNK
    sb = 4 if NB % 4 == 0 else 1
    kern = functools.partial(_gdn_sample_kernel, sb=sb, ls=ls)
    blk = lambda col: pl.BlockSpec((sb, ls, n_qk), lambda i: (i, 0, col))
    vec = lambda n: pl.BlockSpec((1, n), lambda i: (0, 0))
    sspec = lambda: pl.BlockSpec((sb, DN_HEADS, DN_DK, DN_DV), lambda i: (i, 0, 0, 0))
    return pl.pallas_call(
        kern,
        out_shape=(jax.ShapeDtypeStruct((NB, ls, n_qk), F32),
                   jax.ShapeDtypeStruct((NB, DN_HEADS, DN_DK, DN_DV), F32)),
        grid=(NB // sb,),
        in_specs=[pl.BlockSpec((sb, hist, 3 * n_qk), lambda i: (i, 0, 0)),
                  blk(2), blk(3), blk(4), blk(0),
                  pl.BlockSpec((sb, ls, LANES), lambda i: (i, 0, 0)),
                  sspec(),
                  pl.BlockSpec((SHORT_W, 3 * n_qk), lambda i: (0, 0)),
                  vec(LANES), vec(LANES), vec(DN_DV)],
        out_specs=(pl.BlockSpec((sb, ls, n_qk), lambda i: (i, 0, 0)), sspec()),
        scratch_shapes=[pltpu.VMEM((sb, SAMPLE_CHUNK + 8, 3 * n_qk), F32),
                        pltpu.VMEM((sb, SAMPLE_CHUNK, n_qk), F32),
                        pltpu.VMEM((sb, SAMPLE_CHUNK, LANES), F32)],
        compiler_params=_params(("arbitrary",)),
        name="gdn_sample",
    )(st_conv, p_s3, p_s3, p_s3, z_s3, ab_s3, s0, w_conv, alog, dtb, nw)


def _pack_bf16_pairs(x):
    half = x.shape[1] // 2
    lo = lax.bitcast_convert_type(x[:, :half].astype(BF16).astype(F32), jnp.uint32)
    hi = lax.bitcast_convert_type(x[:, half:].astype(BF16).astype(F32), jnp.uint32)
    return jnp.bitwise_or(jnp.bitwise_and(hi, jnp.uint32(0xFFFF0000)), lax.shift_right_logical(lo, jnp.uint32(16)))


def _unpack_bf16_pairs(xp):
    lo = lax.bitcast_convert_type(lax.shift_left(xp, jnp.uint32(16)), F32)
    hi = lax.bitcast_convert_type(jnp.bitwise_and(xp, jnp.uint32(0xFFFF0000)), F32)
    return jnp.concatenate([lo, hi], axis=1).astype(BF16)


def _merge_kernel(cp_ref, cs_ref, ogp_ref, ogs_ref, ga_ref, gb_ref, h_ref, wco_ref, bco_ref, wdo_ref, wo_ref, nf_ref,
                  wr_ref, br_ref, h1_ref, xp_ref, ei_ref, gt_ref, rk_ref, cnt_ref, carry, below, *, n_p):
    step = pl.program_id(0)

    @pl.when(step == 0)
    def _():
        carry[...] = jnp.zeros_like(carry)
        ri = lax.broadcasted_iota(jnp.int32, below.shape, 0)
        ci = lax.broadcasted_iota(jnp.int32, below.shape, 1)
        below[...] = (ri > ci).astype(BF16)

    in_prompt = step < n_p
    c = jnp.where(in_prompt, cp_ref[...], cs_ref[...])
    og = jnp.where(in_prompt, ogp_ref[...], ogs_ref[...])
    ya = _dot(c, wco_ref[...]) + bco_ref[...]
    yb = _dot(og, wdo_ref[...])
    mixed = _sigmoid(ga_ref[...].astype(F32)) * ya + _sigmoid(gb_ref[...].astype(F32)) * yb
    h1 = h_ref[...] + _dot(mixed.astype(BF16), wo_ref[...])
    h1_ref[...] = h1
    xn = h1 * lax.rsqrt(jnp.mean(h1 * h1, axis=-1, keepdims=True) + EPS) * nf_ref[...]
    xp_ref[...] = _pack_bf16_pairs(xn)
    logits = _dot(xn.astype(BF16), wr_ref[...]) + br_ref[...]

    tm = logits.shape[0]
    lane = lax.broadcasted_iota(jnp.int32, (tm, LANES), 1)
    work = logits
    sels, idxs, vals = [], [], []
    for _ in range(TOP_K):
        m = jnp.max(work, axis=-1, keepdims=True)
        idx = jnp.min(jnp.where(work == m, lane, N_EXPERTS - 1), axis=-1, keepdims=True)
        sel = lane == idx
        sels.append(sel)
        idxs.append(idx)
        vals.append(m)
        work = jnp.where(sel, -jnp.inf, work)
    exps = [jnp.exp(v - vals[0]) for v in vals]
    denom = exps[0]
    for e in exps[1:]:
        denom = denom + e
    onehot = jnp.zeros((tm, LANES), F32)
    for sel in sels:
        onehot = onehot + sel.astype(F32)
    before = _dot(below[...], onehot.astype(BF16)) + carry[...]
    ei = jnp.zeros((tm, LANES), jnp.int32)
    gt = jnp.zeros((tm, LANES), F32)
    rk = jnp.zeros((tm, LANES), jnp.int32)
    for k in range(TOP_K):
        at_k = lane == k
        r_k = jnp.sum(jnp.where(sels[k], before, 0.0), axis=-1, keepdims=True).astype(jnp.int32)
        ei = jnp.where(at_k, idxs[k], ei)
        gt = jnp.where(at_k, exps[k] / denom, gt)
        rk = jnp.where(at_k, r_k, rk)
    ei_ref[...] = ei
    gt_ref[...] = gt
    rk_ref[...] = rk
    carry[...] = carry[...] + jnp.sum(onehot, axis=0, keepdims=True)
    cnt_ref[...] = carry[...]


def _merge(c_p, c_s, og_p, og_s, pg, h, wco, bco, wdo, wo, nf, wr, br):
    T = h.shape[0]
    tm = _row_tile(math.gcd(c_p.shape[0], c_s.shape[0]), 512)
    n_p = c_p.shape[0] // tm
    row = lambda col: pl.BlockSpec((tm, D_MODEL), lambda i: (i, col))
    part_p = lambda: pl.BlockSpec((tm, D_MODEL), lambda i: (jnp.minimum(i, n_p - 1), 0))
    part_s = lambda: pl.BlockSpec((tm, D_MODEL), lambda i: (jnp.maximum(i - n_p, 0), 0))
    full = lambda a, b: pl.BlockSpec((a, b), lambda i: (0, 0))
    lanes = lambda: pl.BlockSpec((tm, LANES), lambda i: (i, 0))
    return pl.pallas_call(
        functools.partial(_merge_kernel, n_p=n_p),
        out_shape=(jax.ShapeDtypeStruct((T, D_MODEL), F32),
                   jax.ShapeDtypeStruct((T, D_MODEL // 2), jnp.uint32),
                   jax.ShapeDtypeStruct((T, LANES), jnp.int32),
                   jax.ShapeDtypeStruct((T, LANES), F32),
                   jax.ShapeDtypeStruct((T, LANES), jnp.int32),
                   jax.ShapeDtypeStruct((1, LANES), F32)),
        grid=(T // tm,),
        in_specs=[part_p(), part_s(), part_p(), part_s(), row(1), row(2), row(0),
                  full(D_MODEL, D_MODEL), full(1, D_MODEL), full(D_MODEL, D_MODEL), full(D_MODEL, D_MODEL),
                  full(1, D_MODEL), full(D_MODEL, LANES), full(1, LANES)],
        out_specs=(row(0), pl.BlockSpec((tm, D_MODEL // 2), lambda i: (i, 0)), lanes(), lanes(), lanes(),
                   full(1, LANES)),
        scratch_shapes=[pltpu.VMEM((1, LANES), F32), pltpu.VMEM((tm, tm), BF16)],
        compiler_params=_params(("arbitrary",)),
        name="merge",
    )(c_p, c_s, og_p, og_s, pg, pg, h, wco, bco, wdo, wo, nf, wr, br)


ISSUE_UNROLL = 8


def _row_copy(src, src_row, dst, dst_row, sem):
    return pltpu.make_async_copy(src.at[pl.ds(src_row, 1), :], dst.at[pl.ds(dst_row, 1), :], sem)


def _dispatch_kernel(dest_ref, x_ref, xb_out, sem, *, tm):
    def issue(t, carry):
        for k in range(TOP_K):
            _row_copy(x_ref, t, xb_out, dest_ref[t * TOP_K + k], sem).start(priority=k % 2)
        return carry

    lax.fori_loop(0, tm, issue, 0, unroll=ISSUE_UNROLL)
    for k in range(TOP_K):
        pltpu.make_async_copy(x_ref, xb_out.at[pl.ds(0, tm), :], sem).wait()


def _dispatch(dest, xp, n_rows):
    T, W = xp.shape
    tm = _row_tile(T, 512)
    return pl.pallas_call(
        functools.partial(_dispatch_kernel, tm=tm),
        out_shape=jax.ShapeDtypeStruct((n_rows, W), xp.dtype),
        grid=(T // tm,),
        in_specs=[pl.BlockSpec((tm * TOP_K,), lambda i: (i,), memory_space=pltpu.SMEM),
                  pl.BlockSpec((tm, W), lambda i: (i, 0))],
        out_specs=pl.BlockSpec(memory_space=pl.ANY),
        scratch_shapes=[pltpu.SemaphoreType.DMA],
        compiler_params=_params(("arbitrary",)),
        name="moe_dispatch",
    )(dest, xp)


def _moe_kernel(be_ref, nv_ref, first_ref, slot_ref, next_ref, x_ref, wu_hbm, bu_ref, wd_hbm, bd_ref, o_ref,
                wu32, wd32, wub, wdb, sems):
    i = pl.program_id(0)

    def fetch(e, s):
        pltpu.make_async_copy(wu_hbm.at[e], wu32.at[s], sems.at[0, s]).start()
        pltpu.make_async_copy(wd_hbm.at[e], wd32.at[s], sems.at[1, s]).start()

    @pl.when(i == 0)
    def _():
        fetch(be_ref[0], 0)

    @pl.when(first_ref[i] == 1)
    def _():
        s = slot_ref[i]
        pltpu.make_async_copy(wu_hbm.at[0], wu32.at[s], sems.at[0, s]).wait()
        pltpu.make_async_copy(wd_hbm.at[0], wd32.at[s], sems.at[1, s]).wait()
        wub[...] = wu32[s].astype(BF16)
        wdb[...] = wd32[s].astype(BF16)

        @pl.when(next_ref[i] >= 0)
        def _():
            fetch(next_ref[i], 1 - s)

    @pl.when(nv_ref[i] > 0)
    def _():
        rows = lax.broadcasted_iota(jnp.int32, (MOE_ROWS, 1), 0)
        xp = jnp.where(rows < nv_ref[i], x_ref[...], jnp.uint32(0))
        hmid = _dot(_unpack_bf16_pairs(xp), wub[...]) + bu_ref[0]
        hg = jnp.minimum(hmid[:, :D_FF], SWIGLU_LIMIT)
        hl = jnp.clip(hmid[:, D_FF:], -SWIGLU_LIMIT, SWIGLU_LIMIT)
        act = hg * _sigmoid(SWIGLU_ALPHA * hg) * (hl + 1.0)
        o_ref[...] = _dot(act.astype(BF16), wdb[...]) + bd_ref[0]

    @pl.when(nv_ref[i] == 0)
    def _():
        o_ref[...] = jnp.zeros_like(o_ref)


def _moe(plan, xb, w_up, b_up, w_down, b_down):
    R = xb.shape[0]
    nb = R // MOE_ROWS
    bias = lambda n: pl.BlockSpec((1, 1, n), lambda i, be, *_: (be[i], 0, 0))
    grid_spec = pltpu.PrefetchScalarGridSpec(
        num_scalar_prefetch=len(plan),
        grid=(nb,),
        in_specs=[pl.BlockSpec((MOE_ROWS, D_MODEL // 2), lambda i, *_: (i, 0)),
                  pl.BlockSpec(memory_space=pl.ANY), bias(2 * D_FF),
                  pl.BlockSpec(memory_space=pl.ANY), bias(D_MODEL)],
        out_specs=pl.BlockSpec((MOE_ROWS, D_MODEL), lambda i, *_: (i, 0)),
        scratch_shapes=[pltpu.VMEM((2, D_MODEL, 2 * D_FF), F32), pltpu.VMEM((2, D_FF, D_MODEL), F32),
                        pltpu.VMEM((D_MODEL, 2 * D_FF), BF16), pltpu.VMEM((D_FF, D_MODEL), BF16),
                        pltpu.SemaphoreType.DMA((2, 2))],
    )
    return pl.pallas_call(
        _moe_kernel,
        out_shape=jax.ShapeDtypeStruct((R, D_MODEL), F32),
        grid_spec=grid_spec,
        compiler_params=pltpu.CompilerParams(dimension_semantics=("arbitrary",), vmem_limit_bytes=MOE_VMEM_LIMIT),
        name="moe_experts",
    )(*plan, xb, w_up, b_up, w_down, b_down)


def _dispatch_plan(counts, ei, rk, T):
    A = T * TOP_K
    n_blocks = -(-A // MOE_ROWS) + N_EXPERTS
    counts = counts[0, :N_EXPERTS].astype(jnp.int32)
    padded = (counts + MOE_ROWS - 1) // MOE_ROWS * MOE_ROWS
    pend = jnp.cumsum(padded)
    pstart = pend - padded
    experts = jnp.arange(N_EXPERTS, dtype=jnp.int32)
    seg_start = jnp.sum(jnp.where(ei[:, :TOP_K, None] == experts, pstart, 0), axis=-1)
    dest = (seg_start + rk[:, :TOP_K]).reshape(-1).astype(jnp.int32)
    starts = jnp.arange(n_blocks, dtype=jnp.int32) * MOE_ROWS
    block_e = jnp.minimum(jnp.sum((pend[None, :] <= starts[:, None]).astype(jnp.int32), axis=1), N_EXPERTS - 1)
    seg_end = jnp.sum(jnp.where(block_e[:, None] == experts, pstart + counts, 0), axis=-1)
    rows_valid = jnp.clip(seg_end - starts, 0, MOE_ROWS).astype(jnp.int32)
    prev_e = jnp.concatenate([jnp.full((1,), -1, jnp.int32), block_e[:-1]])
    first = jnp.logical_and(block_e != prev_e, rows_valid > 0).astype(jnp.int32)
    slot = (jnp.cumsum(first) - 1) % 2
    later = jnp.logical_and(experts[None, :] > experts[:, None], (counts > 0)[None, :])
    next_of = jnp.min(jnp.where(later, experts[None, :], N_EXPERTS), axis=1)
    next_of = jnp.where(next_of == N_EXPERTS, -1, next_of)
    next_e = jnp.sum(jnp.where(block_e[:, None] == experts, next_of, 0), axis=-1)
    plan = tuple(a.astype(jnp.int32) for a in (block_e, rows_valid, first, slot, next_e))
    return dest, plan, n_blocks * MOE_ROWS


def _combine_kernel(dest_ref, dnext_ref, gt_ref, h_ref, nw_ref, yb_ref, yp_ref, ys_ref, buf, res, sems, osems,
                    *, tm, piece, lp, seq, n_prompt_rows):
    i = pl.program_id(0)
    last = pl.num_programs(0) - 1
    slot = lax.rem(i, 2)
    n_head = FRONT + N_META

    def request(d_ref, s):
        def issue(t, carry):
            for k in range(TOP_K):
                _row_copy(yb_ref, d_ref[t * TOP_K + k], buf.at[s, k], t, sems.at[s]).start(priority=k % 2)
            return carry

        lax.fori_loop(0, tm, issue, 0, unroll=ISSUE_UNROLL)

    def writes(tile, s, start):
        for j in range(tm // piece):
            r = tile * tm + j * piece
            src = res.at[s, pl.ds(j * piece, piece), :]
            b = lax.div(r, lp)
            off = r - b * lp

            @pl.when(jnp.logical_and(r < n_prompt_rows, off >= n_head))
            def _():
                cp = pltpu.make_async_copy(
                    src, yp_ref.at[pl.ds(pl.multiple_of(b * seq + off - n_head, 8), piece), :], osems.at[s])
                cp.start() if start else cp.wait()

            @pl.when(r >= n_prompt_rows)
            def _():
                cp = pltpu.make_async_copy(
                    src, ys_ref.at[pl.ds(pl.multiple_of(r - n_prompt_rows, 8), piece), :], osems.at[s])
                cp.start() if start else cp.wait()

    @pl.when(i == 0)
    def _():
        request(dest_ref, 0)

    @pl.when(i + 1 <= last)
    def _():
        request(dnext_ref, 1 - slot)

    @pl.when(i > 0)
    def _():
        writes(i - 1, 1 - slot, start=False)

    for k in range(TOP_K):
        pltpu.make_async_copy(yb_ref.at[pl.ds(0, tm), :], buf.at[slot, k], sems.at[slot]).wait()
    gt = gt_ref[...]
    x = h_ref[...]
    for k in range(TOP_K):
        x = x + gt[:, k:k + 1] * buf[slot, k]
    res[slot] = x * lax.rsqrt(jnp.mean(x * x, axis=-1, keepdims=True) + EPS) * nw_ref[...]
    writes(i, slot, start=True)

    @pl.when(i == last)
    def _():
        writes(i, slot, start=False)


def _combine(dest, gt, h1, nw, yb, B, LP, SEQ):
    T = h1.shape[0]
    n_prompt_rows = B * LP
    n_sample_rows = T - n_prompt_rows
    tm = _row_tile(T, 512)
    n = T // tm
    piece = math.gcd(math.gcd(CHUNK, n_sample_rows), tm)
    assert piece % 8 == 0
    row = lambda: pl.BlockSpec((tm, D_MODEL), lambda i: (i, 0))
    return pl.pallas_call(
        functools.partial(_combine_kernel, tm=tm, piece=piece, lp=LP, seq=SEQ, n_prompt_rows=n_prompt_rows),
        out_shape=(jax.ShapeDtypeStruct((B * SEQ, D_MODEL), F32),
                   jax.ShapeDtypeStruct((n_sample_rows, D_MODEL), F32)),
        grid=(n,),
        in_specs=[pl.BlockSpec((tm * TOP_K,), lambda i: (i,), memory_space=pltpu.SMEM),
                  pl.BlockSpec((tm * TOP_K,), lambda i: (jnp.minimum(i + 1, n - 1),), memory_space=pltpu.SMEM),
                  pl.BlockSpec((tm, LANES), lambda i: (i, 0)),
                  row(),
                  pl.BlockSpec((1, D_MODEL), lambda i: (0, 0)),
                  pl.BlockSpec(memory_space=pl.ANY)],
        out_specs=(pl.BlockSpec(memory_space=pl.ANY), pl.BlockSpec(memory_space=pl.ANY)),
        scratch_shapes=[pltpu.VMEM((2, TOP_K, tm, D_MODEL), F32), pltpu.VMEM((2, tm, D_MODEL), F32),
                        pltpu.SemaphoreType.DMA((2,)), pltpu.SemaphoreType.DMA((2,))],
        compiler_params=_params(("arbitrary",)),
        name="moe_combine_final",
    )(dest, dest, gt, h1, nw, yb)


def _pad_lanes(v, fill=0.0):
    v = v.reshape(1, -1).astype(F32)
    return jnp.pad(v, ((0, 0), (0, LANES - v.shape[1])), constant_values=fill)


def kernel(x_prompt, x_sample, state_conf_conv, state_dn_conv, state_dn_S, meta_tokens, norm_mix, w_in, w_conf_dw, b_conf_dw, ln_conf_g, ln_conf_b, w_conf_out, b_conf_out, w_dn_conv, dn_a_log, dn_dt_bias, dn_norm_w, w_dn_out, w_out, norm_ffn, w_router, b_router, w_up, b_up, w_down, b_down, norm_final):
    B, SEQ, D = x_prompt.shape
    NB, LS, _ = x_sample.shape
    depth = w_in.shape[0]
    assert D == D_MODEL and depth == 1 and SEQ % CHUNK == 0 and LS >= SHORT_W - 1
    LP = FRONT + N_META + SEQ
    TP = B * LP
    T = TP + NB * LS
    n_qk = DN_HEADS * DN_DK
    o_q = 2 * D_MODEL
    o_a = o_q + 4 * n_qk
    o_gate = o_a + 2 * DN_HEADS

    h0 = _tokens(x_prompt, x_sample.reshape(NB * LS, D), meta_tokens.astype(F32))

    w_in0 = w_in[0]
    o_z = o_q + 3 * n_qk
    w_gate = jnp.concatenate([w_in0[:, o_z:o_a], w_in0[:, o_gate:]], axis=1)
    w_ab = jnp.pad(w_in0[:, o_a:o_gate], ((0, 0), (0, LANES - 2 * DN_HEADS))).astype(BF16)
    wco = w_conf_out[0].astype(BF16)
    wdo = w_dn_out[0].astype(BF16)
    wo = w_out[0].astype(BF16)
    wr = jnp.pad(w_router[0], ((0, 0), (0, LANES - N_EXPERTS))).astype(BF16)
    br = _pad_lanes(b_router[0], fill=-1e30)
    alog = _pad_lanes(dn_a_log[0])
    dtb = _pad_lanes(dn_dt_bias[0])
    row = lambda v: v.reshape(1, -1).astype(F32)

    xn, ab = _rms_ab(h0, row(norm_mix[0]), w_ab)
    n_main = o_z // D_MODEL
    p = _mm_in(xn, w_in0, n_main, F32)
    pg = _mm_in(xn, w_gate, 3, BF16)
    p_s3 = p[TP:].reshape(NB, LS, n_main * D_MODEL)
    z_s3 = pg[TP:, :n_qk].astype(F32).reshape(NB, LS, n_qk)
    ab_s3 = ab[TP:].reshape(NB, LS, LANES)

    c_p, ust_p = _conf_prompt(p, B, LP, w_conf_dw[0], row(b_conf_dw[0]), row(ln_conf_g[0]), row(ln_conf_b[0]))
    c_s, conf_state_s = _conf_sample(p_s3, state_conf_conv[0], w_conf_dw[0], row(b_conf_dw[0]),
                                     row(ln_conf_g[0]), row(ln_conf_b[0]))

    og_p, s_p = _gdn_prompt(p, pg, ab, B, LP, w_dn_conv[0], alog, dtb, row(dn_norm_w[0]))
    og_s, s_s = _gdn_sample(p_s3, z_s3, ab_s3, state_dn_conv[0], state_dn_S[0], w_dn_conv[0], alog, dtb,
                            row(dn_norm_w[0]))

    h1, xp2, ei, gt, rk, counts = _merge(c_p, c_s.reshape(NB * LS, D).astype(BF16),
                                         og_p, og_s.reshape(NB * LS, n_qk).astype(BF16), pg, h0,
                                         wco, row(b_conf_out[0]), wdo, wo, row(norm_ffn[0]), wr, br)

    dest, plan, R = _dispatch_plan(counts, ei, rk, T)
    xb = _dispatch(dest, xp2, R)
    yb = _moe(plan, xb, w_up[0], b_up[0].reshape(N_EXPERTS, 1, -1), w_down[0], b_down[0].reshape(N_EXPERTS, 1, -1))
    yp, ys = _combine(dest, gt, h1, row(norm_final), yb, B, LP, SEQ)

    y_prompt = yp.reshape(B, SEQ, D)
    y_sample = ys.reshape(NB, LS, D)
    hist = CONV_W - 1
    conf_conv_prompt = ust_p[:, HALO - hist:][None]
    dn_conv_prompt = jnp.stack([p[(b + 1) * LP - (SHORT_W - 1):(b + 1) * LP, o_q:o_q + 3 * n_qk]
                                for b in range(B)])[None]
    dn_conv_sample = p_s3[:, LS - (SHORT_W - 1):, o_q:o_q + 3 * n_qk][None]
    return (y_prompt, y_sample, conf_conv_prompt, dn_conv_prompt, s_p[None],
            conf_state_s[None], dn_conv_sample, s_s[None])
```

```python
import functools
import math

import jax
import jax.numpy as jnp
from jax import lax
from jax.experimental import pallas as pl
from jax.experimental.pallas import tpu as pltpu

D_MODEL = 1024
N_META = 16
CONV_W = 31
SHORT_W = 4
DN_HEADS = 8
DN_DK = 128
DN_DV = 128
CHUNK = 64
N_EXPERTS = 32
TOP_K = 4
D_FF = 1024
SWIGLU_LIMIT = 7.0
SWIGLU_ALPHA = 1.702
EPS = 1e-6

FRONT = (-N_META) % CHUNK
SAMPLE_CHUNK = 16
STACK = 128
LANES = 128
HALO = 32
MOE_ROWS = 512
VMEM_LIMIT = 48 * 1024 * 1024
MOE_VMEM_LIMIT = 58 * 1024 * 1024

F32 = jnp.float32
BF16 = jnp.bfloat16


def _row_tile(n, pref):
    best = 16
    for t in range(16, min(n, pref) + 1, 16):
        if n % t == 0:
            best = t
    assert n % best == 0
    return best


def _sigmoid(x):
    return 1.0 / (1.0 + jnp.exp(-x))


def _dot(a, b):
    return jnp.dot(a, b, preferred_element_type=F32)


def _dot_nt(a, b):
    return lax.dot_general(a, b, (((1,), (1,)), ((), ())), preferred_element_type=F32)


def _dot_tn(a, b):
    return lax.dot_general(a, b, (((0,), (0,)), ((), ())), preferred_element_type=F32)


def _params(sem):
    return pltpu.CompilerParams(dimension_semantics=sem, vmem_limit_bytes=VMEM_LIMIT)


def _tokens_kernel(xp_ref, xs_ref, meta_ref, h_ref, head, sems, *, lp, tr, n_prompt_rows):
    b = pl.program_id(0)
    j = pl.program_id(1)
    n_head = FRONT + N_META
    row0 = pl.multiple_of(b * lp, 8)
    body = pltpu.make_async_copy(xp_ref.at[0], h_ref.at[pl.ds(row0 + n_head + j * tr, tr), :], sems.at[0])
    body.start()

    @pl.when(j == 0)
    def _():
        head[0:FRONT, :] = jnp.zeros((FRONT, D_MODEL), F32)
        head[FRONT:, :] = meta_ref[...]
        front = pltpu.make_async_copy(head, h_ref.at[pl.ds(row0, n_head), :], sems.at[1])
        front.start()
        front.wait()

    @pl.when(jnp.logical_and(b == 0, j == 0))
    def _():
        tail = pltpu.make_async_copy(xs_ref, h_ref.at[pl.ds(n_prompt_rows, xs_ref.shape[0]), :], sems.at[2])
        tail.start()
        tail.wait()

    body.wait()


def _tokens(x_prompt, x_sample2, meta):
    B, SEQ, D = x_prompt.shape
    lp = FRONT + N_META + SEQ
    T = B * lp + x_sample2.shape[0]
    tr = _row_tile(SEQ, 1024)
    return pl.pallas_call(
        functools.partial(_tokens_kernel, lp=lp, tr=tr, n_prompt_rows=B * lp),
        out_shape=jax.ShapeDtypeStruct((T, D), F32),
        grid=(B, SEQ // tr),
        in_specs=[pl.BlockSpec((1, tr, D), lambda b, j: (b, j, 0)),
                  pl.BlockSpec(x_sample2.shape, lambda b, j: (0, 0)),
                  pl.BlockSpec((N_META, D), lambda b, j: (0, 0))],
        out_specs=pl.BlockSpec(memory_space=pl.ANY),
        scratch_shapes=[pltpu.VMEM((FRONT + N_META, D), F32), pltpu.SemaphoreType.DMA((3,))],
        compiler_params=_params(("arbitrary", "arbitrary")),
        name="token_layout",
    )(x_prompt, x_sample2, meta)


def _rms_ab_kernel(h_ref, nw_ref, wab_ref, xn_ref, ab_ref):
    x = h_ref[...]
    y = x * lax.rsqrt(jnp.mean(x * x, axis=-1, keepdims=True) + EPS) * nw_ref[...]
    yb = y.astype(BF16)
    xn_ref[...] = yb
    ab_ref[...] = _dot(yb, wab_ref[...])


def _rms_ab(h, norm_w, w_ab):
    T = h.shape[0]
    tm = _row_tile(T, 1024)
    return pl.pallas_call(
        _rms_ab_kernel,
        out_shape=(jax.ShapeDtypeStruct((T, D_MODEL), BF16), jax.ShapeDtypeStruct((T, LANES), F32)),
        grid=(T // tm,),
        in_specs=[pl.BlockSpec((tm, D_MODEL), lambda i: (i, 0)),
                  pl.BlockSpec((1, D_MODEL), lambda i: (0, 0)),
                  pl.BlockSpec((D_MODEL, LANES), lambda i: (0, 0))],
        out_specs=(pl.BlockSpec((tm, D_MODEL), lambda i: (i, 0)),
                   pl.BlockSpec((tm, LANES), lambda i: (i, 0))),
        compiler_params=_params(("arbitrary",)),
        name="rms_ab",
    )(h, norm_w, w_ab)


def _mm_in_kernel(x_ref, w_ref, o_ref, wb_ref):
    @pl.when(pl.program_id(1) == 0)
    def _():
        wb_ref[...] = w_ref[...].astype(BF16)

    o_ref[...] = _dot(x_ref[...], wb_ref[...]).astype(o_ref.dtype)


def _mm_in(xn, w, n_tiles, out_dtype):
    T, K = xn.shape
    tm = _row_tile(T, 2304)
    tn = 1024
    N = n_tiles * tn
    assert N <= w.shape[1]
    return pl.pallas_call(
        _mm_in_kernel,
        out_shape=jax.ShapeDtypeStruct((T, N), out_dtype),
        grid=(N // tn, T // tm),
        in_specs=[pl.BlockSpec((tm, K), lambda j, i: (i, 0)),
                  pl.BlockSpec((K, tn), lambda j, i: (0, j))],
        out_specs=pl.BlockSpec((tm, tn), lambda j, i: (i, j)),
        scratch_shapes=[pltpu.VMEM((K, tn), BF16)],
        compiler_params=_params(("arbitrary", "arbitrary")),
        name="in_proj",
    )(xn, w)


def _ln_silu(x, g, b):
    mu = jnp.mean(x, axis=-1, keepdims=True)
    xc = x - mu
    var = jnp.mean(xc * xc, axis=-1, keepdims=True)
    y = xc * lax.rsqrt(var + EPS) * g + b
    return y * _sigmoid(y)


def _conf_prompt_kernel(pa_ref, pb_ref, ha_ref, hb_ref, wdw_ref, bdw_ref, lng_ref, lnb_ref,
                        c_ref, ust_ref, ubuf, cbuf, *, tl, rt, ct):
    t = pl.program_id(1)
    u = pa_ref[...] * _sigmoid(pb_ref[...])
    uh = ha_ref[...] * _sigmoid(hb_ref[...])
    ubuf[0:HALO, :] = jnp.where(t > 0, uh, 0.0)
    ubuf[HALO:, :] = u
    first = HALO - (CONV_W - 1)
    for r0 in range(0, tl, rt):
        for c0 in range(0, D_MODEL, ct):
            acc = jnp.zeros((rt, ct), F32)
            for s in range(8):
                part = None
                for w in range(CONV_W):
                    if (first + w) % 8 != s:
                        continue
                    base = r0 + (first + w) // 8 * 8
                    term = ubuf[base:base + rt + (8 if s else 0), c0:c0 + ct] * wdw_ref[w:w + 1, c0:c0 + ct]
                    part = term if part is None else part + term
                if part is not None:
                    acc = acc + part[s:s + rt, :]
            cbuf[r0:r0 + rt, c0:c0 + ct] = acc + bdw_ref[:, c0:c0 + ct]
    c_ref[...] = _ln_silu(cbuf[...], lng_ref[...], lnb_ref[...]).astype(BF16)

    @pl.when(t == pl.num_programs(1) - 1)
    def _():
        ust_ref[0] = ubuf[tl:tl + HALO, :]


def _conf_prompt(p, B, LP, w_dw, b_dw, ln_g, ln_b):
    tl = 192 if LP % 192 == 0 else CHUNK
    nt = LP // tl
    hb = tl // HALO
    kern = functools.partial(_conf_prompt_kernel, tl=tl, rt=64, ct=128)
    halo_idx = lambda b, t: (jnp.maximum((b * nt + t) * hb - 1, 0), 0)
    halo_idx1 = lambda b, t: (jnp.maximum((b * nt + t) * hb - 1, 0), 1)
    vec = lambda: pl.BlockSpec((1, D_MODEL), lambda b, t: (0, 0))
    return pl.pallas_call(
        kern,
        out_shape=(jax.ShapeDtypeStruct((B * LP, D_MODEL), BF16),
                   jax.ShapeDtypeStruct((B, HALO, D_MODEL), F32)),
        grid=(B, nt),
        in_specs=[pl.BlockSpec((tl, D_MODEL), lambda b, t: (b * nt + t, 0)),
                  pl.BlockSpec((tl, D_MODEL), lambda b, t: (b * nt + t, 1)),
                  pl.BlockSpec((HALO, D_MODEL), halo_idx),
                  pl.BlockSpec((HALO, D_MODEL), halo_idx1),
                  pl.BlockSpec((CONV_W, D_MODEL), lambda b, t: (0, 0)),
                  vec(), vec(), vec()],
        out_specs=(pl.BlockSpec((tl, D_MODEL), lambda b, t: (b * nt + t, 0)),
                   pl.BlockSpec((1, HALO, D_MODEL), lambda b, t: (b, 0, 0))),
        scratch_shapes=[pltpu.VMEM((HALO + tl, D_MODEL), F32), pltpu.VMEM((tl, D_MODEL), F32)],
        compiler_params=_params(("arbitrary", "arbitrary")),
        name="conf_prompt",
    )(p, p, p, p, w_dw, b_dw, ln_g, ln_b)


def _conf_sample_kernel(st_ref, pa_ref, pb_ref, wdw_ref, bdw_ref, lng_ref, lnb_ref,
                        c_ref, nst_ref, xh, *, sb, ls):
    hist = CONV_W - 1
    for s in range(sb):
        u = pa_ref[s] * _sigmoid(pb_ref[s])
        xh[0:hist, :] = st_ref[s]
        xh[hist:hist + ls, :] = u
        acc = jnp.zeros((ls, D_MODEL), F32)
        for w in range(CONV_W):
            acc = acc + xh[w:w + ls, :] * wdw_ref[w:w + 1, :]
        c_ref[s] = _ln_silu(acc + bdw_ref[...], lng_ref[...], lnb_ref[...])
        nst_ref[s] = xh[ls:ls + hist, :]


def _conf_sample(p_s3, state, w_dw, b_dw, ln_g, ln_b):
    NB, ls, _ = p_s3.shape
    hist = CONV_W - 1
    sb = 8 if NB % 8 == 0 else 1
    kern = functools.partial(_conf_sample_kernel, sb=sb, ls=ls)
    vec = lambda: pl.BlockSpec((1, D_MODEL), lambda i: (0, 0))
    return pl.pallas_call(
        kern,
        out_shape=(jax.ShapeDtypeStruct((NB, ls, D_MODEL), F32),
                   jax.ShapeDtypeStruct((NB, hist, D_MODEL), F32)),
        grid=(NB // sb,),
        in_specs=[pl.BlockSpec((sb, hist, D_MODEL), lambda i: (i, 0, 0)),
                  pl.BlockSpec((sb, ls, D_MODEL), lambda i: (i, 0, 0)),
                  pl.BlockSpec((sb, ls, D_MODEL), lambda i: (i, 0, 1)),
                  pl.BlockSpec((CONV_W, D_MODEL), lambda i: (0, 0)),
                  vec(), vec(), vec()],
        out_specs=(pl.BlockSpec((sb, ls, D_MODEL), lambda i: (i, 0, 0)),
                   pl.BlockSpec((sb, hist, D_MODEL), lambda i: (i, 0, 0))),
        scratch_shapes=[pltpu.VMEM((hist + ls + 8, D_MODEL), F32)],
        compiler_params=_params(("arbitrary",)),
        name="conf_sample",
    )(state, p_s3, p_s3, w_dw, b_dw, ln_g, ln_b)


def _split(a):
    hi = a.astype(BF16)
    return hi, (a - hi.astype(F32)).astype(BF16)


def _mm3(a, b):
    ah, al = a
    bh, bl = b
    return _dot(jnp.concatenate([ah, al, ah], axis=1), jnp.concatenate([bh, bh, bl], axis=0))


def _tri_inverse(ms, i, j, C, nil):
    same = lambda n: (i >> (n.bit_length() - 1)) == (j >> (n.bit_length() - 1))
    base = min(16, C)
    eye = (i == j).astype(F32)
    bdot = lambda a, b: _dot(a.astype(BF16), b.astype(BF16))
    dps = [jnp.where(same(base), m, 0.0) for m in ms]
    xs = [eye - d for d in dps]
    for _ in range(max(0, (min(base, nil) - 1).bit_length() - 1)):
        dps = [bdot(d, d) for d in dps]
        xs = [x + bdot(d, x) for d, x in zip(dps, xs)]
    blk = base
    while blk < C:
        sel = jnp.logical_and(same(2 * blk), jnp.logical_not(same(blk)))
        ys = [bdot(jnp.where(sel, m, 0.0), x) for m, x in zip(ms, xs)]
        xs = [x - bdot(x, y) for x, y in zip(xs, ys)]
        blk *= 2
    xsp = [_split(x) for x in xs]
    res = [eye - x - _mm3(_split(m), xp) for m, x, xp in zip(ms, xs, xsp)]
    return [x + _dot(xp[0], r.astype(BF16)) for x, xp, r in zip(xs, xsp, res)]


def _gdn_chunks(seqs, alog_ref, dtb_ref, nw_ref, nil):
    C = seqs[0][0].shape[0]
    G = STACK // C
    ri = lax.broadcasted_iota(jnp.int32, (C, C), 0)
    ci = lax.broadcasted_iota(jnp.int32, (C, C), 1)
    tril = (ri >= ci).astype(BF16)
    i = lax.broadcasted_iota(jnp.int32, (STACK, STACK), 0)
    j = lax.broadcasted_iota(jnp.int32, (STACK, STACK), 1)
    shift = C.bit_length() - 1
    same = (i >> shift) == (j >> shift)
    causal = jnp.logical_and(same, i >= j)
    strict = jnp.logical_and(same, i > j)

    pre = []
    for n, (xq, xk, xv, z, ab, valid, s_ref) in enumerate(seqs):
        ok = valid > 0.5
        xa = ab + dtb_ref[...]
        softplus = jnp.maximum(xa, 0.0) + jnp.log(1.0 + jnp.exp(-jnp.abs(xa)))
        g_all = jnp.where(ok, -jnp.exp(alog_ref[...]) * softplus, 0.0)
        beta_all = jnp.where(ok, _sigmoid(ab), 0.0)
        g1 = g_all.astype(BF16)
        r1 = g_all - g1.astype(F32)
        g2 = r1.astype(BF16)
        g3 = (r1 - g2.astype(F32)).astype(BF16)
        gc_all = _dot(tril, g1) + _dot(tril, g2) + _dot(tril, g3)
        ok_st = jnp.concatenate([valid] * G, axis=0) > 0.5
        for h0 in range(0, DN_HEADS, G):
            heads = list(range(h0, h0 + G))
            stack = lambda x: jnp.concatenate([x[:, h * DN_DK:(h + 1) * DN_DK] for h in heads], axis=0)
            col = lambda a, off: jnp.concatenate([a[:, off + h:off + h + 1] for h in heads], axis=0)
            q = stack(xq)
            k = stack(xk)
            q = jnp.where(ok_st, q * lax.rsqrt(jnp.sum(q * q, axis=-1, keepdims=True) + EPS) * (DN_DK ** -0.5), 0.0)
            k = jnp.where(ok_st, k * lax.rsqrt(jnp.sum(k * k, axis=-1, keepdims=True) + EPS), 0.0)
            v = jnp.where(ok_st, stack(xv), 0.0)
            gc = col(gc_all, 0)
            beta = col(beta_all, DN_HEADS)
            g_last = jnp.concatenate([jnp.broadcast_to(gc_all[C - 1:C, h:h + 1], (C, 1)) for h in heads], axis=0)
            gb = jnp.broadcast_to(gc, (STACK, STACK))
            decay = jnp.where(causal, jnp.exp(jnp.where(causal, gb - gb.T, 0.0)), 0.0)
            egc = jnp.exp(gc)
            kb = k * beta
            pre.append(dict(n=n, heads=heads, s_ref=s_ref, q=q, kb=kb, kbf=k.astype(BF16), decay=decay, egc=egc,
                            rhs=jnp.concatenate([v * beta, kb * egc], axis=1),
                            k_dec=(k * jnp.exp(g_last - gc)).astype(BF16), zs=stack(z),
                            s_decay=[jnp.exp(gc_all[C - 1:C, h:h + 1]) for h in heads]))
    ms = [jnp.where(strict, _dot_nt(p["kb"].astype(BF16), p["kbf"]) * p["decay"], 0.0) for p in pre]
    qks = [jnp.where(causal, _dot_nt(p["q"].astype(BF16), p["kbf"]) * p["decay"], 0.0).astype(BF16) for p in pre]
    invs = _tri_inverse(ms, i, j, C, nil)
    sols = [_mm3(_split(inv), _split(p["rhs"])) for inv, p in zip(invs, pre)]
    wss = []
    for p, sol in zip(pre, sols):
        w = sol[:, DN_DV:].astype(BF16)
        q_dec = (p["q"] * p["egc"]).astype(BF16)
        wss.append([_dot(jnp.concatenate([w[g * C:(g + 1) * C], q_dec[g * C:(g + 1) * C]], axis=0),
                         p["s_ref"][h].astype(BF16)) for g, h in enumerate(p["heads"])])
    outs = [[None] * DN_HEADS for _ in seqs]
    for p, sol, ws, qk in zip(pre, sols, wss, qks):
        s_ref = p["s_ref"]
        v_new = [(sol[g * C:(g + 1) * C, :DN_DV] - ws[g][:C]).astype(BF16) for g in range(G)]
        for g, h in enumerate(p["heads"]):
            s_ref[h] = s_ref[h] * p["s_decay"][g] + _dot_tn(p["k_dec"][g * C:(g + 1) * C], v_new[g])
        o = jnp.concatenate([w[C:] for w in ws], axis=0) + _dot(qk, jnp.concatenate(v_new, axis=0))
        o = o * lax.rsqrt(jnp.mean(o * o, axis=-1, keepdims=True) + EPS) * nw_ref[...]
        og = o * (p["zs"] * _sigmoid(p["zs"]))
        for g, h in enumerate(p["heads"]):
            outs[p["n"]][h] = og[g * C:(g + 1) * C]
    return outs


def _short_conv_silu(xbuf, wc_ref, rows):
    first = 8 - (SHORT_W - 1)
    acc = xbuf[first:first + rows, :] * wc_ref[0:1, :]
    for w in range(1, SHORT_W):
        acc = acc + xbuf[first + w:first + w + rows, :] * wc_ref[w:w + 1, :]
    return acc * _sigmoid(acc)


def _gdn_prompt_kernel(*refs, ns):
    seq_refs = [refs[5 * s:5 * s + 5] for s in range(ns)]
    wc_ref, alog_ref, dtb_ref, nw_ref, o_ref, s_ref, xbuf = refs[5 * ns:]
    c = pl.program_id(1)
    n_qk = DN_HEADS * DN_DK

    @pl.when(c == 0)
    def _():
        s_ref[...] = jnp.zeros_like(s_ref)
        xbuf[:, 0:8, :] = jnp.zeros((ns, 8, xbuf.shape[2]), F32)

    rows = lax.broadcasted_iota(jnp.int32, (CHUNK, 1), 0)
    valid = jnp.logical_or(rows >= FRONT, c > 0).astype(F32)
    seqs = []
    for s, (q_ref, k_ref, v_ref, z_ref, ab_ref) in enumerate(seq_refs):
        xb = xbuf.at[s]
        xb[8:8 + CHUNK, 0:n_qk] = q_ref[...]
        xb[8:8 + CHUNK, n_qk:2 * n_qk] = k_ref[...]
        xb[8:8 + CHUNK, 2 * n_qk:] = v_ref[...]
        x = _short_conv_silu(xb, wc_ref, CHUNK)
        xb[0:8, :] = xb[CHUNK:CHUNK + 8, :]
        seqs.append((x[:, 0:n_qk], x[:, n_qk:2 * n_qk], x[:, 2 * n_qk:], z_ref[...].astype(F32), ab_ref[...],
                     valid, s_ref.at[s, 0]))
    outs = _gdn_chunks(seqs, alog_ref, dtb_ref, nw_ref, nil=CHUNK)
    for s in range(ns):
        for h in range(DN_HEADS):
            o_ref[s, :, h * DN_DV:(h + 1) * DN_DV] = outs[s][h].astype(BF16)


def _gdn_prompt(p, pg, ab, B, LP, w_conv, alog, dtb, nw):
    nc = LP // CHUNK
    n_qk = DN_HEADS * DN_DK
    ns = 4 if B % 4 == 0 else (2 if B % 2 == 0 else 1)
    groups = B // ns
    vec = lambda n: pl.BlockSpec((1, n), lambda g, c: (0, 0))
    in_specs, args = [], []
    for s in range(ns):
        for src, col in ((p, 2), (p, 3), (p, 4), (pg, 0)):
            in_specs.append(pl.BlockSpec((CHUNK, n_qk), lambda g, c, s=s, col=col: ((s * groups + g) * nc + c, col)))
            args.append(src)
        in_specs.append(pl.BlockSpec((CHUNK, LANES), lambda g, c, s=s: ((s * groups + g) * nc + c, 0)))
        args.append(ab)
    in_specs += [pl.BlockSpec((SHORT_W, 3 * n_qk), lambda g, c: (0, 0)), vec(LANES), vec(LANES), vec(DN_DV)]
    og, s_out = pl.pallas_call(
        functools.partial(_gdn_prompt_kernel, ns=ns),
        out_shape=(jax.ShapeDtypeStruct((ns, groups * LP, n_qk), BF16),
                   jax.ShapeDtypeStruct((ns, groups, DN_HEADS, DN_DK, DN_DV), F32)),
        grid=(groups, nc),
        in_specs=in_specs,
        out_specs=(pl.BlockSpec((ns, CHUNK, n_qk), lambda g, c: (0, g * nc + c, 0)),
                   pl.BlockSpec((ns, 1, DN_HEADS, DN_DK, DN_DV), lambda g, c: (0, g, 0, 0, 0))),
        scratch_shapes=[pltpu.VMEM((ns, CHUNK + 8, 3 * n_qk), F32)],
        compiler_params=_params(("arbitrary", "arbitrary")),
        name="gdn_prompt",
    )(*args, w_conv, alog, dtb, nw)
    return og.reshape(B * LP, n_qk), s_out.reshape(B, DN_HEADS, DN_DK, DN_DV)


def _gdn_sample_kernel(st_ref, q_ref, k_ref, v_ref, z_ref, ab_ref, s0_ref, wc_ref, alog_ref, dtb_ref, nw_ref,
                       o_ref, s_ref, xbuf, zbuf, abbuf, *, sb, ls):
    n_qk = DN_HEADS * DN_DK
    C = SAMPLE_CHUNK
    hist = SHORT_W - 1
    xbuf[...] = jnp.zeros_like(xbuf)
    zbuf[...] = jnp.zeros_like(zbuf)
    abbuf[...] = jnp.zeros_like(abbuf)
    s_ref[...] = s0_ref[...]
    valid = (lax.broadcasted_iota(jnp.int32, (C, 1), 0) < ls).astype(F32)
    seqs = []
    for s in range(sb):
        xb = xbuf.at[s]
        xb[8 - hist:8, :] = st_ref[s]
        xb[8:8 + ls, 0:n_qk] = q_ref[s]
        xb[8:8 + ls, n_qk:2 * n_qk] = k_ref[s]
        xb[8:8 + ls, 2 * n_qk:] = v_ref[s]
        zbuf[s, 0:ls, :] = z_ref[s]
        abbuf[s, 0:ls, :] = ab_ref[s]
        x = _short_conv_silu(xb, wc_ref, C)
        seqs.append((x[:, 0:n_qk], x[:, n_qk:2 * n_qk], x[:, 2 * n_qk:], zbuf[s], abbuf[s], valid, s_ref.at[s]))
    outs = _gdn_chunks(seqs, alog_ref, dtb_ref, nw_ref, nil=ls)
    for s in range(sb):
        for h in range(DN_HEADS):
            o_ref[s, :, h * DN_DV:(h + 1) * DN_DV] = outs[s][h][0:ls, :]


def _gdn_sample(p_s3, z_s3, ab_s3, st_conv, s0, w_conv, alog, dtb, nw):
    NB, ls, _ = p_s3.shape
    n_qk = DN_HEADS * DN_DK
    hist = SHORT_W - 1
    assert ls <= SAMPLE_CHUNK
    sb = 4 if NB % 4 == 0 else 1
    kern = functools.partial(_gdn_sample_kernel, sb=sb, ls=ls)
    blk = lambda col: pl.BlockSpec((sb, ls, n_qk), lambda i: (i, 0, col))
    vec = lambda n: pl.BlockSpec((1, n), lambda i: (0, 0))
    sspec = lambda: pl.BlockSpec((sb, DN_HEADS, DN_DK, DN_DV), lambda i: (i, 0, 0, 0))
    return pl.pallas_call(
        kern,
        out_shape=(jax.ShapeDtypeStruct((NB, ls, n_qk), F32),
                   jax.ShapeDtypeStruct((NB, DN_HEADS, DN_DK, DN_DV), F32)),
        grid=(NB // sb,),
        in_specs=[pl.BlockSpec((sb, hist, 3 * n_qk), lambda i: (i, 0, 0)),
                  blk(2), blk(3), blk(4), blk(0),
                  pl.BlockSpec((sb, ls, LANES), lambda i: (i, 0, 0)),
                  sspec(),
                  pl.BlockSpec((SHORT_W, 3 * n_qk), lambda i: (0, 0)),
                  vec(LANES), vec(LANES), vec(DN_DV)],
        out_specs=(pl.BlockSpec((sb, ls, n_qk), lambda i: (i, 0, 0)), sspec()),
        scratch_shapes=[pltpu.VMEM((sb, SAMPLE_CHUNK + 8, 3 * n_qk), F32),
                        pltpu.VMEM((sb, SAMPLE_CHUNK, n_qk), F32),
                        pltpu.VMEM((sb, SAMPLE_CHUNK, LANES), F32)],
        compiler_params=_params(("arbitrary",)),
        name="gdn_sample",
    )(st_conv, p_s3, p_s3, p_s3, z_s3, ab_s3, s0, w_conv, alog, dtb, nw)


def _pack_bf16_pairs(x):
    half = x.shape[1] // 2
    lo = lax.bitcast_convert_type(x[:, :half].astype(BF16).astype(F32), jnp.uint32)
    hi = lax.bitcast_convert_type(x[:, half:].astype(BF16).astype(F32), jnp.uint32)
    return jnp.bitwise_or(jnp.bitwise_and(hi, jnp.uint32(0xFFFF0000)), lax.shift_right_logical(lo, jnp.uint32(16)))


def _unpack_bf16_pairs(xp):
    lo = lax.bitcast_convert_type(lax.shift_left(xp, jnp.uint32(16)), F32)
    hi = lax.bitcast_convert_type(jnp.bitwise_and(xp, jnp.uint32(0xFFFF0000)), F32)
    return jnp.concatenate([lo, hi], axis=1).astype(BF16)


def _merge_kernel(cp_ref, cs_ref, ogp_ref, ogs_ref, ga_ref, gb_ref, h_ref, wco_ref, bco_ref, wdo_ref, wo_ref, nf_ref,
                  wr_ref, br_ref, h1_ref, xp_ref, ei_ref, gt_ref, rk_ref, cnt_ref, carry, below, *, n_p):
    step = pl.program_id(0)

    @pl.when(step == 0)
    def _():
        carry[...] = jnp.zeros_like(carry)
        ri = lax.broadcasted_iota(jnp.int32, below.shape, 0)
        ci = lax.broadcasted_iota(jnp.int32, below.shape, 1)
        below[...] = (ri > ci).astype(BF16)

    in_prompt = step < n_p
    c = jnp.where(in_prompt, cp_ref[...], cs_ref[...])
    og = jnp.where(in_prompt, ogp_ref[...], ogs_ref[...])
    ya = _dot(c, wco_ref[...]) + bco_ref[...]
    yb = _dot(og, wdo_ref[...])
    mixed = _sigmoid(ga_ref[...].astype(F32)) * ya + _sigmoid(gb_ref[...].astype(F32)) * yb
    h1 = h_ref[...] + _dot(mixed.astype(BF16), wo_ref[...])
    h1_ref[...] = h1
    xn = h1 * lax.rsqrt(jnp.mean(h1 * h1, axis=-1, keepdims=True) + EPS) * nf_ref[...]
    xp_ref[...] = _pack_bf16_pairs(xn)
    logits = _dot(xn.astype(BF16), wr_ref[...]) + br_ref[...]

    tm = logits.shape[0]
    lane = lax.broadcasted_iota(jnp.int32, (tm, LANES), 1)
    work = logits
    sels, idxs, vals = [], [], []
    for _ in range(TOP_K):
        m = jnp.max(work, axis=-1, keepdims=True)
        idx = jnp.min(jnp.where(work == m, lane, N_EXPERTS - 1), axis=-1, keepdims=True)
        sel = lane == idx
        sels.append(sel)
        idxs.append(idx)
        vals.append(m)
        work = jnp.where(sel, -jnp.inf, work)
    exps = [jnp.exp(v - vals[0]) for v in vals]
    denom = exps[0]
    for e in exps[1:]:
        denom = denom + e
    onehot = jnp.zeros((tm, LANES), F32)
    for sel in sels:
        onehot = onehot + sel.astype(F32)
    before = _dot(below[...], onehot.astype(BF16)) + carry[...]
    ei = jnp.zeros((tm, LANES), jnp.int32)
    gt = jnp.zeros((tm, LANES), F32)
    rk = jnp.zeros((tm, LANES), jnp.int32)
    for k in range(TOP_K):
        at_k = lane == k
        r_k = jnp.sum(jnp.where(sels[k], before, 0.0), axis=-1, keepdims=True).astype(jnp.int32)
        ei = jnp.where(at_k, idxs[k], ei)
        gt = jnp.where(at_k, exps[k] / denom, gt)
        rk = jnp.where(at_k, r_k, rk)
    ei_ref[...] = ei
    gt_ref[...] = gt
    rk_ref[...] = rk
    carry[...] = carry[...] + jnp.sum(onehot, axis=0, keepdims=True)
    cnt_ref[...] = carry[...]


def _merge(c_p, c_s, og_p, og_s, pg, h, wco, bco, wdo, wo, nf, wr, br):
    T = h.shape[0]
    tm = _row_tile(math.gcd(c_p.shape[0], c_s.shape[0]), 512)
    n_p = c_p.shape[0] // tm
    row = lambda col: pl.BlockSpec((tm, D_MODEL), lambda i: (i, col))
    part_p = lambda: pl.BlockSpec((tm, D_MODEL), lambda i: (jnp.minimum(i, n_p - 1), 0))
    part_s = lambda: pl.BlockSpec((tm, D_MODEL), lambda i: (jnp.maximum(i - n_p, 0), 0))
    full = lambda a, b: pl.BlockSpec((a, b), lambda i: (0, 0))
    lanes = lambda: pl.BlockSpec((tm, LANES), lambda i: (i, 0))
    return pl.pallas_call(
        functools.partial(_merge_kernel, n_p=n_p),
        out_shape=(jax.ShapeDtypeStruct((T, D_MODEL), F32),
                   jax.ShapeDtypeStruct((T, D_MODEL // 2), jnp.uint32),
                   jax.ShapeDtypeStruct((T, LANES), jnp.int32),
                   jax.ShapeDtypeStruct((T, LANES), F32),
                   jax.ShapeDtypeStruct((T, LANES), jnp.int32),
                   jax.ShapeDtypeStruct((1, LANES), F32)),
        grid=(T // tm,),
        in_specs=[part_p(), part_s(), part_p(), part_s(), row(1), row(2), row(0),
                  full(D_MODEL, D_MODEL), full(1, D_MODEL), full(D_MODEL, D_MODEL), full(D_MODEL, D_MODEL),
                  full(1, D_MODEL), full(D_MODEL, LANES), full(1, LANES)],
        out_specs=(row(0), pl.BlockSpec((tm, D_MODEL // 2), lambda i: (i, 0)), lanes(), lanes(), lanes(),
                   full(1, LANES)),
        scratch_shapes=[pltpu.VMEM((1, LANES), F32), pltpu.VMEM((tm, tm), BF16)],
        compiler_params=_params(("arbitrary",)),
        name="merge",
    )(c_p, c_s, og_p, og_s, pg, pg, h, wco, bco, wdo, wo, nf, wr, br)


ISSUE_UNROLL = 8


def _row_copy(src, src_row, dst, dst_row, sem):
    return pltpu.make_async_copy(src.at[pl.ds(src_row, 1), :], dst.at[pl.ds(dst_row, 1), :], sem)


def _dispatch_kernel(dest_ref, nv_ref, x_ref, xb_out, zeros, sem, zsem, *, tm):
    @pl.when(pl.program_id(0) == 0)
    def _():
        zeros[...] = jnp.zeros_like(zeros)

        def clear(start):
            def body(i, carry):
                @pl.when(nv_ref[i] < MOE_ROWS)
                def _():
                    cp = pltpu.make_async_copy(
                        zeros, xb_out.at[pl.ds(pl.multiple_of(i * MOE_ROWS, MOE_ROWS), MOE_ROWS), :], zsem)
                    cp.start() if start else cp.wait()
                return carry
            lax.fori_loop(0, nv_ref.shape[0], body, 0)

        clear(True)
        clear(False)

    def issue(t, carry):
        for k in range(TOP_K):
            _row_copy(x_ref, t, xb_out, dest_ref[t * TOP_K + k], sem).start(priority=k % 2)
        return carry

    lax.fori_loop(0, tm, issue, 0, unroll=ISSUE_UNROLL)
    for k in range(TOP_K):
        pltpu.make_async_copy(x_ref, xb_out.at[pl.ds(0, tm), :], sem).wait()


def _dispatch(dest, rows_valid, xp):
    T, W = xp.shape
    tm = _row_tile(T, 512)
    n_rows = rows_valid.shape[0] * MOE_ROWS
    return pl.pallas_call(
        functools.partial(_dispatch_kernel, tm=tm),
        out_shape=jax.ShapeDtypeStruct((n_rows, W), xp.dtype),
        grid=(T // tm,),
        in_specs=[pl.BlockSpec((tm * TOP_K,), lambda i: (i,), memory_space=pltpu.SMEM),
                  pl.BlockSpec(memory_space=pltpu.SMEM),
                  pl.BlockSpec((tm, W), lambda i: (i, 0))],
        out_specs=pl.BlockSpec(memory_space=pl.ANY),
        scratch_shapes=[pltpu.VMEM((MOE_ROWS, W), xp.dtype), pltpu.SemaphoreType.DMA, pltpu.SemaphoreType.DMA],
        compiler_params=_params(("arbitrary",)),
        name="moe_dispatch",
    )(dest, rows_valid, xp)


def _moe_kernel(be_ref, nv_ref, first_ref, slot_ref, next_ref, x_ref, wu_hbm, bu_ref, wd_hbm, bd_ref, o_ref,
                wu32, wd32, wub, wdb, sems):
    i = pl.program_id(0)

    def fetch(e, s):
        pltpu.make_async_copy(wu_hbm.at[e], wu32.at[s], sems.at[0, s]).start()
        pltpu.make_async_copy(wd_hbm.at[e], wd32.at[s], sems.at[1, s]).start()

    @pl.when(i == 0)
    def _():
        fetch(be_ref[0], 0)

    @pl.when(first_ref[i] == 1)
    def _():
        s = slot_ref[i]
        pltpu.make_async_copy(wu_hbm.at[0], wu32.at[s], sems.at[0, s]).wait()
        pltpu.make_async_copy(wd_hbm.at[0], wd32.at[s], sems.at[1, s]).wait()
        wub[...] = wu32[s].astype(BF16)
        wdb[...] = wd32[s].astype(BF16)

        @pl.when(next_ref[i] >= 0)
        def _():
            fetch(next_ref[i], 1 - s)

    @pl.when(nv_ref[i] > 0)
    def _():
        hmid = _dot(_unpack_bf16_pairs(x_ref[...]), wub[...]) + bu_ref[0]
        hg = jnp.minimum(hmid[:, :D_FF], SWIGLU_LIMIT)
        hl = jnp.clip(hmid[:, D_FF:], -SWIGLU_LIMIT, SWIGLU_LIMIT)
        act = hg * _sigmoid(SWIGLU_ALPHA * hg) * (hl + 1.0)
        o_ref[...] = _dot(act.astype(BF16), wdb[...]) + bd_ref[0]

    @pl.when(nv_ref[i] == 0)
    def _():
        o_ref[...] = jnp.zeros_like(o_ref)


def _moe(plan, xb, w_up, b_up, w_down, b_down):
    R = xb.shape[0]
    nb = R // MOE_ROWS
    bias = lambda n: pl.BlockSpec((1, 1, n), lambda i, be, *_: (be[i], 0, 0))
    grid_spec = pltpu.PrefetchScalarGridSpec(
        num_scalar_prefetch=len(plan),
        grid=(nb,),
        in_specs=[pl.BlockSpec((MOE_ROWS, D_MODEL // 2), lambda i, *_: (i, 0)),
                  pl.BlockSpec(memory_space=pl.ANY), bias(2 * D_FF),
                  pl.BlockSpec(memory_space=pl.ANY), bias(D_MODEL)],
        out_specs=pl.BlockSpec((MOE_ROWS, D_MODEL), lambda i, *_: (i, 0)),
        scratch_shapes=[pltpu.VMEM((2, D_MODEL, 2 * D_FF), F32), pltpu.VMEM((2, D_FF, D_MODEL), F32),
                        pltpu.VMEM((D_MODEL, 2 * D_FF), BF16), pltpu.VMEM((D_FF, D_MODEL), BF16),
                        pltpu.SemaphoreType.DMA((2, 2))],
    )
    return pl.pallas_call(
        _moe_kernel,
        out_shape=jax.ShapeDtypeStruct((R, D_MODEL), F32),
        grid_spec=grid_spec,
        compiler_params=pltpu.CompilerParams(dimension_semantics=("arbitrary",), vmem_limit_bytes=MOE_VMEM_LIMIT),
        name="moe_experts",
    )(*plan, xb, w_up, b_up, w_down, b_down)


def _dispatch_plan(counts, ei, rk, T):
    A = T * TOP_K
    n_blocks = -(-A // MOE_ROWS) + N_EXPERTS
    counts = counts[0, :N_EXPERTS].astype(jnp.int32)
    padded = (counts + MOE_ROWS - 1) // MOE_ROWS * MOE_ROWS
    pend = jnp.cumsum(padded)
    pstart = pend - padded
    experts = jnp.arange(N_EXPERTS, dtype=jnp.int32)
    seg_start = jnp.sum(jnp.where(ei[:, :TOP_K, None] == experts, pstart, 0), axis=-1)
    dest = (seg_start + rk[:, :TOP_K]).reshape(-1).astype(jnp.int32)
    starts = jnp.arange(n_blocks, dtype=jnp.int32) * MOE_ROWS
    block_e = jnp.minimum(jnp.sum((pend[None, :] <= starts[:, None]).astype(jnp.int32), axis=1), N_EXPERTS - 1)
    seg_end = jnp.sum(jnp.where(block_e[:, None] == experts, pstart + counts, 0), axis=-1)
    rows_valid = jnp.clip(seg_end - starts, 0, MOE_ROWS).astype(jnp.int32)
    prev_e = jnp.concatenate([jnp.full((1,), -1, jnp.int32), block_e[:-1]])
    first = jnp.logical_and(block_e != prev_e, rows_valid > 0).astype(jnp.int32)
    slot = (jnp.cumsum(first) - 1) % 2
    later = jnp.logical_and(experts[None, :] > experts[:, None], (counts > 0)[None, :])
    next_of = jnp.min(jnp.where(later, experts[None, :], N_EXPERTS), axis=1)
    next_of = jnp.where(next_of == N_EXPERTS, -1, next_of)
    next_e = jnp.sum(jnp.where(block_e[:, None] == experts, next_of, 0), axis=-1)
    plan = tuple(a.astype(jnp.int32) for a in (block_e, rows_valid, first, slot, next_e))
    return dest, plan


def _combine_kernel(dest_ref, dnext_ref, gt_ref, h_ref, nw_ref, yb_ref, yp_ref, ys_ref, buf, res, sems, osems,
                    *, tm, piece, lp, seq, n_prompt_rows):
    i = pl.program_id(0)
    last = pl.num_programs(0) - 1
    slot = lax.rem(i, 2)
    n_head = FRONT + N_META

    def request(d_ref, s):
        def issue(t, carry):
            for k in range(TOP_K):
                _row_copy(yb_ref, d_ref[t * TOP_K + k], buf.at[s, k], t, sems.at[s]).start(priority=k % 2)
            return carry

        lax.fori_loop(0, tm, issue, 0, unroll=ISSUE_UNROLL)

    def writes(tile, s, start):
        for j in range(tm // piece):
            r = tile * tm + j * piece
            src = res.at[s, pl.ds(j * piece, piece), :]
            b = lax.div(r, lp)
            off = r - b * lp

            @pl.when(jnp.logical_and(r < n_prompt_rows, off >= n_head))
            def _():
                cp = pltpu.make_async_copy(
                    src, yp_ref.at[pl.ds(pl.multiple_of(b * seq + off - n_head, 8), piece), :], osems.at[s])
                cp.start() if start else cp.wait()

            @pl.when(r >= n_prompt_rows)
            def _():
                cp = pltpu.make_async_copy(
                    src, ys_ref.at[pl.ds(pl.multiple_of(r - n_prompt_rows, 8), piece), :], osems.at[s])
                cp.start() if start else cp.wait()

    @pl.when(i == 0)
    def _():
        request(dest_ref, 0)

    @pl.when(i + 1 <= last)
    def _():
        request(dnext_ref, 1 - slot)

    @pl.when(i > 0)
    def _():
        writes(i - 1, 1 - slot, start=False)

    for k in range(TOP_K):
        pltpu.make_async_copy(yb_ref.at[pl.ds(0, tm), :], buf.at[slot, k], sems.at[slot]).wait()
    gt = gt_ref[...]
    x = h_ref[...]
    for k in range(TOP_K):
        x = x + gt[:, k:k + 1] * buf[slot, k]
    res[slot] = x * lax.rsqrt(jnp.mean(x * x, axis=-1, keepdims=True) + EPS) * nw_ref[...]
    writes(i, slot, start=True)

    @pl.when(i == last)
    def _():
        writes(i, slot, start=False)


def _combine(dest, gt, h1, nw, yb, B, LP, SEQ):
    T = h1.shape[0]
    n_prompt_rows = B * LP
    n_sample_rows = T - n_prompt_rows
    tm = _row_tile(T, 512)
    n = T // tm
    piece = math.gcd(math.gcd(CHUNK, n_sample_rows), tm)
    assert piece % 8 == 0
    row = lambda: pl.BlockSpec((tm, D_MODEL), lambda i: (i, 0))
    return pl.pallas_call(
        functools.partial(_combine_kernel, tm=tm, piece=piece, lp=LP, seq=SEQ, n_prompt_rows=n_prompt_rows),
        out_shape=(jax.ShapeDtypeStruct((B * SEQ, D_MODEL), F32),
                   jax.ShapeDtypeStruct((n_sample_rows, D_MODEL), F32)),
        grid=(n,),
        in_specs=[pl.BlockSpec((tm * TOP_K,), lambda i: (i,), memory_space=pltpu.SMEM),
                  pl.BlockSpec((tm * TOP_K,), lambda i: (jnp.minimum(i + 1, n - 1),), memory_space=pltpu.SMEM),
                  pl.BlockSpec((tm, LANES), lambda i: (i, 0)),
                  row(),
                  pl.BlockSpec((1, D_MODEL), lambda i: (0, 0)),
                  pl.BlockSpec(memory_space=pl.ANY)],
        out_specs=(pl.BlockSpec(memory_space=pl.ANY), pl.BlockSpec(memory_space=pl.ANY)),
        scratch_shapes=[pltpu.VMEM((2, TOP_K, tm, D_MODEL), F32), pltpu.VMEM((2, tm, D_MODEL), F32),
                        pltpu.SemaphoreType.DMA((2,)), pltpu.SemaphoreType.DMA((2,))],
        compiler_params=_params(("arbitrary",)),
        name="moe_combine_final",
    )(dest, dest, gt, h1, nw, yb)


def _pad_lanes(v, fill=0.0):
    v = v.reshape(1, -1).astype(F32)
    return jnp.pad(v, ((0, 0), (0, LANES - v.shape[1])), constant_values=fill)


def kernel(x_prompt, x_sample, state_conf_conv, state_dn_conv, state_dn_S, meta_tokens, norm_mix, w_in, w_conf_dw, b_conf_dw, ln_conf_g, ln_conf_b, w_conf_out, b_conf_out, w_dn_conv, dn_a_log, dn_dt_bias, dn_norm_w, w_dn_out, w_out, norm_ffn, w_router, b_router, w_up, b_up, w_down, b_down, norm_final):
    B, SEQ, D = x_prompt.shape
    NB, LS, _ = x_sample.shape
    depth = w_in.shape[0]
    assert D == D_MODEL and depth == 1 and SEQ % CHUNK == 0 and LS >= SHORT_W - 1
    LP = FRONT + N_META + SEQ
    TP = B * LP
    T = TP + NB * LS
    n_qk = DN_HEADS * DN_DK
    o_q = 2 * D_MODEL
    o_a = o_q + 4 * n_qk
    o_gate = o_a + 2 * DN_HEADS

    h0 = _tokens(x_prompt, x_sample.reshape(NB * LS, D), meta_tokens.astype(F32))

    w_in0 = w_in[0]
    o_z = o_q + 3 * n_qk
    w_gate = jnp.concatenate([w_in0[:, o_z:o_a], w_in0[:, o_gate:]], axis=1)
    w_ab = jnp.pad(w_in0[:, o_a:o_gate], ((0, 0), (0, LANES - 2 * DN_HEADS))).astype(BF16)
    wco = w_conf_out[0].astype(BF16)
    wdo = w_dn_out[0].astype(BF16)
    wo = w_out[0].astype(BF16)
    wr = jnp.pad(w_router[0], ((0, 0), (0, LANES - N_EXPERTS))).astype(BF16)
    br = _pad_lanes(b_router[0], fill=-1e30)
    alog = _pad_lanes(dn_a_log[0])
    dtb = _pad_lanes(dn_dt_bias[0])
    row = lambda v: v.reshape(1, -1).astype(F32)

    xn, ab = _rms_ab(h0, row(norm_mix[0]), w_ab)
    n_main = o_z // D_MODEL
    p = _mm_in(xn, w_in0, n_main, F32)
    pg = _mm_in(xn, w_gate, 3, BF16)
    p_s3 = p[TP:].reshape(NB, LS, n_main * D_MODEL)
    z_s3 = pg[TP:, :n_qk].astype(F32).reshape(NB, LS, n_qk)
    ab_s3 = ab[TP:].reshape(NB, LS, LANES)

    c_p, ust_p = _conf_prompt(p, B, LP, w_conf_dw[0], row(b_conf_dw[0]), row(ln_conf_g[0]), row(ln_conf_b[0]))
    c_s, conf_state_s = _conf_sample(p_s3, state_conf_conv[0], w_conf_dw[0], row(b_conf_dw[0]),
                                     row(ln_conf_g[0]), row(ln_conf_b[0]))

    og_p, s_p = _gdn_prompt(p, pg, ab, B, LP, w_dn_conv[0], alog, dtb, row(dn_norm_w[0]))
    og_s, s_s = _gdn_sample(p_s3, z_s3, ab_s3, state_dn_conv[0], state_dn_S[0], w_dn_conv[0], alog, dtb,
                            row(dn_norm_w[0]))

    h1, xp2, ei, gt, rk, counts = _merge(c_p, c_s.reshape(NB * LS, D).astype(BF16),
                                         og_p, og_s.reshape(NB * LS, n_qk).astype(BF16), pg, h0,
                                         wco, row(b_conf_out[0]), wdo, wo, row(norm_ffn[0]), wr, br)

    dest, plan = _dispatch_plan(counts, ei, rk, T)
    xb = _dispatch(dest, plan[1], xp2)
    yb = _moe(plan, xb, w_up[0], b_up[0].reshape(N_EXPERTS, 1, -1), w_down[0], b_down[0].reshape(N_EXPERTS, 1, -1))
    yp, ys = _combine(dest, gt, h1, row(norm_final), yb, B, LP, SEQ)

    y_prompt = yp.reshape(B, SEQ, D)
    y_sample = ys.reshape(NB, LS, D)
    hist = CONV_W - 1
    conf_conv_prompt = ust_p[:, HALO - hist:][None]
    dn_conv_prompt = jnp.stack([p[(b + 1) * LP - (SHORT_W - 1):(b + 1) * LP, o_q:o_q + 3 * n_qk]
                                for b in range(B)])[None]
    dn_conv_sample = p_s3[:, LS - (SHORT_W - 1):, o_q:o_q + 3 * n_qk][None]
    return (y_prompt, y_sample, conf_conv_prompt, dn_conv_prompt, s_p[None],
            conf_state_s[None], dn_conv_sample, s_s[None])
```

```python
import functools
import math

import jax
import jax.numpy as jnp
from jax import lax
from jax.experimental import pallas as pl
from jax.experimental.pallas import tpu as pltpu

D_MODEL = 1024
N_META = 16
CONV_W = 31
SHORT_W = 4
DN_HEADS = 8
DN_DK = 128
DN_DV = 128
CHUNK = 64
N_EXPERTS = 32
TOP_K = 4
D_FF = 1024
SWIGLU_LIMIT = 7.0
SWIGLU_ALPHA = 1.702
EPS = 1e-6

FRONT = (-N_META) % CHUNK
SAMPLE_CHUNK = 16
STACK = 128
LANES = 128
HALO = 32
MOE_ROWS = 512
VMEM_LIMIT = 48 * 1024 * 1024
MOE_VMEM_LIMIT = 58 * 1024 * 1024

F32 = jnp.float32
BF16 = jnp.bfloat16


def _row_tile(n, pref):
    best = 16
    for t in range(16, min(n, pref) + 1, 16):
        if n % t == 0:
            best = t
    assert n % best == 0
    return best


def _sigmoid(x):
    return 1.0 / (1.0 + jnp.exp(-x))


def _dot(a, b):
    return jnp.dot(a, b, preferred_element_type=F32)


def _dot_nt(a, b):
    return lax.dot_general(a, b, (((1,), (1,)), ((), ())), preferred_element_type=F32)


def _dot_tn(a, b):
    return lax.dot_general(a, b, (((0,), (0,)), ((), ())), preferred_element_type=F32)


def _params(sem):
    return pltpu.CompilerParams(dimension_semantics=sem, vmem_limit_bytes=VMEM_LIMIT)


def _tokens_kernel(xp_ref, xs_ref, meta_ref, nw_ref, wab_ref, h_ref, xn_ref, ab_ref,
                   xn_buf, ab_buf, head, head_xn, head_ab, s_xn, s_ab, sems, *, lp, tr, n_prompt_rows):
    b = pl.program_id(0)
    j = pl.program_id(1)
    n_head = FRONT + N_META
    row0 = pl.multiple_of(b * lp, 16)

    def norm(x):
        y = (x * lax.rsqrt(jnp.mean(x * x, axis=-1, keepdims=True) + EPS) * nw_ref[...]).astype(BF16)
        return y, _dot(y, wab_ref[...])

    def copies(h_src, xn_src, ab_src, first_row, n_rows, s0):
        rows = pl.ds(first_row, n_rows)
        return [pltpu.make_async_copy(h_src, h_ref.at[rows, :], sems.at[s0]),
                pltpu.make_async_copy(xn_src, xn_ref.at[rows, :], sems.at[s0 + 1]),
                pltpu.make_async_copy(ab_src, ab_ref.at[rows, :], sems.at[s0 + 2])]

    xn_buf[...], ab_buf[...] = norm(xp_ref[0])
    body = copies(xp_ref.at[0], xn_buf, ab_buf, row0 + n_head + j * tr, tr, 0)
    for cp in body:
        cp.start()

    @pl.when(j == 0)
    def _():
        head[0:FRONT, :] = jnp.zeros((FRONT, D_MODEL), F32)
        head[FRONT:, :] = meta_ref[...]
        head_xn[...], head_ab[...] = norm(head[...])
        front = copies(head, head_xn, head_ab, row0, n_head, 3)
        for cp in front:
            cp.start()
        for cp in front:
            cp.wait()

    @pl.when(jnp.logical_and(b == 0, j == 0))
    def _():
        s_xn[...], s_ab[...] = norm(xs_ref[...])
        tail = copies(xs_ref, s_xn, s_ab, n_prompt_rows, xs_ref.shape[0], 6)
        for cp in tail:
            cp.start()
        for cp in tail:
            cp.wait()

    for cp in body:
        cp.wait()


def _tokens(x_prompt, x_sample2, meta, norm_w, w_ab):
    B, SEQ, D = x_prompt.shape
    lp = FRONT + N_META + SEQ
    ns = x_sample2.shape[0]
    T = B * lp + ns
    tr = _row_tile(SEQ, 1024)
    n_head = FRONT + N_META
    assert lp % 16 == 0 and ns % 16 == 0
    any_spec = lambda: pl.BlockSpec(memory_space=pl.ANY)
    return pl.pallas_call(
        functools.partial(_tokens_kernel, lp=lp, tr=tr, n_prompt_rows=B * lp),
        out_shape=(jax.ShapeDtypeStruct((T, D), F32), jax.ShapeDtypeStruct((T, D), BF16),
                   jax.ShapeDtypeStruct((T, LANES), F32)),
        grid=(B, SEQ // tr),
        in_specs=[pl.BlockSpec((1, tr, D), lambda b, j: (b, j, 0)),
                  pl.BlockSpec(x_sample2.shape, lambda b, j: (0, 0)),
                  pl.BlockSpec((N_META, D), lambda b, j: (0, 0)),
                  pl.BlockSpec((1, D), lambda b, j: (0, 0)),
                  pl.BlockSpec((D, LANES), lambda b, j: (0, 0))],
        out_specs=(any_spec(), any_spec(), any_spec()),
        scratch_shapes=[pltpu.VMEM((tr, D), BF16), pltpu.VMEM((tr, LANES), F32),
                        pltpu.VMEM((n_head, D), F32), pltpu.VMEM((n_head, D), BF16), pltpu.VMEM((n_head, LANES), F32),
                        pltpu.VMEM((ns, D), BF16), pltpu.VMEM((ns, LANES), F32),
                        pltpu.SemaphoreType.DMA((9,))],
        compiler_params=_params(("arbitrary", "arbitrary")),
        name="token_layout_norm",
    )(x_prompt, x_sample2, meta, norm_w, w_ab)


def _mm_in_kernel(x_ref, w_ref, o_ref, wb_ref):
    @pl.when(pl.program_id(1) == 0)
    def _():
        wb_ref[...] = w_ref[...].astype(BF16)

    o_ref[...] = _dot(x_ref[...], wb_ref[...]).astype(o_ref.dtype)


def _mm_in(xn, w, n_tiles, out_dtype):
    T, K = xn.shape
    tm = _row_tile(T, 2304)
    tn = 1024
    N = n_tiles * tn
    assert N <= w.shape[1]
    return pl.pallas_call(
        _mm_in_kernel,
        out_shape=jax.ShapeDtypeStruct((T, N), out_dtype),
        grid=(N // tn, T // tm),
        in_specs=[pl.BlockSpec((tm, K), lambda j, i: (i, 0)),
                  pl.BlockSpec((K, tn), lambda j, i: (0, j))],
        out_specs=pl.BlockSpec((tm, tn), lambda j, i: (i, j)),
        scratch_shapes=[pltpu.VMEM((K, tn), BF16)],
        compiler_params=_params(("arbitrary", "arbitrary")),
        name="in_proj",
    )(xn, w)


def _ln_silu(x, g, b):
    mu = jnp.mean(x, axis=-1, keepdims=True)
    xc = x - mu
    var = jnp.mean(xc * xc, axis=-1, keepdims=True)
    y = xc * lax.rsqrt(var + EPS) * g + b
    return y * _sigmoid(y)


def _conf_prompt_kernel(pa_ref, pb_ref, ha_ref, hb_ref, wdw_ref, bdw_ref, lng_ref, lnb_ref,
                        c_ref, ust_ref, ubuf, cbuf, *, tl, rt, ct):
    t = pl.program_id(1)
    u = pa_ref[...] * _sigmoid(pb_ref[...])
    uh = ha_ref[...] * _sigmoid(hb_ref[...])
    ubuf[0:HALO, :] = jnp.where(t > 0, uh, 0.0)
    ubuf[HALO:, :] = u
    first = HALO - (CONV_W - 1)
    for r0 in range(0, tl, rt):
        for c0 in range(0, D_MODEL, ct):
            acc = jnp.zeros((rt, ct), F32)
            for s in range(8):
                part = None
                for w in range(CONV_W):
                    if (first + w) % 8 != s:
                        continue
                    base = r0 + (first + w) // 8 * 8
                    term = ubuf[base:base + rt + (8 if s else 0), c0:c0 + ct] * wdw_ref[w:w + 1, c0:c0 + ct]
                    part = term if part is None else part + term
                if part is not None:
                    acc = acc + part[s:s + rt, :]
            cbuf[r0:r0 + rt, c0:c0 + ct] = acc + bdw_ref[:, c0:c0 + ct]
    c_ref[...] = _ln_silu(cbuf[...], lng_ref[...], lnb_ref[...]).astype(BF16)

    @pl.when(t == pl.num_programs(1) - 1)
    def _():
        ust_ref[0] = ubuf[tl:tl + HALO, :]


def _conf_prompt(p, B, LP, w_dw, b_dw, ln_g, ln_b):
    tl = 192 if LP % 192 == 0 else CHUNK
    nt = LP // tl
    hb = tl // HALO
    kern = functools.partial(_conf_prompt_kernel, tl=tl, rt=64, ct=128)
    halo_idx = lambda b, t: (jnp.maximum((b * nt + t) * hb - 1, 0), 0)
    halo_idx1 = lambda b, t: (jnp.maximum((b * nt + t) * hb - 1, 0), 1)
    vec = lambda: pl.BlockSpec((1, D_MODEL), lambda b, t: (0, 0))
    return pl.pallas_call(
        kern,
        out_shape=(jax.ShapeDtypeStruct((B * LP, D_MODEL), BF16),
                   jax.ShapeDtypeStruct((B, HALO, D_MODEL), F32)),
        grid=(B, nt),
        in_specs=[pl.BlockSpec((tl, D_MODEL), lambda b, t: (b * nt + t, 0)),
                  pl.BlockSpec((tl, D_MODEL), lambda b, t: (b * nt + t, 1)),
                  pl.BlockSpec((HALO, D_MODEL), halo_idx),
                  pl.BlockSpec((HALO, D_MODEL), halo_idx1),
                  pl.BlockSpec((CONV_W, D_MODEL), lambda b, t: (0, 0)),
                  vec(), vec(), vec()],
        out_specs=(pl.BlockSpec((tl, D_MODEL), lambda b, t: (b * nt + t, 0)),
                   pl.BlockSpec((1, HALO, D_MODEL), lambda b, t: (b, 0, 0))),
        scratch_shapes=[pltpu.VMEM((HALO + tl, D_MODEL), F32), pltpu.VMEM((tl, D_MODEL), F32)],
        compiler_params=_params(("arbitrary", "arbitrary")),
        name="conf_prompt",
    )(p, p, p, p, w_dw, b_dw, ln_g, ln_b)


def _conf_sample_kernel(st_ref, pa_ref, pb_ref, wdw_ref, bdw_ref, lng_ref, lnb_ref,
                        c_ref, nst_ref, xh, *, sb, ls):
    hist = CONV_W - 1
    for s in range(sb):
        u = pa_ref[s] * _sigmoid(pb_ref[s])
        xh[0:hist, :] = st_ref[s]
        xh[hist:hist + ls, :] = u
        acc = jnp.zeros((ls, D_MODEL), F32)
        for w in range(CONV_W):
            acc = acc + xh[w:w + ls, :] * wdw_ref[w:w + 1, :]
        c_ref[s] = _ln_silu(acc + bdw_ref[...], lng_ref[...], lnb_ref[...])
        nst_ref[s] = xh[ls:ls + hist, :]


def _conf_sample(p_s3, state, w_dw, b_dw, ln_g, ln_b):
    NB, ls, _ = p_s3.shape
    hist = CONV_W - 1
    sb = 8 if NB % 8 == 0 else 1
    kern = functools.partial(_conf_sample_kernel, sb=sb, ls=ls)
    vec = lambda: pl.BlockSpec((1, D_MODEL), lambda i: (0, 0))
    return pl.pallas_call(
        kern,
        out_shape=(jax.ShapeDtypeStruct((NB, ls, D_MODEL), F32),
                   jax.ShapeDtypeStruct((NB, hist, D_MODEL), F32)),
        grid=(NB // sb,),
        in_specs=[pl.BlockSpec((sb, hist, D_MODEL), lambda i: (i, 0, 0)),
                  pl.BlockSpec((sb, ls, D_MODEL), lambda i: (i, 0, 0)),
                  pl.BlockSpec((sb, ls, D_MODEL), lambda i: (i, 0, 1)),
                  pl.BlockSpec((CONV_W, D_MODEL), lambda i: (0, 0)),
                  vec(), vec(), vec()],
        out_specs=(pl.BlockSpec((sb, ls, D_MODEL), lambda i: (i, 0, 0)),
                   pl.BlockSpec((sb, hist, D_MODEL), lambda i: (i, 0, 0))),
        scratch_shapes=[pltpu.VMEM((hist + ls + 8, D_MODEL), F32)],
        compiler_params=_params(("arbitrary",)),
        name="conf_sample",
    )(state, p_s3, p_s3, w_dw, b_dw, ln_g, ln_b)


def _split(a):
    hi = a.astype(BF16)
    return hi, (a - hi.astype(F32)).astype(BF16)


def _mm3(a, b):
    ah, al = a
    bh, bl = b
    return _dot(jnp.concatenate([ah, al, ah], axis=1), jnp.concatenate([bh, bh, bl], axis=0))


def _tri_inverse(ms, i, j, C, nil):
    same = lambda n: (i >> (n.bit_length() - 1)) == (j >> (n.bit_length() - 1))
    base = min(16, C)
    eye = (i == j).astype(F32)
    bdot = lambda a, b: _dot(a.astype(BF16), b.astype(BF16))
    dps = [jnp.where(same(base), m, 0.0) for m in ms]
    xs = [eye - d for d in dps]
    for _ in range(max(0, (min(base, nil) - 1).bit_length() - 1)):
        dps = [bdot(d, d) for d in dps]
        xs = [x + bdot(d, x) for d, x in zip(dps, xs)]
    blk = base
    while blk < C:
        sel = jnp.logical_and(same(2 * blk), jnp.logical_not(same(blk)))
        ys = [bdot(jnp.where(sel, m, 0.0), x) for m, x in zip(ms, xs)]
        xs = [x - bdot(x, y) for x, y in zip(xs, ys)]
        blk *= 2
    xsp = [_split(x) for x in xs]
    res = [eye - x - _mm3(_split(m), xp) for m, x, xp in zip(ms, xs, xsp)]
    return [x + _dot(xp[0], r.astype(BF16)) for x, xp, r in zip(xs, xsp, res)]


def _gdn_chunks(seqs, alog_ref, dtb_ref, nw_ref, nil):
    C = seqs[0][0].shape[0]
    G = STACK // C
    ri = lax.broadcasted_iota(jnp.int32, (C, C), 0)
    ci = lax.broadcasted_iota(jnp.int32, (C, C), 1)
    tril = (ri >= ci).astype(BF16)
    i = lax.broadcasted_iota(jnp.int32, (STACK, STACK), 0)
    j = lax.broadcasted_iota(jnp.int32, (STACK, STACK), 1)
    shift = C.bit_length() - 1
    same = (i >> shift) == (j >> shift)
    causal = jnp.logical_and(same, i >= j)
    strict = jnp.logical_and(same, i > j)

    pre = []
    for n, (xq, xk, xv, z, ab, valid, s_ref) in enumerate(seqs):
        ok = valid > 0.5
        xa = ab + dtb_ref[...]
        softplus = jnp.maximum(xa, 0.0) + jnp.log(1.0 + jnp.exp(-jnp.abs(xa)))
        g_all = jnp.where(ok, -jnp.exp(alog_ref[...]) * softplus, 0.0)
        beta_all = jnp.where(ok, _sigmoid(ab), 0.0)
        g1 = g_all.astype(BF16)
        r1 = g_all - g1.astype(F32)
        g2 = r1.astype(BF16)
        g3 = (r1 - g2.astype(F32)).astype(BF16)
        gc_all = _dot(tril, g1) + _dot(tril, g2) + _dot(tril, g3)
        ok_st = jnp.concatenate([valid] * G, axis=0) > 0.5
        for h0 in range(0, DN_HEADS, G):
            heads = list(range(h0, h0 + G))
            stack = lambda x: jnp.concatenate([x[:, h * DN_DK:(h + 1) * DN_DK] for h in heads], axis=0)
            col = lambda a, off: jnp.concatenate([a[:, off + h:off + h + 1] for h in heads], axis=0)
            q = stack(xq)
            k = stack(xk)
            q = jnp.where(ok_st, q * lax.rsqrt(jnp.sum(q * q, axis=-1, keepdims=True) + EPS) * (DN_DK ** -0.5), 0.0)
            k = jnp.where(ok_st, k * lax.rsqrt(jnp.sum(k * k, axis=-1, keepdims=True) + EPS), 0.0)
            v = jnp.where(ok_st, stack(xv), 0.0)
            gc = col(gc_all, 0)
            beta = col(beta_all, DN_HEADS)
            g_last = jnp.concatenate([jnp.broadcast_to(gc_all[C - 1:C, h:h + 1], (C, 1)) for h in heads], axis=0)
            gb = jnp.broadcast_to(gc, (STACK, STACK))
            decay = jnp.where(causal, jnp.exp(jnp.where(causal, gb - gb.T, 0.0)), 0.0)
            egc = jnp.exp(gc)
            kb = k * beta
            pre.append(dict(n=n, heads=heads, s_ref=s_ref, q=q, kb=kb, kbf=k.astype(BF16), decay=decay, egc=egc,
                            rhs=jnp.concatenate([v * beta, kb * egc], axis=1),
                            k_dec=(k * jnp.exp(g_last - gc)).astype(BF16), zs=stack(z),
                            s_decay=[jnp.exp(gc_all[C - 1:C, h:h + 1]) for h in heads]))
    ms = [jnp.where(strict, _dot_nt(p["kb"].astype(BF16), p["kbf"]) * p["decay"], 0.0) for p in pre]
    qks = [jnp.where(causal, _dot_nt(p["q"].astype(BF16), p["kbf"]) * p["decay"], 0.0).astype(BF16) for p in pre]
    invs = _tri_inverse(ms, i, j, C, nil)
    sols = [_mm3(_split(inv), _split(p["rhs"])) for inv, p in zip(invs, pre)]
    wss = []
    for p, sol in zip(pre, sols):
        w = sol[:, DN_DV:].astype(BF16)
        q_dec = (p["q"] * p["egc"]).astype(BF16)
        wss.append([_dot(jnp.concatenate([w[g * C:(g + 1) * C], q_dec[g * C:(g + 1) * C]], axis=0),
                         p["s_ref"][h].astype(BF16)) for g, h in enumerate(p["heads"])])
    outs = [[None] * DN_HEADS for _ in seqs]
    for p, sol, ws, qk in zip(pre, sols, wss, qks):
        s_ref = p["s_ref"]
        v_new = [(sol[g * C:(g + 1) * C, :DN_DV] - ws[g][:C]).astype(BF16) for g in range(G)]
        for g, h in enumerate(p["heads"]):
            s_ref[h] = s_ref[h] * p["s_decay"][g] + _dot_tn(p["k_dec"][g * C:(g + 1) * C], v_new[g])
        o = jnp.concatenate([w[C:] for w in ws], axis=0) + _dot(qk, jnp.concatenate(v_new, axis=0))
        o = o * lax.rsqrt(jnp.mean(o * o, axis=-1, keepdims=True) + EPS) * nw_ref[...]
        og = o * (p["zs"] * _sigmoid(p["zs"]))
        for g, h in enumerate(p["heads"]):
            outs[p["n"]][h] = og[g * C:(g + 1) * C]
    return outs


def _short_conv_silu(xbuf, wc_ref, rows):
    first = 8 - (SHORT_W - 1)
    acc = xbuf[first:first + rows, :] * wc_ref[0:1, :]
    for w in range(1, SHORT_W):
        acc = acc + xbuf[first + w:first + w + rows, :] * wc_ref[w:w + 1, :]
    return acc * _sigmoid(acc)


def _gdn_prompt_kernel(*refs, ns):
    seq_refs = [refs[5 * s:5 * s + 5] for s in range(ns)]
    wc_ref, alog_ref, dtb_ref, nw_ref, o_ref, s_ref, xbuf = refs[5 * ns:]
    c = pl.program_id(1)
    n_qk = DN_HEADS * DN_DK

    @pl.when(c == 0)
    def _():
        s_ref[...] = jnp.zeros_like(s_ref)
        xbuf[:, 0:8, :] = jnp.zeros((ns, 8, xbuf.shape[2]), F32)

    rows = lax.broadcasted_iota(jnp.int32, (CHUNK, 1), 0)
    valid = jnp.logical_or(rows >= FRONT, c > 0).astype(F32)
    seqs = []
    for s, (q_ref, k_ref, v_ref, z_ref, ab_ref) in enumerate(seq_refs):
        xb = xbuf.at[s]
        xb[8:8 + CHUNK, 0:n_qk] = q_ref[...]
        xb[8:8 + CHUNK, n_qk:2 * n_qk] = k_ref[...]
        xb[8:8 + CHUNK, 2 * n_qk:] = v_ref[...]
        x = _short_conv_silu(xb, wc_ref, CHUNK)
        xb[0:8, :] = xb[CHUNK:CHUNK + 8, :]
        seqs.append((x[:, 0:n_qk], x[:, n_qk:2 * n_qk], x[:, 2 * n_qk:], z_ref[...].astype(F32), ab_ref[...],
                     valid, s_ref.at[s, 0]))
    outs = _gdn_chunks(seqs, alog_ref, dtb_ref, nw_ref, nil=CHUNK)
    for s in range(ns):
        for h in range(DN_HEADS):
            o_ref[s, :, h * DN_DV:(h + 1) * DN_DV] = outs[s][h].astype(BF16)


def _gdn_prompt(p, pg, ab, B, LP, w_conv, alog, dtb, nw):
    nc = LP // CHUNK
    n_qk = DN_HEADS * DN_DK
    ns = 4 if B % 4 == 0 else (2 if B % 2 == 0 else 1)
    groups = B // ns
    vec = lambda n: pl.BlockSpec((1, n), lambda g, c: (0, 0))
    in_specs, args = [], []
    for s in range(ns):
        for src, col in ((p, 2), (p, 3), (p, 4), (pg, 0)):
            in_specs.append(pl.BlockSpec((CHUNK, n_qk), lambda g, c, s=s, col=col: ((s * groups + g) * nc + c, col)))
            args.append(src)
        in_specs.append(pl.BlockSpec((CHUNK, LANES), lambda g, c, s=s: ((s * groups + g) * nc + c, 0)))
        args.append(ab)
    in_specs += [pl.BlockSpec((SHORT_W, 3 * n_qk), lambda g, c: (0, 0)), vec(LANES), vec(LANES), vec(DN_DV)]
    og, s_out = pl.pallas_call(
        functools.partial(_gdn_prompt_kernel, ns=ns),
        out_shape=(jax.ShapeDtypeStruct((ns, groups * LP, n_qk), BF16),
                   jax.ShapeDtypeStruct((ns, groups, DN_HEADS, DN_DK, DN_DV), F32)),
        grid=(groups, nc),
        in_specs=in_specs,
        out_specs=(pl.BlockSpec((ns, CHUNK, n_qk), lambda g, c: (0, g * nc + c, 0)),
                   pl.BlockSpec((ns, 1, DN_HEADS, DN_DK, DN_DV), lambda g, c: (0, g, 0, 0, 0))),
        scratch_shapes=[pltpu.VMEM((ns, CHUNK + 8, 3 * n_qk), F32)],
        compiler_params=_params(("arbitrary", "arbitrary")),
        name="gdn_prompt",
    )(*args, w_conv, alog, dtb, nw)
    return og.reshape(B * LP, n_qk), s_out.reshape(B, DN_HEADS, DN_DK, DN_DV)


def _gdn_sample_kernel(st_ref, q_ref, k_ref, v_ref, z_ref, ab_ref, s0_ref, wc_ref, alog_ref, dtb_ref, nw_ref,
                       o_ref, s_ref, xbuf, zbuf, abbuf, *, sb, ls):
    n_qk = DN_HEADS * DN_DK
    C = SAMPLE_CHUNK
    hist = SHORT_W - 1
    xbuf[...] = jnp.zeros_like(xbuf)
    zbuf[...] = jnp.zeros_like(zbuf)
    abbuf[...] = jnp.zeros_like(abbuf)
    s_ref[...] = s0_ref[...]
    valid = (lax.broadcasted_iota(jnp.int32, (C, 1), 0) < ls).astype(F32)
    seqs = []
    for s in range(sb):
        xb = xbuf.at[s]
        xb[8 - hist:8, :] = st_ref[s]
        xb[8:8 + ls, 0:n_qk] = q_ref[s]
        xb[8:8 + ls, n_qk:2 * n_qk] = k_ref[s]
        xb[8:8 + ls, 2 * n_qk:] = v_ref[s]
        zbuf[s, 0:ls, :] = z_ref[s]
        abbuf[s, 0:ls, :] = ab_ref[s]
        x = _short_conv_silu(xb, wc_ref, C)
        seqs.append((x[:, 0:n_qk], x[:, n_qk:2 * n_qk], x[:, 2 * n_qk:], zbuf[s], abbuf[s], valid, s_ref.at[s]))
    outs = _gdn_chunks(seqs, alog_ref, dtb_ref, nw_ref, nil=ls)
    for s in range(sb):
        for h in range(DN_HEADS):
            o_ref[s, :, h * DN_DV:(h + 1) * DN_DV] = outs[s][h][0:ls, :]


def _gdn_sample(p_s3, z_s3, ab_s3, st_conv, s0, w_conv, alog, dtb, nw):
    NB, ls, _ = p_s3.shape
    n_qk = DN_HEADS * DN_DK
    hist = SHORT_W - 1
    assert ls <= SAMPLE_CHUNK
    sb = 4 if NB % 4 == 0 else 1
    kern = functools.partial(_gdn_sample_kernel, sb=sb, ls=ls)
    blk = lambda col: pl.BlockSpec((sb, ls, n_qk), lambda i: (i, 0, col))
    vec = lambda n: pl.BlockSpec((1, n), lambda i: (0, 0))
    sspec = lambda: pl.BlockSpec((sb, DN_HEADS, DN_DK, DN_DV), lambda i: (i, 0, 0, 0))
    return pl.pallas_call(
        kern,
        out_shape=(jax.ShapeDtypeStruct((NB, ls, n_qk), F32),
                   jax.ShapeDtypeStruct((NB, DN_HEADS, DN_DK, DN_DV), F32)),
        grid=(NB // sb,),
        in_specs=[pl.BlockSpec((sb, hist, 3 * n_qk), lambda i: (i, 0, 0)),
                  blk(2), blk(3), blk(4), blk(0),
                  pl.BlockSpec((sb, ls, LANES), lambda i: (i, 0, 0)),
                  sspec(),
                  pl.BlockSpec((SHORT_W, 3 * n_qk), lambda i: (0, 0)),
                  vec(LANES), vec(LANES), vec(DN_DV)],
        out_specs=(pl.BlockSpec((sb, ls, n_qk), lambda i: (i, 0, 0)), sspec()),
        scratch_shapes=[pltpu.VMEM((sb, SAMPLE_CHUNK + 8, 3 * n_qk), F32),
                        pltpu.VMEM((sb, SAMPLE_CHUNK, n_qk), F32),
                        pltpu.VMEM((sb, SAMPLE_CHUNK, LANES), F32)],
        compiler_params=_params(("arbitrary",)),
        name="gdn_sample",
    )(st_conv, p_s3, p_s3, p_s3, z_s3, ab_s3, s0, w_conv, alog, dtb, nw)


def _pack_bf16_pairs(x):
    half = x.shape[1] // 2
    lo = lax.bitcast_convert_type(x[:, :half].astype(BF16).astype(F32), jnp.uint32)
    hi = lax.bitcast_convert_type(x[:, half:].astype(BF16).astype(F32), jnp.uint32)
    return jnp.bitwise_or(jnp.bitwise_and(hi, jnp.uint32(0xFFFF0000)), lax.shift_right_logical(lo, jnp.uint32(16)))


def _unpack_bf16_pairs(xp):
    lo = lax.bitcast_convert_type(lax.shift_left(xp, jnp.uint32(16)), F32)
    hi = lax.bitcast_convert_type(jnp.bitwise_and(xp, jnp.uint32(0xFFFF0000)), F32)
    return jnp.concatenate([lo, hi], axis=1).astype(BF16)


def _merge_kernel(cp_ref, cs_ref, ogp_ref, ogs_ref, ga_ref, gb_ref, h_ref, wco_ref, bco_ref, wdo_ref, wo_ref, nf_ref,
                  wr_ref, br_ref, h1_ref, xp_ref, ei_ref, gt_ref, rk_ref, cnt_ref, carry, below, *, n_p):
    step = pl.program_id(0)

    @pl.when(step == 0)
    def _():
        carry[...] = jnp.zeros_like(carry)
        ri = lax.broadcasted_iota(jnp.int32, below.shape, 0)
        ci = lax.broadcasted_iota(jnp.int32, below.shape, 1)
        below[...] = (ri > ci).astype(BF16)

    in_prompt = step < n_p
    c = jnp.where(in_prompt, cp_ref[...], cs_ref[...])
    og = jnp.where(in_prompt, ogp_ref[...], ogs_ref[...])
    ya = _dot(c, wco_ref[...]) + bco_ref[...]
    yb = _dot(og, wdo_ref[...])
    mixed = _sigmoid(ga_ref[...].astype(F32)) * ya + _sigmoid(gb_ref[...].astype(F32)) * yb
    h1 = h_ref[...] + _dot(mixed.astype(BF16), wo_ref[...])
    h1_ref[...] = h1
    xn = h1 * lax.rsqrt(jnp.mean(h1 * h1, axis=-1, keepdims=True) + EPS) * nf_ref[...]
    xp_ref[...] = _pack_bf16_pairs(xn)
    logits = _dot(xn.astype(BF16), wr_ref[...]) + br_ref[...]

    tm = logits.shape[0]
    lane = lax.broadcasted_iota(jnp.int32, (tm, LANES), 1)
    work = logits
    sels, idxs, vals = [], [], []
    for _ in range(TOP_K):
        m = jnp.max(work, axis=-1, keepdims=True)
        idx = jnp.min(jnp.where(work == m, lane, N_EXPERTS - 1), axis=-1, keepdims=True)
        sel = lane == idx
        sels.append(sel)
        idxs.append(idx)
        vals.append(m)
        work = jnp.where(sel, -jnp.inf, work)
    exps = [jnp.exp(v - vals[0]) for v in vals]
    denom = exps[0]
    for e in exps[1:]:
        denom = denom + e
    onehot = jnp.zeros((tm, LANES), F32)
    for sel in sels:
        onehot = onehot + sel.astype(F32)
    before = _dot(below[...], onehot.astype(BF16)) + carry[...]
    ei = jnp.zeros((tm, LANES), jnp.int32)
    gt = jnp.zeros((tm, LANES), F32)
    rk = jnp.zeros((tm, LANES), jnp.int32)
    for k in range(TOP_K):
        at_k = lane == k
        r_k = jnp.sum(jnp.where(sels[k], before, 0.0), axis=-1, keepdims=True).astype(jnp.int32)
        ei = jnp.where(at_k, idxs[k], ei)
        gt = jnp.where(at_k, exps[k] / denom, gt)
        rk = jnp.where(at_k, r_k, rk)
    ei_ref[...] = ei
    gt_ref[...] = gt
    rk_ref[...] = rk
    carry[...] = carry[...] + jnp.sum(onehot, axis=0, keepdims=True)
    cnt_ref[...] = carry[...]


def _merge(c_p, c_s, og_p, og_s, pg, h, wco, bco, wdo, wo, nf, wr, br):
    T = h.shape[0]
    tm = _row_tile(math.gcd(c_p.shape[0], c_s.shape[0]), 512)
    n_p = c_p.shape[0] // tm
    row = lambda col: pl.BlockSpec((tm, D_MODEL), lambda i: (i, col))
    part_p = lambda: pl.BlockSpec((tm, D_MODEL), lambda i: (jnp.minimum(i, n_p - 1), 0))
    part_s = lambda: pl.BlockSpec((tm, D_MODEL), lambda i: (jnp.maximum(i - n_p, 0), 0))
    full = lambda a, b: pl.BlockSpec((a, b), lambda i: (0, 0))
    lanes = lambda: pl.BlockSpec((tm, LANES), lambda i: (i, 0))
    return pl.pallas_call(
        functools.partial(_merge_kernel, n_p=n_p),
        out_shape=(jax.ShapeDtypeStruct((T, D_MODEL), F32),
                   jax.ShapeDtypeStruct((T, D_MODEL // 2), jnp.uint32),
                   jax.ShapeDtypeStruct((T, LANES), jnp.int32),
                   jax.ShapeDtypeStruct((T, LANES), F32),
                   jax.ShapeDtypeStruct((T, LANES), jnp.int32),
                   jax.ShapeDtypeStruct((1, LANES), F32)),
        grid=(T // tm,),
        in_specs=[part_p(), part_s(), part_p(), part_s(), row(1), row(2), row(0),
                  full(D_MODEL, D_MODEL), full(1, D_MODEL), full(D_MODEL, D_MODEL), full(D_MODEL, D_MODEL),
                  full(1, D_MODEL), full(D_MODEL, LANES), full(1, LANES)],
        out_specs=(row(0), pl.BlockSpec((tm, D_MODEL // 2), lambda i: (i, 0)), lanes(), lanes(), lanes(),
                   full(1, LANES)),
        scratch_shapes=[pltpu.VMEM((1, LANES), F32), pltpu.VMEM((tm, tm), BF16)],
        compiler_params=_params(("arbitrary",)),
        name="merge",
    )(c_p, c_s, og_p, og_s, pg, pg, h, wco, bco, wdo, wo, nf, wr, br)


ISSUE_UNROLL = 8


def _row_copy(src, src_row, dst, dst_row, sem):
    return pltpu.make_async_copy(src.at[pl.ds(src_row, 1), :], dst.at[pl.ds(dst_row, 1), :], sem)


def _dispatch_kernel(dest_ref, nv_ref, x_ref, xb_out, zeros, sem, zsem, *, tm):
    @pl.when(pl.program_id(0) == 0)
    def _():
        zeros[...] = jnp.zeros_like(zeros)

        def clear(start):
            def body(i, carry):
                @pl.when(nv_ref[i] < MOE_ROWS)
                def _():
                    cp = pltpu.make_async_copy(
                        zeros, xb_out.at[pl.ds(pl.multiple_of(i * MOE_ROWS, MOE_ROWS), MOE_ROWS), :], zsem)
                    cp.start() if start else cp.wait()
                return carry
            lax.fori_loop(0, nv_ref.shape[0], body, 0)

        clear(True)
        clear(False)

    def issue(t, carry):
        for k in range(TOP_K):
            _row_copy(x_ref, t, xb_out, dest_ref[t * TOP_K + k], sem).start(priority=k % 2)
        return carry

    lax.fori_loop(0, tm, issue, 0, unroll=ISSUE_UNROLL)
    for k in range(TOP_K):
        pltpu.make_async_copy(x_ref, xb_out.at[pl.ds(0, tm), :], sem).wait()


def _dispatch(dest, rows_valid, xp):
    T, W = xp.shape
    tm = _row_tile(T, 512)
    n_rows = rows_valid.shape[0] * MOE_ROWS
    return pl.pallas_call(
        functools.partial(_dispatch_kernel, tm=tm),
        out_shape=jax.ShapeDtypeStruct((n_rows, W), xp.dtype),
        grid=(T // tm,),
        in_specs=[pl.BlockSpec((tm * TOP_K,), lambda i: (i,), memory_space=pltpu.SMEM),
                  pl.BlockSpec(memory_space=pltpu.SMEM),
                  pl.BlockSpec((tm, W), lambda i: (i, 0))],
        out_specs=pl.BlockSpec(memory_space=pl.ANY),
        scratch_shapes=[pltpu.VMEM((MOE_ROWS, W), xp.dtype), pltpu.SemaphoreType.DMA, pltpu.SemaphoreType.DMA],
        compiler_params=_params(("arbitrary",)),
        name="moe_dispatch",
    )(dest, rows_valid, xp)


def _moe_kernel(be_ref, nv_ref, first_ref, slot_ref, next_ref, x_ref, wu_hbm, bu_ref, wd_hbm, bd_ref, o_ref,
                wu32, wd32, wub, wdb, sems):
    i = pl.program_id(0)

    def fetch(e, s):
        pltpu.make_async_copy(wu_hbm.at[e], wu32.at[s], sems.at[0, s]).start()
        pltpu.make_async_copy(wd_hbm.at[e], wd32.at[s], sems.at[1, s]).start()

    @pl.when(i == 0)
    def _():
        fetch(be_ref[0], 0)

    @pl.when(first_ref[i] == 1)
    def _():
        s = slot_ref[i]
        pltpu.make_async_copy(wu_hbm.at[0], wu32.at[s], sems.at[0, s]).wait()
        pltpu.make_async_copy(wd_hbm.at[0], wd32.at[s], sems.at[1, s]).wait()
        wub[...] = wu32[s].astype(BF16)
        wdb[...] = wd32[s].astype(BF16)

        @pl.when(next_ref[i] >= 0)
        def _():
            fetch(next_ref[i], 1 - s)

    @pl.when(nv_ref[i] > 0)
    def _():
        hmid = _dot(_unpack_bf16_pairs(x_ref[...]), wub[...]) + bu_ref[0]
        hg = jnp.minimum(hmid[:, :D_FF], SWIGLU_LIMIT)
        hl = jnp.clip(hmid[:, D_FF:], -SWIGLU_LIMIT, SWIGLU_LIMIT)
        act = hg * _sigmoid(SWIGLU_ALPHA * hg) * (hl + 1.0)
        o_ref[...] = _dot(act.astype(BF16), wdb[...]) + bd_ref[0]

    @pl.when(nv_ref[i] == 0)
    def _():
        o_ref[...] = jnp.zeros_like(o_ref)


def _moe(plan, xb, w_up, b_up, w_down, b_down):
    R = xb.shape[0]
    nb = R // MOE_ROWS
    bias = lambda n: pl.BlockSpec((1, 1, n), lambda i, be, *_: (be[i], 0, 0))
    grid_spec = pltpu.PrefetchScalarGridSpec(
        num_scalar_prefetch=len(plan),
        grid=(nb,),
        in_specs=[pl.BlockSpec((MOE_ROWS, D_MODEL // 2), lambda i, *_: (i, 0)),
                  pl.BlockSpec(memory_space=pl.ANY), bias(2 * D_FF),
                  pl.BlockSpec(memory_space=pl.ANY), bias(D_MODEL)],
        out_specs=pl.BlockSpec((MOE_ROWS, D_MODEL), lambda i, *_: (i, 0)),
        scratch_shapes=[pltpu.VMEM((2, D_MODEL, 2 * D_FF), F32), pltpu.VMEM((2, D_FF, D_MODEL), F32),
                        pltpu.VMEM((D_MODEL, 2 * D_FF), BF16), pltpu.VMEM((D_FF, D_MODEL), BF16),
                        pltpu.SemaphoreType.DMA((2, 2))],
    )
    return pl.pallas_call(
        _moe_kernel,
        out_shape=jax.ShapeDtypeStruct((R, D_MODEL), F32),
        grid_spec=grid_spec,
        compiler_params=pltpu.CompilerParams(dimension_semantics=("arbitrary",), vmem_limit_bytes=MOE_VMEM_LIMIT),
        name="moe_experts",
    )(*plan, xb, w_up, b_up, w_down, b_down)


def _dispatch_plan(counts, ei, rk, T):
    A = T * TOP_K
    n_blocks = -(-A // MOE_ROWS) + N_EXPERTS
    counts = counts[0, :N_EXPERTS].astype(jnp.int32)
    padded = (counts + MOE_ROWS - 1) // MOE_ROWS * MOE_ROWS
    pend = jnp.cumsum(padded)
    pstart = pend - padded
    experts = jnp.arange(N_EXPERTS, dtype=jnp.int32)
    seg_start = jnp.sum(jnp.where(ei[:, :TOP_K, None] == experts, pstart, 0), axis=-1)
    dest = (seg_start + rk[:, :TOP_K]).reshape(-1).astype(jnp.int32)
    starts = jnp.arange(n_blocks, dtype=jnp.int32) * MOE_ROWS
    block_e = jnp.minimum(jnp.sum((pend[None, :] <= starts[:, None]).astype(jnp.int32), axis=1), N_EXPERTS - 1)
    seg_end = jnp.sum(jnp.where(block_e[:, None] == experts, pstart + counts, 0), axis=-1)
    rows_valid = jnp.clip(seg_end - starts, 0, MOE_ROWS).astype(jnp.int32)
    prev_e = jnp.concatenate([jnp.full((1,), -1, jnp.int32), block_e[:-1]])
    first = jnp.logical_and(block_e != prev_e, rows_valid > 0).astype(jnp.int32)
    slot = (jnp.cumsum(first) - 1) % 2
    later = jnp.logical_and(experts[None, :] > experts[:, None], (counts > 0)[None, :])
    next_of = jnp.min(jnp.where(later, experts[None, :], N_EXPERTS), axis=1)
    next_of = jnp.where(next_of == N_EXPERTS, -1, next_of)
    next_e = jnp.sum(jnp.where(block_e[:, None] == experts, next_of, 0), axis=-1)
    plan = tuple(a.astype(jnp.int32) for a in (block_e, rows_valid, first, slot, next_e))
    return dest, plan


def _combine_kernel(dest_ref, dnext_ref, gt_ref, h_ref, nw_ref, yb_ref, yp_ref, ys_ref, buf, res, sems, osems,
                    *, tm, piece, lp, seq, n_prompt_rows):
    i = pl.program_id(0)
    last = pl.num_programs(0) - 1
    slot = lax.rem(i, 2)
    n_head = FRONT + N_META

    def request(d_ref, s):
        def issue(t, carry):
            for k in range(TOP_K):
                _row_copy(yb_ref, d_ref[t * TOP_K + k], buf.at[s, k], t, sems.at[s]).start(priority=k % 2)
            return carry

        lax.fori_loop(0, tm, issue, 0, unroll=ISSUE_UNROLL)

    def writes(tile, s, start):
        for j in range(tm // piece):
            r = tile * tm + j * piece
            src = res.at[s, pl.ds(j * piece, piece), :]
            b = lax.div(r, lp)
            off = r - b * lp

            @pl.when(jnp.logical_and(r < n_prompt_rows, off >= n_head))
            def _():
                cp = pltpu.make_async_copy(
                    src, yp_ref.at[pl.ds(pl.multiple_of(b * seq + off - n_head, 8), piece), :], osems.at[s])
                cp.start() if start else cp.wait()

            @pl.when(r >= n_prompt_rows)
            def _():
                cp = pltpu.make_async_copy(
                    src, ys_ref.at[pl.ds(pl.multiple_of(r - n_prompt_rows, 8), piece), :], osems.at[s])
                cp.start() if start else cp.wait()

    @pl.when(i == 0)
    def _():
        request(dest_ref, 0)

    @pl.when(i + 1 <= last)
    def _():
        request(dnext_ref, 1 - slot)

    @pl.when(i > 0)
    def _():
        writes(i - 1, 1 - slot, start=False)

    for k in range(TOP_K):
        pltpu.make_async_copy(yb_ref.at[pl.ds(0, tm), :], buf.at[slot, k], sems.at[slot]).wait()
    gt = gt_ref[...]
    x = h_ref[...]
    for k in range(TOP_K):
        x = x + gt[:, k:k + 1] * buf[slot, k]
    res[slot] = x * lax.rsqrt(jnp.mean(x * x, axis=-1, keepdims=True) + EPS) * nw_ref[...]
    writes(i, slot, start=True)

    @pl.when(i == last)
    def _():
        writes(i, slot, start=False)


def _combine(dest, gt, h1, nw, yb, B, LP, SEQ):
    T = h1.shape[0]
    n_prompt_rows = B * LP
    n_sample_rows = T - n_prompt_rows
    tm = _row_tile(T, 512)
    n = T // tm
    piece = math.gcd(math.gcd(CHUNK, n_sample_rows), tm)
    assert piece % 8 == 0
    row = lambda: pl.BlockSpec((tm, D_MODEL), lambda i: (i, 0))
    return pl.pallas_call(
        functools.partial(_combine_kernel, tm=tm, piece=piece, lp=LP, seq=SEQ, n_prompt_rows=n_prompt_rows),
        out_shape=(jax.ShapeDtypeStruct((B * SEQ, D_MODEL), F32),
                   jax.ShapeDtypeStruct((n_sample_rows, D_MODEL), F32)),
        grid=(n,),
        in_specs=[pl.BlockSpec((tm * TOP_K,), lambda i: (i,), memory_space=pltpu.SMEM),
                  pl.BlockSpec((tm * TOP_K,), lambda i: (jnp.minimum(i + 1, n - 1),), memory_space=pltpu.SMEM),
                  pl.BlockSpec((tm, LANES), lambda i: (i, 0)),
                  row(),
                  pl.BlockSpec((1, D_MODEL), lambda i: (0, 0)),
                  pl.BlockSpec(memory_space=pl.ANY)],
        out_specs=(pl.BlockSpec(memory_space=pl.ANY), pl.BlockSpec(memory_space=pl.ANY)),
        scratch_shapes=[pltpu.VMEM((2, TOP_K, tm, D_MODEL), F32), pltpu.VMEM((2, tm, D_MODEL), F32),
                        pltpu.SemaphoreType.DMA((2,)), pltpu.SemaphoreType.DMA((2,))],
        compiler_params=_params(("arbitrary",)),
        name="moe_combine_final",
    )(dest, dest, gt, h1, nw, yb)


def _pad_lanes(v, fill=0.0):
    v = v.reshape(1, -1).astype(F32)
    return jnp.pad(v, ((0, 0), (0, LANES - v.shape[1])), constant_values=fill)


def kernel(x_prompt, x_sample, state_conf_conv, state_dn_conv, state_dn_S, meta_tokens, norm_mix, w_in, w_conf_dw, b_conf_dw, ln_conf_g, ln_conf_b, w_conf_out, b_conf_out, w_dn_conv, dn_a_log, dn_dt_bias, dn_norm_w, w_dn_out, w_out, norm_ffn, w_router, b_router, w_up, b_up, w_down, b_down, norm_final):
    B, SEQ, D = x_prompt.shape
    NB, LS, _ = x_sample.shape
    depth = w_in.shape[0]
    assert D == D_MODEL and depth == 1 and SEQ % CHUNK == 0 and LS >= SHORT_W - 1
    LP = FRONT + N_META + SEQ
    TP = B * LP
    T = TP + NB * LS
    n_qk = DN_HEADS * DN_DK
    o_q = 2 * D_MODEL
    o_a = o_q + 4 * n_qk
    o_gate = o_a + 2 * DN_HEADS


    w_in0 = w_in[0]
    o_z = o_q + 3 * n_qk
    w_gate = jnp.concatenate([w_in0[:, o_z:o_a], w_in0[:, o_gate:]], axis=1)
    w_ab = jnp.pad(w_in0[:, o_a:o_gate], ((0, 0), (0, LANES - 2 * DN_HEADS))).astype(BF16)
    wco = w_conf_out[0].astype(BF16)
    wdo = w_dn_out[0].astype(BF16)
    wo = w_out[0].astype(BF16)
    wr = jnp.pad(w_router[0], ((0, 0), (0, LANES - N_EXPERTS))).astype(BF16)
    br = _pad_lanes(b_router[0], fill=-1e30)
    alog = _pad_lanes(dn_a_log[0])
    dtb = _pad_lanes(dn_dt_bias[0])
    row = lambda v: v.reshape(1, -1).astype(F32)

    h0, xn, ab = _tokens(x_prompt, x_sample.reshape(NB * LS, D), meta_tokens.astype(F32), row(norm_mix[0]), w_ab)
    n_main = o_z // D_MODEL
    p = _mm_in(xn, w_in0, n_main, F32)
    pg = _mm_in(xn, w_gate, 3, BF16)
    p_s3 = p[TP:].reshape(NB, LS, n_main * D_MODEL)
    z_s3 = pg[TP:, :n_qk].astype(F32).reshape(NB, LS, n_qk)
    ab_s3 = ab[TP:].reshape(NB, LS, LANES)

    c_p, ust_p = _conf_prompt(p, B, LP, w_conf_dw[0], row(b_conf_dw[0]), row(ln_conf_g[0]), row(ln_conf_b[0]))
    c_s, conf_state_s = _conf_sample(p_s3, state_conf_conv[0], w_conf_dw[0], row(b_conf_dw[0]),
                                     row(ln_conf_g[0]), row(ln_conf_b[0]))

    og_p, s_p = _gdn_prompt(p, pg, ab, B, LP, w_dn_conv[0], alog, dtb, row(dn_norm_w[0]))
    og_s, s_s = _gdn_sample(p_s3, z_s3, ab_s3, state_dn_conv[0], state_dn_S[0], w_dn_conv[0], alog, dtb,
                            row(dn_norm_w[0]))

    h1, xp2, ei, gt, rk, counts = _merge(c_p, c_s.reshape(NB * LS, D).astype(BF16),
                                         og_p, og_s.reshape(NB * LS, n_qk).astype(BF16), pg, h0,
                                         wco, row(b_conf_out[0]), wdo, wo, row(norm_ffn[0]), wr, br)

    dest, plan = _dispatch_plan(counts, ei, rk, T)
    xb = _dispatch(dest, plan[1], xp2)
    yb = _moe(plan, xb, w_up[0], b_up[0].reshape(N_EXPERTS, 1, -1), w_down[0], b_down[0].reshape(N_EXPERTS, 1, -1))
    yp, ys = _combine(dest, gt, h1, row(norm_final), yb, B, LP, SEQ)

    y_prompt = yp.reshape(B, SEQ, D)
    y_sample = ys.reshape(NB, LS, D)
    hist = CONV_W - 1
    conf_conv_prompt = ust_p[:, HALO - hist:][None]
    dn_conv_prompt = jnp.stack([p[(b + 1) * LP - (SHORT_W - 1):(b + 1) * LP, o_q:o_q + 3 * n_qk]
                                for b in range(B)])[None]
    dn_conv_sample = p_s3[:, LS - (SHORT_W - 1):, o_q:o_q + 3 * n_qk][None]
    return (y_prompt, y_sample, conf_conv_prompt, dn_conv_prompt, s_p[None],
            conf_state_s[None], dn_conv_sample, s_s[None])
```

```python
import functools
import math

import jax
import jax.numpy as jnp
from jax import lax
from jax.experimental import pallas as pl
from jax.experimental.pallas import tpu as pltpu

D_MODEL = 1024
N_META = 16
CONV_W = 31
SHORT_W = 4
DN_HEADS = 8
DN_DK = 128
DN_DV = 128
CHUNK = 64
N_EXPERTS = 32
TOP_K = 4
D_FF = 1024
SWIGLU_LIMIT = 7.0
SWIGLU_ALPHA = 1.702
EPS = 1e-6

FRONT = (-N_META) % CHUNK
SAMPLE_CHUNK = 16
STACK = 128
LANES = 128
HALO = 32
MOE_ROWS = 512
VMEM_LIMIT = 48 * 1024 * 1024
MOE_VMEM_LIMIT = 58 * 1024 * 1024

F32 = jnp.float32
BF16 = jnp.bfloat16


def _row_tile(n, pref):
    best = 16
    for t in range(16, min(n, pref) + 1, 16):
        if n % t == 0:
            best = t
    assert n % best == 0
    return best


def _sigmoid(x):
    return 1.0 / (1.0 + jnp.exp(-x))


def _dot(a, b):
    return jnp.dot(a, b, preferred_element_type=F32)


def _dot_nt(a, b):
    return lax.dot_general(a, b, (((1,), (1,)), ((), ())), preferred_element_type=F32)


def _dot_tn(a, b):
    return lax.dot_general(a, b, (((0,), (0,)), ((), ())), preferred_element_type=F32)


def _params(sem):
    return pltpu.CompilerParams(dimension_semantics=sem, vmem_limit_bytes=VMEM_LIMIT)


def _tokens_kernel(xp_ref, xs_ref, meta_ref, nw_ref, wab_ref, h_ref, xn_ref, ab_ref,
                   xn_buf, ab_buf, head, head_xn, head_ab, s_xn, s_ab, sems, *, lp, tr, n_prompt_rows):
    b = pl.program_id(0)
    j = pl.program_id(1)
    n_head = FRONT + N_META
    row0 = pl.multiple_of(b * lp, 16)

    def norm(x):
        y = (x * lax.rsqrt(jnp.mean(x * x, axis=-1, keepdims=True) + EPS) * nw_ref[...]).astype(BF16)
        return y, _dot(y, wab_ref[...])

    def copies(h_src, xn_src, ab_src, first_row, n_rows, s0):
        rows = pl.ds(first_row, n_rows)
        return [pltpu.make_async_copy(h_src, h_ref.at[rows, :], sems.at[s0]),
                pltpu.make_async_copy(xn_src, xn_ref.at[rows, :], sems.at[s0 + 1]),
                pltpu.make_async_copy(ab_src, ab_ref.at[rows, :], sems.at[s0 + 2])]

    xn_buf[...], ab_buf[...] = norm(xp_ref[0])
    body = copies(xp_ref.at[0], xn_buf, ab_buf, row0 + n_head + j * tr, tr, 0)
    for cp in body:
        cp.start()

    @pl.when(j == 0)
    def _():
        head[0:FRONT, :] = jnp.zeros((FRONT, D_MODEL), F32)
        head[FRONT:, :] = meta_ref[...]
        head_xn[...], head_ab[...] = norm(head[...])
        front = copies(head, head_xn, head_ab, row0, n_head, 3)
        for cp in front:
            cp.start()
        for cp in front:
            cp.wait()

    @pl.when(jnp.logical_and(b == 0, j == 0))
    def _():
        s_xn[...], s_ab[...] = norm(xs_ref[...])
        tail = copies(xs_ref, s_xn, s_ab, n_prompt_rows, xs_ref.shape[0], 6)
        for cp in tail:
            cp.start()
        for cp in tail:
            cp.wait()

    for cp in body:
        cp.wait()


def _tokens(x_prompt, x_sample2, meta, norm_w, w_ab):
    B, SEQ, D = x_prompt.shape
    lp = FRONT + N_META + SEQ
    ns = x_sample2.shape[0]
    T = B * lp + ns
    tr = _row_tile(SEQ, 1024)
    n_head = FRONT + N_META
    assert lp % 16 == 0 and ns % 16 == 0
    any_spec = lambda: pl.BlockSpec(memory_space=pl.ANY)
    return pl.pallas_call(
        functools.partial(_tokens_kernel, lp=lp, tr=tr, n_prompt_rows=B * lp),
        out_shape=(jax.ShapeDtypeStruct((T, D), F32), jax.ShapeDtypeStruct((T, D), BF16),
                   jax.ShapeDtypeStruct((T, LANES), F32)),
        grid=(B, SEQ // tr),
        in_specs=[pl.BlockSpec((1, tr, D), lambda b, j: (b, j, 0)),
                  pl.BlockSpec(x_sample2.shape, lambda b, j: (0, 0)),
                  pl.BlockSpec((N_META, D), lambda b, j: (0, 0)),
                  pl.BlockSpec((1, D), lambda b, j: (0, 0)),
                  pl.BlockSpec((D, LANES), lambda b, j: (0, 0))],
        out_specs=(any_spec(), any_spec(), any_spec()),
        scratch_shapes=[pltpu.VMEM((tr, D), BF16), pltpu.VMEM((tr, LANES), F32),
                        pltpu.VMEM((n_head, D), F32), pltpu.VMEM((n_head, D), BF16), pltpu.VMEM((n_head, LANES), F32),
                        pltpu.VMEM((ns, D), BF16), pltpu.VMEM((ns, LANES), F32),
                        pltpu.SemaphoreType.DMA((9,))],
        compiler_params=_params(("arbitrary", "arbitrary")),
        name="token_layout_norm",
    )(x_prompt, x_sample2, meta, norm_w, w_ab)


X_RING = 3


def _mm_in_kernel(x_hbm, w_ref, o_ref, wb_ref, xbuf, sems, *, tm, ni, n_steps):
    s = pl.program_id(0) * ni + pl.program_id(1)

    def tile_copy(step, slot):
        row = pl.multiple_of(lax.rem(step, ni) * tm, 16)
        return pltpu.make_async_copy(x_hbm.at[pl.ds(row, tm), :], xbuf.at[slot], sems.at[slot])

    @pl.when(s == 0)
    def _():
        for first in range(min(X_RING - 1, n_steps)):
            tile_copy(first, first).start()

    @pl.when(s + X_RING - 1 < n_steps)
    def _():
        tile_copy(s + X_RING - 1, lax.rem(s + X_RING - 1, X_RING)).start()

    @pl.when(pl.program_id(1) == 0)
    def _():
        wb_ref[...] = w_ref[...].astype(BF16)

    slot = lax.rem(s, X_RING)
    tile_copy(s, slot).wait()
    o_ref[...] = _dot(xbuf[slot], wb_ref[...]).astype(o_ref.dtype)


def _mm_in(xn, w, n_tiles, out_dtype):
    T, K = xn.shape
    tm = _row_tile(T, 2304)
    tn = 1024
    N = n_tiles * tn
    assert N <= w.shape[1]
    ni = T // tm
    return pl.pallas_call(
        functools.partial(_mm_in_kernel, tm=tm, ni=ni, n_steps=n_tiles * ni),
        out_shape=jax.ShapeDtypeStruct((T, N), out_dtype),
        grid=(N // tn, ni),
        in_specs=[pl.BlockSpec(memory_space=pl.ANY),
                  pl.BlockSpec((K, tn), lambda j, i: (0, j))],
        out_specs=pl.BlockSpec((tm, tn), lambda j, i: (i, j)),
        scratch_shapes=[pltpu.VMEM((K, tn), BF16), pltpu.VMEM((X_RING, tm, K), BF16),
                        pltpu.SemaphoreType.DMA((X_RING,))],
        compiler_params=_params(("arbitrary", "arbitrary")),
        name="in_proj",
    )(xn, w)


def _ln_silu(x, g, b):
    mu = jnp.mean(x, axis=-1, keepdims=True)
    xc = x - mu
    var = jnp.mean(xc * xc, axis=-1, keepdims=True)
    y = xc * lax.rsqrt(var + EPS) * g + b
    return y * _sigmoid(y)


def _conf_prompt_kernel(pa_ref, pb_ref, ha_ref, hb_ref, wdw_ref, bdw_ref, lng_ref, lnb_ref,
                        c_ref, ust_ref, ubuf, cbuf, *, tl, rt, ct):
    t = pl.program_id(1)
    u = pa_ref[...] * _sigmoid(pb_ref[...])
    uh = ha_ref[...] * _sigmoid(hb_ref[...])
    ubuf[0:HALO, :] = jnp.where(t > 0, uh, 0.0)
    ubuf[HALO:, :] = u
    first = HALO - (CONV_W - 1)
    for r0 in range(0, tl, rt):
        for c0 in range(0, D_MODEL, ct):
            acc = jnp.zeros((rt, ct), F32)
            for s in range(8):
                part = None
                for w in range(CONV_W):
                    if (first + w) % 8 != s:
                        continue
                    base = r0 + (first + w) // 8 * 8
                    term = ubuf[base:base + rt + (8 if s else 0), c0:c0 + ct] * wdw_ref[w:w + 1, c0:c0 + ct]
                    part = term if part is None else part + term
                if part is not None:
                    acc = acc + part[s:s + rt, :]
            cbuf[r0:r0 + rt, c0:c0 + ct] = acc + bdw_ref[:, c0:c0 + ct]
    c_ref[...] = _ln_silu(cbuf[...], lng_ref[...], lnb_ref[...]).astype(BF16)

    @pl.when(t == pl.num_programs(1) - 1)
    def _():
        ust_ref[0] = ubuf[tl:tl + HALO, :]


def _conf_prompt(p, B, LP, w_dw, b_dw, ln_g, ln_b):
    tl = 192 if LP % 192 == 0 else CHUNK
    nt = LP // tl
    hb = tl // HALO
    kern = functools.partial(_conf_prompt_kernel, tl=tl, rt=64, ct=128)
    halo_idx = lambda b, t: (jnp.maximum((b * nt + t) * hb - 1, 0), 0)
    halo_idx1 = lambda b, t: (jnp.maximum((b * nt + t) * hb - 1, 0), 1)
    vec = lambda: pl.BlockSpec((1, D_MODEL), lambda b, t: (0, 0))
    return pl.pallas_call(
        kern,
        out_shape=(jax.ShapeDtypeStruct((B * LP, D_MODEL), BF16),
                   jax.ShapeDtypeStruct((B, HALO, D_MODEL), F32)),
        grid=(B, nt),
        in_specs=[pl.BlockSpec((tl, D_MODEL), lambda b, t: (b * nt + t, 0)),
                  pl.BlockSpec((tl, D_MODEL), lambda b, t: (b * nt + t, 1)),
                  pl.BlockSpec((HALO, D_MODEL), halo_idx),
                  pl.BlockSpec((HALO, D_MODEL), halo_idx1),
                  pl.BlockSpec((CONV_W, D_MODEL), lambda b, t: (0, 0)),
                  vec(), vec(), vec()],
        out_specs=(pl.BlockSpec((tl, D_MODEL), lambda b, t: (b * nt + t, 0)),
                   pl.BlockSpec((1, HALO, D_MODEL), lambda b, t: (b, 0, 0))),
        scratch_shapes=[pltpu.VMEM((HALO + tl, D_MODEL), F32), pltpu.VMEM((tl, D_MODEL), F32)],
        compiler_params=_params(("arbitrary", "arbitrary")),
        name="conf_prompt",
    )(p, p, p, p, w_dw, b_dw, ln_g, ln_b)


def _conf_sample_kernel(st_ref, pa_ref, pb_ref, wdw_ref, bdw_ref, lng_ref, lnb_ref,
                        c_ref, nst_ref, xh, *, sb, ls):
    hist = CONV_W - 1
    for s in range(sb):
        u = pa_ref[s] * _sigmoid(pb_ref[s])
        xh[0:hist, :] = st_ref[s]
        xh[hist:hist + ls, :] = u
        acc = jnp.zeros((ls, D_MODEL), F32)
        for w in range(CONV_W):
            acc = acc + xh[w:w + ls, :] * wdw_ref[w:w + 1, :]
        c_ref[s] = _ln_silu(acc + bdw_ref[...], lng_ref[...], lnb_ref[...])
        nst_ref[s] = xh[ls:ls + hist, :]


def _conf_sample(p_s3, state, w_dw, b_dw, ln_g, ln_b):
    NB, ls, _ = p_s3.shape
    hist = CONV_W - 1
    sb = 8 if NB % 8 == 0 else 1
    kern = functools.partial(_conf_sample_kernel, sb=sb, ls=ls)
    vec = lambda: pl.BlockSpec((1, D_MODEL), lambda i: (0, 0))
    return pl.pallas_call(
        kern,
        out_shape=(jax.ShapeDtypeStruct((NB, ls, D_MODEL), F32),
                   jax.ShapeDtypeStruct((NB, hist, D_MODEL), F32)),
        grid=(NB // sb,),
        in_specs=[pl.BlockSpec((sb, hist, D_MODEL), lambda i: (i, 0, 0)),
                  pl.BlockSpec((sb, ls, D_MODEL), lambda i: (i, 0, 0)),
                  pl.BlockSpec((sb, ls, D_MODEL), lambda i: (i, 0, 1)),
                  pl.BlockSpec((CONV_W, D_MODEL), lambda i: (0, 0)),
                  vec(), vec(), vec()],
        out_specs=(pl.BlockSpec((sb, ls, D_MODEL), lambda i: (i, 0, 0)),
                   pl.BlockSpec((sb, hist, D_MODEL), lambda i: (i, 0, 0))),
        scratch_shapes=[pltpu.VMEM((hist + ls + 8, D_MODEL), F32)],
        compiler_params=_params(("arbitrary",)),
        name="conf_sample",
    )(state, p_s3, p_s3, w_dw, b_dw, ln_g, ln_b)


def _split(a):
    hi = a.astype(BF16)
    return hi, (a - hi.astype(F32)).astype(BF16)


def _mm3(a, b):
    ah, al = a
    bh, bl = b
    return _dot(jnp.concatenate([ah, al, ah], axis=1), jnp.concatenate([bh, bh, bl], axis=0))


def _tri_inverse(ms, i, j, C, nil):
    same = lambda n: (i >> (n.bit_length() - 1)) == (j >> (n.bit_length() - 1))
    base = min(16, C)
    eye = (i == j).astype(F32)
    bdot = lambda a, b: _dot(a.astype(BF16), b.astype(BF16))
    dps = [jnp.where(same(base), m, 0.0) for m in ms]
    xs = [eye - d for d in dps]
    for _ in range(max(0, (min(base, nil) - 1).bit_length() - 1)):
        dps = [bdot(d, d) for d in dps]
        xs = [x + bdot(d, x) for d, x in zip(dps, xs)]
    blk = base
    while blk < C:
        sel = jnp.logical_and(same(2 * blk), jnp.logical_not(same(blk)))
        ys = [bdot(jnp.where(sel, m, 0.0), x) for m, x in zip(ms, xs)]
        xs = [x - bdot(x, y) for x, y in zip(xs, ys)]
        blk *= 2
    xsp = [_split(x) for x in xs]
    res = [eye - x - _mm3(_split(m), xp) for m, x, xp in zip(ms, xs, xsp)]
    return [x + _dot(xp[0], r.astype(BF16)) for x, xp, r in zip(xs, xsp, res)]


def _gdn_chunks(seqs, alog_ref, dtb_ref, nw_ref, nil):
    C = seqs[0][0].shape[0]
    G = STACK // C
    ri = lax.broadcasted_iota(jnp.int32, (C, C), 0)
    ci = lax.broadcasted_iota(jnp.int32, (C, C), 1)
    tril = (ri >= ci).astype(BF16)
    i = lax.broadcasted_iota(jnp.int32, (STACK, STACK), 0)
    j = lax.broadcasted_iota(jnp.int32, (STACK, STACK), 1)
    shift = C.bit_length() - 1
    same = (i >> shift) == (j >> shift)
    causal = jnp.logical_and(same, i >= j)
    strict = jnp.logical_and(same, i > j)

    pre = []
    for n, (xq, xk, xv, z, ab, valid, s_ref) in enumerate(seqs):
        ok = valid > 0.5
        xa = ab + dtb_ref[...]
        softplus = jnp.maximum(xa, 0.0) + jnp.log(1.0 + jnp.exp(-jnp.abs(xa)))
        g_all = jnp.where(ok, -jnp.exp(alog_ref[...]) * softplus, 0.0)
        beta_all = jnp.where(ok, _sigmoid(ab), 0.0)
        g1 = g_all.astype(BF16)
        r1 = g_all - g1.astype(F32)
        g2 = r1.astype(BF16)
        g3 = (r1 - g2.astype(F32)).astype(BF16)
        gc_all = _dot(tril, g1) + _dot(tril, g2) + _dot(tril, g3)
        ok_st = jnp.concatenate([valid] * G, axis=0) > 0.5
        for h0 in range(0, DN_HEADS, G):
            heads = list(range(h0, h0 + G))
            stack = lambda x: jnp.concatenate([x[:, h * DN_DK:(h + 1) * DN_DK] for h in heads], axis=0)
            col = lambda a, off: jnp.concatenate([a[:, off + h:off + h + 1] for h in heads], axis=0)
            q = stack(xq)
            k = stack(xk)
            q = jnp.where(ok_st, q * lax.rsqrt(jnp.sum(q * q, axis=-1, keepdims=True) + EPS) * (DN_DK ** -0.5), 0.0)
            k = jnp.where(ok_st, k * lax.rsqrt(jnp.sum(k * k, axis=-1, keepdims=True) + EPS), 0.0)
            v = jnp.where(ok_st, stack(xv), 0.0)
            gc = col(gc_all, 0)
            beta = col(beta_all, DN_HEADS)
            g_last = jnp.concatenate([jnp.broadcast_to(gc_all[C - 1:C, h:h + 1], (C, 1)) for h in heads], axis=0)
            gb = jnp.broadcast_to(gc, (STACK, STACK))
            decay = jnp.where(causal, jnp.exp(jnp.where(causal, gb - gb.T, 0.0)), 0.0)
            egc = jnp.exp(gc)
            kb = k * beta
            pre.append(dict(n=n, heads=heads, s_ref=s_ref, q=q, kb=kb, kbf=k.astype(BF16), decay=decay, egc=egc,
                            rhs=jnp.concatenate([v * beta, kb * egc], axis=1),
                            k_dec=(k * jnp.exp(g_last - gc)).astype(BF16), zs=stack(z),
                            s_decay=[jnp.exp(gc_all[C - 1:C, h:h + 1]) for h in heads]))
    ms = [jnp.where(strict, _dot_nt(p["kb"].astype(BF16), p["kbf"]) * p["decay"], 0.0) for p in pre]
    qks = [jnp.where(causal, _dot_nt(p["q"].astype(BF16), p["kbf"]) * p["decay"], 0.0).astype(BF16) for p in pre]
    invs = _tri_inverse(ms, i, j, C, nil)
    sols = [_mm3(_split(inv), _split(p["rhs"])) for inv, p in zip(invs, pre)]
    wss = []
    for p, sol in zip(pre, sols):
        w = sol[:, DN_DV:].astype(BF16)
        q_dec = (p["q"] * p["egc"]).astype(BF16)
        wss.append([_dot(jnp.concatenate([w[g * C:(g + 1) * C], q_dec[g * C:(g + 1) * C]], axis=0),
                         p["s_ref"][h].astype(BF16)) for g, h in enumerate(p["heads"])])
    outs = [[None] * DN_HEADS for _ in seqs]
    for p, sol, ws, qk in zip(pre, sols, wss, qks):
        s_ref = p["s_ref"]
        v_new = [(sol[g * C:(g + 1) * C, :DN_DV] - ws[g][:C]).astype(BF16) for g in range(G)]
        for g, h in enumerate(p["heads"]):
            s_ref[h] = s_ref[h] * p["s_decay"][g] + _dot_tn(p["k_dec"][g * C:(g + 1) * C], v_new[g])
        o = jnp.concatenate([w[C:] for w in ws], axis=0) + _dot(qk, jnp.concatenate(v_new, axis=0))
        o = o * lax.rsqrt(jnp.mean(o * o, axis=-1, keepdims=True) + EPS) * nw_ref[...]
        og = o * (p["zs"] * _sigmoid(p["zs"]))
        for g, h in enumerate(p["heads"]):
            outs[p["n"]][h] = og[g * C:(g + 1) * C]
    return outs


def _short_conv_silu(xbuf, wc_ref, rows):
    first = 8 - (SHORT_W - 1)
    acc = xbuf[first:first + rows, :] * wc_ref[0:1, :]
    for w in range(1, SHORT_W):
        acc = acc + xbuf[first + w:first + w + rows, :] * wc_ref[w:w + 1, :]
    return acc * _sigmoid(acc)


def _gdn_prompt_kernel(*refs, ns):
    seq_refs = [refs[5 * s:5 * s + 5] for s in range(ns)]
    wc_ref, alog_ref, dtb_ref, nw_ref, o_ref, s_ref, xbuf = refs[5 * ns:]
    c = pl.program_id(1)
    n_qk = DN_HEADS * DN_DK

    @pl.when(c == 0)
    def _():
        s_ref[...] = jnp.zeros_like(s_ref)
        xbuf[:, 0:8, :] = jnp.zeros((ns, 8, xbuf.shape[2]), F32)

    rows = lax.broadcasted_iota(jnp.int32, (CHUNK, 1), 0)
    valid = jnp.logical_or(rows >= FRONT, c > 0).astype(F32)
    seqs = []
    for s, (q_ref, k_ref, v_ref, z_ref, ab_ref) in enumerate(seq_refs):
        xb = xbuf.at[s]
        xb[8:8 + CHUNK, 0:n_qk] = q_ref[...]
        xb[8:8 + CHUNK, n_qk:2 * n_qk] = k_ref[...]
        xb[8:8 + CHUNK, 2 * n_qk:] = v_ref[...]
        x = _short_conv_silu(xb, wc_ref, CHUNK)
        xb[0:8, :] = xb[CHUNK:CHUNK + 8, :]
        seqs.append((x[:, 0:n_qk], x[:, n_qk:2 * n_qk], x[:, 2 * n_qk:], z_ref[...].astype(F32), ab_ref[...],
                     valid, s_ref.at[s, 0]))
    outs = _gdn_chunks(seqs, alog_ref, dtb_ref, nw_ref, nil=CHUNK)
    for s in range(ns):
        for h in range(DN_HEADS):
            o_ref[s, :, h * DN_DV:(h + 1) * DN_DV] = outs[s][h].astype(BF16)


def _gdn_prompt(p, pg, ab, B, LP, w_conv, alog, dtb, nw):
    nc = LP // CHUNK
    n_qk = DN_HEADS * DN_DK
    ns = 4 if B % 4 == 0 else (2 if B % 2 == 0 else 1)
    groups = B // ns
    vec = lambda n: pl.BlockSpec((1, n), lambda g, c: (0, 0))
    in_specs, args = [], []
    for s in range(ns):
        for src, col in ((p, 2), (p, 3), (p, 4), (pg, 0)):
            in_specs.append(pl.BlockSpec((CHUNK, n_qk), lambda g, c, s=s, col=col: ((s * groups + g) * nc + c, col)))
            args.append(src)
        in_specs.append(pl.BlockSpec((CHUNK, LANES), lambda g, c, s=s: ((s * groups + g) * nc + c, 0)))
        args.append(ab)
    in_specs += [pl.BlockSpec((SHORT_W, 3 * n_qk), lambda g, c: (0, 0)), vec(LANES), vec(LANES), vec(DN_DV)]
    og, s_out = pl.pallas_call(
        functools.partial(_gdn_prompt_kernel, ns=ns),
        out_shape=(jax.ShapeDtypeStruct((ns, groups * LP, n_qk), BF16),
                   jax.ShapeDtypeStruct((ns, groups, DN_HEADS, DN_DK, DN_DV), F32)),
        grid=(groups, nc),
        in_specs=in_specs,
        out_specs=(pl.BlockSpec((ns, CHUNK, n_qk), lambda g, c: (0, g * nc + c, 0)),
                   pl.BlockSpec((ns, 1, DN_HEADS, DN_DK, DN_DV), lambda g, c: (0, g, 0, 0, 0))),
        scratch_shapes=[pltpu.VMEM((ns, CHUNK + 8, 3 * n_qk), F32)],
        compiler_params=_params(("arbitrary", "arbitrary")),
        name="gdn_prompt",
    )(*args, w_conv, alog, dtb, nw)
    return og.reshape(B * LP, n_qk), s_out.reshape(B, DN_HEADS, DN_DK, DN_DV)


def _gdn_sample_kernel(st_ref, q_ref, k_ref, v_ref, z_ref, ab_ref, s0_ref, wc_ref, alog_ref, dtb_ref, nw_ref,
                       o_ref, s_ref, xbuf, zbuf, abbuf, *, sb, ls):
    n_qk = DN_HEADS * DN_DK
    C = SAMPLE_CHUNK
    hist = SHORT_W - 1
    xbuf[...] = jnp.zeros_like(xbuf)
    zbuf[...] = jnp.zeros_like(zbuf)
    abbuf[...] = jnp.zeros_like(abbuf)
    s_ref[...] = s0_ref[...]
    valid = (lax.broadcasted_iota(jnp.int32, (C, 1), 0) < ls).astype(F32)
    seqs = []
    for s in range(sb):
        xb = xbuf.at[s]
        xb[8 - hist:8, :] = st_ref[s]
        xb[8:8 + ls, 0:n_qk] = q_ref[s]
        xb[8:8 + ls, n_qk:2 * n_qk] = k_ref[s]
        xb[8:8 + ls, 2 * n_qk:] = v_ref[s]
        zbuf[s, 0:ls, :] = z_ref[s]
        abbuf[s, 0:ls, :] = ab_ref[s]
        x = _short_conv_silu(xb, wc_ref, C)
        seqs.append((x[:, 0:n_qk], x[:, n_qk:2 * n_qk], x[:, 2 * n_qk:], zbuf[s], abbuf[s], valid, s_ref.at[s]))
    outs = _gdn_chunks(seqs, alog_ref, dtb_ref, nw_ref, nil=ls)
    for s in range(sb):
        for h in range(DN_HEADS):
            o_ref[s, :, h * DN_DV:(h + 1) * DN_DV] = outs[s][h][0:ls, :]


def _gdn_sample(p_s3, z_s3, ab_s3, st_conv, s0, w_conv, alog, dtb, nw):
    NB, ls, _ = p_s3.shape
    n_qk = DN_HEADS * DN_DK
    hist = SHORT_W - 1
    assert ls <= SAMPLE_CHUNK
    sb = 4 if NB % 4 == 0 else 1
    kern = functools.partial(_gdn_sample_kernel, sb=sb, ls=ls)
    blk = lambda col: pl.BlockSpec((sb, ls, n_qk), lambda i: (i, 0, col))
    vec = lambda n: pl.BlockSpec((1, n), lambda i: (0, 0))
    sspec = lambda: pl.BlockSpec((sb, DN_HEADS, DN_DK, DN_DV), lambda i: (i, 0, 0, 0))
    return pl.pallas_call(
        kern,
        out_shape=(jax.ShapeDtypeStruct((NB, ls, n_qk), F32),
                   jax.ShapeDtypeStruct((NB, DN_HEADS, DN_DK, DN_DV), F32)),
        grid=(NB // sb,),
        in_specs=[pl.BlockSpec((sb, hist, 3 * n_qk), lambda i: (i, 0, 0)),
                  blk(2), blk(3), blk(4), blk(0),
                  pl.BlockSpec((sb, ls, LANES), lambda i: (i, 0, 0)),
                  sspec(),
                  pl.BlockSpec((SHORT_W, 3 * n_qk), lambda i: (0, 0)),
                  vec(LANES), vec(LANES), vec(DN_DV)],
        out_specs=(pl.BlockSpec((sb, ls, n_qk), lambda i: (i, 0, 0)), sspec()),
        scratch_shapes=[pltpu.VMEM((sb, SAMPLE_CHUNK + 8, 3 * n_qk), F32),
                        pltpu.VMEM((sb, SAMPLE_CHUNK, n_qk), F32),
                        pltpu.VMEM((sb, SAMPLE_CHUNK, LANES), F32)],
        compiler_params=_params(("arbitrary",)),
        name="gdn_sample",
    )(st_conv, p_s3, p_s3, p_s3, z_s3, ab_s3, s0, w_conv, alog, dtb, nw)


def _pack_bf16_pairs(x):
    half = x.shape[1] // 2
    lo = lax.bitcast_convert_type(x[:, :half].astype(BF16).astype(F32), jnp.uint32)
    hi = lax.bitcast_convert_type(x[:, half:].astype(BF16).astype(F32), jnp.uint32)
    return jnp.bitwise_or(jnp.bitwise_and(hi, jnp.uint32(0xFFFF0000)), lax.shift_right_logical(lo, jnp.uint32(16)))


def _unpack_bf16_pairs(xp):
    lo = lax.bitcast_convert_type(lax.shift_left(xp, jnp.uint32(16)), F32)
    hi = lax.bitcast_convert_type(jnp.bitwise_and(xp, jnp.uint32(0xFFFF0000)), F32)
    return jnp.concatenate([lo, hi], axis=1).astype(BF16)


def _merge_kernel(cp_ref, cs_ref, ogp_ref, ogs_ref, ga_ref, gb_ref, h_ref, wco_ref, bco_ref, wdo_ref, wo_ref, nf_ref,
                  wr_ref, br_ref, h1_ref, xp_ref, ei_ref, gt_ref, rk_ref, cnt_ref, carry, below, *, n_p):
    step = pl.program_id(0)

    @pl.when(step == 0)
    def _():
        carry[...] = jnp.zeros_like(carry)
        ri = lax.broadcasted_iota(jnp.int32, below.shape, 0)
        ci = lax.broadcasted_iota(jnp.int32, below.shape, 1)
        below[...] = (ri > ci).astype(BF16)

    in_prompt = step < n_p
    c = jnp.where(in_prompt, cp_ref[...], cs_ref[...])
    og = jnp.where(in_prompt, ogp_ref[...], ogs_ref[...])
    ya = _dot(c, wco_ref[...]) + bco_ref[...]
    yb = _dot(og, wdo_ref[...])
    mixed = _sigmoid(ga_ref[...].astype(F32)) * ya + _sigmoid(gb_ref[...].astype(F32)) * yb
    h1 = h_ref[...] + _dot(mixed.astype(BF16), wo_ref[...])
    h1_ref[...] = h1
    xn = h1 * lax.rsqrt(jnp.mean(h1 * h1, axis=-1, keepdims=True) + EPS) * nf_ref[...]
    xp_ref[...] = _pack_bf16_pairs(xn)
    logits = _dot(xn.astype(BF16), wr_ref[...]) + br_ref[...]

    tm = logits.shape[0]
    lane = lax.broadcasted_iota(jnp.int32, (tm, LANES), 1)
    work = logits
    sels, idxs, vals = [], [], []
    for _ in range(TOP_K):
        m = jnp.max(work, axis=-1, keepdims=True)
        idx = jnp.min(jnp.where(work == m, lane, N_EXPERTS - 1), axis=-1, keepdims=True)
        sel = lane == idx
        sels.append(sel)
        idxs.append(idx)
        vals.append(m)
        work = jnp.where(sel, -jnp.inf, work)
    exps = [jnp.exp(v - vals[0]) for v in vals]
    denom = exps[0]
    for e in exps[1:]:
        denom = denom + e
    onehot = jnp.zeros((tm, LANES), F32)
    for sel in sels:
        onehot = onehot + sel.astype(F32)
    before = _dot(below[...], onehot.astype(BF16)) + carry[...]
    ei = jnp.zeros((tm, LANES), jnp.int32)
    gt = jnp.zeros((tm, LANES), F32)
    rk = jnp.zeros((tm, LANES), jnp.int32)
    for k in range(TOP_K):
        at_k = lane == k
        r_k = jnp.sum(jnp.where(sels[k], before, 0.0), axis=-1, keepdims=True).astype(jnp.int32)
        ei = jnp.where(at_k, idxs[k], ei)
        gt = jnp.where(at_k, exps[k] / denom, gt)
        rk = jnp.where(at_k, r_k, rk)
    ei_ref[...] = ei
    gt_ref[...] = gt
    rk_ref[...] = rk
    carry[...] = carry[...] + jnp.sum(onehot, axis=0, keepdims=True)
    cnt_ref[...] = carry[...]


def _merge(c_p, c_s, og_p, og_s, pg, h, wco, bco, wdo, wo, nf, wr, br):
    T = h.shape[0]
    tm = _row_tile(math.gcd(c_p.shape[0], c_s.shape[0]), 512)
    n_p = c_p.shape[0] // tm
    row = lambda col: pl.BlockSpec((tm, D_MODEL), lambda i: (i, col))
    part_p = lambda: pl.BlockSpec((tm, D_MODEL), lambda i: (jnp.minimum(i, n_p - 1), 0))
    part_s = lambda: pl.BlockSpec((tm, D_MODEL), lambda i: (jnp.maximum(i - n_p, 0), 0))
    full = lambda a, b: pl.BlockSpec((a, b), lambda i: (0, 0))
    lanes = lambda: pl.BlockSpec((tm, LANES), lambda i: (i, 0))
    return pl.pallas_call(
        functools.partial(_merge_kernel, n_p=n_p),
        out_shape=(jax.ShapeDtypeStruct((T, D_MODEL), F32),
                   jax.ShapeDtypeStruct((T, D_MODEL // 2), jnp.uint32),
                   jax.ShapeDtypeStruct((T, LANES), jnp.int32),
                   jax.ShapeDtypeStruct((T, LANES), F32),
                   jax.ShapeDtypeStruct((T, LANES), jnp.int32),
                   jax.ShapeDtypeStruct((1, LANES), F32)),
        grid=(T // tm,),
        in_specs=[part_p(), part_s(), part_p(), part_s(), row(1), row(2), row(0),
                  full(D_MODEL, D_MODEL), full(1, D_MODEL), full(D_MODEL, D_MODEL), full(D_MODEL, D_MODEL),
                  full(1, D_MODEL), full(D_MODEL, LANES), full(1, LANES)],
        out_specs=(row(0), pl.BlockSpec((tm, D_MODEL // 2), lambda i: (i, 0)), lanes(), lanes(), lanes(),
                   full(1, LANES)),
        scratch_shapes=[pltpu.VMEM((1, LANES), F32), pltpu.VMEM((tm, tm), BF16)],
        compiler_params=_params(("arbitrary",)),
        name="merge",
    )(c_p, c_s, og_p, og_s, pg, pg, h, wco, bco, wdo, wo, nf, wr, br)


ISSUE_UNROLL = 8


def _row_copy(src, src_row, dst, dst_row, sem):
    return pltpu.make_async_copy(src.at[pl.ds(src_row, 1), :], dst.at[pl.ds(dst_row, 1), :], sem)


def _dispatch_kernel(dest_ref, nv_ref, x_ref, xb_out, zeros, sem, zsem, *, tm):
    @pl.when(pl.program_id(0) == 0)
    def _():
        zeros[...] = jnp.zeros_like(zeros)

        def clear(start):
            def body(i, carry):
                @pl.when(nv_ref[i] < MOE_ROWS)
                def _():
                    cp = pltpu.make_async_copy(
                        zeros, xb_out.at[pl.ds(pl.multiple_of(i * MOE_ROWS, MOE_ROWS), MOE_ROWS), :], zsem)
                    cp.start() if start else cp.wait()
                return carry
            lax.fori_loop(0, nv_ref.shape[0], body, 0)

        clear(True)
        clear(False)

    def issue(t, carry):
        for k in range(TOP_K):
            _row_copy(x_ref, t, xb_out, dest_ref[t * TOP_K + k], sem).start(priority=k % 2)
        return carry

    lax.fori_loop(0, tm, issue, 0, unroll=ISSUE_UNROLL)
    for k in range(TOP_K):
        pltpu.make_async_copy(x_ref, xb_out.at[pl.ds(0, tm), :], sem).wait()


def _dispatch(dest, rows_valid, xp):
    T, W = xp.shape
    tm = _row_tile(T, 512)
    n_rows = rows_valid.shape[0] * MOE_ROWS
    return pl.pallas_call(
        functools.partial(_dispatch_kernel, tm=tm),
        out_shape=jax.ShapeDtypeStruct((n_rows, W), xp.dtype),
        grid=(T // tm,),
        in_specs=[pl.BlockSpec((tm * TOP_K,), lambda i: (i,), memory_space=pltpu.SMEM),
                  pl.BlockSpec(memory_space=pltpu.SMEM),
                  pl.BlockSpec((tm, W), lambda i: (i, 0))],
        out_specs=pl.BlockSpec(memory_space=pl.ANY),
        scratch_shapes=[pltpu.VMEM((MOE_ROWS, W), xp.dtype), pltpu.SemaphoreType.DMA, pltpu.SemaphoreType.DMA],
        compiler_params=_params(("arbitrary",)),
        name="moe_dispatch",
    )(dest, rows_valid, xp)


def _moe_kernel(be_ref, nv_ref, first_ref, slot_ref, next_ref, x_ref, wu_hbm, bu_ref, wd_hbm, bd_ref, o_ref,
                wu32, wd32, wub, wdb, sems):
    i = pl.program_id(0)

    def fetch(e, s):
        pltpu.make_async_copy(wu_hbm.at[e], wu32.at[s], sems.at[0, s]).start()
        pltpu.make_async_copy(wd_hbm.at[e], wd32.at[s], sems.at[1, s]).start()

    @pl.when(i == 0)
    def _():
        fetch(be_ref[0], 0)

    @pl.when(first_ref[i] == 1)
    def _():
        s = slot_ref[i]
        pltpu.make_async_copy(wu_hbm.at[0], wu32.at[s], sems.at[0, s]).wait()
        pltpu.make_async_copy(wd_hbm.at[0], wd32.at[s], sems.at[1, s]).wait()
        wub[...] = wu32[s].astype(BF16)
        wdb[...] = wd32[s].astype(BF16)

        @pl.when(next_ref[i] >= 0)
        def _():
            fetch(next_ref[i], 1 - s)

    @pl.when(nv_ref[i] > 0)
    def _():
        hmid = _dot(_unpack_bf16_pairs(x_ref[...]), wub[...]) + bu_ref[0]
        hg = jnp.minimum(hmid[:, :D_FF], SWIGLU_LIMIT)
        hl = jnp.clip(hmid[:, D_FF:], -SWIGLU_LIMIT, SWIGLU_LIMIT)
        act = hg * _sigmoid(SWIGLU_ALPHA * hg) * (hl + 1.0)
        o_ref[...] = _dot(act.astype(BF16), wdb[...]) + bd_ref[0]

    @pl.when(nv_ref[i] == 0)
    def _():
        o_ref[...] = jnp.zeros_like(o_ref)


def _moe(plan, xb, w_up, b_up, w_down, b_down):
    R = xb.shape[0]
    nb = R // MOE_ROWS
    bias = lambda n: pl.BlockSpec((1, 1, n), lambda i, be, *_: (be[i], 0, 0))
    grid_spec = pltpu.PrefetchScalarGridSpec(
        num_scalar_prefetch=len(plan),
        grid=(nb,),
        in_specs=[pl.BlockSpec((MOE_ROWS, D_MODEL // 2), lambda i, *_: (i, 0)),
                  pl.BlockSpec(memory_space=pl.ANY), bias(2 * D_FF),
                  pl.BlockSpec(memory_space=pl.ANY), bias(D_MODEL)],
        out_specs=pl.BlockSpec((MOE_ROWS, D_MODEL), lambda i, *_: (i, 0)),
        scratch_shapes=[pltpu.VMEM((2, D_MODEL, 2 * D_FF), F32), pltpu.VMEM((2, D_FF, D_MODEL), F32),
                        pltpu.VMEM((D_MODEL, 2 * D_FF), BF16), pltpu.VMEM((D_FF, D_MODEL), BF16),
                        pltpu.SemaphoreType.DMA((2, 2))],
    )
    return pl.pallas_call(
        _moe_kernel,
        out_shape=jax.ShapeDtypeStruct((R, D_MODEL), F32),
        grid_spec=grid_spec,
        compiler_params=pltpu.CompilerParams(dimension_semantics=("arbitrary",), vmem_limit_bytes=MOE_VMEM_LIMIT),
        name="moe_experts",
    )(*plan, xb, w_up, b_up, w_down, b_down)


def _dispatch_plan(counts, ei, rk, T):
    A = T * TOP_K
    n_blocks = -(-A // MOE_ROWS) + N_EXPERTS
    counts = counts[0, :N_EXPERTS].astype(jnp.int32)
    padded = (counts + MOE_ROWS - 1) // MOE_ROWS * MOE_ROWS
    pend = jnp.cumsum(padded)
    pstart = pend - padded
    experts = jnp.arange(N_EXPERTS, dtype=jnp.int32)
    seg_start = jnp.sum(jnp.where(ei[:, :TOP_K, None] == experts, pstart, 0), axis=-1)
    dest = (seg_start + rk[:, :TOP_K]).reshape(-1).astype(jnp.int32)
    starts = jnp.arange(n_blocks, dtype=jnp.int32) * MOE_ROWS
    block_e = jnp.minimum(jnp.sum((pend[None, :] <= starts[:, None]).astype(jnp.int32), axis=1), N_EXPERTS - 1)
    seg_end = jnp.sum(jnp.where(block_e[:, None] == experts, pstart + counts, 0), axis=-1)
    rows_valid = jnp.clip(seg_end - starts, 0, MOE_ROWS).astype(jnp.int32)
    prev_e = jnp.concatenate([jnp.full((1,), -1, jnp.int32), block_e[:-1]])
    first = jnp.logical_and(block_e != prev_e, rows_valid > 0).astype(jnp.int32)
    slot = (jnp.cumsum(first) - 1) % 2
    later = jnp.logical_and(experts[None, :] > experts[:, None], (counts > 0)[None, :])
    next_of = jnp.min(jnp.where(later, experts[None, :], N_EXPERTS), axis=1)
    next_of = jnp.where(next_of == N_EXPERTS, -1, next_of)
    next_e = jnp.sum(jnp.where(block_e[:, None] == experts, next_of, 0), axis=-1)
    plan = tuple(a.astype(jnp.int32) for a in (block_e, rows_valid, first, slot, next_e))
    return dest, plan


def _combine_kernel(dest_ref, dnext_ref, gt_ref, h_ref, nw_ref, yb_ref, yp_ref, ys_ref, buf, res, sems, osems,
                    *, tm, piece, lp, seq, n_prompt_rows):
    i = pl.program_id(0)
    last = pl.num_programs(0) - 1
    slot = lax.rem(i, 2)
    n_head = FRONT + N_META

    def request(d_ref, s):
        def issue(t, carry):
            for k in range(TOP_K):
                _row_copy(yb_ref, d_ref[t * TOP_K + k], buf.at[s, k], t, sems.at[s]).start(priority=k % 2)
            return carry

        lax.fori_loop(0, tm, issue, 0, unroll=ISSUE_UNROLL)

    def writes(tile, s, start):
        for j in range(tm // piece):
            r = tile * tm + j * piece
            src = res.at[s, pl.ds(j * piece, piece), :]
            b = lax.div(r, lp)
            off = r - b * lp

            @pl.when(jnp.logical_and(r < n_prompt_rows, off >= n_head))
            def _():
                cp = pltpu.make_async_copy(
                    src, yp_ref.at[pl.ds(pl.multiple_of(b * seq + off - n_head, 8), piece), :], osems.at[s])
                cp.start() if start else cp.wait()

            @pl.when(r >= n_prompt_rows)
            def _():
                cp = pltpu.make_async_copy(
                    src, ys_ref.at[pl.ds(pl.multiple_of(r - n_prompt_rows, 8), piece), :], osems.at[s])
                cp.start() if start else cp.wait()

    @pl.when(i == 0)
    def _():
        request(dest_ref, 0)

    @pl.when(i + 1 <= last)
    def _():
        request(dnext_ref, 1 - slot)

    @pl.when(i > 0)
    def _():
        writes(i - 1, 1 - slot, start=False)

    for k in range(TOP_K):
        pltpu.make_async_copy(yb_ref.at[pl.ds(0, tm), :], buf.at[slot, k], sems.at[slot]).wait()
    gt = gt_ref[...]
    x = h_ref[...]
    for k in range(TOP_K):
        x = x + gt[:, k:k + 1] * buf[slot, k]
    res[slot] = x * lax.rsqrt(jnp.mean(x * x, axis=-1, keepdims=True) + EPS) * nw_ref[...]
    writes(i, slot, start=True)

    @pl.when(i == last)
    def _():
        writes(i, slot, start=False)


def _combine(dest, gt, h1, nw, yb, B, LP, SEQ):
    T = h1.shape[0]
    n_prompt_rows = B * LP
    n_sample_rows = T - n_prompt_rows
    tm = _row_tile(T, 512)
    n = T // tm
    piece = math.gcd(math.gcd(CHUNK, n_sample_rows), tm)
    assert piece % 8 == 0
    row = lambda: pl.BlockSpec((tm, D_MODEL), lambda i: (i, 0))
    return pl.pallas_call(
        functools.partial(_combine_kernel, tm=tm, piece=piece, lp=LP, seq=SEQ, n_prompt_rows=n_prompt_rows),
        out_shape=(jax.ShapeDtypeStruct((B * SEQ, D_MODEL), F32),
                   jax.ShapeDtypeStruct((n_sample_rows, D_MODEL), F32)),
        grid=(n,),
        in_specs=[pl.BlockSpec((tm * TOP_K,), lambda i: (i,), memory_space=pltpu.SMEM),
                  pl.BlockSpec((tm * TOP_K,), lambda i: (jnp.minimum(i + 1, n - 1),), memory_space=pltpu.SMEM),
                  pl.BlockSpec((tm, LANES), lambda i: (i, 0)),
                  row(),
                  pl.BlockSpec((1, D_MODEL), lambda i: (0, 0)),
                  pl.BlockSpec(memory_space=pl.ANY)],
        out_specs=(pl.BlockSpec(memory_space=pl.ANY), pl.BlockSpec(memory_space=pl.ANY)),
        scratch_shapes=[pltpu.VMEM((2, TOP_K, tm, D_MODEL), F32), pltpu.VMEM((2, tm, D_MODEL), F32),
                        pltpu.SemaphoreType.DMA((2,)), pltpu.SemaphoreType.DMA((2,))],
        compiler_params=_params(("arbitrary",)),
        name="moe_combine_final",
    )(dest, dest, gt, h1, nw, yb)


def _pad_lanes(v, fill=0.0):
    v = v.reshape(1, -1).astype(F32)
    return jnp.pad(v, ((0, 0), (0, LANES - v.shape[1])), constant_values=fill)


def kernel(x_prompt, x_sample, state_conf_conv, state_dn_conv, state_dn_S, meta_tokens, norm_mix, w_in, w_conf_dw, b_conf_dw, ln_conf_g, ln_conf_b, w_conf_out, b_conf_out, w_dn_conv, dn_a_log, dn_dt_bias, dn_norm_w, w_dn_out, w_out, norm_ffn, w_router, b_router, w_up, b_up, w_down, b_down, norm_final):
    B, SEQ, D = x_prompt.shape
    NB, LS, _ = x_sample.shape
    depth = w_in.shape[0]
    assert D == D_MODEL and depth == 1 and SEQ % CHUNK == 0 and LS >= SHORT_W - 1
    LP = FRONT + N_META + SEQ
    TP = B * LP
    T = TP + NB * LS
    n_qk = DN_HEADS * DN_DK
    o_q = 2 * D_MODEL
    o_a = o_q + 4 * n_qk
    o_gate = o_a + 2 * DN_HEADS


    w_in0 = w_in[0]
    o_z = o_q + 3 * n_qk
    w_gate = jnp.concatenate([w_in0[:, o_z:o_a], w_in0[:, o_gate:]], axis=1)
    w_ab = jnp.pad(w_in0[:, o_a:o_gate], ((0, 0), (0, LANES - 2 * DN_HEADS))).astype(BF16)
    wco = w_conf_out[0].astype(BF16)
    wdo = w_dn_out[0].astype(BF16)
    wo = w_out[0].astype(BF16)
    wr = jnp.pad(w_router[0], ((0, 0), (0, LANES - N_EXPERTS))).astype(BF16)
    br = _pad_lanes(b_router[0], fill=-1e30)
    alog = _pad_lanes(dn_a_log[0])
    dtb = _pad_lanes(dn_dt_bias[0])
    row = lambda v: v.reshape(1, -1).astype(F32)

    h0, xn, ab = _tokens(x_prompt, x_sample.reshape(NB * LS, D), meta_tokens.astype(F32), row(norm_mix[0]), w_ab)
    n_main = o_z // D_MODEL
    p = _mm_in(xn, w_in0, n_main, F32)
    pg = _mm_in(xn, w_gate, 3, BF16)
    p_s3 = p[TP:].reshape(NB, LS, n_main * D_MODEL)
    z_s3 = pg[TP:, :n_qk].astype(F32).reshape(NB, LS, n_qk)
    ab_s3 = ab[TP:].reshape(NB, LS, LANES)

    c_p, ust_p = _conf_prompt(p, B, LP, w_conf_dw[0], row(b_conf_dw[0]), row(ln_conf_g[0]), row(ln_conf_b[0]))
    c_s, conf_state_s = _conf_sample(p_s3, state_conf_conv[0], w_conf_dw[0], row(b_conf_dw[0]),
                                     row(ln_conf_g[0]), row(ln_conf_b[0]))

    og_p, s_p = _gdn_prompt(p, pg, ab, B, LP, w_dn_conv[0], alog, dtb, row(dn_norm_w[0]))
    og_s, s_s = _gdn_sample(p_s3, z_s3, ab_s3, state_dn_conv[0], state_dn_S[0], w_dn_conv[0], alog, dtb,
                            row(dn_norm_w[0]))

    h1, xp2, ei, gt, rk, counts = _merge(c_p, c_s.reshape(NB * LS, D).astype(BF16),
                                         og_p, og_s.reshape(NB * LS, n_qk).astype(BF16), pg, h0,
                                         wco, row(b_conf_out[0]), wdo, wo, row(norm_ffn[0]), wr, br)

    dest, plan = _dispatch_plan(counts, ei, rk, T)
    xb = _dispatch(dest, plan[1], xp2)
    yb = _moe(plan, xb, w_up[0], b_up[0].reshape(N_EXPERTS, 1, -1), w_down[0], b_down[0].reshape(N_EXPERTS, 1, -1))
    yp, ys = _combine(dest, gt, h1, row(norm_final), yb, B, LP, SEQ)

    y_prompt = yp.reshape(B, SEQ, D)
    y_sample = ys.reshape(NB, LS, D)
    hist = CONV_W - 1
    conf_conv_prompt = ust_p[:, HALO - hist:][None]
    dn_conv_prompt = jnp.stack([p[(b + 1) * LP - (SHORT_W - 1):(b + 1) * LP, o_q:o_q + 3 * n_qk]
                                for b in range(B)])[None]
    dn_conv_sample = p_s3[:, LS - (SHORT_W - 1):, o_q:o_q + 3 * n_qk][None]
    return (y_prompt, y_sample, conf_conv_prompt, dn_conv_prompt, s_p[None],
            conf_state_s[None], dn_conv_sample, s_s[None])
```
